```python
import math
import jax, jax.numpy as jnp
from jax import lax
import numpy as np

D_MODEL = 1024
BATCH = 8
SEQ = 4096
DEPTH = 2

CHUNK = 64
EPS = 1e-6
SB_HEADS = 8
SB_HEAD_DIM = 64
SB_WIDTH = SB_HEADS * SB_HEAD_DIM
SB_BLOCK = 128
DN_HEADS = 4
DN_HEAD_DIM = 128
DN_WIDTH = DN_HEADS * DN_HEAD_DIM
CONV_WIDTH = 4
DN_CHUNK = CHUNK
ADA_SCALE = 0.3
IN_SIZES = [SB_WIDTH] * 4 + [DN_WIDTH] * 4 + [DN_HEADS] * 2 + [D_MODEL] * 2
IN_COLS = sum(IN_SIZES)
IN_SPLITS = [int(s) for s in np.cumsum(IN_SIZES)[:-1]]

kernel_name = "hybrid_stickbreak_gated_deltanet_block"


def rms_norm(x, g):
    xf = x.astype(jnp.float32)
    y = xf * lax.rsqrt(jnp.mean(xf * xf, axis=-1, keepdims=True) + EPS)
    return (y * g.astype(jnp.float32)).astype(x.dtype)


def l2_norm(x):
    xf = x.astype(jnp.float32)
    return xf * lax.rsqrt(jnp.sum(xf * xf, axis=-1, keepdims=True) + EPS)


def to_heads(t, n, d):
    b, s, _ = t.shape
    return t.reshape(b, s, n, d).transpose(0, 2, 1, 3)


def from_heads(t):
    b, h, s, d = t.shape
    return t.transpose(0, 2, 1, 3).reshape(b, s, h * d)


def stick_breaking_attention(q, k, v):
    S = q.shape[2]
    scale = 1.0 / math.sqrt(SB_HEAD_DIM)
    outs = []
    for start in range(0, S, SB_BLOCK):
        end = start + SB_BLOCK
        qb = q[:, :, start:end].astype(jnp.float32)
        kb = k[:, :, :end].astype(jnp.float32)
        vb = v[:, :, :end].astype(jnp.float32)
        z = jnp.einsum('bhqd,bhkd->bhqk', qb, kb) * scale
        q_pos = start + jnp.arange(SB_BLOCK)
        k_pos = jnp.arange(end)
        valid = k_pos[None, :] < q_pos[:, None]
        log_keep = jnp.where(valid, jax.nn.log_sigmoid(-z), 0.0)
        later = lax.cumsum(log_keep, axis=3, reverse=True) - log_keep
        w = jnp.where(valid, jnp.exp(jax.nn.log_sigmoid(z) + later), 0.0)
        outs.append(jnp.einsum('bhqk,bhkd->bhqd', w, vb))
    return jnp.concatenate(outs, axis=2).astype(v.dtype)


def causal_short_conv(x, w):
    K, C = w.shape
    out = lax.conv_general_dilated(x, w[:, None, :], window_strides=(1,), padding=[(K - 1, 0)],
                                   dimension_numbers=('NWC', 'WIO', 'NWC'), feature_group_count=C)
    return jax.nn.silu(out)


def gated_delta_rule(q, k, v, beta, g):
    B, H, S, Dk = q.shape
    Dv = v.shape[-1]
    C = DN_CHUNK
    N = S // C
    f32 = jnp.float32
    q = q.astype(f32) * (Dk ** -0.5)
    k, v, beta, g = k.astype(f32), v.astype(f32), beta.astype(f32), g.astype(f32)
    q, k, v = (t.reshape(B, H, N, C, t.shape[-1]) for t in (q, k, v))
    beta = beta.reshape(B, H, N, C)
    g = jnp.cumsum(g.reshape(B, H, N, C), axis=-1)
    tril = jnp.tril(jnp.ones((C, C), dtype=bool))
    strict = jnp.tril(jnp.ones((C, C), dtype=bool), -1)
    decay = jnp.exp(jnp.where(tril, g[..., :, None] - g[..., None, :], -jnp.inf))
    k_beta = k * beta[..., None]
    v_beta = v * beta[..., None]
    m = jnp.where(strict, jnp.einsum('bhnid,bhnjd->bhnij', k_beta, k) * decay, 0.0)
    u = lax.linalg.triangular_solve(m, v_beta, left_side=True, lower=True, unit_diagonal=True)
    w = lax.linalg.triangular_solve(m, k_beta * jnp.exp(g)[..., None], left_side=True,
                                    lower=True, unit_diagonal=True)
    intra = jnp.where(tril, jnp.einsum('bhnid,bhnjd->bhnij', q, k) * decay, 0.0)
    xs = tuple(jnp.moveaxis(t, 2, 0) for t in (q, k, u, w, g, intra))

    def step(state, inp):
        qc, kc, uc, wc, gc, ac = inp
        v_new = uc - jnp.einsum('bhcd,bhde->bhce', wc, state)
        o = (jnp.einsum('bhcd,bhde->bhce', qc * jnp.exp(gc)[..., None], state)
             + jnp.einsum('bhij,bhje->bhie', ac, v_new))
        g_last = gc[..., -1]
        state = (state * jnp.exp(g_last)[..., None, None]
                 + jnp.einsum('bhcd,bhce->bhde', kc * jnp.exp(g_last[..., None] - gc)[..., None], v_new))
        return state, o

    _, o = lax.scan(step, jnp.zeros((B, H, Dk, Dv), f32), xs)
    return jnp.moveaxis(o, 0, 2).reshape(B, H, S, Dv)


def hybrid_layer(x, c, ada_w, ada_b, norm_g, w_in, sb_q_g, sb_k_g, conv_w, dn_a_log,
                 dn_dt_bias, dn_norm_g, w_branch_sb, w_branch_dn, w_out):
    B, S, _ = x.shape
    mod = jax.nn.silu(c) @ ada_w + ada_b
    shift, scale, gate = jnp.split(mod, 3, axis=-1)
    h = rms_norm(x, norm_g) * (1.0 + scale[:, None]) + shift[:, None]

    p = h @ w_in
    (sb_q, sb_k, sb_v, sb_z, dn_q, dn_k, dn_v, dn_z, dn_b, dn_a,
     merge_sb, merge_dn) = jnp.split(p, IN_SPLITS, axis=-1)

    qa = rms_norm(to_heads(sb_q, SB_HEADS, SB_HEAD_DIM), sb_q_g)
    ka = rms_norm(to_heads(sb_k, SB_HEADS, SB_HEAD_DIM), sb_k_g)
    va = to_heads(sb_v, SB_HEADS, SB_HEAD_DIM)
    o_sb = from_heads(stick_breaking_attention(qa, ka, va)) * jax.nn.silu(sb_z)

    qkv = causal_short_conv(jnp.concatenate([dn_q, dn_k, dn_v], axis=-1), conv_w)
    qb, kb, vb = jnp.split(qkv, 3, axis=-1)
    qb = l2_norm(to_heads(qb, DN_HEADS, DN_HEAD_DIM))
    kb = l2_norm(to_heads(kb, DN_HEADS, DN_HEAD_DIM))
    vb = to_heads(vb, DN_HEADS, DN_HEAD_DIM)
    beta = jax.nn.sigmoid(dn_b.astype(jnp.float32)).transpose(0, 2, 1)
    log_decay = (-jnp.exp(dn_a_log.astype(jnp.float32))
                 * jax.nn.softplus(dn_a.astype(jnp.float32) + dn_dt_bias.astype(jnp.float32)))
    log_decay = log_decay.transpose(0, 2, 1)
    o_dn = gated_delta_rule(qb, kb, vb, beta, log_decay)
    o_dn = o_dn.transpose(0, 2, 1, 3)
    z_dn = dn_z.reshape(B, S, DN_HEADS, DN_HEAD_DIM).astype(jnp.float32)
    o_dn = (rms_norm(o_dn, dn_norm_g) * jax.nn.silu(z_dn)).reshape(B, S, DN_WIDTH).astype(x.dtype)

    y = (jax.nn.sigmoid(merge_sb) * (o_sb @ w_branch_sb)
         + jax.nn.sigmoid(merge_dn) * (o_dn @ w_branch_dn))
    out = y @ w_out
    return x + gate[:, None] * out


def _fwd_setup_inputs(seed: int = 0) -> dict:
    key = jax.random.key(seed)
    ks = jax.random.split(key, 16)
    f32 = jnp.float32
    D = D_MODEL
    nrm = lambda k, shape, s: jax.random.normal(k, shape, f32) * s
    dt = jnp.exp(jax.random.uniform(ks[10], (DEPTH, DN_HEADS), f32, math.log(1e-3), math.log(1e-1)))
    return {
        "x": nrm(ks[0], (BATCH, SEQ, D), 1.0),
        "c": nrm(ks[1], (BATCH, D), 1.0),
        "ada_w": nrm(ks[2], (DEPTH, D, 3 * D), ADA_SCALE * D ** -0.5),
        "ada_b": nrm(ks[3], (DEPTH, 3 * D), 0.02),
        "norm_g": 1.0 + nrm(ks[4], (DEPTH, D), 0.02),
        "w_in": nrm(ks[5], (DEPTH, D, IN_COLS), D ** -0.5),
        "sb_q_g": 1.0 + nrm(ks[6], (DEPTH, SB_HEAD_DIM), 0.02),
        "sb_k_g": 1.0 + nrm(ks[7], (DEPTH, SB_HEAD_DIM), 0.02),
        "conv_w": nrm(ks[8], (DEPTH, CONV_WIDTH, 3 * DN_WIDTH), CONV_WIDTH ** -0.5),
        "dn_a_log": jnp.log(jax.random.uniform(ks[9], (DEPTH, DN_HEADS), f32, 1.0, 16.0)),
        "dn_dt_bias": dt + jnp.log(-jnp.expm1(-dt)),
        "dn_norm_g": 1.0 + nrm(ks[11], (DEPTH, DN_HEAD_DIM), 0.02),
        "w_branch_sb": nrm(ks[12], (DEPTH, SB_WIDTH, D), SB_WIDTH ** -0.5),
        "w_branch_dn": nrm(ks[13], (DEPTH, DN_WIDTH, D), DN_WIDTH ** -0.5),
        "w_out": nrm(ks[14], (DEPTH, D, D), D ** -0.5),
    }


def _fwd_reference(x, c, ada_w, ada_b, norm_g, w_in, sb_q_g, sb_k_g, conv_w, dn_a_log,
              dn_dt_bias, dn_norm_g, w_branch_sb, w_branch_dn, w_out):
    for l in range(DEPTH):
        x = hybrid_layer(x, c, ada_w[l], ada_b[l], norm_g[l], w_in[l], sb_q_g[l], sb_k_g[l],
                         conv_w[l], dn_a_log[l], dn_dt_bias[l], dn_norm_g[l],
                         w_branch_sb[l], w_branch_dn[l], w_out[l])
    return x


import jax as _jax
import jax.numpy as _jnp

TWIN_FORMAT = 'train_step'
FWD_PARAMS = ['x', 'c', 'ada_w', 'ada_b', 'norm_g', 'w_in', 'sb_q_g', 'sb_k_g', 'conv_w', 'dn_a_log', 'dn_dt_bias', 'dn_norm_g', 'w_branch_sb', 'w_branch_dn', 'w_out']
TWIN_WEIGHTS = ['ada_w', 'ada_b', 'norm_g', 'w_in', 'sb_q_g', 'sb_k_g', 'conv_w', 'dn_a_log', 'dn_dt_bias', 'dn_norm_g', 'w_branch_sb', 'w_branch_dn', 'w_out']
TWIN_DIFF_INPUT = 'x'
TWIN_INPUTS = ['x', 'c', 'ada_w', 'ada_b', 'norm_g', 'w_in', 'sb_q_g', 'sb_k_g', 'conv_w', 'dn_a_log', 'dn_dt_bias', 'dn_norm_g', 'w_branch_sb', 'w_branch_dn', 'w_out', 'loss_target', 'm_ada_w', 'm_ada_b', 'm_norm_g', 'm_w_in', 'm_sb_q_g', 'm_sb_k_g', 'm_conv_w', 'm_dn_a_log', 'm_dn_dt_bias', 'm_dn_norm_g', 'm_w_branch_sb', 'm_w_branch_dn', 'm_w_out', 'v_ada_w', 'v_ada_b', 'v_norm_g', 'v_w_in', 'v_sb_q_g', 'v_sb_k_g', 'v_conv_w', 'v_dn_a_log', 'v_dn_dt_bias', 'v_dn_norm_g', 'v_w_branch_sb', 'v_w_branch_dn', 'v_w_out']
TWIN_OUTPUTS = ['loss', 'grad_x', 'grad_ada_w', 'grad_ada_b', 'grad_norm_g', 'grad_w_in', 'grad_sb_q_g', 'grad_sb_k_g', 'grad_conv_w', 'grad_dn_a_log', 'grad_dn_dt_bias', 'grad_dn_norm_g', 'grad_w_branch_sb', 'grad_w_branch_dn', 'grad_w_out', 'delta_ada_w', 'delta_ada_b', 'delta_norm_g', 'delta_w_in', 'delta_sb_q_g', 'delta_sb_k_g', 'delta_conv_w', 'delta_dn_a_log', 'delta_dn_dt_bias', 'delta_dn_norm_g', 'delta_w_branch_sb', 'delta_w_branch_dn', 'delta_w_out', 'new_m_ada_w', 'new_m_ada_b', 'new_m_norm_g', 'new_m_w_in', 'new_m_sb_q_g', 'new_m_sb_k_g', 'new_m_conv_w', 'new_m_dn_a_log', 'new_m_dn_dt_bias', 'new_m_dn_norm_g', 'new_m_w_branch_sb', 'new_m_w_branch_dn', 'new_m_w_out', 'new_v_ada_w', 'new_v_ada_b', 'new_v_norm_g', 'new_v_w_in', 'new_v_sb_q_g', 'new_v_sb_k_g', 'new_v_conv_w', 'new_v_dn_a_log', 'new_v_dn_dt_bias', 'new_v_dn_norm_g', 'new_v_w_branch_sb', 'new_v_w_branch_dn', 'new_v_w_out']
TWIN_LEAF_KINDS = {'loss': 'loss', 'grad_x': 'grad_x', 'grad_ada_w': 'grad_w', 'grad_ada_b': 'grad_w', 'grad_norm_g': 'grad_w', 'grad_w_in': 'grad_w', 'grad_sb_q_g': 'grad_w', 'grad_sb_k_g': 'grad_w', 'grad_conv_w': 'grad_w', 'grad_dn_a_log': 'grad_w', 'grad_dn_dt_bias': 'grad_w', 'grad_dn_norm_g': 'grad_w', 'grad_w_branch_sb': 'grad_w', 'grad_w_branch_dn': 'grad_w', 'grad_w_out': 'grad_w', 'delta_ada_w': 'delta_w', 'delta_ada_b': 'delta_w', 'delta_norm_g': 'delta_w', 'delta_w_in': 'delta_w', 'delta_sb_q_g': 'delta_w', 'delta_sb_k_g': 'delta_w', 'delta_conv_w': 'delta_w', 'delta_dn_a_log': 'delta_w', 'delta_dn_dt_bias': 'delta_w', 'delta_dn_norm_g': 'delta_w', 'delta_w_branch_sb': 'delta_w', 'delta_w_branch_dn': 'delta_w', 'delta_w_out': 'delta_w', 'new_m_ada_w': 'new_m', 'new_m_ada_b': 'new_m', 'new_m_norm_g': 'new_m', 'new_m_w_in': 'new_m', 'new_m_sb_q_g': 'new_m', 'new_m_sb_k_g': 'new_m', 'new_m_conv_w': 'new_m', 'new_m_dn_a_log': 'new_m', 'new_m_dn_dt_bias': 'new_m', 'new_m_dn_norm_g': 'new_m', 'new_m_w_branch_sb': 'new_m', 'new_m_w_branch_dn': 'new_m', 'new_m_w_out': 'new_m', 'new_v_ada_w': 'new_v', 'new_v_ada_b': 'new_v', 'new_v_norm_g': 'new_v', 'new_v_w_in': 'new_v', 'new_v_sb_q_g': 'new_v', 'new_v_sb_k_g': 'new_v', 'new_v_conv_w': 'new_v', 'new_v_dn_a_log': 'new_v', 'new_v_dn_dt_bias': 'new_v', 'new_v_dn_norm_g': 'new_v', 'new_v_w_branch_sb': 'new_v', 'new_v_w_branch_dn': 'new_v', 'new_v_w_out': 'new_v'}


def _forward(args):
    return _fwd_reference(*[args[k] for k in FWD_PARAMS])


def _output_shape():
    out = _jax.eval_shape(lambda: _forward(_fwd_setup_inputs(0)))
    return out.shape, out.dtype

N_MICROBATCH = 1
ADAM_LR = 0.001
ADAM_B1 = 0.9
ADAM_B2 = 0.999
ADAM_EPS = 1e-08
ADAM_WD = 0.01
ADAM_STEP = 10
PER_EXAMPLE_BATCH_AXIS = {'x': 0, 'c': 0, 'loss_target': 0}
SHARED_INPUTS = []
_WEIGHT_DTYPES = {'ada_w': _jnp.float32, 'ada_b': _jnp.float32, 'norm_g': _jnp.float32, 'w_in': _jnp.float32, 'sb_q_g': _jnp.float32, 'sb_k_g': _jnp.float32, 'conv_w': _jnp.float32, 'dn_a_log': _jnp.float32, 'dn_dt_bias': _jnp.float32, 'dn_norm_g': _jnp.float32, 'w_branch_sb': _jnp.float32, 'w_branch_dn': _jnp.float32, 'w_out': _jnp.float32}
MOMENT_SCALE = {'ada_w': 1.469269e-01, 'ada_b': 3.109094e-01, 'norm_g': 2.477217e-01, 'w_in': 1.543870e-02, 'sb_q_g': 1.110944e-01, 'sb_k_g': 1.102657e-01, 'conv_w': 2.122578e-02, 'dn_a_log': 4.640583e-01, 'dn_dt_bias': 4.520346e-01, 'dn_norm_g': 9.980343e-01, 'w_branch_sb': 1.130085e-02, 'w_branch_dn': 2.003932e-02, 'w_out': 2.240510e-02}


def _to_microbatches(a, axis):
    t = _jnp.moveaxis(a, axis, 0)
    t = t.reshape((N_MICROBATCH, t.shape[0] // N_MICROBATCH) + t.shape[1:])
    return _jnp.moveaxis(t, 1, axis + 1)


def setup_inputs(seed: int = 0) -> dict:
    inp = _fwd_setup_inputs(seed)
    key = _jax.random.fold_in(_jax.random.key(seed), 7919)
    shape, _ = _output_shape()
    out = dict(inp)
    out["loss_target"] = _jax.random.normal(_jax.random.fold_in(key, 0), shape, _jnp.float32)
    for i, name in enumerate(TWIN_WEIGHTS):
        w = inp[name].astype(_jnp.float32)
        if MOMENT_SCALE is None:
            s = _jnp.sqrt(_jnp.mean(_jnp.square(w)) + 1e-30)
        else:
            s = MOMENT_SCALE[name]
        km, kv = _jax.random.split(_jax.random.fold_in(key, i + 1))
        out[name] = w
        out["m_" + name] = s * _jax.random.normal(km, w.shape, _jnp.float32)
        out["v_" + name] = (s * s) * _jax.random.uniform(kv, w.shape, _jnp.float32, 0.5, 1.5)
    if N_MICROBATCH > 1:
        for name, axis in PER_EXAMPLE_BATCH_AXIS.items():
            out[name] = _to_microbatches(out[name], axis)
    return {'x': out['x'], 'c': out['c'], 'ada_w': out['ada_w'], 'ada_b': out['ada_b'], 'norm_g': out['norm_g'], 'w_in': out['w_in'], 'sb_q_g': out['sb_q_g'], 'sb_k_g': out['sb_k_g'], 'conv_w': out['conv_w'], 'dn_a_log': out['dn_a_log'], 'dn_dt_bias': out['dn_dt_bias'], 'dn_norm_g': out['dn_norm_g'], 'w_branch_sb': out['w_branch_sb'], 'w_branch_dn': out['w_branch_dn'], 'w_out': out['w_out'], 'loss_target': out['loss_target'], 'm_ada_w': out['m_ada_w'], 'm_ada_b': out['m_ada_b'], 'm_norm_g': out['m_norm_g'], 'm_w_in': out['m_w_in'], 'm_sb_q_g': out['m_sb_q_g'], 'm_sb_k_g': out['m_sb_k_g'], 'm_conv_w': out['m_conv_w'], 'm_dn_a_log': out['m_dn_a_log'], 'm_dn_dt_bias': out['m_dn_dt_bias'], 'm_dn_norm_g': out['m_dn_norm_g'], 'm_w_branch_sb': out['m_w_branch_sb'], 'm_w_branch_dn': out['m_w_branch_dn'], 'm_w_out': out['m_w_out'], 'v_ada_w': out['v_ada_w'], 'v_ada_b': out['v_ada_b'], 'v_norm_g': out['v_norm_g'], 'v_w_in': out['v_w_in'], 'v_sb_q_g': out['v_sb_q_g'], 'v_sb_k_g': out['v_sb_k_g'], 'v_conv_w': out['v_conv_w'], 'v_dn_a_log': out['v_dn_a_log'], 'v_dn_dt_bias': out['v_dn_dt_bias'], 'v_dn_norm_g': out['v_dn_norm_g'], 'v_w_branch_sb': out['v_w_branch_sb'], 'v_w_branch_dn': out['v_w_branch_dn'], 'v_w_out': out['v_w_out']}


def _loss(weights, diff, rest, loss_target):
    with _jax.named_scope("forward"):
        args = {**rest, TWIN_DIFF_INPUT: diff, **{k: w.astype(_WEIGHT_DTYPES[k]) for k, w in weights.items()}}
        y = _forward(args)
    with _jax.named_scope("loss_head"):
        err = _jnp.square(y.astype(_jnp.float32) - loss_target)
        return 0.5 * _jnp.sum(_jnp.mean(err, axis=-1)) if err.ndim else 0.5 * err


def _adamw(w, g, m, v):
    m = ADAM_B1 * m + (1.0 - ADAM_B1) * g
    v = ADAM_B2 * v + (1.0 - ADAM_B2) * _jnp.square(g)
    m_hat = m / (1.0 - ADAM_B1 ** ADAM_STEP)
    v_hat = v / (1.0 - ADAM_B2 ** ADAM_STEP)
    delta = -ADAM_LR * (m_hat / (_jnp.sqrt(v_hat) + ADAM_EPS) + ADAM_WD * w)
    return delta, m, v


def reference(x, c, ada_w, ada_b, norm_g, w_in, sb_q_g, sb_k_g, conv_w, dn_a_log, dn_dt_bias, dn_norm_g, w_branch_sb, w_branch_dn, w_out, loss_target, m_ada_w, m_ada_b, m_norm_g, m_w_in, m_sb_q_g, m_sb_k_g, m_conv_w, m_dn_a_log, m_dn_dt_bias, m_dn_norm_g, m_w_branch_sb, m_w_branch_dn, m_w_out, v_ada_w, v_ada_b, v_norm_g, v_w_in, v_sb_q_g, v_sb_k_g, v_conv_w, v_dn_a_log, v_dn_dt_bias, v_dn_norm_g, v_w_branch_sb, v_w_branch_dn, v_w_out):
    given = dict(x=x, c=c, ada_w=ada_w, ada_b=ada_b, norm_g=norm_g, w_in=w_in, sb_q_g=sb_q_g, sb_k_g=sb_k_g, conv_w=conv_w, dn_a_log=dn_a_log, dn_dt_bias=dn_dt_bias, dn_norm_g=dn_norm_g, w_branch_sb=w_branch_sb, w_branch_dn=w_branch_dn, w_out=w_out, loss_target=loss_target, m_ada_w=m_ada_w, m_ada_b=m_ada_b, m_norm_g=m_norm_g, m_w_in=m_w_in, m_sb_q_g=m_sb_q_g, m_sb_k_g=m_sb_k_g, m_conv_w=m_conv_w, m_dn_a_log=m_dn_a_log, m_dn_dt_bias=m_dn_dt_bias, m_dn_norm_g=m_dn_norm_g, m_w_branch_sb=m_w_branch_sb, m_w_branch_dn=m_w_branch_dn, m_w_out=m_w_out, v_ada_w=v_ada_w, v_ada_b=v_ada_b, v_norm_g=v_norm_g, v_w_in=v_w_in, v_sb_q_g=v_sb_q_g, v_sb_k_g=v_sb_k_g, v_conv_w=v_conv_w, v_dn_a_log=v_dn_a_log, v_dn_dt_bias=v_dn_dt_bias, v_dn_norm_g=v_dn_norm_g, v_w_branch_sb=v_w_branch_sb, v_w_branch_dn=v_w_branch_dn, v_w_out=v_w_out)
    weights = {n: given[n] for n in TWIN_WEIGHTS}
    shared = {n: given[n] for n in SHARED_INPUTS}
    per_example = {n: given[n] for n in ['x', 'c']}
    grad_fn = _jax.value_and_grad(_loss, argnums=(0, 1))

    def one_microbatch(ex, loss_target):
        ex = dict(ex)
        diff = ex.pop(TWIN_DIFF_INPUT)
        return grad_fn(weights, diff, {**shared, **ex}, loss_target)

    if N_MICROBATCH == 1:
        loss, (grad_w, grad_x) = one_microbatch(per_example, given["loss_target"])
    else:
        def body(carry, xs):
            loss_sum, grad_sum = carry
            l_k, (gw_k, gx_k) = one_microbatch(xs[0], xs[1])
            with _jax.named_scope("update"):
                return (loss_sum + l_k, _jax.tree.map(_jnp.add, grad_sum, gw_k)), gx_k

        init = (_jnp.zeros((), _jnp.float32), _jax.tree.map(_jnp.zeros_like, weights))
        (loss, grad_w), grad_x = _jax.lax.scan(body, init, (per_example, given["loss_target"]))
    with _jax.named_scope("update"):
        delta_w, new_m, new_v = {}, {}, {}
        for n in TWIN_WEIGHTS:
            delta_w[n], new_m[n], new_v[n] = _adamw(weights[n], grad_w[n], given["m_" + n], given["v_" + n])
    return (loss, grad_x, *[grad_w[n] for n in TWIN_WEIGHTS], *[delta_w[n] for n in TWIN_WEIGHTS],
            *[new_m[n] for n in TWIN_WEIGHTS], *[new_v[n] for n in TWIN_WEIGHTS])
```

```python
import functools
import math

import jax
import jax.numpy as jnp
from jax import lax
from jax.experimental import pallas as pl
from jax.experimental.pallas import tpu as pltpu

F32 = jnp.float32
BF16 = jnp.bfloat16
HI = lax.Precision.HIGHEST

N_DEV = 8
EPS = 1e-6
SB_HEADS, SB_DH = 8, 64
DN_HEADS, DN_DH = 4, 128
SB_W = SB_HEADS * SB_DH
DN_W = DN_HEADS * DN_DH
CONV_K = 4
BLK = 128
LANES = 128
IN_MAIN = 4 * SB_W + 4 * DN_W
IN_COLS = IN_MAIN + 2 * DN_HEADS
ADAM_LR, ADAM_B1, ADAM_B2, ADAM_EPS, ADAM_WD, ADAM_STEP = 0.001, 0.9, 0.999, 1e-08, 0.01, 10
VMEM_LIMIT = 56 * 1024 * 1024
SUM_ROWS = 64


def _cp(*sem, vmem=VMEM_LIMIT):
    return pltpu.CompilerParams(dimension_semantics=sem if sem else None, vmem_limit_bytes=vmem)


def _dot(a, b, prec=HI):
    return lax.dot_general(a, b, (((1,), (0,)), ((), ())), precision=prec, preferred_element_type=F32)


def _dot_nt(a, b, prec=HI):
    return lax.dot_general(a, b, (((1,), (1,)), ((), ())), precision=prec, preferred_element_type=F32)


def _bdot(a, b):
    return lax.dot_general(a.astype(BF16), b.astype(BF16), (((1,), (0,)), ((), ())), preferred_element_type=F32)


def _bdot_nt(a, b):
    return lax.dot_general(a.astype(BF16), b.astype(BF16), (((1,), (1,)), ((), ())), preferred_element_type=F32)


def _bdot_tn(a, b):
    return lax.dot_general(a.astype(BF16), b.astype(BF16), (((0,), (0,)), ((), ())), preferred_element_type=F32)


def _split_dot(a, b01):
    hi = a.astype(BF16)
    lo = (a - hi.astype(F32)).astype(BF16)
    return (lax.dot_general(hi, b01, (((1,), (0,)), ((), ())), preferred_element_type=F32)
            + lax.dot_general(lo, b01, (((1,), (0,)), ((), ())), preferred_element_type=F32))


def _sigmoid(x):
    return 1.0 / (1.0 + jnp.exp(-x))


def _silu(x):
    return x * _sigmoid(x)


def _softplus(x):
    return jnp.maximum(x, 0.0) + jnp.log(1.0 + jnp.exp(-jnp.abs(x)))


def _rms(x):
    return x * lax.rsqrt(jnp.mean(x * x, axis=-1, keepdims=True) + EPS)


def _prenorm(x, g, shift, scale):
    return _rms(x) * g * (1.0 + scale) + shift


def _inproj_fwd(x, mod, g, w, name):
    S, D = x.shape
    N = w.shape[1]
    tm = min(512, S)
    tn = 896 if N % 896 == 0 else 128

    def body(x_ref, mod_ref, g_ref, w_ref, p_ref, h_ref):
        @pl.when(pl.program_id(1) == 0)
        def _():
            h = _prenorm(x_ref[...], g_ref[...], mod_ref[:, 0:D], mod_ref[:, D:2 * D])
            h_ref[...] = h.astype(BF16)

        p_ref[...] = jnp.dot(h_ref[...], w_ref[...], preferred_element_type=F32)

    return pl.pallas_call(
        body, name=name, grid=(S // tm, N // tn),
        in_specs=[pl.BlockSpec((tm, D), lambda i, j: (i, 0)), pl.BlockSpec((1, 3 * D), lambda i, j: (0, 0)),
                  pl.BlockSpec((1, D), lambda i, j: (0, 0)), pl.BlockSpec((D, tn), lambda i, j: (0, j))],
        out_specs=[pl.BlockSpec((tm, tn), lambda i, j: (i, j)), pl.BlockSpec((tm, D), lambda i, j: (i, 0))],
        out_shape=[jax.ShapeDtypeStruct((S, N), F32), jax.ShapeDtypeStruct((S, D), BF16)],
        compiler_params=_cp("parallel", "arbitrary"),
    )(x, mod, g, w)


def _inproj_bwd_dx(dp, w, x, mod, g, dxn, name):
    S, N = dp.shape
    D = x.shape[1]
    tm = min(256, S)
    tk = 896 if N % 896 == 0 else 128
    nk = N // tk

    def body(dp_ref, w_ref, x_ref, mod_ref, g_ref, dxn_ref, dx_ref, dmod_ref, dg_ref, acc):
        i, k = pl.program_id(0), pl.program_id(1)

        @pl.when(k == 0)
        def _():
            acc[...] = jnp.zeros_like(acc)

        @pl.when((i == 0) & (k == 0))
        def _():
            dmod_ref[...] = jnp.zeros_like(dmod_ref)
            dg_ref[...] = jnp.zeros_like(dg_ref)

        acc[...] += lax.dot_general(dp_ref[...], w_ref[...], (((1,), (1,)), ((), ())), preferred_element_type=F32)

        @pl.when(k == nk - 1)
        def _():
            _, vjp = jax.vjp(_prenorm, x_ref[...], g_ref[...], mod_ref[:, 0:D], mod_ref[:, D:2 * D])
            dx, dg, dshift, dscale = vjp(acc[...])
            dx_ref[...] = dxn_ref[...] + dx
            dg_ref[...] += dg
            dmod_ref[:, 0:D] += dshift
            dmod_ref[:, D:2 * D] += dscale

    return pl.pallas_call(
        body, name=name, grid=(S // tm, nk),
        in_specs=[pl.BlockSpec((tm, tk), lambda i, k: (i, k)), pl.BlockSpec((D, tk), lambda i, k: (0, k)),
                  pl.BlockSpec((tm, D), lambda i, k: (i, 0)), pl.BlockSpec((1, 3 * D), lambda i, k: (0, 0)),
                  pl.BlockSpec((1, D), lambda i, k: (0, 0)), pl.BlockSpec((tm, D), lambda i, k: (i, 0))],
        out_specs=[pl.BlockSpec((tm, D), lambda i, k: (i, 0)), pl.BlockSpec((1, 3 * D), lambda i, k: (0, 0)),
                   pl.BlockSpec((1, D), lambda i, k: (0, 0))],
        out_shape=[jax.ShapeDtypeStruct((S, D), F32), jax.ShapeDtypeStruct((1, 3 * D), F32),
                   jax.ShapeDtypeStruct((1, D), F32)],
        scratch_shapes=[pltpu.VMEM((tm, D), F32)],
        compiler_params=_cp("arbitrary", "arbitrary"),
    )(dp, w, x, mod, g, dxn)


def _matmul_tn(a_t, b, name):
    M, K = a_t.shape
    N = b.shape[1]
    tn = 896 if N % 896 == 0 else (512 if N % 512 == 0 else 128)
    tk = min(512, K)
    nk = K // tk

    def body(a_ref, b_ref, o_ref):
        @pl.when(pl.program_id(1) == 0)
        def _():
            o_ref[...] = jnp.zeros_like(o_ref)

        o_ref[...] += jnp.dot(a_ref[...], b_ref[...], preferred_element_type=F32)

    return pl.pallas_call(
        body, name=name, grid=(N // tn, nk),
        in_specs=[pl.BlockSpec((M, tk), lambda j, k: (0, k)), pl.BlockSpec((tk, tn), lambda j, k: (k, j))],
        out_specs=pl.BlockSpec((M, tn), lambda j, k: (0, j)),
        out_shape=jax.ShapeDtypeStruct((M, N), F32),
        compiler_params=_cp("parallel", "arbitrary"),
    )(a_t, b)


def _qk_norm(t, g, scale):
    return _rms(t) * g * scale


def _sb_tile(qi, kj, diag, r, c, upper):
    z = _dot_nt(qi, kj)
    sp = _softplus(z)
    valid = c < r + jnp.where(diag, 0, BLK)
    lk = jnp.where(valid, -sp, 0.0)
    later_in = _split_dot(lk, upper)
    return z, sp, valid, lk, later_in


def _sb_fwd(q, k, v, gq, gk, name):
    H, S, dh = q.shape
    nb = S // BLK
    scale = 1.0 / math.sqrt(dh)

    def body(q_ref, k_ref, v_ref, gq_ref, gk_ref, o_ref, qa, ka):
        qa[...] = _qk_norm(q_ref[0], gq_ref[...], scale)
        ka[...] = _qk_norm(k_ref[0], gk_ref[...], 1.0)
        r = lax.broadcasted_iota(jnp.int32, (BLK, BLK), 0)
        c = lax.broadcasted_iota(jnp.int32, (BLK, BLK), 1)
        upper = (r > c).astype(BF16)

        def qblock(i, _):
            qi = qa[pl.ds(pl.multiple_of(i * BLK, BLK), BLK), :]

            def kblock(jj, carry):
                cl, acc = carry
                off = pl.multiple_of((i - jj) * BLK, BLK)
                kj = ka[pl.ds(off, BLK), :]
                vj = v_ref[0, pl.ds(off, BLK), :]
                z, sp, valid, lk, later_in = _sb_tile(qi, kj, jj == 0, r, c, upper)
                w = jnp.where(valid, jnp.exp(z - sp + later_in + cl), 0.0)
                acc = acc + _dot(w, vj)
                cl = cl + jnp.sum(lk, axis=1, keepdims=True)
                return cl, acc

            _, acc = lax.fori_loop(0, i + 1, kblock, (jnp.zeros((BLK, 1), F32), jnp.zeros((BLK, dh), F32)))
            o_ref[0, pl.ds(pl.multiple_of(i * BLK, BLK), BLK), :] = acc
            return 0

        lax.fori_loop(0, nb, qblock, 0)

    hs = pl.BlockSpec((1, S, dh), lambda h: (h, 0, 0))
    gs = pl.BlockSpec((1, dh), lambda h: (0, 0))
    return pl.pallas_call(
        body, name=name, grid=(H,), in_specs=[hs, hs, hs, gs, gs], out_specs=hs,
        out_shape=jax.ShapeDtypeStruct((H, S, dh), F32),
        scratch_shapes=[pltpu.VMEM((S, dh), F32), pltpu.VMEM((S, dh), F32)],
        compiler_params=_cp("parallel"),
    )(q, k, v, gq, gk)


def _sb_bwd(q, k, v, gq, gk, o, do, name):
    H, S, dh = q.shape
    nb = S // BLK
    scale = 1.0 / math.sqrt(dh)

    def body(q_ref, k_ref, v_ref, gq_ref, gk_ref, o_ref, do_ref, dq_ref, dk_ref, dv_ref, dgq_ref, dgk_ref,
             qa, ka, dqa, dka):
        qa[...] = _qk_norm(q_ref[0], gq_ref[...], scale)
        ka[...] = _qk_norm(k_ref[0], gk_ref[...], 1.0)
        dka[...] = jnp.zeros_like(dka)
        dv_ref[...] = jnp.zeros_like(dv_ref)
        r = lax.broadcasted_iota(jnp.int32, (BLK, BLK), 0)
        c = lax.broadcasted_iota(jnp.int32, (BLK, BLK), 1)
        upper = (r > c).astype(BF16)
        lower_incl = (r >= c).astype(BF16)

        def qblock(i, _):
            rows = pl.ds(pl.multiple_of(i * BLK, BLK), BLK)
            qi = qa[rows, :]
            doi = do_ref[0, rows, :]
            total = jnp.sum(doi * o_ref[0, rows, :], axis=1, keepdims=True)

            def kblock(jj, carry):
                cl, cd, dqi = carry
                cols = pl.ds(pl.multiple_of((i - jj) * BLK, BLK), BLK)
                kj = ka[cols, :]
                vj = v_ref[0, cols, :]
                z, sp, valid, lk, later_in = _sb_tile(qi, kj, jj == 0, r, c, upper)
                w = jnp.where(valid, jnp.exp(z - sp + later_in + cl), 0.0)
                dl = _dot_nt(doi, vj) * w
                prefix = total - cd - _split_dot(dl, lower_incl)
                sig = jnp.exp(z - sp)
                dz = jnp.where(valid, dl * (1.0 - sig) - sig * prefix, 0.0)
                dqi = dqi + _dot(dz, kj)
                dka[cols, :] += _dot(dz.T, qi)
                dv_ref[0, cols, :] += _dot(w.T, doi)
                cl = cl + jnp.sum(lk, axis=1, keepdims=True)
                cd = cd + jnp.sum(dl, axis=1, keepdims=True)
                return cl, cd, dqi

            zero = jnp.zeros((BLK, 1), F32)
            _, _, dqi = lax.fori_loop(0, i + 1, kblock, (zero, zero, jnp.zeros((BLK, dh), F32)))
            dqa[rows, :] = dqi
            return 0

        lax.fori_loop(0, nb, qblock, 0)
        _, vq = jax.vjp(lambda t, g: _qk_norm(t, g, scale), q_ref[0], gq_ref[...])
        dq, dgq = vq(dqa[...])
        dq_ref[0] = dq
        dgq_ref[0] = dgq
        _, vk = jax.vjp(lambda t, g: _qk_norm(t, g, 1.0), k_ref[0], gk_ref[...])
        dk, dgk = vk(dka[...])
        dk_ref[0] = dk
        dgk_ref[0] = dgk

    hs = pl.BlockSpec((1, S, dh), lambda h: (h, 0, 0))
    gs = pl.BlockSpec((1, dh), lambda h: (0, 0))
    go = pl.BlockSpec((1, 1, dh), lambda h: (h, 0, 0))
    sd = jax.ShapeDtypeStruct((H, S, dh), F32)
    gd = jax.ShapeDtypeStruct((H, 1, dh), F32)
    return pl.pallas_call(
        body, name=name, grid=(H,), in_specs=[hs, hs, hs, gs, gs, hs, hs], out_specs=[hs, hs, hs, go, go],
        out_shape=[sd, sd, sd, gd, gd],
        scratch_shapes=[pltpu.VMEM((S, dh), F32)] * 4,
        compiler_params=_cp("parallel", vmem=60 * 1024 * 1024),
    )(q, k, v, gq, gk, o, do)


def _shift_down(x, s, rows):
    if s == 0:
        return x
    return jnp.where(rows >= s, pltpu.roll(x, s, 0), 0.0)


def _shift_up(x, s, rows, n):
    if s == 0:
        return x
    return jnp.where(rows < n - s, pltpu.roll(x, n - s, 0), 0.0)


def _conv(x, w_ref, rows):
    y = x * w_ref[CONV_K - 1:CONV_K, :]
    for kk in range(CONV_K - 1):
        y = y + _shift_down(x, CONV_K - 1 - kk, rows) * w_ref[kk:kk + 1, :]
    return y


def _act_norm(y, normed):
    s = _silu(y)
    n = s * lax.rsqrt(jnp.sum(s * s, axis=-1, keepdims=True) + EPS)
    return jnp.where(normed, n, s)


def _dn_prep_fwd(p, conv_w, col0, name):
    S = p.shape[0]
    nblk = 3 * DN_HEADS
    b0 = col0 // DN_DH

    def body(x_ref, w_ref, o_ref):
        rows = lax.broadcasted_iota(jnp.int32, (S, DN_DH), 0)
        y = _conv(x_ref[...], w_ref, rows)
        o_ref[...] = _act_norm(y, pl.program_id(0) < 2 * DN_HEADS)

    return pl.pallas_call(
        body, name=name, grid=(nblk,),
        in_specs=[pl.BlockSpec((S, DN_DH), lambda j: (0, b0 + j)), pl.BlockSpec((CONV_K, DN_DH), lambda j: (0, j))],
        out_specs=pl.BlockSpec((S, DN_DH), lambda j: (0, j)),
        out_shape=jax.ShapeDtypeStruct((S, 3 * DN_W), F32),
        compiler_params=_cp("parallel"),
    )(p, conv_w)


def _dn_prep_bwd(p, conv_w, col0, dout, name):
    S = p.shape[0]
    nblk = 3 * DN_HEADS
    b0 = col0 // DN_DH

    def body(x_ref, w_ref, do_ref, dx_ref, dw_ref):
        rows = lax.broadcasted_iota(jnp.int32, (S, DN_DH), 0)
        x = x_ref[...]
        y = _conv(x, w_ref, rows)
        normed = pl.program_id(0) < 2 * DN_HEADS
        _, vjp = jax.vjp(lambda t: _act_norm(t, normed), y)
        (dy,) = vjp(do_ref[...])
        dx = dy * w_ref[CONV_K - 1:CONV_K, :]
        dw_ref[CONV_K - 1:CONV_K, :] = jnp.sum(dy * x, axis=0, keepdims=True)
        for kk in range(CONV_K - 1):
            s = CONV_K - 1 - kk
            dx = dx + _shift_up(dy, s, rows, S) * w_ref[kk:kk + 1, :]
            dw_ref[kk:kk + 1, :] = jnp.sum(dy * _shift_down(x, s, rows), axis=0, keepdims=True)
        dx_ref[...] = dx.astype(BF16)

    return pl.pallas_call(
        body, name=name, grid=(nblk,),
        in_specs=[pl.BlockSpec((S, DN_DH), lambda j: (0, b0 + j)), pl.BlockSpec((CONV_K, DN_DH), lambda j: (0, j)),
                  pl.BlockSpec((S, DN_DH), lambda j: (0, j))],
        out_specs=[pl.BlockSpec((S, DN_DH), lambda j: (0, j)), pl.BlockSpec((CONV_K, DN_DH), lambda j: (0, j))],
        out_shape=[jax.ShapeDtypeStruct((S, 3 * DN_W), BF16), jax.ShapeDtypeStruct((CONV_K, 3 * DN_W), F32)],
        compiler_params=_cp("parallel"),
    )(p, conv_w, dout)


def _gate_fn(x, pv):
    lane = lax.broadcasted_iota(jnp.int32, x.shape, 1)
    decay = -jnp.exp(pv[0:1, :]) * _softplus(x + pv[1:2, :])
    return jnp.where(lane < DN_HEADS, _sigmoid(x), decay)


def _dn_gate_fwd(p, pv, blk, name):
    S = p.shape[0]

    def body(x_ref, pv_ref, o_ref):
        o_ref[...] = _gate_fn(x_ref[...], pv_ref[...])

    return pl.pallas_call(
        body, name=name, grid=(1,),
        in_specs=[pl.BlockSpec((S, LANES), lambda i: (0, blk)), pl.BlockSpec((2, LANES), lambda i: (0, 0))],
        out_specs=pl.BlockSpec((S, LANES), lambda i: (0, 0)),
        out_shape=jax.ShapeDtypeStruct((S, LANES), F32),
        compiler_params=_cp("arbitrary"),
    )(p, pv)


def _dn_gate_bwd(p, pv, blk, dout, name):
    S = p.shape[0]

    def body(x_ref, pv_ref, do_ref, dx_ref, dpv_ref):
        _, vjp = jax.vjp(_gate_fn, x_ref[...], pv_ref[...])
        dx, dpv = vjp(do_ref[...])
        dx_ref[...] = dx.astype(BF16)
        dpv_ref[...] = dpv

    return pl.pallas_call(
        body, name=name, grid=(1,),
        in_specs=[pl.BlockSpec((S, LANES), lambda i: (0, blk)), pl.BlockSpec((2, LANES), lambda i: (0, 0)),
                  pl.BlockSpec((S, LANES), lambda i: (0, 0))],
        out_specs=[pl.BlockSpec((S, LANES), lambda i: (0, 0)), pl.BlockSpec((2, LANES), lambda i: (0, 0))],
        out_shape=[jax.ShapeDtypeStruct((S, LANES), BF16), jax.ShapeDtypeStruct((2, LANES), F32)],
        compiler_params=_cp("arbitrary"),
    )(p, pv, dout)


@jax.custom_vjp
def _mm(a, b):
    return _dot(a, b)


_mm.defvjp(lambda a, b: (_dot(a, b), (a, b)), lambda res, g: (_dot_nt(g, res[1]), _dot(res[0].T, g)))


@jax.custom_vjp
def _mm_nt(a, b):
    return _dot_nt(a, b)


_mm_nt.defvjp(lambda a, b: (_dot_nt(a, b), (a, b)), lambda res, g: (_dot(g, res[1]), _dot(g.T, res[0])))


def _delta_chunk(state, q, k, v, beta, a_col, a_row, mm=_dot, mm_nt=_dot_nt):
    C = q.shape[0]
    r = lax.broadcasted_iota(jnp.int32, (C, C), 0)
    c = lax.broadcasted_iota(jnp.int32, (C, C), 1)
    tril, strict = r >= c, r > c
    eye = (r == c).astype(F32)
    g_c = mm(tril.astype(F32), jnp.broadcast_to(a_col, (C, C)))
    g_r = mm(jnp.broadcast_to(a_row, (C, C)), (r <= c).astype(F32))
    decay = jnp.where(tril, jnp.exp(jnp.where(tril, g_c - g_r, 0.0)), 0.0)
    eg = jnp.exp(g_c)
    g_last = jnp.sum(jnp.where(r == C - 1, g_c, 0.0), axis=0, keepdims=True)
    qs = q * (float(q.shape[1]) ** -0.5)
    kb = k * beta
    neg_m = jnp.where(strict, -(mm_nt(kb, k) * decay), 0.0)
    inv = eye + neg_m
    pw = neg_m
    for _ in range(int(math.log2(C)) - 1):
        pw = mm(pw, pw)
        inv = inv + mm(inv, pw)
    u = mm(inv, v * beta)
    w = mm(inv, kb * eg)
    intra = jnp.where(tril, mm_nt(qs, k) * decay, 0.0)
    v_new = u - mm(w, state)
    o = mm(qs * eg, state) + mm(intra, v_new)
    nxt = state * jnp.exp(g_last) + mm((k * jnp.exp(g_last - g_c)).T, v_new)
    return o, nxt


def _delta_specs(S):
    nc = S // BLK
    col = lambda off: pl.BlockSpec((S, DN_DH), lambda h: (0, off + h))
    vec = pl.BlockSpec((1, S, 1), lambda h: (h, 0, 0))
    row = pl.BlockSpec((1, nc, BLK), lambda h: (h, 0, 0))
    st = pl.BlockSpec((1, nc, DN_DH, DN_DH), lambda h: (h, 0, 0, 0))
    return nc, col, vec, row, st


def _delta_fwd(qkv, beta, a_col, a_row, name):
    S = qkv.shape[0]
    nc, col, vec, row, st = _delta_specs(S)

    def body(q_ref, k_ref, v_ref, b_ref, ac_ref, ar_ref, o_ref, st_ref):
        def chunk(ci, state):
            rows = pl.ds(pl.multiple_of(ci * BLK, BLK), BLK)
            st_ref[0, ci] = state
            o, nxt = _delta_chunk(state, q_ref[rows, :], k_ref[rows, :], v_ref[rows, :], b_ref[0, rows, :],
                                  ac_ref[0, rows, :], ar_ref[0, pl.ds(ci, 1), :])
            o_ref[rows, :] = o
            return nxt

        lax.fori_loop(0, nc, chunk, jnp.zeros((DN_DH, DN_DH), F32))

    return pl.pallas_call(
        body, name=name, grid=(DN_HEADS,),
        in_specs=[col(0), col(DN_HEADS), col(2 * DN_HEADS), vec, vec, row],
        out_specs=[col(0), st],
        out_shape=[jax.ShapeDtypeStruct((S, DN_W), F32), jax.ShapeDtypeStruct((DN_HEADS, nc, DN_DH, DN_DH), F32)],
        compiler_params=_cp("parallel"),
    )(qkv, qkv, qkv, beta, a_col, a_row)


def _delta_bwd(qkv, beta, a_col, a_row, states, do, name):
    S = qkv.shape[0]
    nc, col, vec, row, st = _delta_specs(S)
    chunk_fn = functools.partial(_delta_chunk, mm=_mm, mm_nt=_mm_nt)

    def body(q_ref, k_ref, v_ref, b_ref, ac_ref, ar_ref, st_ref, do_ref,
             dq_ref, dk_ref, dv_ref, db_ref, dac_ref, dar_ref):
        def chunk(t, dstate):
            ci = nc - 1 - t
            rows = pl.ds(pl.multiple_of(ci * BLK, BLK), BLK)
            _, vjp = jax.vjp(chunk_fn, st_ref[0, ci], q_ref[rows, :], k_ref[rows, :], v_ref[rows, :],
                             b_ref[0, rows, :], ac_ref[0, rows, :], ar_ref[0, pl.ds(ci, 1), :])
            dprev, dq, dk, dv, db, dac, dar = vjp((do_ref[rows, :], dstate))
            dq_ref[rows, :] = dq
            dk_ref[rows, :] = dk
            dv_ref[rows, :] = dv
            db_ref[0, rows, :] = db
            dac_ref[0, rows, :] = dac
            dar_ref[0, pl.ds(ci, 1), :] = dar
            return dprev

        lax.fori_loop(0, nc, chunk, jnp.zeros((DN_DH, DN_DH), F32))

    vs = jax.ShapeDtypeStruct((DN_HEADS, S, 1), F32)
    d3 = jax.ShapeDtypeStruct((S, DN_W), F32)
    return pl.pallas_call(
        body, name=name, grid=(DN_HEADS,),
        in_specs=[col(0), col(DN_HEADS), col(2 * DN_HEADS), vec, vec, row, st, col(0)],
        out_specs=[col(0), col(0), col(0), vec, vec, row],
        out_shape=[d3, d3, d3, vs, vs, jax.ShapeDtypeStruct((DN_HEADS, nc, BLK), F32)],
        compiler_params=_cp("parallel"),
    )(qkv, qkv, qkv, beta, a_col, a_row, states, do)


def _gate_sb(o, z):
    return o * _silu(z)


def _gate_dn(o, z, g):
    return jnp.concatenate(
        [_rms(o[:, h * DN_DH:(h + 1) * DN_DH]) * g * _silu(z[:, h * DN_DH:(h + 1) * DN_DH]) for h in range(DN_HEADS)],
        axis=1)


def _merge_specs(S, D, tm):
    row = lambda w, blk: pl.BlockSpec((tm, w), lambda i: (i, blk))
    full = lambda a, b: pl.BlockSpec((a, b), lambda i: (0, 0))
    return [row(D, 0), full(1, D), row(SB_W, 0), row(SB_W, 3), row(DN_W, 0), row(DN_W, 7),
            row(D, IN_MAIN // D), row(D, IN_MAIN // D + 1), full(1, DN_DH), full(SB_W, D), full(DN_W, D), full(D, D)]


def _merge_fwd(x, gate, o_sb, o_dn, p, ng, wbs, wbd, wo, name):
    S, D = x.shape
    tm = min(512, S)

    def body(x_ref, gate_ref, osb_ref, zsb_ref, odn_ref, zdn_ref, msb_ref, mdn_ref, ng_ref, wbs_ref, wbd_ref, wo_ref,
             out_ref):
        a = _gate_sb(osb_ref[...], zsb_ref[...])
        b = _gate_dn(odn_ref[...], zdn_ref[...], ng_ref[...])
        y = _sigmoid(msb_ref[...]) * _bdot(a, wbs_ref[...]) + _sigmoid(mdn_ref[...]) * _bdot(b, wbd_ref[...])
        out_ref[...] = x_ref[...] + gate_ref[...] * _bdot(y, wo_ref[...])

    return pl.pallas_call(
        body, name=name, grid=(S // tm,), in_specs=_merge_specs(S, D, tm),
        out_specs=pl.BlockSpec((tm, D), lambda i: (i, 0)), out_shape=jax.ShapeDtypeStruct((S, D), F32),
        compiler_params=_cp("parallel"),
    )(x, gate, o_sb, p, o_dn, p, p, p, ng, wbs, wbd, wo)


def _merge_bwd(dxn, gate, o_sb, o_dn, p, ng, wbs, wbd, wo, name):
    S, D = dxn.shape
    tm = min(256, S)

    def body(dxn_ref, gate_ref, osb_ref, zsb_ref, odn_ref, zdn_ref, msb_ref, mdn_ref, ng_ref, wbs_ref, wbd_ref, wo_ref,
             dosb_ref, dzsb_ref, dodn_ref, dzdn_ref, dmsb_ref, dmdn_ref, dwo_ref, dwbs_ref, dwbd_ref, dgate_ref, dng_ref):
        @pl.when(pl.program_id(0) == 0)
        def _():
            for ref in (dwo_ref, dwbs_ref, dwbd_ref, dgate_ref, dng_ref):
                ref[...] = jnp.zeros_like(ref)

        a, vjp_a = jax.vjp(_gate_sb, osb_ref[...], zsb_ref[...])
        b, vjp_b = jax.vjp(_gate_dn, odn_ref[...], zdn_ref[...], ng_ref[...])
        a16, b16 = a.astype(BF16), b.astype(BF16)
        ps = jnp.dot(a16, wbs_ref[...], preferred_element_type=F32)
        pd = jnp.dot(b16, wbd_ref[...], preferred_element_type=F32)
        ss, sd = _sigmoid(msb_ref[...]), _sigmoid(mdn_ref[...])
        y16 = (ss * ps + sd * pd).astype(BF16)
        out = jnp.dot(y16, wo_ref[...], preferred_element_type=F32)
        dxn_v = dxn_ref[...]
        dgate_ref[...] += jnp.sum(dxn_v * out, axis=0, keepdims=True)
        dout16 = (dxn_v * gate_ref[...]).astype(BF16)
        dwo_ref[...] += _bdot_tn(y16, dout16)
        dy = _bdot_nt(dout16, wo_ref[...])
        dmsb_ref[...] = (dy * ps * ss * (1.0 - ss)).astype(BF16)
        dmdn_ref[...] = (dy * pd * sd * (1.0 - sd)).astype(BF16)
        dps16, dpd16 = (dy * ss).astype(BF16), (dy * sd).astype(BF16)
        dwbs_ref[...] += _bdot_tn(a16, dps16)
        dwbd_ref[...] += _bdot_tn(b16, dpd16)
        dosb, dzsb = vjp_a(_bdot_nt(dps16, wbs_ref[...]))
        dodn, dzdn, dng = vjp_b(_bdot_nt(dpd16, wbd_ref[...]))
        dosb_ref[...] = dosb
        dzsb_ref[...] = dzsb.astype(BF16)
        dodn_ref[...] = dodn
        dzdn_ref[...] = dzdn.astype(BF16)
        dng_ref[...] += dng

    row = lambda w: pl.BlockSpec((tm, w), lambda i: (i, 0))
    full = lambda a, b: pl.BlockSpec((a, b), lambda i: (0, 0))
    sds = jax.ShapeDtypeStruct
    return pl.pallas_call(
        body, name=name, grid=(S // tm,), in_specs=_merge_specs(S, D, tm),
        out_specs=[row(SB_W), row(SB_W), row(DN_W), row(DN_W), row(D), row(D),
                   full(D, D), full(SB_W, D), full(DN_W, D), full(1, D), full(1, DN_DH)],
        out_shape=[sds((S, SB_W), F32), sds((S, SB_W), BF16), sds((S, DN_W), F32), sds((S, DN_W), BF16),
                   sds((S, D), BF16), sds((S, D), BF16), sds((D, D), F32), sds((SB_W, D), F32), sds((DN_W, D), F32),
                   sds((1, D), F32), sds((1, DN_DH), F32)],
        compiler_params=_cp("arbitrary"),
    )(dxn, gate, o_sb, p, o_dn, p, p, p, ng, wbs, wbd, wo)


def _loss_fwd_bwd(y, target, name):
    S, D = y.shape
    tm = min(512, S)

    def body(y_ref, t_ref, l_ref, dy_ref):
        @pl.when(pl.program_id(0) == 0)
        def _():
            l_ref[...] = jnp.zeros_like(l_ref)

        e = y_ref[...] - t_ref[...]
        l_ref[...] += jnp.sum(e * e, axis=0, keepdims=True) * (0.5 / D)
        dy_ref[...] = e * (1.0 / D)

    row = pl.BlockSpec((tm, D), lambda i: (i, 0))
    return pl.pallas_call(
        body, name=name, grid=(S // tm,), in_specs=[row, row],
        out_specs=[pl.BlockSpec((1, D), lambda i: (0, 0)), row],
        out_shape=[jax.ShapeDtypeStruct((1, D), F32), jax.ShapeDtypeStruct((S, D), F32)],
        compiler_params=_cp("arbitrary"),
    )(y, target)


def _mod_fwd(c_all, ada_w, name):
    L, D, n = ada_w.shape

    def body(c_ref, w_ref, o_ref):
        o_ref[0] = _dot(_silu(c_ref[...]), w_ref[0])

    return pl.pallas_call(
        body, name=name, grid=(L,),
        in_specs=[pl.BlockSpec(c_all.shape, lambda l: (0, 0)), pl.BlockSpec((1, D, n), lambda l: (l, 0, 0))],
        out_specs=pl.BlockSpec((1, N_DEV, n), lambda l: (l, 0, 0)),
        out_shape=jax.ShapeDtypeStruct((L, N_DEV, n), F32),
        compiler_params=_cp("parallel"),
    )(c_all, ada_w)


def _mod_bwd_w(c_all_t, dmod, name):
    L, _, n = dmod.shape
    D = c_all_t.shape[0]

    def body(c_ref, d_ref, o_ref):
        o_ref[0] = _dot(_silu(c_ref[...]), d_ref[0])

    return pl.pallas_call(
        body, name=name, grid=(L,),
        in_specs=[pl.BlockSpec(c_all_t.shape, lambda l: (0, 0)), pl.BlockSpec((1, N_DEV, n), lambda l: (l, 0, 0))],
        out_specs=pl.BlockSpec((1, D, n), lambda l: (l, 0, 0)),
        out_shape=jax.ShapeDtypeStruct((L, D, n), F32),
        compiler_params=_cp("parallel"),
    )(c_all_t, dmod)


def _me():
    return lax.axis_index("x"), lax.axis_index("y"), lax.axis_index("c")


def _peer(k):
    x, y, c = _me()
    return (1 - x if k & 4 else x, 1 - y if k & 2 else y, 1 - c if k & 1 else c)


def _lin(dev):
    return 4 * dev[0] + 2 * dev[1] + dev[2]


def _exchange(arrays, scatter, name):
    n_arr = len(arrays)

    def body(*refs):
        ins, outs = refs[:n_arr], refs[n_arr:2 * n_arr]
        send_sems, recv_sems, local_sems = refs[2 * n_arr:]
        me = _lin(_me())
        local, remote = [], []
        for t in range(n_arr):
            src_of = (lambda d, t=t: ins[t].at[d]) if scatter else (lambda d, t=t: ins[t])
            mine = pltpu.make_async_copy(src_of(me), outs[t].at[me], local_sems.at[t])
            mine.start()
            local.append(mine)
            for k in range(1, N_DEV):
                peer = _peer(k)
                cp = pltpu.make_async_remote_copy(
                    src_ref=src_of(_lin(peer)), dst_ref=outs[t].at[me], send_sem=send_sems.at[t, k - 1],
                    recv_sem=recv_sems.at[t, k - 1], device_id=peer, device_id_type=pl.DeviceIdType.MESH)
                cp.start()
                remote.append(cp)
        for t in range(n_arr):
            for k in range(1, N_DEV):
                peer = _peer(k)
                landed = outs[t].at[_lin(peer)]
                pltpu.make_async_remote_copy(
                    src_ref=landed, dst_ref=landed, send_sem=send_sems.at[t, k - 1],
                    recv_sem=recv_sems.at[t, k - 1], device_id=peer, device_id_type=pl.DeviceIdType.MESH).wait_recv()
        for cp in remote:
            cp.wait_send()
        for cp in local:
            cp.wait()

    out_shape = [jax.ShapeDtypeStruct((N_DEV,) + tuple(a.shape[1:] if scatter else a.shape), a.dtype) for a in arrays]
    hbm = pl.BlockSpec(memory_space=pl.ANY)
    return pl.pallas_call(
        body, name=name, in_specs=[hbm] * n_arr, out_specs=[hbm] * n_arr, out_shape=out_shape,
        scratch_shapes=[pltpu.SemaphoreType.DMA((n_arr, N_DEV - 1)), pltpu.SemaphoreType.DMA((n_arr, N_DEV - 1)),
                        pltpu.SemaphoreType.DMA((n_arr,))],
    )(*arrays)


def _sum_slots(a, name):
    _, R, C = a.shape
    tr = SUM_ROWS if R % SUM_ROWS == 0 else R

    def body(a_ref, o_ref):
        acc = a_ref[0]
        for s in range(1, N_DEV):
            acc = acc + a_ref[s]
        o_ref[...] = acc

    return pl.pallas_call(
        body, name=name, grid=(R // tr,),
        in_specs=[pl.BlockSpec((N_DEV, tr, C), lambda i: (0, i, 0))], out_specs=pl.BlockSpec((tr, C), lambda i: (i, 0)),
        out_shape=jax.ShapeDtypeStruct((R, C), F32), compiler_params=_cp("parallel"),
    )(a)


def _adamw(w, g, m, v, name):
    shape = w.shape
    C = shape[-1]
    R = w.size // C
    tr = R
    for cand in (256, 128, 64):
        if R > cand and R % cand == 0:
            tr = cand
            break
    c1 = 1.0 / (1.0 - ADAM_B1 ** ADAM_STEP)
    c2 = 1.0 / (1.0 - ADAM_B2 ** ADAM_STEP)

    def body(w_ref, g_ref, m_ref, v_ref, d_ref, nm_ref, nv_ref):
        gv = g_ref[...]
        nm = ADAM_B1 * m_ref[...] + (1.0 - ADAM_B1) * gv
        nv = ADAM_B2 * v_ref[...] + (1.0 - ADAM_B2) * (gv * gv)
        d_ref[...] = -ADAM_LR * ((nm * c1) / (jnp.sqrt(nv * c2) + ADAM_EPS) + ADAM_WD * w_ref[...])
        nm_ref[...] = nm
        nv_ref[...] = nv

    spec = pl.BlockSpec((tr, C), lambda i: (i, 0))
    sd = jax.ShapeDtypeStruct((R, C), F32)
    outs = pl.pallas_call(
        body, name=name, grid=(R // tr,), in_specs=[spec] * 4, out_specs=[spec] * 3, out_shape=[sd] * 3,
        compiler_params=_cp("parallel"),
    )(*(t.reshape(R, C) for t in (w, g, m, v)))
    return tuple(t.reshape(shape) for t in outs)


def _to_heads(t):
    S = t.shape[0]
    return t.reshape(S, SB_HEADS, SB_DH).transpose(1, 0, 2)


def _from_heads(t):
    S = t.shape[1]
    return t.transpose(1, 0, 2).reshape(S, SB_W)


def _pad_cols(w_in):
    D = w_in.shape[0]
    pad = jnp.zeros((D, LANES - 2 * DN_HEADS), w_in.dtype)
    return jnp.concatenate([w_in[:, :IN_MAIN], w_in[:, IN_COLS:], w_in[:, IN_MAIN:IN_COLS], pad], axis=1)


def _unpad_cols(dw, D):
    return jnp.concatenate([dw[:, :IN_MAIN], dw[:, IN_MAIN + 2 * D:IN_MAIN + 2 * D + 2 * DN_HEADS],
                            dw[:, IN_MAIN:IN_MAIN + 2 * D]], axis=1)


def _gate_params(a_log, dt_bias):
    z = jnp.zeros((LANES,), F32)
    return jnp.stack([z.at[DN_HEADS:2 * DN_HEADS].set(a_log), z.at[DN_HEADS:2 * DN_HEADS].set(dt_bias)])


def _layer_fwd(l, x, mod, wts):
    S, D = x.shape
    tag = f"l{l}_"
    p, h = _inproj_fwd(x, mod, wts["norm_g"], wts["w_in"], tag + "inproj_fwd")
    q, k, v = (_to_heads(p[:, i * SB_W:(i + 1) * SB_W]) for i in range(3))
    o_sb_h = _sb_fwd(q, k, v, wts["sb_q_g"], wts["sb_k_g"], tag + "sb_fwd")
    qkv = _dn_prep_fwd(p, wts["conv_w"], 4 * SB_W, tag + "dn_prep_fwd")
    pv = _gate_params(wts["dn_a_log"], wts["dn_dt_bias"])
    ba_blk = (IN_MAIN + 2 * D) // LANES
    bg = _dn_gate_fwd(p, pv, ba_blk, tag + "dn_gate_fwd")
    bgt = bg[:, :2 * DN_HEADS].T
    beta = bgt[:DN_HEADS].reshape(DN_HEADS, S, 1)
    a_col = bgt[DN_HEADS:].reshape(DN_HEADS, S, 1)
    a_row = bgt[DN_HEADS:].reshape(DN_HEADS, S // BLK, BLK)
    o_dn, states = _delta_fwd(qkv, beta, a_col, a_row, tag + "delta_fwd")
    o_sb = _from_heads(o_sb_h)
    gate = mod[:, 2 * D:]
    out = _merge_fwd(x, gate, o_sb, o_dn, p, wts["dn_norm_g"], wts["w_branch_sb"], wts["w_branch_dn"], wts["w_out"],
                     tag + "merge_fwd")
    saved = dict(x=x, mod=mod, p=p, h=h, q=q, k=k, v=v, o_sb_h=o_sb_h, o_sb=o_sb, qkv=qkv, pv=pv, beta=beta,
                 a_col=a_col, a_row=a_row, o_dn=o_dn, states=states, gate=gate)
    return out, saved


def _layer_bwd(l, dxn, sv, wts):
    S, D = dxn.shape
    tag = f"l{l}_"
    (dosb, dzsb, dodn, dzdn, dmsb, dmdn, dwo, dwbs, dwbd, dgate, dng) = _merge_bwd(
        dxn, sv["gate"], sv["o_sb"], sv["o_dn"], sv["p"], wts["dn_norm_g"], wts["w_branch_sb"], wts["w_branch_dn"],
        wts["w_out"], tag + "merge_bwd")
    dq, dk, dv, dgq, dgk = _sb_bwd(sv["q"], sv["k"], sv["v"], wts["sb_q_g"], wts["sb_k_g"], sv["o_sb_h"],
                                   _to_heads(dosb), tag + "sb_bwd")
    dqkv_n_q, dqkv_n_k, dqkv_n_v, dbeta, dac, dar = _delta_bwd(sv["qkv"], sv["beta"], sv["a_col"], sv["a_row"],
                                                                sv["states"], dodn, tag + "delta_bwd")
    dqkv, dconv = _dn_prep_bwd(sv["p"], wts["conv_w"], 4 * SB_W,
                               jnp.concatenate([dqkv_n_q, dqkv_n_k, dqkv_n_v], axis=1), tag + "dn_prep_bwd")
    da = dac.reshape(DN_HEADS, S) + dar.reshape(DN_HEADS, S)
    dbg = jnp.concatenate([dbeta.reshape(DN_HEADS, S), da], axis=0).T
    dbg = jnp.concatenate([dbg, jnp.zeros((S, LANES - 2 * DN_HEADS), F32)], axis=1)
    ba_blk = (IN_MAIN + 2 * D) // LANES
    dba, dpv = _dn_gate_bwd(sv["p"], sv["pv"], ba_blk, dbg, tag + "dn_gate_bwd")
    dp = jnp.concatenate([_from_heads(dq).astype(BF16), _from_heads(dk).astype(BF16), _from_heads(dv).astype(BF16),
                          dzsb, dqkv, dzdn, dmsb, dmdn, dba], axis=1)
    dx, dmod, dg = _inproj_bwd_dx(dp, wts["w_in"], sv["x"], sv["mod"], wts["norm_g"], dxn, tag + "inproj_bwd_dx")
    dw_in = _matmul_tn(sv["h"].T, dp, tag + "inproj_bwd_dw")
    dmod = dmod.at[:, 2 * D:].set(dgate)
    grads = dict(w_in=_unpad_cols(dw_in, D), w_branch_sb=dwbs, w_branch_dn=dwbd, w_out=dwo, conv_w=dconv,
                 mod=dmod[0], norm_g=dg[0], sb_q_g=jnp.sum(dgq, axis=0)[0], sb_k_g=jnp.sum(dgk, axis=0)[0],
                 dn_a_log=dpv[0, DN_HEADS:2 * DN_HEADS], dn_dt_bias=dpv[1, DN_HEADS:2 * DN_HEADS], dn_norm_g=dng[0])
    return dx, grads


def _pad_rows(a, mult):
    extra = (-a.shape[0]) % mult
    return a if extra == 0 else jnp.concatenate([a, jnp.zeros((extra,) + a.shape[1:], a.dtype)], axis=0)


def _pack_rows(parts, width, mult):
    flat = jnp.concatenate([t.reshape(-1) for t in parts])
    extra = (-flat.shape[0]) % width
    if extra:
        flat = jnp.concatenate([flat, jnp.zeros((extra,), flat.dtype)])
    return _pad_rows(flat.reshape(-1, width), mult)


def _take(flat, off, shape):
    n = math.prod(shape)
    return flat[..., off:off + n].reshape(flat.shape[:-1] + tuple(shape)), off + n


SMALL = ("mod", "norm_g", "sb_q_g", "sb_k_g", "dn_a_log", "dn_dt_bias", "dn_norm_g")


def kernel(x, c, ada_w, ada_b, norm_g, w_in, sb_q_g, sb_k_g, conv_w, dn_a_log, dn_dt_bias, dn_norm_g, w_branch_sb, w_branch_dn, w_out, loss_target, m_ada_w, m_ada_b, m_norm_g, m_w_in, m_sb_q_g, m_sb_k_g, m_conv_w, m_dn_a_log, m_dn_dt_bias, m_dn_norm_g, m_w_branch_sb, m_w_branch_dn, m_w_out, v_ada_w, v_ada_b, v_norm_g, v_w_in, v_sb_q_g, v_sb_k_g, v_conv_w, v_dn_a_log, v_dn_dt_bias, v_dn_norm_g, v_w_branch_sb, v_w_branch_dn, v_w_out):
    L, D = norm_g.shape
    S = x.shape[1]
    n_in = w_in.shape[2]
    n_ada = ada_w.shape[2]
    n_br = w_branch_sb.shape[2]
    n_out = w_out.shape[1]
    n_conv = conv_w.shape[2]
    me = _lin(_me())

    small = _pack_rows([c, conv_w], LANES, 8)
    (small_all,) = _exchange([small], False, "gather_small")
    small_flat = small_all.reshape(N_DEV, -1)
    c_all, off = _take(small_flat, 0, (D,))
    conv_all, _ = _take(small_flat, off, (L, CONV_K, n_conv))
    conv_full = conv_all.transpose(1, 2, 0, 3).reshape(L, CONV_K, N_DEV * n_conv)

    mod_part = _mod_fwd(c_all, ada_w, "mod_fwd")
    wpack = _pack_rows([w_in, w_branch_sb, w_branch_dn, w_out], D, 16).astype(BF16)
    w_all, mod_all = _exchange([wpack, _pack_rows([mod_part], LANES, 8)], False, "gather_weights")
    w_flat = w_all.reshape(N_DEV, -1)
    t, off = _take(w_flat, 0, (L, D, n_in))
    w_in_full = t.transpose(1, 2, 0, 3).reshape(L, D, N_DEV * n_in)
    t, off = _take(w_flat, off, (L, SB_W, n_br))
    wbs_full = t.transpose(1, 2, 0, 3).reshape(L, SB_W, N_DEV * n_br)
    t, off = _take(w_flat, off, (L, DN_W, n_br))
    wbd_full = t.transpose(1, 2, 0, 3).reshape(L, DN_W, N_DEV * n_br)
    t, off = _take(w_flat, off, (L, n_out, D))
    wo_full = t.transpose(1, 0, 2, 3).reshape(L, N_DEV * n_out, D)
    t, _ = _take(mod_all.reshape(N_DEV, -1), 0, (L, N_DEV, n_ada))
    mod_full = t.transpose(1, 2, 0, 3).reshape(L, N_DEV, N_DEV * n_ada) + ada_b[:, None, :]
    mod_mine = lax.dynamic_slice_in_dim(mod_full, me, 1, axis=1)

    wts = [dict(norm_g=norm_g[l:l + 1], w_in=_pad_cols(w_in_full[l]), sb_q_g=sb_q_g[l:l + 1], sb_k_g=sb_k_g[l:l + 1],
                conv_w=conv_full[l], dn_a_log=dn_a_log[l], dn_dt_bias=dn_dt_bias[l], dn_norm_g=dn_norm_g[l:l + 1],
                w_branch_sb=wbs_full[l], w_branch_dn=wbd_full[l], w_out=wo_full[l]) for l in range(L)]

    act = x[0]
    saved = []
    for l in range(L):
        act, sv = _layer_fwd(l, act, mod_mine[l], wts[l])
        saved.append(sv)
    loss_cols, dact = _loss_fwd_bwd(act, loss_target[0], "loss")
    loss = lax.psum(jnp.sum(loss_cols), ("x", "y", "c"))
    grads = [None] * L
    for l in reversed(range(L)):
        dact, grads[l] = _layer_bwd(l, dact, saved[l], wts[l])
    grad_x = dact[None]

    def slices(l):
        g = grads[l]
        return [g["w_in"].reshape(D, N_DEV, n_in).transpose(1, 0, 2).reshape(N_DEV, -1),
                g["w_branch_sb"].reshape(SB_W, N_DEV, n_br).transpose(1, 0, 2).reshape(N_DEV, -1),
                g["w_branch_dn"].reshape(DN_W, N_DEV, n_br).transpose(1, 0, 2).reshape(N_DEV, -1),
                g["w_out"].reshape(N_DEV, -1),
                g["conv_w"].reshape(CONV_K, N_DEV, n_conv).transpose(1, 0, 2).reshape(N_DEV, -1)]

    send = jnp.concatenate([t for l in range(L) for t in slices(l)], axis=1)
    n_big = send.shape[1]
    extra = (-n_big) % (SUM_ROWS * D)
    send = jnp.concatenate([send, jnp.zeros((N_DEV, extra), F32)], axis=1).reshape(N_DEV, -1, D)
    small_g = _pack_rows([grads[l][n] for l in range(L) for n in SMALL], LANES, 8)
    (recv,) = _exchange([send], True, "scatter_grads")
    (small_all_g,) = _exchange([small_g], False, "gather_small_grads")
    red = _sum_slots(recv, "sum_grads").reshape(-1)
    small_sum = _sum_slots(small_all_g, "sum_small_grads").reshape(-1)

    g_out = {}
    off = 0
    per_layer = {k: [] for k in ("w_in", "w_branch_sb", "w_branch_dn", "w_out", "conv_w")}
    for l in range(L):
        for name, shape in (("w_in", (D, n_in)), ("w_branch_sb", (SB_W, n_br)), ("w_branch_dn", (DN_W, n_br)),
                            ("w_out", (n_out, D)), ("conv_w", (CONV_K, n_conv))):
            t, off = _take(red, off, shape)
            per_layer[name].append(t)
    for name, ts in per_layer.items():
        g_out[name] = jnp.stack(ts)
    small_shapes = dict(mod=(3 * D,), norm_g=(D,), sb_q_g=(SB_DH,), sb_k_g=(SB_DH,), dn_a_log=(DN_HEADS,),
                        dn_dt_bias=(DN_HEADS,), dn_norm_g=(DN_DH,))
    off = 0
    off_all = 0
    small_each = small_all_g.reshape(N_DEV, -1)
    per_small = {n: [] for n in SMALL}
    dmod_all = []
    for l in range(L):
        for n in SMALL:
            t, off = _take(small_sum, off, small_shapes[n])
            per_small[n].append(t)
            if n == "mod":
                t_all, _ = _take(small_each, off_all, small_shapes[n])
                dmod_all.append(t_all)
            off_all += math.prod(small_shapes[n])
    for n in SMALL:
        g_out[n if n != "mod" else "ada_b"] = jnp.stack(per_small[n])
    dmod_all = jnp.stack(dmod_all)
    dmod_cols = lax.dynamic_slice_in_dim(dmod_all, me * n_ada, n_ada, axis=2)
    g_out["ada_w"] = _mod_bwd_w(c_all.T, dmod_cols, "mod_bwd_w")

    given = dict(ada_w=(ada_w, m_ada_w, v_ada_w), ada_b=(ada_b, m_ada_b, v_ada_b), norm_g=(norm_g, m_norm_g, v_norm_g),
                 w_in=(w_in, m_w_in, v_w_in), sb_q_g=(sb_q_g, m_sb_q_g, v_sb_q_g), sb_k_g=(sb_k_g, m_sb_k_g, v_sb_k_g),
                 conv_w=(conv_w, m_conv_w, v_conv_w), dn_a_log=(dn_a_log, m_dn_a_log, v_dn_a_log),
                 dn_dt_bias=(dn_dt_bias, m_dn_dt_bias, v_dn_dt_bias), dn_norm_g=(dn_norm_g, m_dn_norm_g, v_dn_norm_g),
                 w_branch_sb=(w_branch_sb, m_w_branch_sb, v_w_branch_sb),
                 w_branch_dn=(w_branch_dn, m_w_branch_dn, v_w_branch_dn), w_out=(w_out, m_w_out, v_w_out))
    order = list(given)
    upd = {n: _adamw(given[n][0], g_out[n], given[n][1], given[n][2], "adamw_" + n) for n in order}
    return (loss, grad_x, *[g_out[n] for n in order], *[upd[n][0] for n in order], *[upd[n][1] for n in order],
            *[upd[n][2] for n in order])
```

```python
import functools
import math

import jax
import jax.numpy as jnp
from jax import lax
from jax.experimental import pallas as pl
from jax.experimental.pallas import tpu as pltpu

F32 = jnp.float32
BF16 = jnp.bfloat16
HI = lax.Precision.HIGHEST

N_DEV = 8
EPS = 1e-6
SB_HEADS, SB_DH = 8, 64
DN_HEADS, DN_DH = 4, 128
SB_W = SB_HEADS * SB_DH
DN_W = DN_HEADS * DN_DH
CONV_K = 4
BLK = 128
SB_KEYS = 512
LANES = 128
IN_MAIN = 4 * SB_W + 4 * DN_W
IN_COLS = IN_MAIN + 2 * DN_HEADS
ADAM_LR, ADAM_B1, ADAM_B2, ADAM_EPS, ADAM_WD, ADAM_STEP = 0.001, 0.9, 0.999, 1e-08, 0.01, 10
VMEM_LIMIT = 56 * 1024 * 1024
SUM_ROWS = 128


def _cp(*sem, vmem=VMEM_LIMIT):
    return pltpu.CompilerParams(dimension_semantics=sem if sem else None, vmem_limit_bytes=vmem)


def _dot(a, b, prec=HI):
    return lax.dot_general(a, b, (((1,), (0,)), ((), ())), precision=prec, preferred_element_type=F32)


def _dot_nt(a, b, prec=HI):
    return lax.dot_general(a, b, (((1,), (1,)), ((), ())), precision=prec, preferred_element_type=F32)


def _bdot(a, b):
    return lax.dot_general(a.astype(BF16), b.astype(BF16), (((1,), (0,)), ((), ())), preferred_element_type=F32)


def _bdot_nt(a, b):
    return lax.dot_general(a.astype(BF16), b.astype(BF16), (((1,), (1,)), ((), ())), preferred_element_type=F32)


def _bdot_tn(a, b):
    return lax.dot_general(a.astype(BF16), b.astype(BF16), (((0,), (0,)), ((), ())), preferred_element_type=F32)


def _split_dot(a, b01):
    hi = a.astype(BF16)
    lo = (a - hi.astype(F32)).astype(BF16)
    return (lax.dot_general(hi, b01, (((1,), (0,)), ((), ())), preferred_element_type=F32)
            + lax.dot_general(lo, b01, (((1,), (0,)), ((), ())), preferred_element_type=F32))


def _sigmoid(x):
    return 1.0 / (1.0 + jnp.exp(-x))


def _silu(x):
    return x * _sigmoid(x)


def _softplus(x):
    return jnp.maximum(x, 0.0) + jnp.log(1.0 + jnp.exp(-jnp.abs(x)))


def _rms(x):
    return x * lax.rsqrt(jnp.mean(x * x, axis=-1, keepdims=True) + EPS)


def _prenorm(x, g, shift, scale):
    return _rms(x) * g * (1.0 + scale) + shift


def _inproj_fwd(x, mod, g, w, name):
    S, D = x.shape
    N = w.shape[1]
    tm = min(512, S)
    tn = 896 if N % 896 == 0 else 128

    def body(x_ref, mod_ref, g_ref, w_ref, p_ref, h_ref):
        @pl.when(pl.program_id(1) == 0)
        def _():
            h = _prenorm(x_ref[...], g_ref[...], mod_ref[:, 0:D], mod_ref[:, D:2 * D])
            h_ref[...] = h.astype(BF16)

        p_ref[...] = jnp.dot(h_ref[...], w_ref[...], preferred_element_type=F32)

    return pl.pallas_call(
        body, name=name, grid=(S // tm, N // tn),
        in_specs=[pl.BlockSpec((tm, D), lambda i, j: (i, 0)), pl.BlockSpec((1, 3 * D), lambda i, j: (0, 0)),
                  pl.BlockSpec((1, D), lambda i, j: (0, 0)), pl.BlockSpec((D, tn), lambda i, j: (0, j))],
        out_specs=[pl.BlockSpec((tm, tn), lambda i, j: (i, j)), pl.BlockSpec((tm, D), lambda i, j: (i, 0))],
        out_shape=[jax.ShapeDtypeStruct((S, N), F32), jax.ShapeDtypeStruct((S, D), BF16)],
        compiler_params=_cp("parallel", "arbitrary"),
    )(x, mod, g, w)


def _inproj_bwd_dx(dp, w, x, mod, g, dxn, name):
    S, N = dp.shape
    D = x.shape[1]
    tm = min(256, S)
    tk = 896 if N % 896 == 0 else 128
    nk = N // tk

    def body(dp_ref, w_ref, x_ref, mod_ref, g_ref, dxn_ref, dx_ref, dmod_ref, dg_ref, acc):
        i, k = pl.program_id(0), pl.program_id(1)

        @pl.when(k == 0)
        def _():
            acc[...] = jnp.zeros_like(acc)

        @pl.when((i == 0) & (k == 0))
        def _():
            dmod_ref[...] = jnp.zeros_like(dmod_ref)
            dg_ref[...] = jnp.zeros_like(dg_ref)

        acc[...] += lax.dot_general(dp_ref[...], w_ref[...], (((1,), (1,)), ((), ())), preferred_element_type=F32)

        @pl.when(k == nk - 1)
        def _():
            _, vjp = jax.vjp(_prenorm, x_ref[...], g_ref[...], mod_ref[:, 0:D], mod_ref[:, D:2 * D])
            dx, dg, dshift, dscale = vjp(acc[...])
            dx_ref[...] = dxn_ref[...] + dx
            dg_ref[...] += dg
            dmod_ref[:, 0:D] += dshift
            dmod_ref[:, D:2 * D] += dscale

    return pl.pallas_call(
        body, name=name, grid=(S // tm, nk),
        in_specs=[pl.BlockSpec((tm, tk), lambda i, k: (i, k)), pl.BlockSpec((D, tk), lambda i, k: (0, k)),
                  pl.BlockSpec((tm, D), lambda i, k: (i, 0)), pl.BlockSpec((1, 3 * D), lambda i, k: (0, 0)),
                  pl.BlockSpec((1, D), lambda i, k: (0, 0)), pl.BlockSpec((tm, D), lambda i, k: (i, 0))],
        out_specs=[pl.BlockSpec((tm, D), lambda i, k: (i, 0)), pl.BlockSpec((1, 3 * D), lambda i, k: (0, 0)),
                   pl.BlockSpec((1, D), lambda i, k: (0, 0))],
        out_shape=[jax.ShapeDtypeStruct((S, D), F32), jax.ShapeDtypeStruct((1, 3 * D), F32),
                   jax.ShapeDtypeStruct((1, D), F32)],
        scratch_shapes=[pltpu.VMEM((tm, D), F32)],
        compiler_params=_cp("arbitrary", "arbitrary"),
    )(dp, w, x, mod, g, dxn)


def _matmul_tn(a_t, b, name):
    M, K = a_t.shape
    N = b.shape[1]
    tn = 896 if N % 896 == 0 else (512 if N % 512 == 0 else 128)
    tk = min(512, K)
    nk = K // tk

    def body(a_ref, b_ref, o_ref):
        @pl.when(pl.program_id(1) == 0)
        def _():
            o_ref[...] = jnp.zeros_like(o_ref)

        o_ref[...] += jnp.dot(a_ref[...], b_ref[...], preferred_element_type=F32)

    return pl.pallas_call(
        body, name=name, grid=(N // tn, nk),
        in_specs=[pl.BlockSpec((M, tk), lambda j, k: (0, k)), pl.BlockSpec((tk, tn), lambda j, k: (k, j))],
        out_specs=pl.BlockSpec((M, tn), lambda j, k: (0, j)),
        out_shape=jax.ShapeDtypeStruct((M, N), F32),
        compiler_params=_cp("parallel", "arbitrary"),
    )(a_t, b)


def _qk_norm(t, g, scale):
    return _rms(t) * g * scale


def _qk_norm_t(t, g_col, scale):
    return t * lax.rsqrt(jnp.mean(t * t, axis=0, keepdims=True) + EPS) * g_col * scale


def _suffix_sums(x, tri):
    half = tri.shape[0]
    lo, hi = x[:, :half], x[:, half:]
    hi_sum = jnp.sum(hi, axis=1, keepdims=True)
    y = jnp.concatenate([_split_dot(lo, tri) + hi_sum, _split_dot(hi, tri)], axis=1)
    return y, hi_sum + jnp.sum(lo, axis=1, keepdims=True)


def _sb_step(qi, kat_blk, cl, upper, valid):
    z = jnp.dot(qi, kat_blk, preferred_element_type=F32)
    lk = jnp.minimum(-z, 0.0) - jnp.log(1.0 + jnp.exp(-jnp.abs(z)))
    if valid is not None:
        lk = jnp.where(valid, lk, 0.0)
    later, tot = _suffix_sums(lk, upper)
    w = jnp.exp(z + lk + later + cl)
    if valid is not None:
        w = jnp.where(valid, w, 0.0)
    return z, lk, w, tot


def _sb_masks(kb):
    half = kb // 2
    r = lax.broadcasted_iota(jnp.int32, (half, half), 0)
    c = lax.broadcasted_iota(jnp.int32, (half, half), 1)
    rq = lax.broadcasted_iota(jnp.int32, (BLK, kb), 0)
    ck = lax.broadcasted_iota(jnp.int32, (BLK, kb), 1)
    return (r > c).astype(BF16), (r >= c).astype(BF16), ck - rq


def _sb_fwd(q, kt, v, gq, gkt, name):
    H, S, dh = q.shape
    kb = min(SB_KEYS, S)
    per = kb // BLK
    nb = S // BLK
    scale = 1.0 / math.sqrt(dh)

    def body(q_ref, kt_ref, v_ref, gq_ref, gkt_ref, o_ref, qa, kat, vb):
        qa[...] = _qk_norm(q_ref[0], gq_ref[...], scale).astype(BF16)
        kat[...] = _qk_norm_t(kt_ref[0], gkt_ref[...], 1.0).astype(BF16)
        vb[...] = v_ref[0].astype(BF16)
        upper, _, diff = _sb_masks(kb)

        def qblock(i, _):
            rows = pl.ds(pl.multiple_of(i * BLK, BLK), BLK)
            qi = qa[rows, :]
            sbd = i // per

            def step(sb, cl, acc, valid):
                cols = pl.ds(pl.multiple_of(sb * kb, kb), kb)
                _, _, w, tot = _sb_step(qi, kat[:, cols], cl, upper, valid)
                return cl + tot, acc + jnp.dot(w.astype(BF16), vb[cols, :], preferred_element_type=F32)

            cl, acc = step(sbd, jnp.zeros((BLK, 1), F32), jnp.zeros((BLK, dh), F32), diff < (i - sbd * per) * BLK)
            _, acc = lax.fori_loop(0, sbd, lambda jj, c: step(sbd - 1 - jj, c[0], c[1], None), (cl, acc))
            o_ref[0, rows, :] = acc
            return 0

        lax.fori_loop(0, nb, qblock, 0)

    hs = pl.BlockSpec((1, S, dh), lambda h: (h, 0, 0))
    ts = pl.BlockSpec((1, dh, S), lambda h: (h, 0, 0))
    return pl.pallas_call(
        body, name=name, grid=(H,),
        in_specs=[hs, ts, hs, pl.BlockSpec((1, dh), lambda h: (0, 0)), pl.BlockSpec((dh, 1), lambda h: (0, 0))],
        out_specs=hs, out_shape=jax.ShapeDtypeStruct((H, S, dh), F32),
        scratch_shapes=[pltpu.VMEM((S, dh), BF16), pltpu.VMEM((dh, S), BF16), pltpu.VMEM((S, dh), BF16)],
        compiler_params=_cp("parallel"),
    )(q, kt, v, gq, gkt)


def _sb_bwd(q, qt, k, kt, v, vt, gq, gqt, gk, gkt, o, do, dot_, name):
    H, S, dh = q.shape
    kb = min(SB_KEYS, S)
    per = kb // BLK
    nb = S // BLK
    scale = 1.0 / math.sqrt(dh)

    def body(q_ref, qt_ref, k_ref, kt_ref, v_ref, vt_ref, gq_ref, gqt_ref, gk_ref, gkt_ref, o_ref, do_ref, dot_ref,
             dq_ref, dkt_ref, dvt_ref, dgq_ref, dgkt_ref, qa, qat, ka, kat, vb, vtb, dob, dotb, dqa, dkat):
        qa[...] = _qk_norm(q_ref[0], gq_ref[...], scale).astype(BF16)
        qat[...] = _qk_norm_t(qt_ref[0], gqt_ref[...], scale).astype(BF16)
        ka[...] = _qk_norm(k_ref[0], gk_ref[...], 1.0).astype(BF16)
        kat[...] = _qk_norm_t(kt_ref[0], gkt_ref[...], 1.0).astype(BF16)
        vb[...] = v_ref[0].astype(BF16)
        vtb[...] = vt_ref[0].astype(BF16)
        dob[...] = do_ref[0].astype(BF16)
        dotb[...] = dot_ref[0].astype(BF16)
        dkat[...] = jnp.zeros_like(dkat)
        dvt_ref[...] = jnp.zeros_like(dvt_ref)
        upper, lower_incl, diff = _sb_masks(kb)

        def qblock(i, _):
            rows = pl.ds(pl.multiple_of(i * BLK, BLK), BLK)
            qi, qit = qa[rows, :], qat[:, rows]
            doi, doit = dob[rows, :], dotb[:, rows]
            total = jnp.sum(doi.astype(F32) * o_ref[0, rows, :], axis=1, keepdims=True)
            sbd = i // per

            def step(sb, cl, cd, dqi, valid):
                cols = pl.ds(pl.multiple_of(sb * kb, kb), kb)
                z, lk, w, tot = _sb_step(qi, kat[:, cols], cl, upper, valid)
                w16 = w.astype(BF16)
                dl = jnp.dot(doi, vtb[:, cols], preferred_element_type=F32) * w16.astype(F32)
                incl, dtot = _suffix_sums(dl, lower_incl)
                sig = jnp.exp(z + lk)
                dz = dl - sig * (dl + (total - cd - incl))
                if valid is not None:
                    dz = jnp.where(valid, dz, 0.0)
                dz16 = dz.astype(BF16)
                dqi = dqi + jnp.dot(dz16, ka[cols, :], preferred_element_type=F32)
                dkat[:, cols] += jnp.dot(qit, dz16, preferred_element_type=F32)
                dvt_ref[0, :, cols] += jnp.dot(doit, w16, preferred_element_type=F32)
                return cl + tot, cd + dtot, dqi

            zero = jnp.zeros((BLK, 1), F32)
            carry = step(sbd, zero, zero, jnp.zeros((BLK, dh), F32), diff < (i - sbd * per) * BLK)
            _, _, dqi = lax.fori_loop(0, sbd, lambda jj, c: step(sbd - 1 - jj, c[0], c[1], c[2], None), carry)
            dqa[rows, :] = dqi
            return 0

        lax.fori_loop(0, nb, qblock, 0)
        _, vq = jax.vjp(lambda t, g: _qk_norm(t, g, scale), q_ref[0], gq_ref[...])
        dq, dgq = vq(dqa[...])
        dq_ref[0] = dq
        dgq_ref[0] = dgq
        _, vk = jax.vjp(lambda t, g: _qk_norm_t(t, g, 1.0), kt_ref[0], gkt_ref[...])
        dkt, dgkt = vk(dkat[...])
        dkt_ref[0] = dkt
        dgkt_ref[0] = dgkt

    hs = pl.BlockSpec((1, S, dh), lambda h: (h, 0, 0))
    ts = pl.BlockSpec((1, dh, S), lambda h: (h, 0, 0))
    gr = pl.BlockSpec((1, dh), lambda h: (0, 0))
    gc = pl.BlockSpec((dh, 1), lambda h: (0, 0))
    sd = jax.ShapeDtypeStruct((H, S, dh), F32)
    td = jax.ShapeDtypeStruct((H, dh, S), F32)
    return pl.pallas_call(
        body, name=name, grid=(H,),
        in_specs=[hs, ts, hs, ts, hs, ts, gr, gc, gr, gc, hs, hs, ts],
        out_specs=[hs, ts, ts, pl.BlockSpec((1, 1, dh), lambda h: (h, 0, 0)), pl.BlockSpec((1, dh, 1), lambda h: (h, 0, 0))],
        out_shape=[sd, td, td, jax.ShapeDtypeStruct((H, 1, dh), F32), jax.ShapeDtypeStruct((H, dh, 1), F32)],
        scratch_shapes=[pltpu.VMEM((S, dh), BF16), pltpu.VMEM((dh, S), BF16)] * 4
        + [pltpu.VMEM((S, dh), F32), pltpu.VMEM((dh, S), F32)],
        compiler_params=_cp("parallel", vmem=60 * 1024 * 1024),
    )(q, qt, k, kt, v, vt, gq, gqt, gk, gkt, o, do, dot_)


def _shift_down(x, s, rows):
    if s == 0:
        return x
    return jnp.where(rows >= s, pltpu.roll(x, s, 0), 0.0)


def _shift_up(x, s, rows, n):
    if s == 0:
        return x
    return jnp.where(rows < n - s, pltpu.roll(x, n - s, 0), 0.0)


def _conv(x, w_ref, rows):
    y = x * w_ref[CONV_K - 1:CONV_K, :]
    for kk in range(CONV_K - 1):
        y = y + _shift_down(x, CONV_K - 1 - kk, rows) * w_ref[kk:kk + 1, :]
    return y


def _act_norm(y, normed):
    s = _silu(y)
    n = s * lax.rsqrt(jnp.sum(s * s, axis=-1, keepdims=True) + EPS)
    return jnp.where(normed, n, s)


def _dn_prep_fwd(p, conv_w, col0, name):
    S = p.shape[0]
    nblk = 3 * DN_HEADS
    b0 = col0 // DN_DH

    def body(x_ref, w_ref, o_ref):
        rows = lax.broadcasted_iota(jnp.int32, (S, DN_DH), 0)
        y = _conv(x_ref[...], w_ref, rows)
        o_ref[...] = _act_norm(y, pl.program_id(0) < 2 * DN_HEADS)

    return pl.pallas_call(
        body, name=name, grid=(nblk,),
        in_specs=[pl.BlockSpec((S, DN_DH), lambda j: (0, b0 + j)), pl.BlockSpec((CONV_K, DN_DH), lambda j: (0, j))],
        out_specs=pl.BlockSpec((S, DN_DH), lambda j: (0, j)),
        out_shape=jax.ShapeDtypeStruct((S, 3 * DN_W), F32),
        compiler_params=_cp("parallel"),
    )(p, conv_w)


def _dn_prep_bwd(p, conv_w, col0, dout, name):
    S = p.shape[0]
    nblk = 3 * DN_HEADS
    b0 = col0 // DN_DH

    def body(x_ref, w_ref, do_ref, dx_ref, dw_ref):
        rows = lax.broadcasted_iota(jnp.int32, (S, DN_DH), 0)
        x = x_ref[...]
        y = _conv(x, w_ref, rows)
        normed = pl.program_id(0) < 2 * DN_HEADS
        _, vjp = jax.vjp(lambda t: _act_norm(t, normed), y)
        (dy,) = vjp(do_ref[...])
        dx = dy * w_ref[CONV_K - 1:CONV_K, :]
        dw_ref[CONV_K - 1:CONV_K, :] = jnp.sum(dy * x, axis=0, keepdims=True)
        for kk in range(CONV_K - 1):
            s = CONV_K - 1 - kk
            dx = dx + _shift_up(dy, s, rows, S) * w_ref[kk:kk + 1, :]
            dw_ref[kk:kk + 1, :] = jnp.sum(dy * _shift_down(x, s, rows), axis=0, keepdims=True)
        dx_ref[...] = dx.astype(BF16)

    return pl.pallas_call(
        body, name=name, grid=(nblk,),
        in_specs=[pl.BlockSpec((S, DN_DH), lambda j: (0, b0 + j)), pl.BlockSpec((CONV_K, DN_DH), lambda j: (0, j)),
                  pl.BlockSpec((S, DN_DH), lambda j: (0, j))],
        out_specs=[pl.BlockSpec((S, DN_DH), lambda j: (0, j)), pl.BlockSpec((CONV_K, DN_DH), lambda j: (0, j))],
        out_shape=[jax.ShapeDtypeStruct((S, 3 * DN_W), BF16), jax.ShapeDtypeStruct((CONV_K, 3 * DN_W), F32)],
        compiler_params=_cp("parallel"),
    )(p, conv_w, dout)


def _gate_fn(x, pv):
    lane = lax.broadcasted_iota(jnp.int32, x.shape, 1)
    decay = -jnp.exp(pv[0:1, :]) * _softplus(x + pv[1:2, :])
    return jnp.where(lane < DN_HEADS, _sigmoid(x), decay)


def _dn_gate_fwd(p, pv, blk, name):
    S = p.shape[0]

    def body(x_ref, pv_ref, o_ref):
        o_ref[...] = _gate_fn(x_ref[...], pv_ref[...])

    return pl.pallas_call(
        body, name=name, grid=(1,),
        in_specs=[pl.BlockSpec((S, LANES), lambda i: (0, blk)), pl.BlockSpec((2, LANES), lambda i: (0, 0))],
        out_specs=pl.BlockSpec((S, LANES), lambda i: (0, 0)),
        out_shape=jax.ShapeDtypeStruct((S, LANES), F32),
        compiler_params=_cp("arbitrary"),
    )(p, pv)


def _dn_gate_bwd(p, pv, blk, dout, name):
    S = p.shape[0]

    def body(x_ref, pv_ref, do_ref, dx_ref, dpv_ref):
        _, vjp = jax.vjp(_gate_fn, x_ref[...], pv_ref[...])
        dx, dpv = vjp(do_ref[...])
        dx_ref[...] = dx.astype(BF16)
        dpv_ref[...] = dpv

    return pl.pallas_call(
        body, name=name, grid=(1,),
        in_specs=[pl.BlockSpec((S, LANES), lambda i: (0, blk)), pl.BlockSpec((2, LANES), lambda i: (0, 0)),
                  pl.BlockSpec((S, LANES), lambda i: (0, 0))],
        out_specs=[pl.BlockSpec((S, LANES), lambda i: (0, 0)), pl.BlockSpec((2, LANES), lambda i: (0, 0))],
        out_shape=[jax.ShapeDtypeStruct((S, LANES), BF16), jax.ShapeDtypeStruct((2, LANES), F32)],
        compiler_params=_cp("arbitrary"),
    )(p, pv, dout)


def _matmuls(prec, differentiable):
    def mm(a, b):
        return _dot(a, b, prec)

    def mm_nt(a, b):
        return _dot_nt(a, b, prec)

    if not differentiable:
        return mm, mm_nt
    dmm, dmm_nt = jax.custom_vjp(mm), jax.custom_vjp(mm_nt)
    dmm.defvjp(lambda a, b: (mm(a, b), (a, b)), lambda res, g: (mm_nt(g, res[1]), mm(res[0].T, g)))
    dmm_nt.defvjp(lambda a, b: (mm_nt(a, b), (a, b)), lambda res, g: (mm(g, res[1]), mm(g.T, res[0])))
    return dmm, dmm_nt


def _delta_chunk(state, q, k, v, beta, a_col, a_row, differentiable=False):
    mm, mm_nt = _matmuls(HI, differentiable)
    mm_sum = mm
    C = q.shape[0]
    r = lax.broadcasted_iota(jnp.int32, (C, C), 0)
    c = lax.broadcasted_iota(jnp.int32, (C, C), 1)
    tril, strict = r >= c, r > c
    eye = (r == c).astype(F32)
    g_c = mm_sum(tril.astype(F32), jnp.broadcast_to(a_col, (C, C)))
    g_r = mm_sum(jnp.broadcast_to(a_row, (C, C)), (r <= c).astype(F32))
    decay = jnp.where(tril, jnp.exp(jnp.where(tril, g_c - g_r, 0.0)), 0.0)
    eg = jnp.exp(g_c)
    g_last = jnp.sum(jnp.where(r == C - 1, g_c, 0.0), axis=0, keepdims=True)
    qs = q * (float(q.shape[1]) ** -0.5)
    kb = k * beta
    neg_m = jnp.where(strict, -(mm_nt(kb, k) * decay), 0.0)
    inv = eye + neg_m
    pw = neg_m
    for _ in range(int(math.log2(C)) - 1):
        pw = mm(pw, pw)
        inv = inv + mm(inv, pw)
    u = mm(inv, v * beta)
    w = mm(inv, kb * eg)
    intra = jnp.where(tril, mm_nt(qs, k) * decay, 0.0)
    v_new = u - mm(w, state)
    o = mm(qs * eg, state) + mm(intra, v_new)
    nxt = state * jnp.exp(g_last) + mm((k * jnp.exp(g_last - g_c)).T, v_new)
    return o, nxt


def _delta_specs(S):
    nc = S // BLK
    col = lambda off: pl.BlockSpec((S, DN_DH), lambda h: (0, off + h))
    vec = pl.BlockSpec((1, S, 1), lambda h: (h, 0, 0))
    row = pl.BlockSpec((1, nc, BLK), lambda h: (h, 0, 0))
    st = pl.BlockSpec((1, nc, DN_DH, DN_DH), lambda h: (h, 0, 0, 0))
    return nc, col, vec, row, st


def _delta_fwd(qkv, beta, a_col, a_row, name):
    S = qkv.shape[0]
    nc, col, vec, row, st = _delta_specs(S)

    def body(q_ref, k_ref, v_ref, b_ref, ac_ref, ar_ref, o_ref, st_ref):
        def chunk(ci, state):
            rows = pl.ds(pl.multiple_of(ci * BLK, BLK), BLK)
            st_ref[0, ci] = state
            o, nxt = _delta_chunk(state, q_ref[rows, :], k_ref[rows, :], v_ref[rows, :], b_ref[0, rows, :],
                                  ac_ref[0, rows, :], ar_ref[0, pl.ds(ci, 1), :])
            o_ref[rows, :] = o
            return nxt

        lax.fori_loop(0, nc, chunk, jnp.zeros((DN_DH, DN_DH), F32))

    return pl.pallas_call(
        body, name=name, grid=(DN_HEADS,),
        in_specs=[col(0), col(DN_HEADS), col(2 * DN_HEADS), vec, vec, row],
        out_specs=[col(0), st],
        out_shape=[jax.ShapeDtypeStruct((S, DN_W), F32), jax.ShapeDtypeStruct((DN_HEADS, nc, DN_DH, DN_DH), F32)],
        compiler_params=_cp("parallel"),
    )(qkv, qkv, qkv, beta, a_col, a_row)


def _delta_bwd(qkv, beta, a_col, a_row, states, do, name):
    S = qkv.shape[0]
    nc, col, vec, row, st = _delta_specs(S)
    chunk_fn = functools.partial(_delta_chunk, differentiable=True)

    def body(q_ref, k_ref, v_ref, b_ref, ac_ref, ar_ref, st_ref, do_ref,
             dq_ref, dk_ref, dv_ref, db_ref, dac_ref, dar_ref):
        def chunk(t, dstate):
            ci = nc - 1 - t
            rows = pl.ds(pl.multiple_of(ci * BLK, BLK), BLK)
            _, vjp = jax.vjp(chunk_fn, st_ref[0, ci], q_ref[rows, :], k_ref[rows, :], v_ref[rows, :],
                             b_ref[0, rows, :], ac_ref[0, rows, :], ar_ref[0, pl.ds(ci, 1), :])
            dprev, dq, dk, dv, db, dac, dar = vjp((do_ref[rows, :], dstate))
            dq_ref[rows, :] = dq
            dk_ref[rows, :] = dk
            dv_ref[rows, :] = dv
            db_ref[0, rows, :] = db
            dac_ref[0, rows, :] = dac
            dar_ref[0, pl.ds(ci, 1), :] = dar
            return dprev

        lax.fori_loop(0, nc, chunk, jnp.zeros((DN_DH, DN_DH), F32))

    vs = jax.ShapeDtypeStruct((DN_HEADS, S, 1), F32)
    d3 = jax.ShapeDtypeStruct((S, DN_W), F32)
    return pl.pallas_call(
        body, name=name, grid=(DN_HEADS,),
        in_specs=[col(0), col(DN_HEADS), col(2 * DN_HEADS), vec, vec, row, st, col(0)],
        out_specs=[col(0), col(0), col(0), vec, vec, row],
        out_shape=[d3, d3, d3, vs, vs, jax.ShapeDtypeStruct((DN_HEADS, nc, BLK), F32)],
        compiler_params=_cp("parallel"),
    )(qkv, qkv, qkv, beta, a_col, a_row, states, do)


def _gate_sb(o, z):
    return o * _silu(z)


def _gate_dn(o, z, g):
    return jnp.concatenate(
        [_rms(o[:, h * DN_DH:(h + 1) * DN_DH]) * g * _silu(z[:, h * DN_DH:(h + 1) * DN_DH]) for h in range(DN_HEADS)],
        axis=1)


def _merge_specs(S, D, tm):
    row = lambda w, blk: pl.BlockSpec((tm, w), lambda i: (i, blk))
    full = lambda a, b: pl.BlockSpec((a, b), lambda i: (0, 0))
    return [row(D, 0), full(1, D), row(SB_W, 0), row(SB_W, 3), row(DN_W, 0), row(DN_W, 7),
            row(D, IN_MAIN // D), row(D, IN_MAIN // D + 1), full(1, DN_DH), full(SB_W, D), full(DN_W, D), full(D, D)]


def _merge_fwd(x, gate, o_sb, o_dn, p, ng, wbs, wbd, wo, name):
    S, D = x.shape
    tm = min(512, S)

    def body(x_ref, gate_ref, osb_ref, zsb_ref, odn_ref, zdn_ref, msb_ref, mdn_ref, ng_ref, wbs_ref, wbd_ref, wo_ref,
             out_ref):
        a = _gate_sb(osb_ref[...], zsb_ref[...])
        b = _gate_dn(odn_ref[...], zdn_ref[...], ng_ref[...])
        y = _sigmoid(msb_ref[...]) * _bdot(a, wbs_ref[...]) + _sigmoid(mdn_ref[...]) * _bdot(b, wbd_ref[...])
        out_ref[...] = x_ref[...] + gate_ref[...] * _bdot(y, wo_ref[...])

    return pl.pallas_call(
        body, name=name, grid=(S // tm,), in_specs=_merge_specs(S, D, tm),
        out_specs=pl.BlockSpec((tm, D), lambda i: (i, 0)), out_shape=jax.ShapeDtypeStruct((S, D), F32),
        compiler_params=_cp("parallel"),
    )(x, gate, o_sb, p, o_dn, p, p, p, ng, wbs, wbd, wo)


def _merge_bwd(dxn, gate, o_sb, o_dn, p, ng, wbs, wbd, wo, name):
    S, D = dxn.shape
    tm = min(256, S)

    def body(dxn_ref, gate_ref, osb_ref, zsb_ref, odn_ref, zdn_ref, msb_ref, mdn_ref, ng_ref, wbs_ref, wbd_ref, wo_ref,
             dosb_ref, dzsb_ref, dodn_ref, dzdn_ref, dmsb_ref, dmdn_ref, dwo_ref, dwbs_ref, dwbd_ref, dgate_ref, dng_ref):
        @pl.when(pl.program_id(0) == 0)
        def _():
            for ref in (dwo_ref, dwbs_ref, dwbd_ref, dgate_ref, dng_ref):
                ref[...] = jnp.zeros_like(ref)

        a, vjp_a = jax.vjp(_gate_sb, osb_ref[...], zsb_ref[...])
        b, vjp_b = jax.vjp(_gate_dn, odn_ref[...], zdn_ref[...], ng_ref[...])
        a16, b16 = a.astype(BF16), b.astype(BF16)
        ps = jnp.dot(a16, wbs_ref[...], preferred_element_type=F32)
        pd = jnp.dot(b16, wbd_ref[...], preferred_element_type=F32)
        ss, sd = _sigmoid(msb_ref[...]), _sigmoid(mdn_ref[...])
        y16 = (ss * ps + sd * pd).astype(BF16)
        out = jnp.dot(y16, wo_ref[...], preferred_element_type=F32)
        dxn_v = dxn_ref[...]
        dgate_ref[...] += jnp.sum(dxn_v * out, axis=0, keepdims=True)
        dout16 = (dxn_v * gate_ref[...]).astype(BF16)
        dwo_ref[...] += _bdot_tn(y16, dout16)
        dy = _bdot_nt(dout16, wo_ref[...])
        dmsb_ref[...] = (dy * ps * ss * (1.0 - ss)).astype(BF16)
        dmdn_ref[...] = (dy * pd * sd * (1.0 - sd)).astype(BF16)
        dps16, dpd16 = (dy * ss).astype(BF16), (dy * sd).astype(BF16)
        dwbs_ref[...] += _bdot_tn(a16, dps16)
        dwbd_ref[...] += _bdot_tn(b16, dpd16)
        dosb, dzsb = vjp_a(_bdot_nt(dps16, wbs_ref[...]))
        dodn, dzdn, dng = vjp_b(_bdot_nt(dpd16, wbd_ref[...]))
        dosb_ref[...] = dosb
        dzsb_ref[...] = dzsb.astype(BF16)
        dodn_ref[...] = dodn
        dzdn_ref[...] = dzdn.astype(BF16)
        dng_ref[...] += dng

    row = lambda w: pl.BlockSpec((tm, w), lambda i: (i, 0))
    full = lambda a, b: pl.BlockSpec((a, b), lambda i: (0, 0))
    sds = jax.ShapeDtypeStruct
    return pl.pallas_call(
        body, name=name, grid=(S // tm,), in_specs=_merge_specs(S, D, tm),
        out_specs=[row(SB_W), row(SB_W), row(DN_W), row(DN_W), row(D), row(D),
                   full(D, D), full(SB_W, D), full(DN_W, D), full(1, D), full(1, DN_DH)],
        out_shape=[sds((S, SB_W), F32), sds((S, SB_W), BF16), sds((S, DN_W), F32), sds((S, DN_W), BF16),
                   sds((S, D), BF16), sds((S, D), BF16), sds((D, D), F32), sds((SB_W, D), F32), sds((DN_W, D), F32),
                   sds((1, D), F32), sds((1, DN_DH), F32)],
        compiler_params=_cp("arbitrary"),
    )(dxn, gate, o_sb, p, o_dn, p, p, p, ng, wbs, wbd, wo)


def _loss_fwd_bwd(y, target, name):
    S, D = y.shape
    tm = min(512, S)

    def body(y_ref, t_ref, l_ref, dy_ref):
        @pl.when(pl.program_id(0) == 0)
        def _():
            l_ref[...] = jnp.zeros_like(l_ref)

        e = y_ref[...] - t_ref[...]
        l_ref[...] += jnp.sum(e * e, axis=0, keepdims=True) * (0.5 / D)
        dy_ref[...] = e * (1.0 / D)

    row = pl.BlockSpec((tm, D), lambda i: (i, 0))
    return pl.pallas_call(
        body, name=name, grid=(S // tm,), in_specs=[row, row],
        out_specs=[pl.BlockSpec((1, D), lambda i: (0, 0)), row],
        out_shape=[jax.ShapeDtypeStruct((1, D), F32), jax.ShapeDtypeStruct((S, D), F32)],
        compiler_params=_cp("arbitrary"),
    )(y, target)


def _mod_fwd(c_all, ada_w, name):
    L, D, n = ada_w.shape

    def body(c_ref, w_ref, o_ref):
        o_ref[0] = _dot(_silu(c_ref[...]), w_ref[0])

    return pl.pallas_call(
        body, name=name, grid=(L,),
        in_specs=[pl.BlockSpec(c_all.shape, lambda l: (0, 0)), pl.BlockSpec((1, D, n), lambda l: (l, 0, 0))],
        out_specs=pl.BlockSpec((1, N_DEV, n), lambda l: (l, 0, 0)),
        out_shape=jax.ShapeDtypeStruct((L, N_DEV, n), F32),
        compiler_params=_cp("parallel"),
    )(c_all, ada_w)


def _mod_bwd_w(c_all_t, dmod, name):
    L, _, n = dmod.shape
    D = c_all_t.shape[0]

    def body(c_ref, d_ref, o_ref):
        o_ref[0] = _dot(_silu(c_ref[...]), d_ref[0])

    return pl.pallas_call(
        body, name=name, grid=(L,),
        in_specs=[pl.BlockSpec(c_all_t.shape, lambda l: (0, 0)), pl.BlockSpec((1, N_DEV, n), lambda l: (l, 0, 0))],
        out_specs=pl.BlockSpec((1, D, n), lambda l: (l, 0, 0)),
        out_shape=jax.ShapeDtypeStruct((L, D, n), F32),
        compiler_params=_cp("parallel"),
    )(c_all_t, dmod)


def _me():
    return lax.axis_index("x"), lax.axis_index("y"), lax.axis_index("c")


def _peer(k):
    x, y, c = _me()
    return (1 - x if k & 4 else x, 1 - y if k & 2 else y, 1 - c if k & 1 else c)


def _lin(dev):
    return 4 * dev[0] + 2 * dev[1] + dev[2]


def _exchange(arrays, scatter, name):
    n_arr = len(arrays)

    def body(*refs):
        ins, outs = refs[:n_arr], refs[n_arr:2 * n_arr]
        send_sems, recv_sems, local_sems = refs[2 * n_arr:]
        me = _lin(_me())
        local, remote = [], []
        for t in range(n_arr):
            src_of = (lambda d, t=t: ins[t].at[d]) if scatter else (lambda d, t=t: ins[t])
            mine = pltpu.make_async_copy(src_of(me), outs[t].at[me], local_sems.at[t])
            mine.start()
            local.append(mine)
            for k in range(1, N_DEV):
                peer = _peer(k)
                cp = pltpu.make_async_remote_copy(
                    src_ref=src_of(_lin(peer)), dst_ref=outs[t].at[me], send_sem=send_sems.at[t, k - 1],
                    recv_sem=recv_sems.at[t, k - 1], device_id=peer, device_id_type=pl.DeviceIdType.MESH)
                cp.start()
                remote.append(cp)
        for t in range(n_arr):
            for k in range(1, N_DEV):
                peer = _peer(k)
                landed = outs[t].at[_lin(peer)]
                pltpu.make_async_remote_copy(
                    src_ref=landed, dst_ref=landed, send_sem=send_sems.at[t, k - 1],
                    recv_sem=recv_sems.at[t, k - 1], device_id=peer, device_id_type=pl.DeviceIdType.MESH).wait_recv()
        for cp in remote:
            cp.wait_send()
        for cp in local:
            cp.wait()

    out_shape = [jax.ShapeDtypeStruct((N_DEV,) + tuple(a.shape[1:] if scatter else a.shape), a.dtype) for a in arrays]
    hbm = pl.BlockSpec(memory_space=pl.ANY)
    return pl.pallas_call(
        body, name=name, in_specs=[hbm] * n_arr, out_specs=[hbm] * n_arr, out_shape=out_shape,
        scratch_shapes=[pltpu.SemaphoreType.DMA((n_arr, N_DEV - 1)), pltpu.SemaphoreType.DMA((n_arr, N_DEV - 1)),
                        pltpu.SemaphoreType.DMA((n_arr,))],
    )(*arrays)


def _sum_slots(a, name):
    _, R, C = a.shape
    tr = SUM_ROWS if R % SUM_ROWS == 0 else R

    def body(a_ref, o_ref):
        acc = a_ref[0]
        for s in range(1, N_DEV):
            acc = acc + a_ref[s]
        o_ref[...] = acc

    return pl.pallas_call(
        body, name=name, grid=(R // tr,),
        in_specs=[pl.BlockSpec((N_DEV, tr, C), lambda i: (0, i, 0))], out_specs=pl.BlockSpec((tr, C), lambda i: (i, 0)),
        out_shape=jax.ShapeDtypeStruct((R, C), F32), compiler_params=_cp("parallel"),
    )(a)


def _adamw(w, g, m, v, name):
    shape = w.shape
    C = shape[-1]
    R = w.size // C
    tr = R
    for cand in (256, 128, 64):
        if R > cand and R % cand == 0:
            tr = cand
            break
    c1 = 1.0 / (1.0 - ADAM_B1 ** ADAM_STEP)
    c2 = 1.0 / (1.0 - ADAM_B2 ** ADAM_STEP)

    def body(w_ref, g_ref, m_ref, v_ref, d_ref, nm_ref, nv_ref):
        gv = g_ref[...]
        nm = ADAM_B1 * m_ref[...] + (1.0 - ADAM_B1) * gv
        nv = ADAM_B2 * v_ref[...] + (1.0 - ADAM_B2) * (gv * gv)
        d_ref[...] = -ADAM_LR * ((nm * c1) / (jnp.sqrt(nv * c2) + ADAM_EPS) + ADAM_WD * w_ref[...])
        nm_ref[...] = nm
        nv_ref[...] = nv

    spec = pl.BlockSpec((tr, C), lambda i: (i, 0))
    sd = jax.ShapeDtypeStruct((R, C), F32)
    outs = pl.pallas_call(
        body, name=name, grid=(R // tr,), in_specs=[spec] * 4, out_specs=[spec] * 3, out_shape=[sd] * 3,
        compiler_params=_cp("parallel"),
    )(*(t.reshape(R, C) for t in (w, g, m, v)))
    return tuple(t.reshape(shape) for t in outs)


def _to_heads(t):
    S = t.shape[0]
    return t.reshape(S, SB_HEADS, SB_DH).transpose(1, 0, 2)


def _from_heads(t):
    S = t.shape[1]
    return t.transpose(1, 0, 2).reshape(S, SB_W)


def _to_heads_t(t):
    S = t.shape[0]
    return t.reshape(S, SB_HEADS, SB_DH).transpose(1, 2, 0)


def _from_heads_t(t):
    S = t.shape[2]
    return t.transpose(2, 0, 1).reshape(S, SB_W)


def _pad_cols(w_in):
    D = w_in.shape[0]
    pad = jnp.zeros((D, LANES - 2 * DN_HEADS), w_in.dtype)
    return jnp.concatenate([w_in[:, :IN_MAIN], w_in[:, IN_COLS:], w_in[:, IN_MAIN:IN_COLS], pad], axis=1)


def _unpad_cols(dw, D):
    return jnp.concatenate([dw[:, :IN_MAIN], dw[:, IN_MAIN + 2 * D:IN_MAIN + 2 * D + 2 * DN_HEADS],
                            dw[:, IN_MAIN:IN_MAIN + 2 * D]], axis=1)


def _gate_params(a_log, dt_bias):
    z = jnp.zeros((LANES,), F32)
    return jnp.stack([z.at[DN_HEADS:2 * DN_HEADS].set(a_log), z.at[DN_HEADS:2 * DN_HEADS].set(dt_bias)])


def _layer_fwd(l, x, mod, wts):
    S, D = x.shape
    tag = f"l{l}_"
    p, h = _inproj_fwd(x, mod, wts["norm_g"], wts["w_in"], tag + "inproj_fwd")
    q, k, v = (_to_heads(p[:, i * SB_W:(i + 1) * SB_W]) for i in range(3))
    qt, kt, vt = (_to_heads_t(p[:, i * SB_W:(i + 1) * SB_W]) for i in range(3))
    o_sb_h = _sb_fwd(q, kt, v, wts["sb_q_g"], wts["sb_k_g"].T, tag + "sb_fwd")
    qkv = _dn_prep_fwd(p, wts["conv_w"], 4 * SB_W, tag + "dn_prep_fwd")
    pv = _gate_params(wts["dn_a_log"], wts["dn_dt_bias"])
    ba_blk = (IN_MAIN + 2 * D) // LANES
    bg = _dn_gate_fwd(p, pv, ba_blk, tag + "dn_gate_fwd")
    bgt = bg[:, :2 * DN_HEADS].T
    beta = bgt[:DN_HEADS].reshape(DN_HEADS, S, 1)
    a_col = bgt[DN_HEADS:].reshape(DN_HEADS, S, 1)
    a_row = bgt[DN_HEADS:].reshape(DN_HEADS, S // BLK, BLK)
    o_dn, states = _delta_fwd(qkv, beta, a_col, a_row, tag + "delta_fwd")
    o_sb = _from_heads(o_sb_h)
    gate = mod[:, 2 * D:]
    out = _merge_fwd(x, gate, o_sb, o_dn, p, wts["dn_norm_g"], wts["w_branch_sb"], wts["w_branch_dn"], wts["w_out"],
                     tag + "merge_fwd")
    saved = dict(x=x, mod=mod, p=p, h=h, q=q, k=k, v=v, qt=qt, kt=kt, vt=vt, o_sb_h=o_sb_h, o_sb=o_sb, qkv=qkv, pv=pv, beta=beta,
                 a_col=a_col, a_row=a_row, o_dn=o_dn, states=states, gate=gate)
    return out, saved


def _layer_bwd(l, dxn, sv, wts):
    S, D = dxn.shape
    tag = f"l{l}_"
    (dosb, dzsb, dodn, dzdn, dmsb, dmdn, dwo, dwbs, dwbd, dgate, dng) = _merge_bwd(
        dxn, sv["gate"], sv["o_sb"], sv["o_dn"], sv["p"], wts["dn_norm_g"], wts["w_branch_sb"], wts["w_branch_dn"],
        wts["w_out"], tag + "merge_bwd")
    dq, dkt, dvt, dgq, dgkt = _sb_bwd(sv["q"], sv["qt"], sv["k"], sv["kt"], sv["v"], sv["vt"], wts["sb_q_g"],
                                       wts["sb_q_g"].T, wts["sb_k_g"], wts["sb_k_g"].T, sv["o_sb_h"],
                                       _to_heads(dosb), _to_heads_t(dosb), tag + "sb_bwd")
    dqkv_n_q, dqkv_n_k, dqkv_n_v, dbeta, dac, dar = _delta_bwd(sv["qkv"], sv["beta"], sv["a_col"], sv["a_row"],
                                                                sv["states"], dodn, tag + "delta_bwd")
    dqkv, dconv = _dn_prep_bwd(sv["p"], wts["conv_w"], 4 * SB_W,
                               jnp.concatenate([dqkv_n_q, dqkv_n_k, dqkv_n_v], axis=1), tag + "dn_prep_bwd")
    da = dac.reshape(DN_HEADS, S) + dar.reshape(DN_HEADS, S)
    dbg = jnp.concatenate([dbeta.reshape(DN_HEADS, S), da], axis=0).T
    dbg = jnp.concatenate([dbg, jnp.zeros((S, LANES - 2 * DN_HEADS), F32)], axis=1)
    ba_blk = (IN_MAIN + 2 * D) // LANES
    dba, dpv = _dn_gate_bwd(sv["p"], sv["pv"], ba_blk, dbg, tag + "dn_gate_bwd")
    dp = jnp.concatenate([_from_heads(dq).astype(BF16), _from_heads_t(dkt).astype(BF16), _from_heads_t(dvt).astype(BF16),
                          dzsb, dqkv, dzdn, dmsb, dmdn, dba], axis=1)
    dx, dmod, dg = _inproj_bwd_dx(dp, wts["w_in"], sv["x"], sv["mod"], wts["norm_g"], dxn, tag + "inproj_bwd_dx")
    dw_in = _matmul_tn(sv["h"].T, dp, tag + "inproj_bwd_dw")
    dmod = dmod.at[:, 2 * D:].set(dgate)
    grads = dict(w_in=_unpad_cols(dw_in, D), w_branch_sb=dwbs, w_branch_dn=dwbd, w_out=dwo, conv_w=dconv,
                 mod=dmod[0], norm_g=dg[0], sb_q_g=jnp.sum(dgq, axis=0)[0], sb_k_g=jnp.sum(dgkt, axis=0)[:, 0],
                 dn_a_log=dpv[0, DN_HEADS:2 * DN_HEADS], dn_dt_bias=dpv[1, DN_HEADS:2 * DN_HEADS], dn_norm_g=dng[0])
    return dx, grads


def _pad_rows(a, mult):
    extra = (-a.shape[0]) % mult
    return a if extra == 0 else jnp.concatenate([a, jnp.zeros((extra,) + a.shape[1:], a.dtype)], axis=0)


def _pack_rows(parts, width, mult):
    flat = jnp.concatenate([t.reshape(-1) for t in parts])
    extra = (-flat.shape[0]) % width
    if extra:
        flat = jnp.concatenate([flat, jnp.zeros((extra,), flat.dtype)])
    return _pad_rows(flat.reshape(-1, width), mult)


def _take(flat, off, shape):
    n = math.prod(shape)
    return flat[..., off:off + n].reshape(flat.shape[:-1] + tuple(shape)), off + n


SMALL = ("mod", "norm_g", "sb_q_g", "sb_k_g", "dn_a_log", "dn_dt_bias", "dn_norm_g")


def kernel(x, c, ada_w, ada_b, norm_g, w_in, sb_q_g, sb_k_g, conv_w, dn_a_log, dn_dt_bias, dn_norm_g, w_branch_sb, w_branch_dn, w_out, loss_target, m_ada_w, m_ada_b, m_norm_g, m_w_in, m_sb_q_g, m_sb_k_g, m_conv_w, m_dn_a_log, m_dn_dt_bias, m_dn_norm_g, m_w_branch_sb, m_w_branch_dn, m_w_out, v_ada_w, v_ada_b, v_norm_g, v_w_in, v_sb_q_g, v_sb_k_g, v_conv_w, v_dn_a_log, v_dn_dt_bias, v_dn_norm_g, v_w_branch_sb, v_w_branch_dn, v_w_out):
    L, D = norm_g.shape
    S = x.shape[1]
    n_in = w_in.shape[2]
    n_ada = ada_w.shape[2]
    n_br = w_branch_sb.shape[2]
    n_out = w_out.shape[1]
    n_conv = conv_w.shape[2]
    me = _lin(_me())

    def cat(a):
        return jnp.concatenate([a[d] for d in range(N_DEV)], axis=1)

    c_all, conv_all = _exchange([c, conv_w.reshape(L * CONV_K, n_conv)], False, "gather_small")
    c_all = c_all.reshape(N_DEV, D)
    conv_full = cat(conv_all).reshape(L, CONV_K, N_DEV * n_conv)

    mod_part = _mod_fwd(c_all, ada_w, "mod_fwd")
    wi_all, wbs_all, wbd_all, wo_all, mod_all = _exchange(
        [w_in.astype(BF16).reshape(L * D, n_in), w_branch_sb.astype(BF16).reshape(L * SB_W, n_br),
         w_branch_dn.astype(BF16).reshape(L * DN_W, n_br), w_out.astype(BF16).reshape(L * n_out, D),
         mod_part.reshape(L * N_DEV, n_ada)], False, "gather_weights")
    w_in_full = cat(wi_all).reshape(L, D, N_DEV * n_in)
    wbs_full = cat(wbs_all).reshape(L, SB_W, N_DEV * n_br)
    wbd_full = cat(wbd_all).reshape(L, DN_W, N_DEV * n_br)
    wo_full = wo_all.reshape(N_DEV, L, n_out, D).transpose(1, 0, 2, 3).reshape(L, N_DEV * n_out, D)
    mod_full = cat(mod_all).reshape(L, N_DEV, N_DEV * n_ada) + ada_b[:, None, :]
    mod_mine = lax.dynamic_slice_in_dim(mod_full, me, 1, axis=1)

    wts = [dict(norm_g=norm_g[l:l + 1], w_in=_pad_cols(w_in_full[l]), sb_q_g=sb_q_g[l:l + 1], sb_k_g=sb_k_g[l:l + 1],
                conv_w=conv_full[l], dn_a_log=dn_a_log[l], dn_dt_bias=dn_dt_bias[l], dn_norm_g=dn_norm_g[l:l + 1],
                w_branch_sb=wbs_full[l], w_branch_dn=wbd_full[l], w_out=wo_full[l]) for l in range(L)]

    act = x[0]
    saved = []
    for l in range(L):
        act, sv = _layer_fwd(l, act, mod_mine[l], wts[l])
        saved.append(sv)
    loss_cols, dact = _loss_fwd_bwd(act, loss_target[0], "loss")
    loss = lax.psum(jnp.sum(loss_cols), ("x", "y", "c"))
    grads = [None] * L
    for l in reversed(range(L)):
        dact, grads[l] = _layer_bwd(l, dact, saved[l], wts[l])
    grad_x = dact[None]

    def col_blocks(name, n):
        return jnp.concatenate([jnp.stack([grads[l][name][:, d * n:(d + 1) * n] for d in range(N_DEV)])
                                for l in range(L)], axis=1)

    send = dict(w_in=col_blocks("w_in", n_in), w_branch_sb=col_blocks("w_branch_sb", n_br),
                w_branch_dn=col_blocks("w_branch_dn", n_br), conv_w=col_blocks("conv_w", n_conv),
                w_out=jnp.concatenate([grads[l]["w_out"].reshape(N_DEV, n_out, D) for l in range(L)], axis=1))
    small_g = _pack_rows([grads[l][n] for l in range(L) for n in SMALL], LANES, 8)
    recv = _exchange(list(send.values()), True, "scatter_grads")
    (small_all_g,) = _exchange([small_g], False, "gather_small_grads")
    small_sum = _sum_slots(small_all_g, "sum_small_grads").reshape(-1)
    shard_shapes = dict(w_in=w_in.shape, w_branch_sb=w_branch_sb.shape, w_branch_dn=w_branch_dn.shape,
                        conv_w=conv_w.shape, w_out=w_out.shape)
    g_out = {n: _sum_slots(r, "sum_" + n).reshape(shard_shapes[n]) for n, r in zip(send, recv)}
    small_shapes = dict(mod=(3 * D,), norm_g=(D,), sb_q_g=(SB_DH,), sb_k_g=(SB_DH,), dn_a_log=(DN_HEADS,),
                        dn_dt_bias=(DN_HEADS,), dn_norm_g=(DN_DH,))
    off = 0
    off_all = 0
    small_each = small_all_g.reshape(N_DEV, -1)
    per_small = {n: [] for n in SMALL}
    dmod_all = []
    for l in range(L):
        for n in SMALL:
            t, off = _take(small_sum, off, small_shapes[n])
            per_small[n].append(t)
            if n == "mod":
                t_all, _ = _take(small_each, off_all, small_shapes[n])
                dmod_all.append(t_all)
            off_all += math.prod(small_shapes[n])
    for n in SMALL:
        g_out[n if n != "mod" else "ada_b"] = jnp.stack(per_small[n])
    dmod_all = jnp.stack(dmod_all)
    dmod_cols = lax.dynamic_slice_in_dim(dmod_all, me * n_ada, n_ada, axis=2)
    g_out["ada_w"] = _mod_bwd_w(c_all.T, dmod_cols, "mod_bwd_w")

    given = dict(ada_w=(ada_w, m_ada_w, v_ada_w), ada_b=(ada_b, m_ada_b, v_ada_b), norm_g=(norm_g, m_norm_g, v_norm_g),
                 w_in=(w_in, m_w_in, v_w_in), sb_q_g=(sb_q_g, m_sb_q_g, v_sb_q_g), sb_k_g=(sb_k_g, m_sb_k_g, v_sb_k_g),
                 conv_w=(conv_w, m_conv_w, v_conv_w), dn_a_log=(dn_a_log, m_dn_a_log, v_dn_a_log),
                 dn_dt_bias=(dn_dt_bias, m_dn_dt_bias, v_dn_dt_bias), dn_norm_g=(dn_norm_g, m_dn_norm_g, v_dn_norm_g),
                 w_branch_sb=(w_branch_sb, m_w_branch_sb, v_w_branch_sb),
                 w_branch_dn=(w_branch_dn, m_w_branch_dn, v_w_branch_dn), w_out=(w_out, m_w_out, v_w_out))
    order = list(given)
    upd = {n: _adamw(given[n][0], g_out[n], given[n][1], given[n][2], "adamw_" + n) for n in order}
    return (loss, grad_x, *[g_out[n] for n in order], *[upd[n][0] for n in order], *[upd[n][1] for n in order],
            *[upd[n][2] for n in order])
```

```python
import functools
import math

import jax
import jax.numpy as jnp
from jax import lax
from jax.experimental import pallas as pl
from jax.experimental.pallas import tpu as pltpu

F32 = jnp.float32
BF16 = jnp.bfloat16
HI = lax.Precision.HIGHEST

N_DEV = 8
EPS = 1e-6
SB_HEADS, SB_DH = 8, 64
DN_HEADS, DN_DH = 4, 128
SB_W = SB_HEADS * SB_DH
DN_W = DN_HEADS * DN_DH
CONV_K = 4
BLK = 128
SB_KEYS = 512
SB_QB = 512
LANES = 128
IN_MAIN = 4 * SB_W + 4 * DN_W
IN_COLS = IN_MAIN + 2 * DN_HEADS
ADAM_LR, ADAM_B1, ADAM_B2, ADAM_EPS, ADAM_WD, ADAM_STEP = 0.001, 0.9, 0.999, 1e-08, 0.01, 10
VMEM_LIMIT = 56 * 1024 * 1024
SUM_ROWS = 128


def _cp(*sem, vmem=VMEM_LIMIT):
    return pltpu.CompilerParams(dimension_semantics=sem if sem else None, vmem_limit_bytes=vmem)


def _dot(a, b, prec=HI):
    return lax.dot_general(a, b, (((1,), (0,)), ((), ())), precision=prec, preferred_element_type=F32)


def _dot_nt(a, b, prec=HI):
    return lax.dot_general(a, b, (((1,), (1,)), ((), ())), precision=prec, preferred_element_type=F32)


def _bdot(a, b):
    return lax.dot_general(a.astype(BF16), b.astype(BF16), (((1,), (0,)), ((), ())), preferred_element_type=F32)


def _bdot_nt(a, b):
    return lax.dot_general(a.astype(BF16), b.astype(BF16), (((1,), (1,)), ((), ())), preferred_element_type=F32)


def _bdot_tn(a, b):
    return lax.dot_general(a.astype(BF16), b.astype(BF16), (((0,), (0,)), ((), ())), preferred_element_type=F32)


def _split_dot(a, b01):
    hi = a.astype(BF16)
    lo = (a - hi.astype(F32)).astype(BF16)
    return (lax.dot_general(hi, b01, (((1,), (0,)), ((), ())), preferred_element_type=F32)
            + lax.dot_general(lo, b01, (((1,), (0,)), ((), ())), preferred_element_type=F32))


def _sigmoid(x):
    return 1.0 / (1.0 + jnp.exp(-x))


def _silu(x):
    return x * _sigmoid(x)


def _softplus(x):
    return jnp.maximum(x, 0.0) + jnp.log(1.0 + jnp.exp(-jnp.abs(x)))


def _rms(x):
    return x * lax.rsqrt(jnp.mean(x * x, axis=-1, keepdims=True) + EPS)


def _prenorm(x, g, shift, scale):
    return _rms(x) * g * (1.0 + scale) + shift


def _inproj_fwd(x, mod, g, w, name):
    S, D = x.shape
    N = w.shape[1]
    tm = min(512, S)
    tn = 896 if N % 896 == 0 else 128

    def body(x_ref, mod_ref, g_ref, w_ref, p_ref, h_ref):
        @pl.when(pl.program_id(1) == 0)
        def _():
            h = _prenorm(x_ref[...], g_ref[...], mod_ref[:, 0:D], mod_ref[:, D:2 * D])
            h_ref[...] = h.astype(BF16)

        p_ref[...] = jnp.dot(h_ref[...], w_ref[...], preferred_element_type=F32)

    return pl.pallas_call(
        body, name=name, grid=(S // tm, N // tn),
        in_specs=[pl.BlockSpec((tm, D), lambda i, j: (i, 0)), pl.BlockSpec((1, 3 * D), lambda i, j: (0, 0)),
                  pl.BlockSpec((1, D), lambda i, j: (0, 0)), pl.BlockSpec((D, tn), lambda i, j: (0, j))],
        out_specs=[pl.BlockSpec((tm, tn), lambda i, j: (i, j)), pl.BlockSpec((tm, D), lambda i, j: (i, 0))],
        out_shape=[jax.ShapeDtypeStruct((S, N), F32), jax.ShapeDtypeStruct((S, D), BF16)],
        compiler_params=_cp("parallel", "arbitrary"),
    )(x, mod, g, w)


def _inproj_bwd_dx(dp, w, x, mod, g, dxn, name):
    S, N = dp.shape
    D = x.shape[1]
    tm = min(256, S)
    tk = 896 if N % 896 == 0 else 128
    nk = N // tk

    def body(dp_ref, w_ref, x_ref, mod_ref, g_ref, dxn_ref, dx_ref, dmod_ref, dg_ref, acc):
        i, k = pl.program_id(0), pl.program_id(1)

        @pl.when(k == 0)
        def _():
            acc[...] = jnp.zeros_like(acc)

        @pl.when((i == 0) & (k == 0))
        def _():
            dmod_ref[...] = jnp.zeros_like(dmod_ref)
            dg_ref[...] = jnp.zeros_like(dg_ref)

        acc[...] += lax.dot_general(dp_ref[...], w_ref[...], (((1,), (1,)), ((), ())), preferred_element_type=F32)

        @pl.when(k == nk - 1)
        def _():
            _, vjp = jax.vjp(_prenorm, x_ref[...], g_ref[...], mod_ref[:, 0:D], mod_ref[:, D:2 * D])
            dx, dg, dshift, dscale = vjp(acc[...])
            dx_ref[...] = dxn_ref[...] + dx
            dg_ref[...] += dg
            dmod_ref[:, 0:D] += dshift
            dmod_ref[:, D:2 * D] += dscale

    return pl.pallas_call(
        body, name=name, grid=(S // tm, nk),
        in_specs=[pl.BlockSpec((tm, tk), lambda i, k: (i, k)), pl.BlockSpec((D, tk), lambda i, k: (0, k)),
                  pl.BlockSpec((tm, D), lambda i, k: (i, 0)), pl.BlockSpec((1, 3 * D), lambda i, k: (0, 0)),
                  pl.BlockSpec((1, D), lambda i, k: (0, 0)), pl.BlockSpec((tm, D), lambda i, k: (i, 0))],
        out_specs=[pl.BlockSpec((tm, D), lambda i, k: (i, 0)), pl.BlockSpec((1, 3 * D), lambda i, k: (0, 0)),
                   pl.BlockSpec((1, D), lambda i, k: (0, 0))],
        out_shape=[jax.ShapeDtypeStruct((S, D), F32), jax.ShapeDtypeStruct((1, 3 * D), F32),
                   jax.ShapeDtypeStruct((1, D), F32)],
        scratch_shapes=[pltpu.VMEM((tm, D), F32)],
        compiler_params=_cp("arbitrary", "arbitrary"),
    )(dp, w, x, mod, g, dxn)


def _matmul_tn(a_t, b, name):
    M, K = a_t.shape
    N = b.shape[1]
    tn = 896 if N % 896 == 0 else (512 if N % 512 == 0 else 128)
    tk = min(512, K)
    nk = K // tk

    def body(a_ref, b_ref, o_ref):
        @pl.when(pl.program_id(1) == 0)
        def _():
            o_ref[...] = jnp.zeros_like(o_ref)

        o_ref[...] += jnp.dot(a_ref[...], b_ref[...], preferred_element_type=F32)

    return pl.pallas_call(
        body, name=name, grid=(N // tn, nk),
        in_specs=[pl.BlockSpec((M, tk), lambda j, k: (0, k)), pl.BlockSpec((tk, tn), lambda j, k: (k, j))],
        out_specs=pl.BlockSpec((M, tn), lambda j, k: (0, j)),
        out_shape=jax.ShapeDtypeStruct((M, N), F32),
        compiler_params=_cp("parallel", "arbitrary"),
    )(a_t, b)


def _qk_norm(t, g, scale):
    return _rms(t) * g * scale


def _qk_norm_t(t, g_col, scale):
    return t * lax.rsqrt(jnp.mean(t * t, axis=0, keepdims=True) + EPS) * g_col * scale


def _suffix_sums(x, tri):
    half = tri.shape[0]
    lo, hi = x[:, :half], x[:, half:]
    hi_sum = jnp.sum(hi, axis=1, keepdims=True)
    y = jnp.concatenate([_split_dot(lo, tri) + hi_sum, _split_dot(hi, tri)], axis=1)
    return y, hi_sum + jnp.sum(lo, axis=1, keepdims=True)


def _sb_step(qi, kat_blk, cl, upper, valid):
    z = jnp.dot(qi, kat_blk, preferred_element_type=F32)
    lk = jnp.minimum(-z, 0.0) - jnp.log(1.0 + jnp.exp(-jnp.abs(z)))
    if valid is not None:
        lk = jnp.where(valid, lk, 0.0)
    later, tot = _suffix_sums(lk, upper)
    w = jnp.exp(z + lk + later + cl)
    if valid is not None:
        w = jnp.where(valid, w, 0.0)
    return z, lk, w, tot


def _sb_masks(kb):
    half = kb // 2
    r = lax.broadcasted_iota(jnp.int32, (half, half), 0)
    c = lax.broadcasted_iota(jnp.int32, (half, half), 1)
    rq = lax.broadcasted_iota(jnp.int32, (SB_QB, kb), 0)
    ck = lax.broadcasted_iota(jnp.int32, (SB_QB, kb), 1)
    return (r > c).astype(BF16), (r >= c).astype(BF16), ck - rq


def _sb_fwd(q, kt, v, gq, gkt, name):
    H, S, dh = q.shape
    kb = min(SB_KEYS, S)
    per = kb // SB_QB
    nb = S // SB_QB
    scale = 1.0 / math.sqrt(dh)

    def body(q_ref, kt_ref, v_ref, gq_ref, gkt_ref, o_ref, qa, kat, vb):
        qa[...] = _qk_norm(q_ref[0], gq_ref[...], scale).astype(BF16)
        kat[...] = _qk_norm_t(kt_ref[0], gkt_ref[...], 1.0).astype(BF16)
        vb[...] = v_ref[0].astype(BF16)
        upper, _, diff = _sb_masks(kb)

        def qblock(i, _):
            rows = pl.ds(pl.multiple_of(i * SB_QB, SB_QB), SB_QB)
            qi = qa[rows, :]
            sbd = i // per

            def step(sb, cl, acc, valid):
                cols = pl.ds(pl.multiple_of(sb * kb, kb), kb)
                _, _, w, tot = _sb_step(qi, kat[:, cols], cl, upper, valid)
                return cl + tot, acc + jnp.dot(w.astype(BF16), vb[cols, :], preferred_element_type=F32)

            cl, acc = step(sbd, jnp.zeros((SB_QB, 1), F32), jnp.zeros((SB_QB, dh), F32),
                           diff < (i - sbd * per) * SB_QB)
            _, acc = lax.fori_loop(0, sbd, lambda jj, c: step(sbd - 1 - jj, c[0], c[1], None), (cl, acc))
            o_ref[0, rows, :] = acc
            return 0

        lax.fori_loop(0, nb, qblock, 0)

    hs = pl.BlockSpec((1, S, dh), lambda h: (h, 0, 0))
    ts = pl.BlockSpec((1, dh, S), lambda h: (h, 0, 0))
    return pl.pallas_call(
        body, name=name, grid=(H,),
        in_specs=[hs, ts, hs, pl.BlockSpec((1, dh), lambda h: (0, 0)), pl.BlockSpec((dh, 1), lambda h: (0, 0))],
        out_specs=hs, out_shape=jax.ShapeDtypeStruct((H, S, dh), F32),
        scratch_shapes=[pltpu.VMEM((S, dh), BF16), pltpu.VMEM((dh, S), BF16), pltpu.VMEM((S, dh), BF16)],
        compiler_params=_cp("parallel"),
    )(q, kt, v, gq, gkt)


def _sb_bwd(q, qt, k, kt, v, vt, gq, gqt, gk, gkt, o, do, dot_, name):
    H, S, dh = q.shape
    kb = min(SB_KEYS, S)
    per = kb // SB_QB
    nb = S // SB_QB
    scale = 1.0 / math.sqrt(dh)

    def body(q_ref, qt_ref, k_ref, kt_ref, v_ref, vt_ref, gq_ref, gqt_ref, gk_ref, gkt_ref, o_ref, do_ref, dot_ref,
             dq_ref, dkt_ref, dvt_ref, dgq_ref, dgkt_ref, qa, qat, ka, kat, vb, vtb, dob, dotb, dqa, dkat):
        qa[...] = _qk_norm(q_ref[0], gq_ref[...], scale).astype(BF16)
        qat[...] = _qk_norm_t(qt_ref[0], gqt_ref[...], scale).astype(BF16)
        ka[...] = _qk_norm(k_ref[0], gk_ref[...], 1.0).astype(BF16)
        kat[...] = _qk_norm_t(kt_ref[0], gkt_ref[...], 1.0).astype(BF16)
        vb[...] = v_ref[0].astype(BF16)
        vtb[...] = vt_ref[0].astype(BF16)
        dob[...] = do_ref[0].astype(BF16)
        dotb[...] = dot_ref[0].astype(BF16)
        dkat[...] = jnp.zeros_like(dkat)
        dvt_ref[...] = jnp.zeros_like(dvt_ref)
        upper, lower_incl, diff = _sb_masks(kb)

        def qblock(i, _):
            rows = pl.ds(pl.multiple_of(i * SB_QB, SB_QB), SB_QB)
            qi, qit = qa[rows, :], qat[:, rows]
            doi, doit = dob[rows, :], dotb[:, rows]
            total = jnp.sum(doi.astype(F32) * o_ref[0, rows, :], axis=1, keepdims=True)
            sbd = i // per

            def step(sb, cl, cd, dqi, valid):
                cols = pl.ds(pl.multiple_of(sb * kb, kb), kb)
                z, lk, w, tot = _sb_step(qi, kat[:, cols], cl, upper, valid)
                w16 = w.astype(BF16)
                dl = jnp.dot(doi, vtb[:, cols], preferred_element_type=F32) * w16.astype(F32)
                incl, dtot = _suffix_sums(dl, lower_incl)
                sig = jnp.exp(z + lk)
                dz = dl - sig * (dl + (total - cd - incl))
                if valid is not None:
                    dz = jnp.where(valid, dz, 0.0)
                dz16 = dz.astype(BF16)
                dqi = dqi + jnp.dot(dz16, ka[cols, :], preferred_element_type=F32)
                dkat[:, cols] += jnp.dot(qit, dz16, preferred_element_type=F32)
                dvt_ref[0, :, cols] += jnp.dot(doit, w16, preferred_element_type=F32)
                return cl + tot, cd + dtot, dqi

            zero = jnp.zeros((SB_QB, 1), F32)
            carry = step(sbd, zero, zero, jnp.zeros((SB_QB, dh), F32), diff < (i - sbd * per) * SB_QB)
            _, _, dqi = lax.fori_loop(0, sbd, lambda jj, c: step(sbd - 1 - jj, c[0], c[1], c[2], None), carry)
            dqa[rows, :] = dqi
            return 0

        lax.fori_loop(0, nb, qblock, 0)
        _, vq = jax.vjp(lambda t, g: _qk_norm(t, g, scale), q_ref[0], gq_ref[...])
        dq, dgq = vq(dqa[...])
        dq_ref[0] = dq
        dgq_ref[0] = dgq
        _, vk = jax.vjp(lambda t, g: _qk_norm_t(t, g, 1.0), kt_ref[0], gkt_ref[...])
        dkt, dgkt = vk(dkat[...])
        dkt_ref[0] = dkt
        dgkt_ref[0] = dgkt

    hs = pl.BlockSpec((1, S, dh), lambda h: (h, 0, 0))
    ts = pl.BlockSpec((1, dh, S), lambda h: (h, 0, 0))
    gr = pl.BlockSpec((1, dh), lambda h: (0, 0))
    gc = pl.BlockSpec((dh, 1), lambda h: (0, 0))
    sd = jax.ShapeDtypeStruct((H, S, dh), F32)
    td = jax.ShapeDtypeStruct((H, dh, S), F32)
    return pl.pallas_call(
        body, name=name, grid=(H,),
        in_specs=[hs, ts, hs, ts, hs, ts, gr, gc, gr, gc, hs, hs, ts],
        out_specs=[hs, ts, ts, pl.BlockSpec((1, 1, dh), lambda h: (h, 0, 0)), pl.BlockSpec((1, dh, 1), lambda h: (h, 0, 0))],
        out_shape=[sd, td, td, jax.ShapeDtypeStruct((H, 1, dh), F32), jax.ShapeDtypeStruct((H, dh, 1), F32)],
        scratch_shapes=[pltpu.VMEM((S, dh), BF16), pltpu.VMEM((dh, S), BF16)] * 4
        + [pltpu.VMEM((S, dh), F32), pltpu.VMEM((dh, S), F32)],
        compiler_params=_cp("parallel", vmem=60 * 1024 * 1024),
    )(q, qt, k, kt, v, vt, gq, gqt, gk, gkt, o, do, dot_)


def _shift_down(x, s, rows):
    if s == 0:
        return x
    return jnp.where(rows >= s, pltpu.roll(x, s, 0), 0.0)


def _shift_up(x, s, rows, n):
    if s == 0:
        return x
    return jnp.where(rows < n - s, pltpu.roll(x, n - s, 0), 0.0)


def _conv(x, w_ref, rows):
    y = x * w_ref[CONV_K - 1:CONV_K, :]
    for kk in range(CONV_K - 1):
        y = y + _shift_down(x, CONV_K - 1 - kk, rows) * w_ref[kk:kk + 1, :]
    return y


def _act_norm(y, normed):
    s = _silu(y)
    n = s * lax.rsqrt(jnp.sum(s * s, axis=-1, keepdims=True) + EPS)
    return jnp.where(normed, n, s)


def _dn_prep_fwd(p, conv_w, col0, name):
    S = p.shape[0]
    nblk = 3 * DN_HEADS
    b0 = col0 // DN_DH

    def body(x_ref, w_ref, o_ref):
        rows = lax.broadcasted_iota(jnp.int32, (S, DN_DH), 0)
        y = _conv(x_ref[...], w_ref, rows)
        o_ref[...] = _act_norm(y, pl.program_id(0) < 2 * DN_HEADS)

    return pl.pallas_call(
        body, name=name, grid=(nblk,),
        in_specs=[pl.BlockSpec((S, DN_DH), lambda j: (0, b0 + j)), pl.BlockSpec((CONV_K, DN_DH), lambda j: (0, j))],
        out_specs=pl.BlockSpec((S, DN_DH), lambda j: (0, j)),
        out_shape=jax.ShapeDtypeStruct((S, 3 * DN_W), F32),
        compiler_params=_cp("parallel"),
    )(p, conv_w)


def _dn_prep_bwd(p, conv_w, col0, dout, name):
    S = p.shape[0]
    nblk = 3 * DN_HEADS
    b0 = col0 // DN_DH

    def body(x_ref, w_ref, do_ref, dx_ref, dw_ref):
        rows = lax.broadcasted_iota(jnp.int32, (S, DN_DH), 0)
        x = x_ref[...]
        y = _conv(x, w_ref, rows)
        normed = pl.program_id(0) < 2 * DN_HEADS
        _, vjp = jax.vjp(lambda t: _act_norm(t, normed), y)
        (dy,) = vjp(do_ref[...])
        dx = dy * w_ref[CONV_K - 1:CONV_K, :]
        dw_ref[CONV_K - 1:CONV_K, :] = jnp.sum(dy * x, axis=0, keepdims=True)
        for kk in range(CONV_K - 1):
            s = CONV_K - 1 - kk
            dx = dx + _shift_up(dy, s, rows, S) * w_ref[kk:kk + 1, :]
            dw_ref[kk:kk + 1, :] = jnp.sum(dy * _shift_down(x, s, rows), axis=0, keepdims=True)
        dx_ref[...] = dx.astype(BF16)

    return pl.pallas_call(
        body, name=name, grid=(nblk,),
        in_specs=[pl.BlockSpec((S, DN_DH), lambda j: (0, b0 + j)), pl.BlockSpec((CONV_K, DN_DH), lambda j: (0, j)),
                  pl.BlockSpec((S, DN_DH), lambda j: (0, j))],
        out_specs=[pl.BlockSpec((S, DN_DH), lambda j: (0, j)), pl.BlockSpec((CONV_K, DN_DH), lambda j: (0, j))],
        out_shape=[jax.ShapeDtypeStruct((S, 3 * DN_W), BF16), jax.ShapeDtypeStruct((CONV_K, 3 * DN_W), F32)],
        compiler_params=_cp("parallel"),
    )(p, conv_w, dout)


def _gate_fn(x, pv):
    lane = lax.broadcasted_iota(jnp.int32, x.shape, 1)
    decay = -jnp.exp(pv[0:1, :]) * _softplus(x + pv[1:2, :])
    return jnp.where(lane < DN_HEADS, _sigmoid(x), decay)


def _dn_gate_fwd(p, pv, blk, name):
    S = p.shape[0]

    def body(x_ref, pv_ref, o_ref):
        o_ref[...] = _gate_fn(x_ref[...], pv_ref[...])

    return pl.pallas_call(
        body, name=name, grid=(1,),
        in_specs=[pl.BlockSpec((S, LANES), lambda i: (0, blk)), pl.BlockSpec((2, LANES), lambda i: (0, 0))],
        out_specs=pl.BlockSpec((S, LANES), lambda i: (0, 0)),
        out_shape=jax.ShapeDtypeStruct((S, LANES), F32),
        compiler_params=_cp("arbitrary"),
    )(p, pv)


def _dn_gate_bwd(p, pv, blk, dout, name):
    S = p.shape[0]

    def body(x_ref, pv_ref, do_ref, dx_ref, dpv_ref):
        _, vjp = jax.vjp(_gate_fn, x_ref[...], pv_ref[...])
        dx, dpv = vjp(do_ref[...])
        dx_ref[...] = dx.astype(BF16)
        dpv_ref[...] = dpv

    return pl.pallas_call(
        body, name=name, grid=(1,),
        in_specs=[pl.BlockSpec((S, LANES), lambda i: (0, blk)), pl.BlockSpec((2, LANES), lambda i: (0, 0)),
                  pl.BlockSpec((S, LANES), lambda i: (0, 0))],
        out_specs=[pl.BlockSpec((S, LANES), lambda i: (0, 0)), pl.BlockSpec((2, LANES), lambda i: (0, 0))],
        out_shape=[jax.ShapeDtypeStruct((S, LANES), BF16), jax.ShapeDtypeStruct((2, LANES), F32)],
        compiler_params=_cp("arbitrary"),
    )(p, pv, dout)


def _t(x):
    return jnp.swapaxes(x, -1, -2)


def _matmuls(prec, differentiable):
    def mm(a, b):
        return lax.dot_general(a, b, (((2,), (1,)), ((0,), (0,))), precision=prec, preferred_element_type=F32)

    def mm_nt(a, b):
        return lax.dot_general(a, b, (((2,), (2,)), ((0,), (0,))), precision=prec, preferred_element_type=F32)

    if not differentiable:
        return mm, mm_nt
    dmm, dmm_nt = jax.custom_vjp(mm), jax.custom_vjp(mm_nt)
    dmm.defvjp(lambda a, b: (mm(a, b), (a, b)), lambda res, g: (mm_nt(g, res[1]), mm(_t(res[0]), g)))
    dmm_nt.defvjp(lambda a, b: (mm_nt(a, b), (a, b)), lambda res, g: (mm(g, res[1]), mm(_t(g), res[0])))
    return dmm, dmm_nt


def _delta_chunk(state, q, k, v, beta, a_col, a_row, differentiable=False):
    mm, mm_nt = _matmuls(lax.Precision.HIGH, differentiable)
    mm_sum, _ = _matmuls(HI, differentiable)
    H, C, _ = q.shape
    r = lax.broadcasted_iota(jnp.int32, (H, C, C), 1)
    c = lax.broadcasted_iota(jnp.int32, (H, C, C), 2)
    tril, strict = r >= c, r > c
    eye = (r == c).astype(F32)
    g_c = mm_sum(tril.astype(F32), jnp.broadcast_to(a_col, (H, C, C)))
    g_r = mm_sum(jnp.broadcast_to(a_row, (H, C, C)), (r <= c).astype(F32))
    decay = jnp.where(tril, jnp.exp(jnp.where(tril, g_c - g_r, 0.0)), 0.0)
    eg = jnp.exp(g_c)
    g_last = jnp.sum(jnp.where(r == C - 1, g_c, 0.0), axis=1, keepdims=True)
    qs = q * (float(q.shape[2]) ** -0.5)
    kb = k * beta
    neg_m = jnp.where(strict, -(mm_nt(kb, k) * decay), 0.0)
    inv = eye + neg_m
    pw = neg_m
    for _ in range(int(math.log2(C)) - 1):
        pw = mm(pw, pw)
        inv = inv + mm(inv, pw)
    u = mm(inv, v * beta)
    w = mm(inv, kb * eg)
    intra = jnp.where(tril, mm_nt(qs, k) * decay, 0.0)
    v_new = u - mm(w, state)
    o = mm(qs * eg, state) + mm(intra, v_new)
    nxt = state * jnp.exp(g_last) + mm(_t(k * jnp.exp(g_last - g_c)), v_new)
    return o, nxt


def _heads(t):
    return jnp.stack([t[:, h * DN_DH:(h + 1) * DN_DH] for h in range(DN_HEADS)])


def _delta_step(state, q, k, v, bg, a_row, differentiable=False):
    lane = lax.broadcasted_iota(jnp.int32, bg.shape, 1)
    pick = lambda j: jnp.stack([jnp.sum(jnp.where(lane == j + h, bg, 0.0), axis=1, keepdims=True)
                                for h in range(DN_HEADS)])
    o, nxt = _delta_chunk(state, _heads(q), _heads(k), _heads(v), pick(0), pick(DN_HEADS), a_row, differentiable)
    return jnp.concatenate([o[h] for h in range(DN_HEADS)], axis=1), nxt


def _delta_fwd(qkv, bg, a_row, name):
    S = qkv.shape[0]
    nc = S // BLK

    def body(q_ref, k_ref, v_ref, bg_ref, ar_ref, o_ref, st_ref, state):
        ci = pl.program_id(0)

        @pl.when(ci == 0)
        def _():
            state[...] = jnp.zeros_like(state)

        st = state[...]
        st_ref[:, 0] = st
        o, nxt = _delta_step(st, q_ref[...], k_ref[...], v_ref[...], bg_ref[...], ar_ref[:, pl.ds(ci, 1), :])
        o_ref[...] = o
        state[...] = nxt

    part = lambda j: pl.BlockSpec((BLK, DN_W), lambda c: (c, j))
    return pl.pallas_call(
        body, name=name, grid=(nc,),
        in_specs=[part(0), part(1), part(2), pl.BlockSpec((BLK, LANES), lambda c: (c, 0)),
                  pl.BlockSpec((DN_HEADS, nc, BLK), lambda c: (0, 0, 0))],
        out_specs=[part(0), pl.BlockSpec((DN_HEADS, 1, DN_DH, DN_DH), lambda c: (0, c, 0, 0))],
        out_shape=[jax.ShapeDtypeStruct((S, DN_W), F32), jax.ShapeDtypeStruct((DN_HEADS, nc, DN_DH, DN_DH), F32)],
        scratch_shapes=[pltpu.VMEM((DN_HEADS, DN_DH, DN_DH), F32)],
        compiler_params=_cp("arbitrary"),
    )(qkv, qkv, qkv, bg, a_row)


def _delta_bwd(qkv, bg, a_row, states, do, name):
    S = qkv.shape[0]
    nc = S // BLK

    def body(q_ref, k_ref, v_ref, bg_ref, ar_ref, st_ref, do_ref, dq_ref, dk_ref, dv_ref, dbg_ref, dar_ref, dstate):
        t = pl.program_id(0)
        ci = nc - 1 - t

        @pl.when(t == 0)
        def _():
            dstate[...] = jnp.zeros_like(dstate)

        _, vjp = jax.vjp(functools.partial(_delta_step, differentiable=True), st_ref[:, 0], q_ref[...], k_ref[...],
                         v_ref[...], bg_ref[...], ar_ref[:, pl.ds(ci, 1), :])
        dprev, dq, dk, dv, dbg, dar = vjp((do_ref[...], dstate[...]))
        dq_ref[...] = dq
        dk_ref[...] = dk
        dv_ref[...] = dv
        dbg_ref[...] = dbg
        dar_ref[:, pl.ds(ci, 1), :] = dar
        dstate[...] = dprev

    part = lambda j: pl.BlockSpec((BLK, DN_W), lambda t: (nc - 1 - t, j))
    lanes = pl.BlockSpec((BLK, LANES), lambda t: (nc - 1 - t, 0))
    rows = pl.BlockSpec((DN_HEADS, nc, BLK), lambda t: (0, 0, 0))
    d3 = jax.ShapeDtypeStruct((S, DN_W), F32)
    return pl.pallas_call(
        body, name=name, grid=(nc,),
        in_specs=[part(0), part(1), part(2), lanes, rows,
                  pl.BlockSpec((DN_HEADS, 1, DN_DH, DN_DH), lambda t: (0, nc - 1 - t, 0, 0)), part(0)],
        out_specs=[part(0), part(0), part(0), lanes, rows],
        out_shape=[d3, d3, d3, jax.ShapeDtypeStruct((S, LANES), F32), jax.ShapeDtypeStruct((DN_HEADS, nc, BLK), F32)],
        scratch_shapes=[pltpu.VMEM((DN_HEADS, DN_DH, DN_DH), F32)],
        compiler_params=_cp("arbitrary"),
    )(qkv, qkv, qkv, bg, a_row, states, do)


def _gate_sb(o, z):
    return o * _silu(z)


def _gate_dn(o, z, g):
    return jnp.concatenate(
        [_rms(o[:, h * DN_DH:(h + 1) * DN_DH]) * g * _silu(z[:, h * DN_DH:(h + 1) * DN_DH]) for h in range(DN_HEADS)],
        axis=1)


def _merge_specs(S, D, tm):
    row = lambda w, blk: pl.BlockSpec((tm, w), lambda i: (i, blk))
    full = lambda a, b: pl.BlockSpec((a, b), lambda i: (0, 0))
    return [row(D, 0), full(1, D), row(SB_W, 0), row(SB_W, 3), row(DN_W, 0), row(DN_W, 7),
            row(D, IN_MAIN // D), row(D, IN_MAIN // D + 1), full(1, DN_DH), full(SB_W, D), full(DN_W, D), full(D, D)]


def _merge_fwd(x, gate, o_sb, o_dn, p, ng, wbs, wbd, wo, name):
    S, D = x.shape
    tm = min(512, S)

    def body(x_ref, gate_ref, osb_ref, zsb_ref, odn_ref, zdn_ref, msb_ref, mdn_ref, ng_ref, wbs_ref, wbd_ref, wo_ref,
             out_ref):
        a = _gate_sb(osb_ref[...], zsb_ref[...])
        b = _gate_dn(odn_ref[...], zdn_ref[...], ng_ref[...])
        y = _sigmoid(msb_ref[...]) * _bdot(a, wbs_ref[...]) + _sigmoid(mdn_ref[...]) * _bdot(b, wbd_ref[...])
        out_ref[...] = x_ref[...] + gate_ref[...] * _bdot(y, wo_ref[...])

    return pl.pallas_call(
        body, name=name, grid=(S // tm,), in_specs=_merge_specs(S, D, tm),
        out_specs=pl.BlockSpec((tm, D), lambda i: (i, 0)), out_shape=jax.ShapeDtypeStruct((S, D), F32),
        compiler_params=_cp("parallel"),
    )(x, gate, o_sb, p, o_dn, p, p, p, ng, wbs, wbd, wo)


def _merge_bwd(dxn, gate, o_sb, o_dn, p, ng, wbs, wbd, wo, name):
    S, D = dxn.shape
    tm = min(256, S)

    def body(dxn_ref, gate_ref, osb_ref, zsb_ref, odn_ref, zdn_ref, msb_ref, mdn_ref, ng_ref, wbs_ref, wbd_ref, wo_ref,
             dosb_ref, dzsb_ref, dodn_ref, dzdn_ref, dmsb_ref, dmdn_ref, dwo_ref, dwbs_ref, dwbd_ref, dgate_ref, dng_ref):
        @pl.when(pl.program_id(0) == 0)
        def _():
            for ref in (dwo_ref, dwbs_ref, dwbd_ref, dgate_ref, dng_ref):
                ref[...] = jnp.zeros_like(ref)

        a, vjp_a = jax.vjp(_gate_sb, osb_ref[...], zsb_ref[...])
        b, vjp_b = jax.vjp(_gate_dn, odn_ref[...], zdn_ref[...], ng_ref[...])
        a16, b16 = a.astype(BF16), b.astype(BF16)
        ps = jnp.dot(a16, wbs_ref[...], preferred_element_type=F32)
        pd = jnp.dot(b16, wbd_ref[...], preferred_element_type=F32)
        ss, sd = _sigmoid(msb_ref[...]), _sigmoid(mdn_ref[...])
        y16 = (ss * ps + sd * pd).astype(BF16)
        out = jnp.dot(y16, wo_ref[...], preferred_element_type=F32)
        dxn_v = dxn_ref[...]
        dgate_ref[...] += jnp.sum(dxn_v * out, axis=0, keepdims=True)
        dout16 = (dxn_v * gate_ref[...]).astype(BF16)
        dwo_ref[...] += _bdot_tn(y16, dout16)
        dy = _bdot_nt(dout16, wo_ref[...])
        dmsb_ref[...] = (dy * ps * ss * (1.0 - ss)).astype(BF16)
        dmdn_ref[...] = (dy * pd * sd * (1.0 - sd)).astype(BF16)
        dps16, dpd16 = (dy * ss).astype(BF16), (dy * sd).astype(BF16)
        dwbs_ref[...] += _bdot_tn(a16, dps16)
        dwbd_ref[...] += _bdot_tn(b16, dpd16)
        dosb, dzsb = vjp_a(_bdot_nt(dps16, wbs_ref[...]))
        dodn, dzdn, dng = vjp_b(_bdot_nt(dpd16, wbd_ref[...]))
        dosb_ref[...] = dosb
        dzsb_ref[...] = dzsb.astype(BF16)
        dodn_ref[...] = dodn
        dzdn_ref[...] = dzdn.astype(BF16)
        dng_ref[...] += dng

    row = lambda w: pl.BlockSpec((tm, w), lambda i: (i, 0))
    full = lambda a, b: pl.BlockSpec((a, b), lambda i: (0, 0))
    sds = jax.ShapeDtypeStruct
    return pl.pallas_call(
        body, name=name, grid=(S // tm,), in_specs=_merge_specs(S, D, tm),
        out_specs=[row(SB_W), row(SB_W), row(DN_W), row(DN_W), row(D), row(D),
                   full(D, D), full(SB_W, D), full(DN_W, D), full(1, D), full(1, DN_DH)],
        out_shape=[sds((S, SB_W), F32), sds((S, SB_W), BF16), sds((S, DN_W), F32), sds((S, DN_W), BF16),
                   sds((S, D), BF16), sds((S, D), BF16), sds((D, D), F32), sds((SB_W, D), F32), sds((DN_W, D), F32),
                   sds((1, D), F32), sds((1, DN_DH), F32)],
        compiler_params=_cp("arbitrary"),
    )(dxn, gate, o_sb, p, o_dn, p, p, p, ng, wbs, wbd, wo)


def _loss_fwd_bwd(y, target, name):
    S, D = y.shape
    tm = min(512, S)

    def body(y_ref, t_ref, l_ref, dy_ref):
        @pl.when(pl.program_id(0) == 0)
        def _():
            l_ref[...] = jnp.zeros_like(l_ref)

        e = y_ref[...] - t_ref[...]
        l_ref[...] += jnp.sum(e * e, axis=0, keepdims=True) * (0.5 / D)
        dy_ref[...] = e * (1.0 / D)

    row = pl.BlockSpec((tm, D), lambda i: (i, 0))
    return pl.pallas_call(
        body, name=name, grid=(S // tm,), in_specs=[row, row],
        out_specs=[pl.BlockSpec((1, D), lambda i: (0, 0)), row],
        out_shape=[jax.ShapeDtypeStruct((1, D), F32), jax.ShapeDtypeStruct((S, D), F32)],
        compiler_params=_cp("arbitrary"),
    )(y, target)


def _mod_fwd(c_all, ada_w, name):
    L, D, n = ada_w.shape

    def body(c_ref, w_ref, o_ref):
        o_ref[0] = _dot(_silu(c_ref[...]), w_ref[0])

    return pl.pallas_call(
        body, name=name, grid=(L,),
        in_specs=[pl.BlockSpec(c_all.shape, lambda l: (0, 0)), pl.BlockSpec((1, D, n), lambda l: (l, 0, 0))],
        out_specs=pl.BlockSpec((1, N_DEV, n), lambda l: (l, 0, 0)),
        out_shape=jax.ShapeDtypeStruct((L, N_DEV, n), F32),
        compiler_params=_cp("parallel"),
    )(c_all, ada_w)


def _mod_bwd_w(c_all_t, dmod, name):
    L, _, n = dmod.shape
    D = c_all_t.shape[0]

    def body(c_ref, d_ref, o_ref):
        o_ref[0] = _dot(_silu(c_ref[...]), d_ref[0])

    return pl.pallas_call(
        body, name=name, grid=(L,),
        in_specs=[pl.BlockSpec(c_all_t.shape, lambda l: (0, 0)), pl.BlockSpec((1, N_DEV, n), lambda l: (l, 0, 0))],
        out_specs=pl.BlockSpec((1, D, n), lambda l: (l, 0, 0)),
        out_shape=jax.ShapeDtypeStruct((L, D, n), F32),
        compiler_params=_cp("parallel"),
    )(c_all_t, dmod)


def _me():
    return lax.axis_index("x"), lax.axis_index("y"), lax.axis_index("c")


def _peer(k):
    x, y, c = _me()
    return (1 - x if k & 4 else x, 1 - y if k & 2 else y, 1 - c if k & 1 else c)


def _lin(dev):
    return 4 * dev[0] + 2 * dev[1] + dev[2]


def _exchange(arrays, scatter, name):
    n_arr = len(arrays)

    def body(*refs):
        ins, outs = refs[:n_arr], refs[n_arr:2 * n_arr]
        send_sems, recv_sems, local_sems = refs[2 * n_arr:]
        me = _lin(_me())
        local, remote = [], []
        for t in range(n_arr):
            src_of = (lambda d, t=t: ins[t].at[d]) if scatter else (lambda d, t=t: ins[t])
            mine = pltpu.make_async_copy(src_of(me), outs[t].at[me], local_sems.at[t])
            mine.start()
            local.append(mine)
            for k in range(1, N_DEV):
                peer = _peer(k)
                cp = pltpu.make_async_remote_copy(
                    src_ref=src_of(_lin(peer)), dst_ref=outs[t].at[me], send_sem=send_sems.at[t, k - 1],
                    recv_sem=recv_sems.at[t, k - 1], device_id=peer, device_id_type=pl.DeviceIdType.MESH)
                cp.start()
                remote.append(cp)
        for t in range(n_arr):
            for k in range(1, N_DEV):
                peer = _peer(k)
                landed = outs[t].at[_lin(peer)]
                pltpu.make_async_remote_copy(
                    src_ref=landed, dst_ref=landed, send_sem=send_sems.at[t, k - 1],
                    recv_sem=recv_sems.at[t, k - 1], device_id=peer, device_id_type=pl.DeviceIdType.MESH).wait_recv()
        for cp in remote:
            cp.wait_send()
        for cp in local:
            cp.wait()

    out_shape = [jax.ShapeDtypeStruct((N_DEV,) + tuple(a.shape[1:] if scatter else a.shape), a.dtype) for a in arrays]
    hbm = pl.BlockSpec(memory_space=pl.ANY)
    return pl.pallas_call(
        body, name=name, in_specs=[hbm] * n_arr, out_specs=[hbm] * n_arr, out_shape=out_shape,
        scratch_shapes=[pltpu.SemaphoreType.DMA((n_arr, N_DEV - 1)), pltpu.SemaphoreType.DMA((n_arr, N_DEV - 1)),
                        pltpu.SemaphoreType.DMA((n_arr,))],
    )(*arrays)


def _sum_slots(a, name):
    _, R, C = a.shape
    tr = SUM_ROWS if R % SUM_ROWS == 0 else R

    def body(a_ref, o_ref):
        acc = a_ref[0].astype(F32)
        for s in range(1, N_DEV):
            acc = acc + a_ref[s].astype(F32)
        o_ref[...] = acc

    return pl.pallas_call(
        body, name=name, grid=(R // tr,),
        in_specs=[pl.BlockSpec((N_DEV, tr, C), lambda i: (0, i, 0))], out_specs=pl.BlockSpec((tr, C), lambda i: (i, 0)),
        out_shape=jax.ShapeDtypeStruct((R, C), F32), compiler_params=_cp("parallel"),
    )(a)


def _adamw(w, g, m, v, name):
    shape = w.shape
    C = shape[-1]
    R = w.size // C
    tr = R
    for cand in (256, 128, 64):
        if R > cand and R % cand == 0:
            tr = cand
            break
    c1 = 1.0 / (1.0 - ADAM_B1 ** ADAM_STEP)
    c2 = 1.0 / (1.0 - ADAM_B2 ** ADAM_STEP)

    def body(w_ref, g_ref, m_ref, v_ref, d_ref, nm_ref, nv_ref):
        gv = g_ref[...]
        nm = ADAM_B1 * m_ref[...] + (1.0 - ADAM_B1) * gv
        nv = ADAM_B2 * v_ref[...] + (1.0 - ADAM_B2) * (gv * gv)
        d_ref[...] = -ADAM_LR * ((nm * c1) / (jnp.sqrt(nv * c2) + ADAM_EPS) + ADAM_WD * w_ref[...])
        nm_ref[...] = nm
        nv_ref[...] = nv

    spec = pl.BlockSpec((tr, C), lambda i: (i, 0))
    sd = jax.ShapeDtypeStruct((R, C), F32)
    outs = pl.pallas_call(
        body, name=name, grid=(R // tr,), in_specs=[spec] * 4, out_specs=[spec] * 3, out_shape=[sd] * 3,
        compiler_params=_cp("parallel"),
    )(*(t.reshape(R, C) for t in (w, g, m, v)))
    return tuple(t.reshape(shape) for t in outs)


def _to_heads(t):
    S = t.shape[0]
    return t.reshape(S, SB_HEADS, SB_DH).transpose(1, 0, 2)


def _from_heads(t):
    S = t.shape[1]
    return t.transpose(1, 0, 2).reshape(S, SB_W)


def _to_heads_t(t):
    S = t.shape[0]
    return t.reshape(S, SB_HEADS, SB_DH).transpose(1, 2, 0)


def _from_heads_t(t):
    S = t.shape[2]
    return t.transpose(2, 0, 1).reshape(S, SB_W)


def _pad_cols(w_in):
    D = w_in.shape[0]
    pad = jnp.zeros((D, LANES - 2 * DN_HEADS), w_in.dtype)
    return jnp.concatenate([w_in[:, :IN_MAIN], w_in[:, IN_COLS:], w_in[:, IN_MAIN:IN_COLS], pad], axis=1)


def _unpad_cols(dw, D):
    return jnp.concatenate([dw[:, :IN_MAIN], dw[:, IN_MAIN + 2 * D:IN_MAIN + 2 * D + 2 * DN_HEADS],
                            dw[:, IN_MAIN:IN_MAIN + 2 * D]], axis=1)


def _gate_params(a_log, dt_bias):
    z = jnp.zeros((LANES,), F32)
    return jnp.stack([z.at[DN_HEADS:2 * DN_HEADS].set(a_log), z.at[DN_HEADS:2 * DN_HEADS].set(dt_bias)])


def _layer_fwd(l, x, mod, wts):
    S, D = x.shape
    tag = f"l{l}_"
    p, h = _inproj_fwd(x, mod, wts["norm_g"], wts["w_in"], tag + "inproj_fwd")
    q, k, v = (_to_heads(p[:, i * SB_W:(i + 1) * SB_W]) for i in range(3))
    qt, kt, vt = (_to_heads_t(p[:, i * SB_W:(i + 1) * SB_W]) for i in range(3))
    o_sb_h = _sb_fwd(q, kt, v, wts["sb_q_g"], wts["sb_k_g"].T, tag + "sb_fwd")
    qkv = _dn_prep_fwd(p, wts["conv_w"], 4 * SB_W, tag + "dn_prep_fwd")
    pv = _gate_params(wts["dn_a_log"], wts["dn_dt_bias"])
    ba_blk = (IN_MAIN + 2 * D) // LANES
    bg = _dn_gate_fwd(p, pv, ba_blk, tag + "dn_gate_fwd")
    a_row = bg[:, DN_HEADS:2 * DN_HEADS].T.reshape(DN_HEADS, S // BLK, BLK)
    o_dn, states = _delta_fwd(qkv, bg, a_row, tag + "delta_fwd")
    o_sb = _from_heads(o_sb_h)
    gate = mod[:, 2 * D:]
    out = _merge_fwd(x, gate, o_sb, o_dn, p, wts["dn_norm_g"], wts["w_branch_sb"], wts["w_branch_dn"], wts["w_out"],
                     tag + "merge_fwd")
    saved = dict(x=x, mod=mod, p=p, h=h, q=q, k=k, v=v, qt=qt, kt=kt, vt=vt, o_sb_h=o_sb_h, o_sb=o_sb, qkv=qkv, pv=pv, bg=bg,
                 a_row=a_row, o_dn=o_dn, states=states, gate=gate)
    return out, saved


def _layer_bwd(l, dxn, sv, wts):
    S, D = dxn.shape
    tag = f"l{l}_"
    (dosb, dzsb, dodn, dzdn, dmsb, dmdn, dwo, dwbs, dwbd, dgate, dng) = _merge_bwd(
        dxn, sv["gate"], sv["o_sb"], sv["o_dn"], sv["p"], wts["dn_norm_g"], wts["w_branch_sb"], wts["w_branch_dn"],
        wts["w_out"], tag + "merge_bwd")
    dq, dkt, dvt, dgq, dgkt = _sb_bwd(sv["q"], sv["qt"], sv["k"], sv["kt"], sv["v"], sv["vt"], wts["sb_q_g"],
                                       wts["sb_q_g"].T, wts["sb_k_g"], wts["sb_k_g"].T, sv["o_sb_h"],
                                       _to_heads(dosb), _to_heads_t(dosb), tag + "sb_bwd")
    dqkv_n_q, dqkv_n_k, dqkv_n_v, dbg, dar = _delta_bwd(sv["qkv"], sv["bg"], sv["a_row"], sv["states"], dodn,
                                                         tag + "delta_bwd")
    dqkv, dconv = _dn_prep_bwd(sv["p"], wts["conv_w"], 4 * SB_W,
                               jnp.concatenate([dqkv_n_q, dqkv_n_k, dqkv_n_v], axis=1), tag + "dn_prep_bwd")
    dbg = dbg.at[:, DN_HEADS:2 * DN_HEADS].add(dar.reshape(DN_HEADS, S).T)
    ba_blk = (IN_MAIN + 2 * D) // LANES
    dba, dpv = _dn_gate_bwd(sv["p"], sv["pv"], ba_blk, dbg, tag + "dn_gate_bwd")
    dp = jnp.concatenate([_from_heads(dq).astype(BF16), _from_heads_t(dkt).astype(BF16), _from_heads_t(dvt).astype(BF16),
                          dzsb, dqkv, dzdn, dmsb, dmdn, dba], axis=1)
    dx, dmod, dg = _inproj_bwd_dx(dp, wts["w_in"], sv["x"], sv["mod"], wts["norm_g"], dxn, tag + "inproj_bwd_dx")
    dw_in = _matmul_tn(sv["h"].T, dp, tag + "inproj_bwd_dw")
    dmod = dmod.at[:, 2 * D:].set(dgate)
    grads = dict(w_in=_unpad_cols(dw_in, D), w_branch_sb=dwbs, w_branch_dn=dwbd, w_out=dwo, conv_w=dconv,
                 mod=dmod[0], norm_g=dg[0], sb_q_g=jnp.sum(dgq, axis=0)[0], sb_k_g=jnp.sum(dgkt, axis=0)[:, 0],
                 dn_a_log=dpv[0, DN_HEADS:2 * DN_HEADS], dn_dt_bias=dpv[1, DN_HEADS:2 * DN_HEADS], dn_norm_g=dng[0])
    return dx, grads


def _pad_rows(a, mult):
    extra = (-a.shape[0]) % mult
    return a if extra == 0 else jnp.concatenate([a, jnp.zeros((extra,) + a.shape[1:], a.dtype)], axis=0)


def _pack_rows(parts, width, mult):
    flat = jnp.concatenate([t.reshape(-1) for t in parts])
    extra = (-flat.shape[0]) % width
    if extra:
        flat = jnp.concatenate([flat, jnp.zeros((extra,), flat.dtype)])
    return _pad_rows(flat.reshape(-1, width), mult)


def _take(flat, off, shape):
    n = math.prod(shape)
    return flat[..., off:off + n].reshape(flat.shape[:-1] + tuple(shape)), off + n


SMALL = ("mod", "norm_g", "sb_q_g", "sb_k_g", "dn_a_log", "dn_dt_bias", "dn_norm_g")


def kernel(x, c, ada_w, ada_b, norm_g, w_in, sb_q_g, sb_k_g, conv_w, dn_a_log, dn_dt_bias, dn_norm_g, w_branch_sb, w_branch_dn, w_out, loss_target, m_ada_w, m_ada_b, m_norm_g, m_w_in, m_sb_q_g, m_sb_k_g, m_conv_w, m_dn_a_log, m_dn_dt_bias, m_dn_norm_g, m_w_branch_sb, m_w_branch_dn, m_w_out, v_ada_w, v_ada_b, v_norm_g, v_w_in, v_sb_q_g, v_sb_k_g, v_conv_w, v_dn_a_log, v_dn_dt_bias, v_dn_norm_g, v_w_branch_sb, v_w_branch_dn, v_w_out):
    L, D = norm_g.shape
    S = x.shape[1]
    n_in = w_in.shape[2]
    n_ada = ada_w.shape[2]
    n_br = w_branch_sb.shape[2]
    n_out = w_out.shape[1]
    n_conv = conv_w.shape[2]
    me = _lin(_me())

    def cat(a):
        return jnp.concatenate([a[d] for d in range(N_DEV)], axis=1)

    c_all, conv_all = _exchange([c, conv_w.reshape(L * CONV_K, n_conv)], False, "gather_small")
    c_all = c_all.reshape(N_DEV, D)
    conv_full = cat(conv_all).reshape(L, CONV_K, N_DEV * n_conv)

    mod_part = _mod_fwd(c_all, ada_w, "mod_fwd")
    wi_all, wbs_all, wbd_all, wo_all, mod_all = _exchange(
        [w_in.astype(BF16).reshape(L * D, n_in), w_branch_sb.astype(BF16).reshape(L * SB_W, n_br),
         w_branch_dn.astype(BF16).reshape(L * DN_W, n_br), w_out.astype(BF16).reshape(L * n_out, D),
         mod_part.reshape(L * N_DEV, n_ada)], False, "gather_weights")
    w_in_full = cat(wi_all).reshape(L, D, N_DEV * n_in)
    wbs_full = cat(wbs_all).reshape(L, SB_W, N_DEV * n_br)
    wbd_full = cat(wbd_all).reshape(L, DN_W, N_DEV * n_br)
    wo_full = wo_all.reshape(N_DEV, L, n_out, D).transpose(1, 0, 2, 3).reshape(L, N_DEV * n_out, D)
    mod_full = cat(mod_all).reshape(L, N_DEV, N_DEV * n_ada) + ada_b[:, None, :]
    mod_mine = lax.dynamic_slice_in_dim(mod_full, me, 1, axis=1)

    wts = [dict(norm_g=norm_g[l:l + 1], w_in=_pad_cols(w_in_full[l]), sb_q_g=sb_q_g[l:l + 1], sb_k_g=sb_k_g[l:l + 1],
                conv_w=conv_full[l], dn_a_log=dn_a_log[l], dn_dt_bias=dn_dt_bias[l], dn_norm_g=dn_norm_g[l:l + 1],
                w_branch_sb=wbs_full[l], w_branch_dn=wbd_full[l], w_out=wo_full[l]) for l in range(L)]

    act = x[0]
    saved = []
    for l in range(L):
        act, sv = _layer_fwd(l, act, mod_mine[l], wts[l])
        saved.append(sv)
    loss_cols, dact = _loss_fwd_bwd(act, loss_target[0], "loss")
    loss = lax.psum(jnp.sum(loss_cols), ("x", "y", "c"))
    grads = [None] * L
    for l in reversed(range(L)):
        dact, grads[l] = _layer_bwd(l, dact, saved[l], wts[l])
    grad_x = dact[None]

    def col_blocks(name, n, dtype):
        return jnp.concatenate([jnp.stack([grads[l][name][:, d * n:(d + 1) * n].astype(dtype) for d in range(N_DEV)])
                                for l in range(L)], axis=1)

    send = dict(w_in=col_blocks("w_in", n_in, BF16), w_branch_sb=col_blocks("w_branch_sb", n_br, BF16),
                w_branch_dn=col_blocks("w_branch_dn", n_br, BF16), conv_w=col_blocks("conv_w", n_conv, F32),
                w_out=jnp.concatenate([grads[l]["w_out"].astype(BF16).reshape(N_DEV, n_out, D) for l in range(L)],
                                      axis=1))
    small_g = _pack_rows([grads[l][n] for l in range(L) for n in SMALL], LANES, 8)
    recv = _exchange(list(send.values()), True, "scatter_grads")
    (small_all_g,) = _exchange([small_g], False, "gather_small_grads")
    small_sum = _sum_slots(small_all_g, "sum_small_grads").reshape(-1)
    shard_shapes = dict(w_in=w_in.shape, w_branch_sb=w_branch_sb.shape, w_branch_dn=w_branch_dn.shape,
                        conv_w=conv_w.shape, w_out=w_out.shape)
    g_out = {n: _sum_slots(r, "sum_" + n).reshape(shard_shapes[n]) for n, r in zip(send, recv)}
    small_shapes = dict(mod=(3 * D,), norm_g=(D,), sb_q_g=(SB_DH,), sb_k_g=(SB_DH,), dn_a_log=(DN_HEADS,),
                        dn_dt_bias=(DN_HEADS,), dn_norm_g=(DN_DH,))
    off = 0
    off_all = 0
    small_each = small_all_g.reshape(N_DEV, -1)
    per_small = {n: [] for n in SMALL}
    dmod_all = []
    for l in range(L):
        for n in SMALL:
            t, off = _take(small_sum, off, small_shapes[n])
            per_small[n].append(t)
            if n == "mod":
                t_all, _ = _take(small_each, off_all, small_shapes[n])
                dmod_all.append(t_all)
            off_all += math.prod(small_shapes[n])
    for n in SMALL:
        g_out[n if n != "mod" else "ada_b"] = jnp.stack(per_small[n])
    dmod_all = jnp.stack(dmod_all)
    dmod_cols = lax.dynamic_slice_in_dim(dmod_all, me * n_ada, n_ada, axis=2)
    g_out["ada_w"] = _mod_bwd_w(c_all.T, dmod_cols, "mod_bwd_w")

    given = dict(ada_w=(ada_w, m_ada_w, v_ada_w), ada_b=(ada_b, m_ada_b, v_ada_b), norm_g=(norm_g, m_norm_g, v_norm_g),
                 w_in=(w_in, m_w_in, v_w_in), sb_q_g=(sb_q_g, m_sb_q_g, v_sb_q_g), sb_k_g=(sb_k_g, m_sb_k_g, v_sb_k_g),
                 conv_w=(conv_w, m_conv_w, v_conv_w), dn_a_log=(dn_a_log, m_dn_a_log, v_dn_a_log),
                 dn_dt_bias=(dn_dt_bias, m_dn_dt_bias, v_dn_dt_bias), dn_norm_g=(dn_norm_g, m_dn_norm_g, v_dn_norm_g),
                 w_branch_sb=(w_branch_sb, m_w_branch_sb, v_w_branch_sb),
                 w_branch_dn=(w_branch_dn, m_w_branch_dn, v_w_branch_dn), w_out=(w_out, m_w_out, v_w_out))
    order = list(given)
    upd = {n: _adamw(given[n][0], g_out[n], given[n][1], given[n][2], "adamw_" + n) for n in order}
    return (loss, grad_x, *[g_out[n] for n in order], *[upd[n][0] for n in order], *[upd[n][1] for n in order],
            *[upd[n][2] for n in order])
```

```python
import functools
import math

import jax
import jax.numpy as jnp
from jax import lax
from jax.experimental import pallas as pl
from jax.experimental.pallas import tpu as pltpu

F32 = jnp.float32
BF16 = jnp.bfloat16
HI = lax.Precision.HIGHEST

N_DEV = 8
EPS = 1e-6
SB_HEADS, SB_DH = 8, 64
DN_HEADS, DN_DH = 4, 128
SB_W = SB_HEADS * SB_DH
DN_W = DN_HEADS * DN_DH
CONV_K = 4
BLK = 128
SB_KEYS = 512
SB_QB = 512
LANES = 128
IN_MAIN = 4 * SB_W + 4 * DN_W
IN_COLS = IN_MAIN + 2 * DN_HEADS
ADAM_LR, ADAM_B1, ADAM_B2, ADAM_EPS, ADAM_WD, ADAM_STEP = 0.001, 0.9, 0.999, 1e-08, 0.01, 10
VMEM_LIMIT = 56 * 1024 * 1024
SUM_ROWS = 128


def _cp(*sem, vmem=VMEM_LIMIT):
    return pltpu.CompilerParams(dimension_semantics=sem if sem else None, vmem_limit_bytes=vmem)


def _dot(a, b, prec=HI):
    return lax.dot_general(a, b, (((1,), (0,)), ((), ())), precision=prec, preferred_element_type=F32)


def _dot_nt(a, b, prec=HI):
    return lax.dot_general(a, b, (((1,), (1,)), ((), ())), precision=prec, preferred_element_type=F32)


def _bdot(a, b):
    return lax.dot_general(a.astype(BF16), b.astype(BF16), (((1,), (0,)), ((), ())), preferred_element_type=F32)


def _bdot_nt(a, b):
    return lax.dot_general(a.astype(BF16), b.astype(BF16), (((1,), (1,)), ((), ())), preferred_element_type=F32)


def _bdot_tn(a, b):
    return lax.dot_general(a.astype(BF16), b.astype(BF16), (((0,), (0,)), ((), ())), preferred_element_type=F32)


def _split_dot(a, b01):
    hi = a.astype(BF16)
    lo = (a - hi.astype(F32)).astype(BF16)
    return (lax.dot_general(hi, b01, (((1,), (0,)), ((), ())), preferred_element_type=F32)
            + lax.dot_general(lo, b01, (((1,), (0,)), ((), ())), preferred_element_type=F32))


def _sigmoid(x):
    return 1.0 / (1.0 + jnp.exp(-x))


def _silu(x):
    return x * _sigmoid(x)


def _softplus(x):
    return jnp.maximum(x, 0.0) + jnp.log(1.0 + jnp.exp(-jnp.abs(x)))


def _rms(x):
    return x * lax.rsqrt(jnp.mean(x * x, axis=-1, keepdims=True) + EPS)


def _prenorm(x, g, shift, scale):
    return _rms(x) * g * (1.0 + scale) + shift


def _inproj_fwd(x, mod, g, w, name):
    S, D = x.shape
    N = w.shape[1]
    tm = min(512, S)
    tn = 896 if N % 896 == 0 else 128

    def body(x_ref, mod_ref, g_ref, w_ref, p_ref, h_ref):
        @pl.when(pl.program_id(1) == 0)
        def _():
            h = _prenorm(x_ref[...], g_ref[...], mod_ref[:, 0:D], mod_ref[:, D:2 * D])
            h_ref[...] = h.astype(BF16)

        p_ref[...] = jnp.dot(h_ref[...], w_ref[...], preferred_element_type=F32)

    return pl.pallas_call(
        body, name=name, grid=(S // tm, N // tn),
        in_specs=[pl.BlockSpec((tm, D), lambda i, j: (i, 0)), pl.BlockSpec((1, 3 * D), lambda i, j: (0, 0)),
                  pl.BlockSpec((1, D), lambda i, j: (0, 0)), pl.BlockSpec((D, tn), lambda i, j: (0, j))],
        out_specs=[pl.BlockSpec((tm, tn), lambda i, j: (i, j)), pl.BlockSpec((tm, D), lambda i, j: (i, 0))],
        out_shape=[jax.ShapeDtypeStruct((S, N), F32), jax.ShapeDtypeStruct((S, D), BF16)],
        compiler_params=_cp("parallel", "arbitrary"),
    )(x, mod, g, w)


def _inproj_bwd_dx(dp, w, x, mod, g, dxn, name):
    S, N = dp.shape
    D = x.shape[1]
    tm = min(256, S)
    tk = 896 if N % 896 == 0 else 128
    nk = N // tk

    def body(dp_ref, w_ref, x_ref, mod_ref, g_ref, dxn_ref, dx_ref, dmod_ref, dg_ref, acc):
        i, k = pl.program_id(0), pl.program_id(1)

        @pl.when(k == 0)
        def _():
            acc[...] = jnp.zeros_like(acc)

        @pl.when((i == 0) & (k == 0))
        def _():
            dmod_ref[...] = jnp.zeros_like(dmod_ref)
            dg_ref[...] = jnp.zeros_like(dg_ref)

        acc[...] += lax.dot_general(dp_ref[...], w_ref[...], (((1,), (1,)), ((), ())), preferred_element_type=F32)

        @pl.when(k == nk - 1)
        def _():
            _, vjp = jax.vjp(_prenorm, x_ref[...], g_ref[...], mod_ref[:, 0:D], mod_ref[:, D:2 * D])
            dx, dg, dshift, dscale = vjp(acc[...])
            dx_ref[...] = dxn_ref[...] + dx
            dg_ref[...] += dg
            dmod_ref[:, 0:D] += dshift
            dmod_ref[:, D:2 * D] += dscale

    return pl.pallas_call(
        body, name=name, grid=(S // tm, nk),
        in_specs=[pl.BlockSpec((tm, tk), lambda i, k: (i, k)), pl.BlockSpec((D, tk), lambda i, k: (0, k)),
                  pl.BlockSpec((tm, D), lambda i, k: (i, 0)), pl.BlockSpec((1, 3 * D), lambda i, k: (0, 0)),
                  pl.BlockSpec((1, D), lambda i, k: (0, 0)), pl.BlockSpec((tm, D), lambda i, k: (i, 0))],
        out_specs=[pl.BlockSpec((tm, D), lambda i, k: (i, 0)), pl.BlockSpec((1, 3 * D), lambda i, k: (0, 0)),
                   pl.BlockSpec((1, D), lambda i, k: (0, 0))],
        out_shape=[jax.ShapeDtypeStruct((S, D), F32), jax.ShapeDtypeStruct((1, 3 * D), F32),
                   jax.ShapeDtypeStruct((1, D), F32)],
        scratch_shapes=[pltpu.VMEM((tm, D), F32)],
        compiler_params=_cp("arbitrary", "arbitrary"),
    )(dp, w, x, mod, g, dxn)


def _matmul_tn(a_t, b, name):
    M, K = a_t.shape
    N = b.shape[1]
    tn = 896 if N % 896 == 0 else (512 if N % 512 == 0 else 128)
    tk = min(512, K)
    nk = K // tk

    def body(a_ref, b_ref, o_ref):
        @pl.when(pl.program_id(1) == 0)
        def _():
            o_ref[...] = jnp.zeros_like(o_ref)

        o_ref[...] += jnp.dot(a_ref[...], b_ref[...], preferred_element_type=F32)

    return pl.pallas_call(
        body, name=name, grid=(N // tn, nk),
        in_specs=[pl.BlockSpec((M, tk), lambda j, k: (0, k)), pl.BlockSpec((tk, tn), lambda j, k: (k, j))],
        out_specs=pl.BlockSpec((M, tn), lambda j, k: (0, j)),
        out_shape=jax.ShapeDtypeStruct((M, N), F32),
        compiler_params=_cp("parallel", "arbitrary"),
    )(a_t, b)


def _qk_norm(t, g, scale):
    return _rms(t) * g * scale


def _qk_norm_t(t, g_col, scale):
    return t * lax.rsqrt(jnp.mean(t * t, axis=0, keepdims=True) + EPS) * g_col * scale


def _suffix_sums(x, tri):
    half = tri.shape[0]
    lo, hi = x[:, :half], x[:, half:]
    hi_sum = jnp.sum(hi, axis=1, keepdims=True)
    y = jnp.concatenate([_split_dot(lo, tri) + hi_sum, _split_dot(hi, tri)], axis=1)
    return y, hi_sum + jnp.sum(lo, axis=1, keepdims=True)


def _sb_step(qi, kat_blk, cl, upper, valid):
    z = jnp.dot(qi, kat_blk, preferred_element_type=F32)
    lk = jnp.minimum(-z, 0.0) - jnp.log(1.0 + jnp.exp(-jnp.abs(z)))
    if valid is not None:
        lk = jnp.where(valid, lk, 0.0)
    later, tot = _suffix_sums(lk, upper)
    w = jnp.exp(z + lk + later + cl)
    if valid is not None:
        w = jnp.where(valid, w, 0.0)
    return z, lk, w, tot


def _sb_masks(kb):
    half = kb // 2
    r = lax.broadcasted_iota(jnp.int32, (half, half), 0)
    c = lax.broadcasted_iota(jnp.int32, (half, half), 1)
    rq = lax.broadcasted_iota(jnp.int32, (SB_QB, kb), 0)
    ck = lax.broadcasted_iota(jnp.int32, (SB_QB, kb), 1)
    return (r > c).astype(BF16), (r >= c).astype(BF16), ck - rq


def _sb_fwd(q, kt, v, gq, gkt, name, carry=None):
    H, S, dh = q.shape
    kb = min(SB_KEYS, S)
    per = kb // SB_QB
    nb = S // SB_QB
    scale = 1.0 / math.sqrt(dh)

    def body(q_ref, kt_ref, v_ref, gq_ref, gkt_ref, o_ref, qa, kat, vb):
        qa[...] = _qk_norm(q_ref[0], gq_ref[...], scale).astype(BF16)
        kat[...] = _qk_norm_t(kt_ref[0], gkt_ref[...], 1.0).astype(BF16)
        vb[...] = v_ref[0].astype(BF16)
        upper, _, diff = _sb_masks(kb)

        def qblock(i, _):
            rows = pl.ds(pl.multiple_of(i * SB_QB, SB_QB), SB_QB)
            qi = qa[rows, :]
            sbd = i // per

            def step(sb, cl, acc, valid):
                cols = pl.ds(pl.multiple_of(sb * kb, kb), kb)
                _, _, w, tot = _sb_step(qi, kat[:, cols], cl, upper, valid)
                return cl + tot, acc + jnp.dot(w.astype(BF16), vb[cols, :], preferred_element_type=F32)

            cl, acc = step(sbd, jnp.zeros((SB_QB, 1), F32), jnp.zeros((SB_QB, dh), F32),
                           diff < (i - sbd * per) * SB_QB)
            _, acc = lax.fori_loop(0, sbd, lambda jj, c: step(sbd - 1 - jj, c[0], c[1], None), (cl, acc))
            o_ref[0, rows, :] = acc
            return 0

        lax.fori_loop(0, nb, qblock, 0)

    hs = pl.BlockSpec((1, S, dh), lambda h: (h, 0, 0))
    ts = pl.BlockSpec((1, dh, S), lambda h: (h, 0, 0))
    return _call_carrying(
        body, name, H, carry, [q, kt, v, gq, gkt],
        in_specs=[hs, ts, hs, pl.BlockSpec((1, dh), lambda h: (0, 0)), pl.BlockSpec((dh, 1), lambda h: (0, 0))],
        out_specs=[hs], out_shape=[jax.ShapeDtypeStruct((H, S, dh), F32)],
        scratch_shapes=[pltpu.VMEM((S, dh), BF16), pltpu.VMEM((dh, S), BF16), pltpu.VMEM((S, dh), BF16)],
        vmem=VMEM_LIMIT)


def _call_carrying(body, name, steps, carry, operands, in_specs, out_specs, out_shape, scratch_shapes, vmem):
    if carry is None:
        res = pl.pallas_call(body, name=name, grid=(steps,), in_specs=in_specs, out_specs=out_specs,
                             out_shape=out_shape, scratch_shapes=scratch_shapes,
                             compiler_params=_cp("parallel", vmem=vmem))(*operands)
        return res, []
    ex = _Exchange(*carry)
    n_in, n_out, n_scr = len(in_specs), len(out_specs), len(scratch_shapes)

    def wrapped(*refs):
        ins, refs = refs[:n_in], refs[n_in:]
        xin, refs = refs[:ex.n], refs[ex.n:]
        outs, refs = refs[:n_out], refs[n_out:]
        xout, refs = refs[:ex.n], refs[ex.n:]
        scr, sems = refs[:n_scr], refs[n_scr:]

        @pl.when(pl.program_id(0) == 0)
        def _():
            ex.start(xin, xout, sems)

        body(*ins, *outs, *scr)

        @pl.when(pl.program_id(0) == steps - 1)
        def _():
            ex.finish(xin, xout, sems)

    res = pl.pallas_call(wrapped, name=name, grid=(steps,), in_specs=in_specs + ex.specs,
                         out_specs=out_specs + ex.specs, out_shape=out_shape + ex.out_shape,
                         scratch_shapes=scratch_shapes + ex.scratch,
                         compiler_params=_cp("arbitrary", vmem=vmem))(*operands, *carry[0])
    return res[:n_out], res[n_out:]


def _sb_bwd(q, qt, k, kt, v, vt, gq, gqt, gk, gkt, o, do, dot_, name, carry=None):
    H, S, dh = q.shape
    kb = min(SB_KEYS, S)
    per = kb // SB_QB
    nb = S // SB_QB
    scale = 1.0 / math.sqrt(dh)

    def body(q_ref, qt_ref, k_ref, kt_ref, v_ref, vt_ref, gq_ref, gqt_ref, gk_ref, gkt_ref, o_ref, do_ref, dot_ref,
             dq_ref, dkt_ref, dvt_ref, dgq_ref, dgkt_ref, qa, qat, ka, kat, vb, vtb, dob, dotb, dqa, dkat):
        qa[...] = _qk_norm(q_ref[0], gq_ref[...], scale).astype(BF16)
        qat[...] = _qk_norm_t(qt_ref[0], gqt_ref[...], scale).astype(BF16)
        ka[...] = _qk_norm(k_ref[0], gk_ref[...], 1.0).astype(BF16)
        kat[...] = _qk_norm_t(kt_ref[0], gkt_ref[...], 1.0).astype(BF16)
        vb[...] = v_ref[0].astype(BF16)
        vtb[...] = vt_ref[0].astype(BF16)
        dob[...] = do_ref[0].astype(BF16)
        dotb[...] = dot_ref[0].astype(BF16)
        dkat[...] = jnp.zeros_like(dkat)
        dvt_ref[...] = jnp.zeros_like(dvt_ref)
        upper, lower_incl, diff = _sb_masks(kb)

        def qblock(i, _):
            rows = pl.ds(pl.multiple_of(i * SB_QB, SB_QB), SB_QB)
            qi, qit = qa[rows, :], qat[:, rows]
            doi, doit = dob[rows, :], dotb[:, rows]
            total = jnp.sum(doi.astype(F32) * o_ref[0, rows, :], axis=1, keepdims=True)
            sbd = i // per

            def step(sb, cl, cd, dqi, valid):
                cols = pl.ds(pl.multiple_of(sb * kb, kb), kb)
                z, lk, w, tot = _sb_step(qi, kat[:, cols], cl, upper, valid)
                w16 = w.astype(BF16)
                dl = jnp.dot(doi, vtb[:, cols], preferred_element_type=F32) * w16.astype(F32)
                incl, dtot = _suffix_sums(dl, lower_incl)
                sig = jnp.exp(z + lk)
                dz = dl - sig * (dl + (total - cd - incl))
                if valid is not None:
                    dz = jnp.where(valid, dz, 0.0)
                dz16 = dz.astype(BF16)
                dqi = dqi + jnp.dot(dz16, ka[cols, :], preferred_element_type=F32)
                dkat[:, cols] += jnp.dot(qit, dz16, preferred_element_type=F32)
                dvt_ref[0, :, cols] += jnp.dot(doit, w16, preferred_element_type=F32)
                return cl + tot, cd + dtot, dqi

            zero = jnp.zeros((SB_QB, 1), F32)
            carry = step(sbd, zero, zero, jnp.zeros((SB_QB, dh), F32), diff < (i - sbd * per) * SB_QB)
            _, _, dqi = lax.fori_loop(0, sbd, lambda jj, c: step(sbd - 1 - jj, c[0], c[1], c[2], None), carry)
            dqa[rows, :] = dqi
            return 0

        lax.fori_loop(0, nb, qblock, 0)
        _, vq = jax.vjp(lambda t, g: _qk_norm(t, g, scale), q_ref[0], gq_ref[...])
        dq, dgq = vq(dqa[...])
        dq_ref[0] = dq
        dgq_ref[0] = dgq
        _, vk = jax.vjp(lambda t, g: _qk_norm_t(t, g, 1.0), kt_ref[0], gkt_ref[...])
        dkt, dgkt = vk(dkat[...])
        dkt_ref[0] = dkt
        dgkt_ref[0] = dgkt

    hs = pl.BlockSpec((1, S, dh), lambda h: (h, 0, 0))
    ts = pl.BlockSpec((1, dh, S), lambda h: (h, 0, 0))
    gr = pl.BlockSpec((1, dh), lambda h: (0, 0))
    gc = pl.BlockSpec((dh, 1), lambda h: (0, 0))
    sd = jax.ShapeDtypeStruct((H, S, dh), F32)
    td = jax.ShapeDtypeStruct((H, dh, S), F32)
    return _call_carrying(
        body, name, H, carry, [q, qt, k, kt, v, vt, gq, gqt, gk, gkt, o, do, dot_],
        in_specs=[hs, ts, hs, ts, hs, ts, gr, gc, gr, gc, hs, hs, ts],
        out_specs=[hs, ts, ts, pl.BlockSpec((1, 1, dh), lambda h: (h, 0, 0)), pl.BlockSpec((1, dh, 1), lambda h: (h, 0, 0))],
        out_shape=[sd, td, td, jax.ShapeDtypeStruct((H, 1, dh), F32), jax.ShapeDtypeStruct((H, dh, 1), F32)],
        scratch_shapes=[pltpu.VMEM((S, dh), BF16), pltpu.VMEM((dh, S), BF16)] * 4
        + [pltpu.VMEM((S, dh), F32), pltpu.VMEM((dh, S), F32)],
        vmem=60 * 1024 * 1024)


def _shift_down(x, s, rows):
    if s == 0:
        return x
    return jnp.where(rows >= s, pltpu.roll(x, s, 0), 0.0)


def _shift_up(x, s, rows, n):
    if s == 0:
        return x
    return jnp.where(rows < n - s, pltpu.roll(x, n - s, 0), 0.0)


def _conv(x, w_ref, rows):
    y = x * w_ref[CONV_K - 1:CONV_K, :]
    for kk in range(CONV_K - 1):
        y = y + _shift_down(x, CONV_K - 1 - kk, rows) * w_ref[kk:kk + 1, :]
    return y


def _act_norm(y, normed):
    s = _silu(y)
    n = s * lax.rsqrt(jnp.sum(s * s, axis=-1, keepdims=True) + EPS)
    return jnp.where(normed, n, s)


def _dn_prep_fwd(p, conv_w, col0, name):
    S = p.shape[0]
    nblk = 3 * DN_HEADS
    b0 = col0 // DN_DH

    def body(x_ref, w_ref, o_ref):
        rows = lax.broadcasted_iota(jnp.int32, (S, DN_DH), 0)
        y = _conv(x_ref[...], w_ref, rows)
        o_ref[...] = _act_norm(y, pl.program_id(0) < 2 * DN_HEADS)

    return pl.pallas_call(
        body, name=name, grid=(nblk,),
        in_specs=[pl.BlockSpec((S, DN_DH), lambda j: (0, b0 + j)), pl.BlockSpec((CONV_K, DN_DH), lambda j: (0, j))],
        out_specs=pl.BlockSpec((S, DN_DH), lambda j: (0, j)),
        out_shape=jax.ShapeDtypeStruct((S, 3 * DN_W), F32),
        compiler_params=_cp("parallel"),
    )(p, conv_w)


def _dn_prep_bwd(p, conv_w, col0, dout, name):
    S = p.shape[0]
    nblk = 3 * DN_HEADS
    b0 = col0 // DN_DH

    def body(x_ref, w_ref, do_ref, dx_ref, dw_ref):
        rows = lax.broadcasted_iota(jnp.int32, (S, DN_DH), 0)
        x = x_ref[...]
        y = _conv(x, w_ref, rows)
        normed = pl.program_id(0) < 2 * DN_HEADS
        _, vjp = jax.vjp(lambda t: _act_norm(t, normed), y)
        (dy,) = vjp(do_ref[...])
        dx = dy * w_ref[CONV_K - 1:CONV_K, :]
        dw_ref[CONV_K - 1:CONV_K, :] = jnp.sum(dy * x, axis=0, keepdims=True)
        for kk in range(CONV_K - 1):
            s = CONV_K - 1 - kk
            dx = dx + _shift_up(dy, s, rows, S) * w_ref[kk:kk + 1, :]
            dw_ref[kk:kk + 1, :] = jnp.sum(dy * _shift_down(x, s, rows), axis=0, keepdims=True)
        dx_ref[...] = dx.astype(BF16)

    return pl.pallas_call(
        body, name=name, grid=(nblk,),
        in_specs=[pl.BlockSpec((S, DN_DH), lambda j: (0, b0 + j)), pl.BlockSpec((CONV_K, DN_DH), lambda j: (0, j)),
                  pl.BlockSpec((S, DN_DH), lambda j: (0, j))],
        out_specs=[pl.BlockSpec((S, DN_DH), lambda j: (0, j)), pl.BlockSpec((CONV_K, DN_DH), lambda j: (0, j))],
        out_shape=[jax.ShapeDtypeStruct((S, 3 * DN_W), BF16), jax.ShapeDtypeStruct((CONV_K, 3 * DN_W), F32)],
        compiler_params=_cp("parallel"),
    )(p, conv_w, dout)


def _gate_fn(x, pv):
    lane = lax.broadcasted_iota(jnp.int32, x.shape, 1)
    decay = -jnp.exp(pv[0:1, :]) * _softplus(x + pv[1:2, :])
    return jnp.where(lane < DN_HEADS, _sigmoid(x), decay)


def _dn_gate_fwd(p, pv, blk, name):
    S = p.shape[0]

    def body(x_ref, pv_ref, o_ref):
        o_ref[...] = _gate_fn(x_ref[...], pv_ref[...])

    return pl.pallas_call(
        body, name=name, grid=(1,),
        in_specs=[pl.BlockSpec((S, LANES), lambda i: (0, blk)), pl.BlockSpec((2, LANES), lambda i: (0, 0))],
        out_specs=pl.BlockSpec((S, LANES), lambda i: (0, 0)),
        out_shape=jax.ShapeDtypeStruct((S, LANES), F32),
        compiler_params=_cp("arbitrary"),
    )(p, pv)


def _dn_gate_bwd(p, pv, blk, dout, name):
    S = p.shape[0]

    def body(x_ref, pv_ref, do_ref, dx_ref, dpv_ref):
        _, vjp = jax.vjp(_gate_fn, x_ref[...], pv_ref[...])
        dx, dpv = vjp(do_ref[...])
        dx_ref[...] = dx.astype(BF16)
        dpv_ref[...] = dpv

    return pl.pallas_call(
        body, name=name, grid=(1,),
        in_specs=[pl.BlockSpec((S, LANES), lambda i: (0, blk)), pl.BlockSpec((2, LANES), lambda i: (0, 0)),
                  pl.BlockSpec((S, LANES), lambda i: (0, 0))],
        out_specs=[pl.BlockSpec((S, LANES), lambda i: (0, 0)), pl.BlockSpec((2, LANES), lambda i: (0, 0))],
        out_shape=[jax.ShapeDtypeStruct((S, LANES), BF16), jax.ShapeDtypeStruct((2, LANES), F32)],
        compiler_params=_cp("arbitrary"),
    )(p, pv, dout)


def _t(x):
    return jnp.swapaxes(x, -1, -2)


def _matmuls(prec, differentiable):
    def mm(a, b):
        return lax.dot_general(a, b, (((2,), (1,)), ((0,), (0,))), precision=prec, preferred_element_type=F32)

    def mm_nt(a, b):
        return lax.dot_general(a, b, (((2,), (2,)), ((0,), (0,))), precision=prec, preferred_element_type=F32)

    if not differentiable:
        return mm, mm_nt
    dmm, dmm_nt = jax.custom_vjp(mm), jax.custom_vjp(mm_nt)
    dmm.defvjp(lambda a, b: (mm(a, b), (a, b)), lambda res, g: (mm_nt(g, res[1]), mm(_t(res[0]), g)))
    dmm_nt.defvjp(lambda a, b: (mm_nt(a, b), (a, b)), lambda res, g: (mm(g, res[1]), mm(_t(g), res[0])))
    return dmm, dmm_nt


def _delta_chunk(state, q, k, v, beta, a_col, a_row, differentiable=False):
    mm, mm_nt = _matmuls(lax.Precision.HIGH, differentiable)
    mm_sum, _ = _matmuls(HI, differentiable)
    H, C, _ = q.shape
    r = lax.broadcasted_iota(jnp.int32, (H, C, C), 1)
    c = lax.broadcasted_iota(jnp.int32, (H, C, C), 2)
    tril, strict = r >= c, r > c
    eye = (r == c).astype(F32)
    g_c = mm_sum(tril.astype(F32), jnp.broadcast_to(a_col, (H, C, C)))
    g_r = mm_sum(jnp.broadcast_to(a_row, (H, C, C)), (r <= c).astype(F32))
    decay = jnp.where(tril, jnp.exp(jnp.where(tril, g_c - g_r, 0.0)), 0.0)
    eg = jnp.exp(g_c)
    g_last = jnp.sum(jnp.where(r == C - 1, g_c, 0.0), axis=1, keepdims=True)
    qs = q * (float(q.shape[2]) ** -0.5)
    kb = k * beta
    neg_m = jnp.where(strict, -(mm_nt(kb, k) * decay), 0.0)
    inv = eye + neg_m
    pw = neg_m
    for _ in range(int(math.log2(C)) - 1):
        pw = mm(pw, pw)
        inv = inv + mm(inv, pw)
    u = mm(inv, v * beta)
    w = mm(inv, kb * eg)
    intra = jnp.where(tril, mm_nt(qs, k) * decay, 0.0)
    v_new = u - mm(w, state)
    o = mm(qs * eg, state) + mm(intra, v_new)
    nxt = state * jnp.exp(g_last) + mm(_t(k * jnp.exp(g_last - g_c)), v_new)
    return o, nxt


def _heads(t):
    return jnp.stack([t[:, h * DN_DH:(h + 1) * DN_DH] for h in range(DN_HEADS)])


def _delta_step(state, q, k, v, bg, a_row, differentiable=False):
    lane = lax.broadcasted_iota(jnp.int32, bg.shape, 1)
    pick = lambda j: jnp.stack([jnp.sum(jnp.where(lane == j + h, bg, 0.0), axis=1, keepdims=True)
                                for h in range(DN_HEADS)])
    o, nxt = _delta_chunk(state, _heads(q), _heads(k), _heads(v), pick(0), pick(DN_HEADS), a_row, differentiable)
    return jnp.concatenate([o[h] for h in range(DN_HEADS)], axis=1), nxt


def _delta_fwd(qkv, bg, a_row, name):
    S = qkv.shape[0]
    nc = S // BLK

    def body(q_ref, k_ref, v_ref, bg_ref, ar_ref, o_ref, st_ref, state):
        ci = pl.program_id(0)

        @pl.when(ci == 0)
        def _():
            state[...] = jnp.zeros_like(state)

        st = state[...]
        st_ref[:, 0] = st
        o, nxt = _delta_step(st, q_ref[...], k_ref[...], v_ref[...], bg_ref[...], ar_ref[:, pl.ds(ci, 1), :])
        o_ref[...] = o
        state[...] = nxt

    part = lambda j: pl.BlockSpec((BLK, DN_W), lambda c: (c, j))
    return pl.pallas_call(
        body, name=name, grid=(nc,),
        in_specs=[part(0), part(1), part(2), pl.BlockSpec((BLK, LANES), lambda c: (c, 0)),
                  pl.BlockSpec((DN_HEADS, nc, BLK), lambda c: (0, 0, 0))],
        out_specs=[part(0), pl.BlockSpec((DN_HEADS, 1, DN_DH, DN_DH), lambda c: (0, c, 0, 0))],
        out_shape=[jax.ShapeDtypeStruct((S, DN_W), F32), jax.ShapeDtypeStruct((DN_HEADS, nc, DN_DH, DN_DH), F32)],
        scratch_shapes=[pltpu.VMEM((DN_HEADS, DN_DH, DN_DH), F32)],
        compiler_params=_cp("arbitrary"),
    )(qkv, qkv, qkv, bg, a_row)


def _delta_bwd(qkv, bg, a_row, states, do, name):
    S = qkv.shape[0]
    nc = S // BLK

    def body(q_ref, k_ref, v_ref, bg_ref, ar_ref, st_ref, do_ref, dq_ref, dk_ref, dv_ref, dbg_ref, dar_ref, dstate):
        t = pl.program_id(0)
        ci = nc - 1 - t

        @pl.when(t == 0)
        def _():
            dstate[...] = jnp.zeros_like(dstate)

        _, vjp = jax.vjp(functools.partial(_delta_step, differentiable=True), st_ref[:, 0], q_ref[...], k_ref[...],
                         v_ref[...], bg_ref[...], ar_ref[:, pl.ds(ci, 1), :])
        dprev, dq, dk, dv, dbg, dar = vjp((do_ref[...], dstate[...]))
        dq_ref[...] = dq
        dk_ref[...] = dk
        dv_ref[...] = dv
        dbg_ref[...] = dbg
        dar_ref[:, pl.ds(ci, 1), :] = dar
        dstate[...] = dprev

    part = lambda j: pl.BlockSpec((BLK, DN_W), lambda t: (nc - 1 - t, j))
    lanes = pl.BlockSpec((BLK, LANES), lambda t: (nc - 1 - t, 0))
    rows = pl.BlockSpec((DN_HEADS, nc, BLK), lambda t: (0, 0, 0))
    d3 = jax.ShapeDtypeStruct((S, DN_W), F32)
    return pl.pallas_call(
        body, name=name, grid=(nc,),
        in_specs=[part(0), part(1), part(2), lanes, rows,
                  pl.BlockSpec((DN_HEADS, 1, DN_DH, DN_DH), lambda t: (0, nc - 1 - t, 0, 0)), part(0)],
        out_specs=[part(0), part(0), part(0), lanes, rows],
        out_shape=[d3, d3, d3, jax.ShapeDtypeStruct((S, LANES), F32), jax.ShapeDtypeStruct((DN_HEADS, nc, BLK), F32)],
        scratch_shapes=[pltpu.VMEM((DN_HEADS, DN_DH, DN_DH), F32)],
        compiler_params=_cp("arbitrary"),
    )(qkv, qkv, qkv, bg, a_row, states, do)


def _gate_sb(o, z):
    return o * _silu(z)


def _gate_dn(o, z, g):
    return jnp.concatenate(
        [_rms(o[:, h * DN_DH:(h + 1) * DN_DH]) * g * _silu(z[:, h * DN_DH:(h + 1) * DN_DH]) for h in range(DN_HEADS)],
        axis=1)


def _merge_specs(S, D, tm):
    row = lambda w, blk: pl.BlockSpec((tm, w), lambda i: (i, blk))
    full = lambda a, b: pl.BlockSpec((a, b), lambda i: (0, 0))
    return [row(D, 0), full(1, D), row(SB_W, 0), row(SB_W, 3), row(DN_W, 0), row(DN_W, 7),
            row(D, IN_MAIN // D), row(D, IN_MAIN // D + 1), full(1, DN_DH), full(SB_W, D), full(DN_W, D), full(D, D)]


def _merge_fwd(x, gate, o_sb, o_dn, p, ng, wbs, wbd, wo, name):
    S, D = x.shape
    tm = min(512, S)

    def body(x_ref, gate_ref, osb_ref, zsb_ref, odn_ref, zdn_ref, msb_ref, mdn_ref, ng_ref, wbs_ref, wbd_ref, wo_ref,
             out_ref):
        a = _gate_sb(osb_ref[...], zsb_ref[...])
        b = _gate_dn(odn_ref[...], zdn_ref[...], ng_ref[...])
        y = _sigmoid(msb_ref[...]) * _bdot(a, wbs_ref[...]) + _sigmoid(mdn_ref[...]) * _bdot(b, wbd_ref[...])
        out_ref[...] = x_ref[...] + gate_ref[...] * _bdot(y, wo_ref[...])

    return pl.pallas_call(
        body, name=name, grid=(S // tm,), in_specs=_merge_specs(S, D, tm),
        out_specs=pl.BlockSpec((tm, D), lambda i: (i, 0)), out_shape=jax.ShapeDtypeStruct((S, D), F32),
        compiler_params=_cp("parallel"),
    )(x, gate, o_sb, p, o_dn, p, p, p, ng, wbs, wbd, wo)


def _merge_bwd(dxn, gate, o_sb, o_dn, p, ng, wbs, wbd, wo, name):
    S, D = dxn.shape
    tm = min(256, S)

    def body(dxn_ref, gate_ref, osb_ref, zsb_ref, odn_ref, zdn_ref, msb_ref, mdn_ref, ng_ref, wbs_ref, wbd_ref, wo_ref,
             dosb_ref, dzsb_ref, dodn_ref, dzdn_ref, dmsb_ref, dmdn_ref, dwo_ref, dwbs_ref, dwbd_ref, dgate_ref, dng_ref):
        @pl.when(pl.program_id(0) == 0)
        def _():
            for ref in (dwo_ref, dwbs_ref, dwbd_ref, dgate_ref, dng_ref):
                ref[...] = jnp.zeros_like(ref)

        a, vjp_a = jax.vjp(_gate_sb, osb_ref[...], zsb_ref[...])
        b, vjp_b = jax.vjp(_gate_dn, odn_ref[...], zdn_ref[...], ng_ref[...])
        a16, b16 = a.astype(BF16), b.astype(BF16)
        ps = jnp.dot(a16, wbs_ref[...], preferred_element_type=F32)
        pd = jnp.dot(b16, wbd_ref[...], preferred_element_type=F32)
        ss, sd = _sigmoid(msb_ref[...]), _sigmoid(mdn_ref[...])
        y16 = (ss * ps + sd * pd).astype(BF16)
        out = jnp.dot(y16, wo_ref[...], preferred_element_type=F32)
        dxn_v = dxn_ref[...]
        dgate_ref[...] += jnp.sum(dxn_v * out, axis=0, keepdims=True)
        dout16 = (dxn_v * gate_ref[...]).astype(BF16)
        dwo_ref[...] += _bdot_tn(y16, dout16)
        dy = _bdot_nt(dout16, wo_ref[...])
        dmsb_ref[...] = (dy * ps * ss * (1.0 - ss)).astype(BF16)
        dmdn_ref[...] = (dy * pd * sd * (1.0 - sd)).astype(BF16)
        dps16, dpd16 = (dy * ss).astype(BF16), (dy * sd).astype(BF16)
        dwbs_ref[...] += _bdot_tn(a16, dps16)
        dwbd_ref[...] += _bdot_tn(b16, dpd16)
        dosb, dzsb = vjp_a(_bdot_nt(dps16, wbs_ref[...]))
        dodn, dzdn, dng = vjp_b(_bdot_nt(dpd16, wbd_ref[...]))
        dosb_ref[...] = dosb
        dzsb_ref[...] = dzsb.astype(BF16)
        dodn_ref[...] = dodn
        dzdn_ref[...] = dzdn.astype(BF16)
        dng_ref[...] += dng

    row = lambda w: pl.BlockSpec((tm, w), lambda i: (i, 0))
    full = lambda a, b: pl.BlockSpec((a, b), lambda i: (0, 0))
    sds = jax.ShapeDtypeStruct
    return pl.pallas_call(
        body, name=name, grid=(S // tm,), in_specs=_merge_specs(S, D, tm),
        out_specs=[row(SB_W), row(SB_W), row(DN_W), row(DN_W), row(D), row(D),
                   full(D, D), full(SB_W, D), full(DN_W, D), full(1, D), full(1, DN_DH)],
        out_shape=[sds((S, SB_W), F32), sds((S, SB_W), BF16), sds((S, DN_W), F32), sds((S, DN_W), BF16),
                   sds((S, D), BF16), sds((S, D), BF16), sds((D, D), F32), sds((SB_W, D), F32), sds((DN_W, D), F32),
                   sds((1, D), F32), sds((1, DN_DH), F32)],
        compiler_params=_cp("arbitrary"),
    )(dxn, gate, o_sb, p, o_dn, p, p, p, ng, wbs, wbd, wo)


def _loss_fwd_bwd(y, target, name):
    S, D = y.shape
    tm = min(512, S)

    def body(y_ref, t_ref, l_ref, dy_ref):
        @pl.when(pl.program_id(0) == 0)
        def _():
            l_ref[...] = jnp.zeros_like(l_ref)

        e = y_ref[...] - t_ref[...]
        l_ref[...] += jnp.sum(e * e, axis=0, keepdims=True) * (0.5 / D)
        dy_ref[...] = e * (1.0 / D)

    row = pl.BlockSpec((tm, D), lambda i: (i, 0))
    return pl.pallas_call(
        body, name=name, grid=(S // tm,), in_specs=[row, row],
        out_specs=[pl.BlockSpec((1, D), lambda i: (0, 0)), row],
        out_shape=[jax.ShapeDtypeStruct((1, D), F32), jax.ShapeDtypeStruct((S, D), F32)],
        compiler_params=_cp("arbitrary"),
    )(y, target)


def _mod_fwd(c_all, ada_w, name):
    L, D, n = ada_w.shape

    def body(c_ref, w_ref, o_ref):
        o_ref[0] = _dot(_silu(c_ref[...]), w_ref[0])

    return pl.pallas_call(
        body, name=name, grid=(L,),
        in_specs=[pl.BlockSpec(c_all.shape, lambda l: (0, 0)), pl.BlockSpec((1, D, n), lambda l: (l, 0, 0))],
        out_specs=pl.BlockSpec((1, N_DEV, n), lambda l: (l, 0, 0)),
        out_shape=jax.ShapeDtypeStruct((L, N_DEV, n), F32),
        compiler_params=_cp("parallel"),
    )(c_all, ada_w)


def _mod_bwd_w(c_all_t, dmod, name):
    L, _, n = dmod.shape
    D = c_all_t.shape[0]

    def body(c_ref, d_ref, o_ref):
        o_ref[0] = _dot(_silu(c_ref[...]), d_ref[0])

    return pl.pallas_call(
        body, name=name, grid=(L,),
        in_specs=[pl.BlockSpec(c_all_t.shape, lambda l: (0, 0)), pl.BlockSpec((1, N_DEV, n), lambda l: (l, 0, 0))],
        out_specs=pl.BlockSpec((1, D, n), lambda l: (l, 0, 0)),
        out_shape=jax.ShapeDtypeStruct((L, D, n), F32),
        compiler_params=_cp("parallel"),
    )(c_all_t, dmod)


def _me():
    return lax.axis_index("x"), lax.axis_index("y"), lax.axis_index("c")


def _peer(k):
    x, y, c = _me()
    return (1 - x if k & 4 else x, 1 - y if k & 2 else y, 1 - c if k & 1 else c)


def _lin(dev):
    return 4 * dev[0] + 2 * dev[1] + dev[2]


class _Exchange:
    def __init__(self, arrays, scatter):
        self.n = len(arrays)
        self.scatter = scatter
        self.out_shape = [jax.ShapeDtypeStruct((N_DEV,) + tuple(a.shape[1:] if scatter else a.shape), a.dtype)
                          for a in arrays]
        self.specs = [pl.BlockSpec(memory_space=pl.ANY)] * self.n
        self.scratch = [pltpu.SemaphoreType.DMA((self.n, N_DEV - 1)), pltpu.SemaphoreType.DMA((self.n, N_DEV - 1)),
                        pltpu.SemaphoreType.DMA((self.n,))]

    def _copies(self, ins, outs, sems):
        send_sems, recv_sems, local_sems = sems
        me = _lin(_me())
        local, remote, landed = [], [], []
        for t in range(self.n):
            src_of = (lambda d, t=t: ins[t].at[d]) if self.scatter else (lambda d, t=t: ins[t])
            local.append(pltpu.make_async_copy(src_of(me), outs[t].at[me], local_sems.at[t]))
            for k in range(1, N_DEV):
                peer = _peer(k)
                pair = dict(send_sem=send_sems.at[t, k - 1], recv_sem=recv_sems.at[t, k - 1], device_id=peer,
                            device_id_type=pl.DeviceIdType.MESH)
                remote.append(pltpu.make_async_remote_copy(src_ref=src_of(_lin(peer)), dst_ref=outs[t].at[me], **pair))
                slot = outs[t].at[_lin(peer)]
                landed.append(pltpu.make_async_remote_copy(src_ref=slot, dst_ref=slot, **pair))
        return local, remote, landed

    def start(self, ins, outs, sems):
        local, remote, _ = self._copies(ins, outs, sems)
        for cp in local + remote:
            cp.start()

    def finish(self, ins, outs, sems):
        local, remote, landed = self._copies(ins, outs, sems)
        for cp in landed:
            cp.wait_recv()
        for cp in remote:
            cp.wait_send()
        for cp in local:
            cp.wait()


def _exchange(arrays, scatter, name):
    ex = _Exchange(arrays, scatter)

    def body(*refs):
        ins, outs, sems = refs[:ex.n], refs[ex.n:2 * ex.n], refs[2 * ex.n:]
        ex.start(ins, outs, sems)
        ex.finish(ins, outs, sems)

    return pl.pallas_call(body, name=name, in_specs=ex.specs, out_specs=ex.specs, out_shape=ex.out_shape,
                          scratch_shapes=ex.scratch)(*arrays)


def _sum_slots(a, name):
    _, R, C = a.shape
    tr = SUM_ROWS if R % SUM_ROWS == 0 else R

    def body(a_ref, o_ref):
        acc = a_ref[0].astype(F32)
        for s in range(1, N_DEV):
            acc = acc + a_ref[s].astype(F32)
        o_ref[...] = acc

    return pl.pallas_call(
        body, name=name, grid=(R // tr,),
        in_specs=[pl.BlockSpec((N_DEV, tr, C), lambda i: (0, i, 0))], out_specs=pl.BlockSpec((tr, C), lambda i: (i, 0)),
        out_shape=jax.ShapeDtypeStruct((R, C), F32), compiler_params=_cp("parallel"),
    )(a)


def _adamw(w, g, m, v, name):
    shape = w.shape
    C = shape[-1]
    R = w.size // C
    tr = R
    for cand in (256, 128, 64):
        if R > cand and R % cand == 0:
            tr = cand
            break
    c1 = 1.0 / (1.0 - ADAM_B1 ** ADAM_STEP)
    c2 = 1.0 / (1.0 - ADAM_B2 ** ADAM_STEP)

    def body(w_ref, g_ref, m_ref, v_ref, d_ref, nm_ref, nv_ref):
        gv = g_ref[...]
        nm = ADAM_B1 * m_ref[...] + (1.0 - ADAM_B1) * gv
        nv = ADAM_B2 * v_ref[...] + (1.0 - ADAM_B2) * (gv * gv)
        d_ref[...] = -ADAM_LR * ((nm * c1) / (jnp.sqrt(nv * c2) + ADAM_EPS) + ADAM_WD * w_ref[...])
        nm_ref[...] = nm
        nv_ref[...] = nv

    spec = pl.BlockSpec((tr, C), lambda i: (i, 0))
    sd = jax.ShapeDtypeStruct((R, C), F32)
    outs = pl.pallas_call(
        body, name=name, grid=(R // tr,), in_specs=[spec] * 4, out_specs=[spec] * 3, out_shape=[sd] * 3,
        compiler_params=_cp("parallel"),
    )(*(t.reshape(R, C) for t in (w, g, m, v)))
    return tuple(t.reshape(shape) for t in outs)


def _to_heads(t):
    S = t.shape[0]
    return t.reshape(S, SB_HEADS, SB_DH).transpose(1, 0, 2)


def _from_heads(t):
    S = t.shape[1]
    return t.transpose(1, 0, 2).reshape(S, SB_W)


def _to_heads_t(t):
    S = t.shape[0]
    return t.reshape(S, SB_HEADS, SB_DH).transpose(1, 2, 0)


def _from_heads_t(t):
    S = t.shape[2]
    return t.transpose(2, 0, 1).reshape(S, SB_W)


def _pad_cols(w_in):
    D = w_in.shape[0]
    pad = jnp.zeros((D, LANES - 2 * DN_HEADS), w_in.dtype)
    return jnp.concatenate([w_in[:, :IN_MAIN], w_in[:, IN_COLS:], w_in[:, IN_MAIN:IN_COLS], pad], axis=1)


def _unpad_cols(dw, D):
    return jnp.concatenate([dw[:, :IN_MAIN], dw[:, IN_MAIN + 2 * D:IN_MAIN + 2 * D + 2 * DN_HEADS],
                            dw[:, IN_MAIN:IN_MAIN + 2 * D]], axis=1)


def _gate_params(a_log, dt_bias):
    z = jnp.zeros((LANES,), F32)
    return jnp.stack([z.at[DN_HEADS:2 * DN_HEADS].set(a_log), z.at[DN_HEADS:2 * DN_HEADS].set(dt_bias)])


def _layer_fwd(l, x, mod, wts, carry=None):
    S, D = x.shape
    tag = f"l{l}_"
    p, h = _inproj_fwd(x, mod, wts["norm_g"], wts["w_in"], tag + "inproj_fwd")
    q, k, v = (_to_heads(p[:, i * SB_W:(i + 1) * SB_W]) for i in range(3))
    qt, kt, vt = (_to_heads_t(p[:, i * SB_W:(i + 1) * SB_W]) for i in range(3))
    (o_sb_h,), carried = _sb_fwd(q, kt, v, wts["sb_q_g"], wts["sb_k_g"].T, tag + "sb_fwd", carry)
    qkv = _dn_prep_fwd(p, wts["conv_w"], 4 * SB_W, tag + "dn_prep_fwd")
    pv = _gate_params(wts["dn_a_log"], wts["dn_dt_bias"])
    ba_blk = (IN_MAIN + 2 * D) // LANES
    bg = _dn_gate_fwd(p, pv, ba_blk, tag + "dn_gate_fwd")
    a_row = bg[:, DN_HEADS:2 * DN_HEADS].T.reshape(DN_HEADS, S // BLK, BLK)
    o_dn, states = _delta_fwd(qkv, bg, a_row, tag + "delta_fwd")
    o_sb = _from_heads(o_sb_h)
    gate = mod[:, 2 * D:]
    out = _merge_fwd(x, gate, o_sb, o_dn, p, wts["dn_norm_g"], wts["w_branch_sb"], wts["w_branch_dn"], wts["w_out"],
                     tag + "merge_fwd")
    saved = dict(x=x, mod=mod, p=p, h=h, q=q, k=k, v=v, qt=qt, kt=kt, vt=vt, o_sb_h=o_sb_h, o_sb=o_sb, qkv=qkv, pv=pv, bg=bg,
                 a_row=a_row, o_dn=o_dn, states=states, gate=gate)
    return out, saved, carried


def _layer_bwd(l, dxn, sv, wts, carry_of=None):
    S, D = dxn.shape
    tag = f"l{l}_"
    (dosb, dzsb, dodn, dzdn, dmsb, dmdn, dwo, dwbs, dwbd, dgate, dng) = _merge_bwd(
        dxn, sv["gate"], sv["o_sb"], sv["o_dn"], sv["p"], wts["dn_norm_g"], wts["w_branch_sb"], wts["w_branch_dn"],
        wts["w_out"], tag + "merge_bwd")
    carry = None if carry_of is None else carry_of(dict(w_out=dwo, w_branch_sb=dwbs, w_branch_dn=dwbd))
    (dq, dkt, dvt, dgq, dgkt), carried = _sb_bwd(
        sv["q"], sv["qt"], sv["k"], sv["kt"], sv["v"], sv["vt"], wts["sb_q_g"], wts["sb_q_g"].T, wts["sb_k_g"],
        wts["sb_k_g"].T, sv["o_sb_h"], _to_heads(dosb), _to_heads_t(dosb), tag + "sb_bwd", carry)
    dqkv_n_q, dqkv_n_k, dqkv_n_v, dbg, dar = _delta_bwd(sv["qkv"], sv["bg"], sv["a_row"], sv["states"], dodn,
                                                         tag + "delta_bwd")
    dqkv, dconv = _dn_prep_bwd(sv["p"], wts["conv_w"], 4 * SB_W,
                               jnp.concatenate([dqkv_n_q, dqkv_n_k, dqkv_n_v], axis=1), tag + "dn_prep_bwd")
    dbg = dbg.at[:, DN_HEADS:2 * DN_HEADS].add(dar.reshape(DN_HEADS, S).T)
    ba_blk = (IN_MAIN + 2 * D) // LANES
    dba, dpv = _dn_gate_bwd(sv["p"], sv["pv"], ba_blk, dbg, tag + "dn_gate_bwd")
    dp = jnp.concatenate([_from_heads(dq).astype(BF16), _from_heads_t(dkt).astype(BF16), _from_heads_t(dvt).astype(BF16),
                          dzsb, dqkv, dzdn, dmsb, dmdn, dba], axis=1)
    dx, dmod, dg = _inproj_bwd_dx(dp, wts["w_in"], sv["x"], sv["mod"], wts["norm_g"], dxn, tag + "inproj_bwd_dx")
    dw_in = _matmul_tn(sv["h"].T, dp, tag + "inproj_bwd_dw")
    dmod = dmod.at[:, 2 * D:].set(dgate)
    grads = dict(w_in=_unpad_cols(dw_in, D), w_branch_sb=dwbs, w_branch_dn=dwbd, w_out=dwo, conv_w=dconv,
                 mod=dmod[0], norm_g=dg[0], sb_q_g=jnp.sum(dgq, axis=0)[0], sb_k_g=jnp.sum(dgkt, axis=0)[:, 0],
                 dn_a_log=dpv[0, DN_HEADS:2 * DN_HEADS], dn_dt_bias=dpv[1, DN_HEADS:2 * DN_HEADS], dn_norm_g=dng[0])
    return dx, grads, carried


def _pad_rows(a, mult):
    extra = (-a.shape[0]) % mult
    return a if extra == 0 else jnp.concatenate([a, jnp.zeros((extra,) + a.shape[1:], a.dtype)], axis=0)


def _pack_rows(parts, width, mult):
    flat = jnp.concatenate([t.reshape(-1) for t in parts])
    extra = (-flat.shape[0]) % width
    if extra:
        flat = jnp.concatenate([flat, jnp.zeros((extra,), flat.dtype)])
    return _pad_rows(flat.reshape(-1, width), mult)


def _take(flat, off, shape):
    n = math.prod(shape)
    return flat[..., off:off + n].reshape(flat.shape[:-1] + tuple(shape)), off + n


SMALL = ("mod", "norm_g", "sb_q_g", "sb_k_g", "dn_a_log", "dn_dt_bias", "dn_norm_g")


def kernel(x, c, ada_w, ada_b, norm_g, w_in, sb_q_g, sb_k_g, conv_w, dn_a_log, dn_dt_bias, dn_norm_g, w_branch_sb, w_branch_dn, w_out, loss_target, m_ada_w, m_ada_b, m_norm_g, m_w_in, m_sb_q_g, m_sb_k_g, m_conv_w, m_dn_a_log, m_dn_dt_bias, m_dn_norm_g, m_w_branch_sb, m_w_branch_dn, m_w_out, v_ada_w, v_ada_b, v_norm_g, v_w_in, v_sb_q_g, v_sb_k_g, v_conv_w, v_dn_a_log, v_dn_dt_bias, v_dn_norm_g, v_w_branch_sb, v_w_branch_dn, v_w_out):
    L, D = norm_g.shape
    S = x.shape[1]
    n_in = w_in.shape[2]
    n_ada = ada_w.shape[2]
    n_br = w_branch_sb.shape[2]
    n_out = w_out.shape[1]
    n_conv = conv_w.shape[2]
    me = _lin(_me())

    def cat(a):
        return jnp.concatenate([a[d] for d in range(N_DEV)], axis=1)

    c_all, conv_all = _exchange([c, conv_w.reshape(L * CONV_K, n_conv)], False, "gather_small")
    c_all = c_all.reshape(N_DEV, D)
    conv_full = cat(conv_all).reshape(L, CONV_K, N_DEV * n_conv)

    mod_part = _mod_fwd(c_all, ada_w, "mod_fwd")

    def shards16(l):
        return [w_in[l].astype(BF16), w_branch_sb[l].astype(BF16), w_branch_dn[l].astype(BF16), w_out[l].astype(BF16)]

    def whole(l, got):
        wi, wbs, wbd, wo = got
        return dict(norm_g=norm_g[l:l + 1], w_in=_pad_cols(cat(wi)), sb_q_g=sb_q_g[l:l + 1], sb_k_g=sb_k_g[l:l + 1],
                    conv_w=conv_full[l], dn_a_log=dn_a_log[l], dn_dt_bias=dn_dt_bias[l], dn_norm_g=dn_norm_g[l:l + 1],
                    w_branch_sb=cat(wbs), w_branch_dn=cat(wbd), w_out=wo.reshape(N_DEV * n_out, D))

    *got, mod_all = _exchange(shards16(0) + [mod_part.reshape(L * N_DEV, n_ada)], False, "gather_weights")
    mod_full = cat(mod_all).reshape(L, N_DEV, N_DEV * n_ada) + ada_b[:, None, :]
    mod_mine = lax.dynamic_slice_in_dim(mod_full, me, 1, axis=1)

    act = x[0]
    saved, wts = [], []
    for l in range(L):
        wts.append(whole(l, got))
        act, sv, got = _layer_fwd(l, act, mod_mine[l], wts[l], (shards16(l + 1), False) if l + 1 < L else None)
        saved.append(sv)
    loss_cols, dact = _loss_fwd_bwd(act, loss_target[0], "loss")
    loss = lax.psum(jnp.sum(loss_cols), ("x", "y", "c"))

    def blocks(name, g):
        if name == "w_out":
            return g.astype(BF16).reshape(N_DEV, n_out, D)
        n = g.shape[1] // N_DEV
        dtype = F32 if name == "conv_w" else BF16
        return jnp.stack([g[:, d * n:(d + 1) * n].astype(dtype) for d in range(N_DEV)])

    early, late = ("w_out", "w_branch_sb", "w_branch_dn"), ("w_in", "conv_w")
    grads, recv, pending = [None] * L, {}, []
    for l in reversed(range(L)):
        keys = [k for k, _ in pending] + [(n, l) for n in early]

        def carry_of(g_early, pending=pending):
            return [a for _, a in pending] + [blocks(n, g_early[n]) for n in early], True

        dact, grads[l], got = _layer_bwd(l, dact, saved[l], wts[l], carry_of)
        recv.update(zip(keys, got))
        pending = [((n, l), blocks(n, grads[l][n])) for n in late]
    recv.update(zip([k for k, _ in pending], _exchange([a for _, a in pending], True, "scatter_grads")))
    grad_x = dact[None]
    small_g = _pack_rows([grads[l][n] for l in range(L) for n in SMALL], LANES, 8)
    (small_all_g,) = _exchange([small_g], False, "gather_small_grads")
    small_sum = _sum_slots(small_all_g, "sum_small_grads").reshape(-1)
    shard_shapes = dict(w_in=w_in.shape, w_branch_sb=w_branch_sb.shape, w_branch_dn=w_branch_dn.shape,
                        conv_w=conv_w.shape, w_out=w_out.shape)
    g_out = {n: jnp.stack([_sum_slots(recv[(n, l)], f"sum_{n}_l{l}").reshape(shape[1:]) for l in range(L)])
             for n, shape in shard_shapes.items()}
    small_shapes = dict(mod=(3 * D,), norm_g=(D,), sb_q_g=(SB_DH,), sb_k_g=(SB_DH,), dn_a_log=(DN_HEADS,),
                        dn_dt_bias=(DN_HEADS,), dn_norm_g=(DN_DH,))
    off = 0
    off_all = 0
    small_each = small_all_g.reshape(N_DEV, -1)
    per_small = {n: [] for n in SMALL}
    dmod_all = []
    for l in range(L):
        for n in SMALL:
            t, off = _take(small_sum, off, small_shapes[n])
            per_small[n].append(t)
            if n == "mod":
                t_all, _ = _take(small_each, off_all, small_shapes[n])
                dmod_all.append(t_all)
            off_all += math.prod(small_shapes[n])
    for n in SMALL:
        g_out[n if n != "mod" else "ada_b"] = jnp.stack(per_small[n])
    dmod_all = jnp.stack(dmod_all)
    dmod_cols = lax.dynamic_slice_in_dim(dmod_all, me * n_ada, n_ada, axis=2)
    g_out["ada_w"] = _mod_bwd_w(c_all.T, dmod_cols, "mod_bwd_w")

    given = dict(ada_w=(ada_w, m_ada_w, v_ada_w), ada_b=(ada_b, m_ada_b, v_ada_b), norm_g=(norm_g, m_norm_g, v_norm_g),
                 w_in=(w_in, m_w_in, v_w_in), sb_q_g=(sb_q_g, m_sb_q_g, v_sb_q_g), sb_k_g=(sb_k_g, m_sb_k_g, v_sb_k_g),
                 conv_w=(conv_w, m_conv_w, v_conv_w), dn_a_log=(dn_a_log, m_dn_a_log, v_dn_a_log),
                 dn_dt_bias=(dn_dt_bias, m_dn_dt_bias, v_dn_dt_bias), dn_norm_g=(dn_norm_g, m_dn_norm_g, v_dn_norm_g),
                 w_branch_sb=(w_branch_sb, m_w_branch_sb, v_w_branch_sb),
                 w_branch_dn=(w_branch_dn, m_w_branch_dn, v_w_branch_dn), w_out=(w_out, m_w_out, v_w_out))
    order = list(given)
    upd = {n: _adamw(given[n][0], g_out[n], given[n][1], given[n][2], "adamw_" + n) for n in order}
    return (loss, grad_x, *[g_out[n] for n in order], *[upd[n][0] for n in order], *[upd[n][1] for n in order],
            *[upd[n][2] for n in order])
```

```python
import functools
import math

import jax
import jax.numpy as jnp
from jax import lax
from jax.experimental import pallas as pl
from jax.experimental.pallas import tpu as pltpu

F32 = jnp.float32
BF16 = jnp.bfloat16
HI = lax.Precision.HIGHEST

N_DEV = 8
EPS = 1e-6
SB_HEADS, SB_DH = 8, 64
DN_HEADS, DN_DH = 4, 128
SB_W = SB_HEADS * SB_DH
DN_W = DN_HEADS * DN_DH
CONV_K = 4
BLK = 128
SB_KEYS = 512
SB_QB = 512
LANES = 128
IN_MAIN = 4 * SB_W + 4 * DN_W
IN_COLS = IN_MAIN + 2 * DN_HEADS
ADAM_LR, ADAM_B1, ADAM_B2, ADAM_EPS, ADAM_WD, ADAM_STEP = 0.001, 0.9, 0.999, 1e-08, 0.01, 10
VMEM_LIMIT = 56 * 1024 * 1024
SUM_ROWS = 128


def _cp(*sem, vmem=VMEM_LIMIT):
    return pltpu.CompilerParams(dimension_semantics=sem if sem else None, vmem_limit_bytes=vmem)


def _dot(a, b, prec=HI):
    return lax.dot_general(a, b, (((1,), (0,)), ((), ())), precision=prec, preferred_element_type=F32)


def _dot_nt(a, b, prec=HI):
    return lax.dot_general(a, b, (((1,), (1,)), ((), ())), precision=prec, preferred_element_type=F32)


def _bdot(a, b):
    return lax.dot_general(a.astype(BF16), b.astype(BF16), (((1,), (0,)), ((), ())), preferred_element_type=F32)


def _bdot_nt(a, b):
    return lax.dot_general(a.astype(BF16), b.astype(BF16), (((1,), (1,)), ((), ())), preferred_element_type=F32)


def _bdot_tn(a, b):
    return lax.dot_general(a.astype(BF16), b.astype(BF16), (((0,), (0,)), ((), ())), preferred_element_type=F32)


def _split_dot(a, b01):
    hi = a.astype(BF16)
    lo = (a - hi.astype(F32)).astype(BF16)
    return (lax.dot_general(hi, b01, (((1,), (0,)), ((), ())), preferred_element_type=F32)
            + lax.dot_general(lo, b01, (((1,), (0,)), ((), ())), preferred_element_type=F32))


def _sigmoid(x):
    return 1.0 / (1.0 + jnp.exp(-x))


def _silu(x):
    return x * _sigmoid(x)


def _softplus(x):
    return jnp.maximum(x, 0.0) + jnp.log(1.0 + jnp.exp(-jnp.abs(x)))


def _rms(x):
    return x * lax.rsqrt(jnp.mean(x * x, axis=-1, keepdims=True) + EPS)


def _prenorm(x, g, shift, scale):
    return _rms(x) * g * (1.0 + scale) + shift


def _inproj_fwd(x, mod, g, w, name):
    S, D = x.shape
    N = w.shape[1]
    tm = min(512, S)
    tn = 896 if N % 896 == 0 else 128

    def body(x_ref, mod_ref, g_ref, w_ref, p_ref, h_ref):
        @pl.when(pl.program_id(1) == 0)
        def _():
            h = _prenorm(x_ref[...], g_ref[...], mod_ref[:, 0:D], mod_ref[:, D:2 * D])
            h_ref[...] = h.astype(BF16)

        p_ref[...] = jnp.dot(h_ref[...], w_ref[...], preferred_element_type=F32)

    return pl.pallas_call(
        body, name=name, grid=(S // tm, N // tn),
        in_specs=[pl.BlockSpec((tm, D), lambda i, j: (i, 0)), pl.BlockSpec((1, 3 * D), lambda i, j: (0, 0)),
                  pl.BlockSpec((1, D), lambda i, j: (0, 0)), pl.BlockSpec((D, tn), lambda i, j: (0, j))],
        out_specs=[pl.BlockSpec((tm, tn), lambda i, j: (i, j)), pl.BlockSpec((tm, D), lambda i, j: (i, 0))],
        out_shape=[jax.ShapeDtypeStruct((S, N), F32), jax.ShapeDtypeStruct((S, D), BF16)],
        compiler_params=_cp("parallel", "arbitrary"),
    )(x, mod, g, w)


def _inproj_bwd_dx(dp, w, x, mod, g, dxn, name, carry=None):
    S, N = dp.shape
    D = x.shape[1]
    tm = min(512, S)
    tk = 896 if N % 896 == 0 else 128
    nk = N // tk

    def body(dp_ref, w_ref, x_ref, mod_ref, g_ref, dxn_ref, dx_ref, dmod_ref, dg_ref, acc):
        i, k = pl.program_id(0), pl.program_id(1)

        @pl.when(k == 0)
        def _():
            acc[...] = jnp.zeros_like(acc)

        @pl.when((i == 0) & (k == 0))
        def _():
            dmod_ref[...] = jnp.zeros_like(dmod_ref)
            dg_ref[...] = jnp.zeros_like(dg_ref)

        acc[...] += lax.dot_general(dp_ref[...], w_ref[...], (((1,), (1,)), ((), ())), preferred_element_type=F32)

        @pl.when(k == nk - 1)
        def _():
            _, vjp = jax.vjp(_prenorm, x_ref[...], g_ref[...], mod_ref[:, 0:D], mod_ref[:, D:2 * D])
            dx, dg, dshift, dscale = vjp(acc[...])
            dx_ref[...] = dxn_ref[...] + dx
            dg_ref[...] += dg
            dmod_ref[:, 0:D] += dshift
            dmod_ref[:, D:2 * D] += dscale

    return _call_carrying(
        body, name, (S // tm, nk), carry, [dp, w, x, mod, g, dxn],
        in_specs=[pl.BlockSpec((tm, tk), lambda i, k: (i, k)), pl.BlockSpec((D, tk), lambda i, k: (0, k)),
                  pl.BlockSpec((tm, D), lambda i, k: (i, 0)), pl.BlockSpec((1, 3 * D), lambda i, k: (0, 0)),
                  pl.BlockSpec((1, D), lambda i, k: (0, 0)), pl.BlockSpec((tm, D), lambda i, k: (i, 0))],
        out_specs=[pl.BlockSpec((tm, D), lambda i, k: (i, 0)), pl.BlockSpec((1, 3 * D), lambda i, k: (0, 0)),
                   pl.BlockSpec((1, D), lambda i, k: (0, 0))],
        out_shape=[jax.ShapeDtypeStruct((S, D), F32), jax.ShapeDtypeStruct((1, 3 * D), F32),
                   jax.ShapeDtypeStruct((1, D), F32)],
        scratch_shapes=[pltpu.VMEM((tm, D), F32)], vmem=VMEM_LIMIT)


def _matmul_tn(a_t, b, name):
    M, K = a_t.shape
    N = b.shape[1]
    tn = 896 if N % 896 == 0 else (512 if N % 512 == 0 else 128)
    tk = min(512, K)
    nk = K // tk

    def body(a_ref, b_ref, o_ref):
        @pl.when(pl.program_id(1) == 0)
        def _():
            o_ref[...] = jnp.zeros_like(o_ref)

        o_ref[...] += jnp.dot(a_ref[...], b_ref[...], preferred_element_type=F32)

    return pl.pallas_call(
        body, name=name, grid=(N // tn, nk),
        in_specs=[pl.BlockSpec((M, tk), lambda j, k: (0, k)), pl.BlockSpec((tk, tn), lambda j, k: (k, j))],
        out_specs=pl.BlockSpec((M, tn), lambda j, k: (0, j)),
        out_shape=jax.ShapeDtypeStruct((M, N), F32),
        compiler_params=_cp("parallel", "arbitrary"),
    )(a_t, b)


def _qk_norm(t, g, scale):
    return _rms(t) * g * scale


def _qk_norm_t(t, g_col, scale):
    return t * lax.rsqrt(jnp.mean(t * t, axis=0, keepdims=True) + EPS) * g_col * scale


def _suffix_sums(x, tri):
    half = tri.shape[0]
    lo, hi = x[:, :half], x[:, half:]
    hi_sum = jnp.sum(hi, axis=1, keepdims=True)
    y = jnp.concatenate([_split_dot(lo, tri) + hi_sum, _split_dot(hi, tri)], axis=1)
    return y, hi_sum + jnp.sum(lo, axis=1, keepdims=True)


def _sb_step(qi, kat_blk, cl, upper, valid):
    z = jnp.dot(qi, kat_blk, preferred_element_type=F32)
    lk = jnp.minimum(-z, 0.0) - jnp.log(1.0 + jnp.exp(-jnp.abs(z)))
    if valid is not None:
        lk = jnp.where(valid, lk, 0.0)
    later, tot = _suffix_sums(lk, upper)
    w = jnp.exp(z + lk + later + cl)
    if valid is not None:
        w = jnp.where(valid, w, 0.0)
    return z, lk, w, tot


def _sb_masks(kb):
    half = kb // 2
    r = lax.broadcasted_iota(jnp.int32, (half, half), 0)
    c = lax.broadcasted_iota(jnp.int32, (half, half), 1)
    rq = lax.broadcasted_iota(jnp.int32, (SB_QB, kb), 0)
    ck = lax.broadcasted_iota(jnp.int32, (SB_QB, kb), 1)
    return (r > c).astype(BF16), (r >= c).astype(BF16), ck - rq


def _sb_fwd(p, gq, gkt, name, carry=None):
    S, dh = p.shape[0], SB_DH
    kb = min(SB_KEYS, S)
    per = kb // SB_QB
    nb = S // SB_QB
    scale = 1.0 / math.sqrt(dh)
    pairs = SB_W // LANES

    def body(q_ref, k_ref, v_ref, gq_ref, gkt_ref, o_ref, kt2, qa, kat, vb):
        kt2[...] = k_ref[...].T
        upper, _, diff = _sb_masks(kb)
        for hh in range(LANES // dh):
            lanes = slice(hh * dh, (hh + 1) * dh)
            qa[...] = _qk_norm(q_ref[:, lanes], gq_ref[...], scale).astype(BF16)
            kat[...] = _qk_norm_t(kt2[lanes, :], gkt_ref[...], 1.0).astype(BF16)
            vb[...] = v_ref[:, lanes].astype(BF16)

            def qblock(i, _):
                rows = pl.ds(pl.multiple_of(i * SB_QB, SB_QB), SB_QB)
                qi = qa[rows, :]
                sbd = i // per

                def step(sb, cl, acc, valid):
                    cols = pl.ds(pl.multiple_of(sb * kb, kb), kb)
                    _, _, w, tot = _sb_step(qi, kat[:, cols], cl, upper, valid)
                    return cl + tot, acc + jnp.dot(w.astype(BF16), vb[cols, :], preferred_element_type=F32)

                cl, acc = step(sbd, jnp.zeros((SB_QB, 1), F32), jnp.zeros((SB_QB, dh), F32),
                               diff < (i - sbd * per) * SB_QB)
                _, acc = lax.fori_loop(0, sbd, lambda jj, c: step(sbd - 1 - jj, c[0], c[1], None), (cl, acc))
                o_ref[rows, lanes] = acc
                return 0

            lax.fori_loop(0, nb, qblock, 0)

    blk = lambda off: pl.BlockSpec((S, LANES), lambda g: (0, off + g))
    return _call_carrying(
        body, name, pairs, carry, [p, p, p, gq, gkt],
        in_specs=[blk(0), blk(pairs), blk(2 * pairs), pl.BlockSpec((1, dh), lambda g: (0, 0)),
                  pl.BlockSpec((dh, 1), lambda g: (0, 0))],
        out_specs=[blk(0)], out_shape=[jax.ShapeDtypeStruct((S, SB_W), F32)],
        scratch_shapes=[pltpu.VMEM((LANES, S), F32), pltpu.VMEM((S, dh), BF16), pltpu.VMEM((dh, S), BF16),
                        pltpu.VMEM((S, dh), BF16)],
        vmem=VMEM_LIMIT)


def _call_carrying(body, name, grid, carry, operands, in_specs, out_specs, out_shape, scratch_shapes, vmem):
    grid = (grid,) if isinstance(grid, int) else tuple(grid)
    if carry is None:
        res = pl.pallas_call(body, name=name, grid=grid, in_specs=in_specs, out_specs=out_specs,
                             out_shape=out_shape, scratch_shapes=scratch_shapes,
                             compiler_params=_cp(*["arbitrary"] * len(grid), vmem=vmem))(*operands)
        return res, []

    def at(corner):
        hit = pl.program_id(0) == corner(grid[0])
        for axis in range(1, len(grid)):
            hit = jnp.logical_and(hit, pl.program_id(axis) == corner(grid[axis]))
        return hit

    ex = _Exchange(*carry)
    n_in, n_out, n_scr = len(in_specs), len(out_specs), len(scratch_shapes)

    def wrapped(*refs):
        ins, refs = refs[:n_in], refs[n_in:]
        xin, refs = refs[:ex.n], refs[ex.n:]
        outs, refs = refs[:n_out], refs[n_out:]
        xout, refs = refs[:ex.n], refs[ex.n:]
        scr, sems = refs[:n_scr], refs[n_scr:]

        @pl.when(at(lambda n: 0))
        def _():
            ex.start(xin, xout, sems)

        body(*ins, *outs, *scr)

        @pl.when(at(lambda n: n - 1))
        def _():
            ex.finish(xin, xout, sems)

    res = pl.pallas_call(wrapped, name=name, grid=grid, in_specs=in_specs + ex.specs,
                         out_specs=out_specs + ex.specs, out_shape=out_shape + ex.out_shape,
                         scratch_shapes=scratch_shapes + ex.scratch,
                         compiler_params=_cp(*["arbitrary"] * len(grid), vmem=vmem))(*operands, *carry[0])
    return res[:n_out], res[n_out:]


def _sb_bwd(p, o, do, gq, gqt, gk, gkt, name, carry=None):
    S, dh = p.shape[0], SB_DH
    kb = min(SB_KEYS, S)
    per = kb // SB_QB
    nb = S // SB_QB
    scale = 1.0 / math.sqrt(dh)
    pairs = SB_W // LANES
    per_pair = LANES // dh

    def body(q_ref, k_ref, v_ref, o_ref, do_ref, gq_ref, gqt_ref, gk_ref, gkt_ref,
             dq_ref, dk_ref, dv_ref, dgq_ref, dgkt_ref,
             qt2, kt2, vt2, dot2, dkt2, dvt2, qa, qat, ka, kat, vb, vtb, dob, dotb, dqa):
        qt2[...] = q_ref[...].T
        kt2[...] = k_ref[...].T
        vt2[...] = v_ref[...].T
        dot2[...] = do_ref[...].T
        dkt2[...] = jnp.zeros_like(dkt2)
        dvt2[...] = jnp.zeros_like(dvt2)
        upper, lower_incl, diff = _sb_masks(kb)
        for hh in range(per_pair):
            lanes = slice(hh * dh, (hh + 1) * dh)
            qa[...] = _qk_norm(q_ref[:, lanes], gq_ref[...], scale).astype(BF16)
            qat[...] = _qk_norm_t(qt2[lanes, :], gqt_ref[...], scale).astype(BF16)
            ka[...] = _qk_norm(k_ref[:, lanes], gk_ref[...], 1.0).astype(BF16)
            kat[...] = _qk_norm_t(kt2[lanes, :], gkt_ref[...], 1.0).astype(BF16)
            vb[...] = v_ref[:, lanes].astype(BF16)
            vtb[...] = vt2[lanes, :].astype(BF16)
            dob[...] = do_ref[:, lanes].astype(BF16)
            dotb[...] = dot2[lanes, :].astype(BF16)

            def qblock(i, _):
                rows = pl.ds(pl.multiple_of(i * SB_QB, SB_QB), SB_QB)
                qi, qit = qa[rows, :], qat[:, rows]
                doi, doit = dob[rows, :], dotb[:, rows]
                total = jnp.sum(doi.astype(F32) * o_ref[rows, lanes], axis=1, keepdims=True)
                sbd = i // per

                def step(sb, cl, cd, dqi, valid):
                    cols = pl.ds(pl.multiple_of(sb * kb, kb), kb)
                    z, lk, w, tot = _sb_step(qi, kat[:, cols], cl, upper, valid)
                    w16 = w.astype(BF16)
                    dl = jnp.dot(doi, vtb[:, cols], preferred_element_type=F32) * w16.astype(F32)
                    incl, dtot = _suffix_sums(dl, lower_incl)
                    sig = jnp.exp(z + lk)
                    dz = dl - sig * (dl + (total - cd - incl))
                    if valid is not None:
                        dz = jnp.where(valid, dz, 0.0)
                    dz16 = dz.astype(BF16)
                    dqi = dqi + jnp.dot(dz16, ka[cols, :], preferred_element_type=F32)
                    dkt2[lanes, cols] += jnp.dot(qit, dz16, preferred_element_type=F32)
                    dvt2[lanes, cols] += jnp.dot(doit, w16, preferred_element_type=F32)
                    return cl + tot, cd + dtot, dqi

                zero = jnp.zeros((SB_QB, 1), F32)
                first = step(sbd, zero, zero, jnp.zeros((SB_QB, dh), F32), diff < (i - sbd * per) * SB_QB)
                _, _, dqi = lax.fori_loop(0, sbd, lambda jj, c: step(sbd - 1 - jj, c[0], c[1], c[2], None), first)
                dqa[rows, :] = dqi
                return 0

            lax.fori_loop(0, nb, qblock, 0)
            _, vq = jax.vjp(lambda t, g: _qk_norm(t, g, scale), q_ref[:, lanes], gq_ref[...])
            dq, dgq = vq(dqa[...])
            dq_ref[:, lanes] = dq.astype(BF16)
            dgq_ref[hh] = dgq
            _, vk = jax.vjp(lambda t, g: _qk_norm_t(t, g, 1.0), kt2[lanes, :], gkt_ref[...])
            dkt, dgkt = vk(dkt2[lanes, :])
            dkt2[lanes, :] = dkt
            dgkt_ref[hh] = dgkt
        dk_ref[...] = dkt2[...].T.astype(BF16)
        dv_ref[...] = dvt2[...].T.astype(BF16)

    blk = lambda off: pl.BlockSpec((S, LANES), lambda g: (0, off + g))
    once = lambda off: pl.BlockSpec((S, LANES), lambda g: (0, off + g), pipeline_mode=pl.Buffered(1))
    gr = pl.BlockSpec((1, dh), lambda g: (0, 0))
    gc = pl.BlockSpec((dh, 1), lambda g: (0, 0))
    sd = jax.ShapeDtypeStruct((S, SB_W), BF16)
    return _call_carrying(
        body, name, pairs, carry, [p, p, p, o, do, gq, gqt, gk, gkt],
        in_specs=[once(0), once(pairs), once(2 * pairs), once(0), once(0), gr, gc, gr, gc],
        out_specs=[blk(0), blk(0), blk(0), pl.BlockSpec((per_pair, 1, dh), lambda g: (g, 0, 0)),
                   pl.BlockSpec((per_pair, dh, 1), lambda g: (g, 0, 0))],
        out_shape=[sd, sd, sd, jax.ShapeDtypeStruct((SB_HEADS, 1, dh), F32),
                   jax.ShapeDtypeStruct((SB_HEADS, dh, 1), F32)],
        scratch_shapes=[pltpu.VMEM((LANES, S), F32)] * 6 + [pltpu.VMEM((S, dh), BF16), pltpu.VMEM((dh, S), BF16)] * 4
        + [pltpu.VMEM((S, dh), F32)],
        vmem=60 * 1024 * 1024)


def _shift_down(x, s, rows):
    if s == 0:
        return x
    return jnp.where(rows >= s, pltpu.roll(x, s, 0), 0.0)


def _shift_up(x, s, rows, n):
    if s == 0:
        return x
    return jnp.where(rows < n - s, pltpu.roll(x, n - s, 0), 0.0)


def _conv(x, w_ref, rows):
    y = x * w_ref[CONV_K - 1:CONV_K, :]
    for kk in range(CONV_K - 1):
        y = y + _shift_down(x, CONV_K - 1 - kk, rows) * w_ref[kk:kk + 1, :]
    return y


def _act_norm(y, normed):
    s = _silu(y)
    n = s * lax.rsqrt(jnp.sum(s * s, axis=-1, keepdims=True) + EPS)
    return jnp.where(normed, n, s)


def _dn_prep_fwd(p, conv_w, col0, name):
    S = p.shape[0]
    nblk = 3 * DN_HEADS
    b0 = col0 // DN_DH

    def body(x_ref, w_ref, o_ref):
        rows = lax.broadcasted_iota(jnp.int32, (S, DN_DH), 0)
        y = _conv(x_ref[...], w_ref, rows)
        o_ref[...] = _act_norm(y, pl.program_id(0) < 2 * DN_HEADS)

    return pl.pallas_call(
        body, name=name, grid=(nblk,),
        in_specs=[pl.BlockSpec((S, DN_DH), lambda j: (0, b0 + j)), pl.BlockSpec((CONV_K, DN_DH), lambda j: (0, j))],
        out_specs=pl.BlockSpec((S, DN_DH), lambda j: (0, j)),
        out_shape=jax.ShapeDtypeStruct((S, 3 * DN_W), F32),
        compiler_params=_cp("parallel"),
    )(p, conv_w)


def _dn_prep_bwd(p, conv_w, col0, dout, name):
    S = p.shape[0]
    nblk = 3 * DN_HEADS
    b0 = col0 // DN_DH

    def body(x_ref, w_ref, do_ref, dx_ref, dw_ref):
        rows = lax.broadcasted_iota(jnp.int32, (S, DN_DH), 0)
        x = x_ref[...]
        y = _conv(x, w_ref, rows)
        normed = pl.program_id(0) < 2 * DN_HEADS
        _, vjp = jax.vjp(lambda t: _act_norm(t, normed), y)
        (dy,) = vjp(do_ref[...])
        dx = dy * w_ref[CONV_K - 1:CONV_K, :]
        dw_ref[CONV_K - 1:CONV_K, :] = jnp.sum(dy * x, axis=0, keepdims=True)
        for kk in range(CONV_K - 1):
            s = CONV_K - 1 - kk
            dx = dx + _shift_up(dy, s, rows, S) * w_ref[kk:kk + 1, :]
            dw_ref[kk:kk + 1, :] = jnp.sum(dy * _shift_down(x, s, rows), axis=0, keepdims=True)
        dx_ref[...] = dx.astype(BF16)

    return pl.pallas_call(
        body, name=name, grid=(nblk,),
        in_specs=[pl.BlockSpec((S, DN_DH), lambda j: (0, b0 + j)), pl.BlockSpec((CONV_K, DN_DH), lambda j: (0, j)),
                  pl.BlockSpec((S, DN_DH), lambda j: (0, j))],
        out_specs=[pl.BlockSpec((S, DN_DH), lambda j: (0, j)), pl.BlockSpec((CONV_K, DN_DH), lambda j: (0, j))],
        out_shape=[jax.ShapeDtypeStruct((S, 3 * DN_W), BF16), jax.ShapeDtypeStruct((CONV_K, 3 * DN_W), F32)],
        compiler_params=_cp("parallel"),
    )(p, conv_w, dout)


def _gate_fn(x, pv):
    lane = lax.broadcasted_iota(jnp.int32, x.shape, 1)
    decay = -jnp.exp(pv[0:1, :]) * _softplus(x + pv[1:2, :])
    return jnp.where(lane < DN_HEADS, _sigmoid(x), decay)


def _dn_gate_fwd(p, pv, blk, name):
    S = p.shape[0]

    def body(x_ref, pv_ref, o_ref):
        o_ref[...] = _gate_fn(x_ref[...], pv_ref[...])

    return pl.pallas_call(
        body, name=name, grid=(1,),
        in_specs=[pl.BlockSpec((S, LANES), lambda i: (0, blk)), pl.BlockSpec((2, LANES), lambda i: (0, 0))],
        out_specs=pl.BlockSpec((S, LANES), lambda i: (0, 0)),
        out_shape=jax.ShapeDtypeStruct((S, LANES), F32),
        compiler_params=_cp("arbitrary"),
    )(p, pv)


def _dn_gate_bwd(p, pv, blk, dout, name):
    S = p.shape[0]

    def body(x_ref, pv_ref, do_ref, dx_ref, dpv_ref):
        _, vjp = jax.vjp(_gate_fn, x_ref[...], pv_ref[...])
        dx, dpv = vjp(do_ref[...])
        dx_ref[...] = dx.astype(BF16)
        dpv_ref[...] = dpv

    return pl.pallas_call(
        body, name=name, grid=(1,),
        in_specs=[pl.BlockSpec((S, LANES), lambda i: (0, blk)), pl.BlockSpec((2, LANES), lambda i: (0, 0)),
                  pl.BlockSpec((S, LANES), lambda i: (0, 0))],
        out_specs=[pl.BlockSpec((S, LANES), lambda i: (0, 0)), pl.BlockSpec((2, LANES), lambda i: (0, 0))],
        out_shape=[jax.ShapeDtypeStruct((S, LANES), BF16), jax.ShapeDtypeStruct((2, LANES), F32)],
        compiler_params=_cp("arbitrary"),
    )(p, pv, dout)


def _t(x):
    return jnp.swapaxes(x, -1, -2)


def _matmuls(prec, differentiable):
    def mm(a, b):
        return lax.dot_general(a, b, (((2,), (1,)), ((0,), (0,))), precision=prec, preferred_element_type=F32)

    def mm_nt(a, b):
        return lax.dot_general(a, b, (((2,), (2,)), ((0,), (0,))), precision=prec, preferred_element_type=F32)

    if not differentiable:
        return mm, mm_nt
    dmm, dmm_nt = jax.custom_vjp(mm), jax.custom_vjp(mm_nt)
    dmm.defvjp(lambda a, b: (mm(a, b), (a, b)), lambda res, g: (mm_nt(g, res[1]), mm(_t(res[0]), g)))
    dmm_nt.defvjp(lambda a, b: (mm_nt(a, b), (a, b)), lambda res, g: (mm(g, res[1]), mm(_t(g), res[0])))
    return dmm, dmm_nt


def _delta_chunk(state, q, k, v, beta, a_col, a_row, differentiable=False):
    mm, mm_nt = _matmuls(lax.Precision.HIGH, differentiable)
    mm_sum, _ = _matmuls(HI, differentiable)
    H, C, _ = q.shape
    r = lax.broadcasted_iota(jnp.int32, (H, C, C), 1)
    c = lax.broadcasted_iota(jnp.int32, (H, C, C), 2)
    tril, strict = r >= c, r > c
    eye = (r == c).astype(F32)
    g_c = mm_sum(tril.astype(F32), jnp.broadcast_to(a_col, (H, C, C)))
    g_r = mm_sum(jnp.broadcast_to(a_row, (H, C, C)), (r <= c).astype(F32))
    decay = jnp.where(tril, jnp.exp(jnp.where(tril, g_c - g_r, 0.0)), 0.0)
    eg = jnp.exp(g_c)
    g_last = jnp.sum(jnp.where(r == C - 1, g_c, 0.0), axis=1, keepdims=True)
    qs = q * (float(q.shape[2]) ** -0.5)
    kb = k * beta
    neg_m = jnp.where(strict, -(mm_nt(kb, k) * decay), 0.0)
    inv = eye + neg_m
    pw = neg_m
    for _ in range(int(math.log2(C)) - 1):
        pw = mm(pw, pw)
        inv = inv + mm(inv, pw)
    u = mm(inv, v * beta)
    w = mm(inv, kb * eg)
    intra = jnp.where(tril, mm_nt(qs, k) * decay, 0.0)
    v_new = u - mm(w, state)
    o = mm(qs * eg, state) + mm(intra, v_new)
    nxt = state * jnp.exp(g_last) + mm(_t(k * jnp.exp(g_last - g_c)), v_new)
    return o, nxt


def _heads(t):
    return jnp.stack([t[:, h * DN_DH:(h + 1) * DN_DH] for h in range(DN_HEADS)])


def _delta_step(state, q, k, v, bg, a_row, differentiable=False):
    lane = lax.broadcasted_iota(jnp.int32, bg.shape, 1)
    pick = lambda j: jnp.stack([jnp.sum(jnp.where(lane == j + h, bg, 0.0), axis=1, keepdims=True)
                                for h in range(DN_HEADS)])
    o, nxt = _delta_chunk(state, _heads(q), _heads(k), _heads(v), pick(0), pick(DN_HEADS), a_row, differentiable)
    return jnp.concatenate([o[h] for h in range(DN_HEADS)], axis=1), nxt


def _delta_fwd(qkv, bg, a_row, name):
    S = qkv.shape[0]
    nc = S // BLK

    def body(q_ref, k_ref, v_ref, bg_ref, ar_ref, o_ref, st_ref, state):
        ci = pl.program_id(0)

        @pl.when(ci == 0)
        def _():
            state[...] = jnp.zeros_like(state)

        st = state[...]
        st_ref[:, 0] = st
        o, nxt = _delta_step(st, q_ref[...], k_ref[...], v_ref[...], bg_ref[...], ar_ref[:, pl.ds(ci, 1), :])
        o_ref[...] = o
        state[...] = nxt

    part = lambda j: pl.BlockSpec((BLK, DN_W), lambda c: (c, j))
    return pl.pallas_call(
        body, name=name, grid=(nc,),
        in_specs=[part(0), part(1), part(2), pl.BlockSpec((BLK, LANES), lambda c: (c, 0)),
                  pl.BlockSpec((DN_HEADS, nc, BLK), lambda c: (0, 0, 0))],
        out_specs=[part(0), pl.BlockSpec((DN_HEADS, 1, DN_DH, DN_DH), lambda c: (0, c, 0, 0))],
        out_shape=[jax.ShapeDtypeStruct((S, DN_W), F32), jax.ShapeDtypeStruct((DN_HEADS, nc, DN_DH, DN_DH), F32)],
        scratch_shapes=[pltpu.VMEM((DN_HEADS, DN_DH, DN_DH), F32)],
        compiler_params=_cp("arbitrary"),
    )(qkv, qkv, qkv, bg, a_row)


def _delta_bwd(qkv, bg, a_row, states, do, name):
    S = qkv.shape[0]
    nc = S // BLK

    def body(q_ref, k_ref, v_ref, bg_ref, ar_ref, st_ref, do_ref, dq_ref, dk_ref, dv_ref, dbg_ref, dar_ref, dstate):
        t = pl.program_id(0)
        ci = nc - 1 - t

        @pl.when(t == 0)
        def _():
            dstate[...] = jnp.zeros_like(dstate)

        _, vjp = jax.vjp(functools.partial(_delta_step, differentiable=True), st_ref[:, 0], q_ref[...], k_ref[...],
                         v_ref[...], bg_ref[...], ar_ref[:, pl.ds(ci, 1), :])
        dprev, dq, dk, dv, dbg, dar = vjp((do_ref[...], dstate[...]))
        dq_ref[...] = dq
        dk_ref[...] = dk
        dv_ref[...] = dv
        dbg_ref[...] = dbg
        dar_ref[:, pl.ds(ci, 1), :] = dar
        dstate[...] = dprev

    part = lambda j: pl.BlockSpec((BLK, DN_W), lambda t: (nc - 1 - t, j))
    lanes = pl.BlockSpec((BLK, LANES), lambda t: (nc - 1 - t, 0))
    rows = pl.BlockSpec((DN_HEADS, nc, BLK), lambda t: (0, 0, 0))
    d3 = jax.ShapeDtypeStruct((S, DN_W), F32)
    return pl.pallas_call(
        body, name=name, grid=(nc,),
        in_specs=[part(0), part(1), part(2), lanes, rows,
                  pl.BlockSpec((DN_HEADS, 1, DN_DH, DN_DH), lambda t: (0, nc - 1 - t, 0, 0)), part(0)],
        out_specs=[part(0), part(0), part(0), lanes, rows],
        out_shape=[d3, d3, d3, jax.ShapeDtypeStruct((S, LANES), F32), jax.ShapeDtypeStruct((DN_HEADS, nc, BLK), F32)],
        scratch_shapes=[pltpu.VMEM((DN_HEADS, DN_DH, DN_DH), F32)],
        compiler_params=_cp("arbitrary"),
    )(qkv, qkv, qkv, bg, a_row, states, do)


def _gate_sb(o, z):
    return o * _silu(z)


def _gate_dn(o, z, g):
    return jnp.concatenate(
        [_rms(o[:, h * DN_DH:(h + 1) * DN_DH]) * g * _silu(z[:, h * DN_DH:(h + 1) * DN_DH]) for h in range(DN_HEADS)],
        axis=1)


def _merge_specs(S, D, tm):
    row = lambda w, blk: pl.BlockSpec((tm, w), lambda i: (i, blk))
    full = lambda a, b: pl.BlockSpec((a, b), lambda i: (0, 0))
    return [row(D, 0), full(1, D), row(SB_W, 0), row(SB_W, 3), row(DN_W, 0), row(DN_W, 7),
            row(D, IN_MAIN // D), row(D, IN_MAIN // D + 1), full(1, DN_DH), full(SB_W, D), full(DN_W, D), full(D, D)]


def _merge_fwd(x, gate, o_sb, o_dn, p, ng, wbs, wbd, wo, name):
    S, D = x.shape
    tm = min(512, S)

    def body(x_ref, gate_ref, osb_ref, zsb_ref, odn_ref, zdn_ref, msb_ref, mdn_ref, ng_ref, wbs_ref, wbd_ref, wo_ref,
             out_ref):
        a = _gate_sb(osb_ref[...], zsb_ref[...])
        b = _gate_dn(odn_ref[...], zdn_ref[...], ng_ref[...])
        y = _sigmoid(msb_ref[...]) * _bdot(a, wbs_ref[...]) + _sigmoid(mdn_ref[...]) * _bdot(b, wbd_ref[...])
        out_ref[...] = x_ref[...] + gate_ref[...] * _bdot(y, wo_ref[...])

    return pl.pallas_call(
        body, name=name, grid=(S // tm,), in_specs=_merge_specs(S, D, tm),
        out_specs=pl.BlockSpec((tm, D), lambda i: (i, 0)), out_shape=jax.ShapeDtypeStruct((S, D), F32),
        compiler_params=_cp("parallel"),
    )(x, gate, o_sb, p, o_dn, p, p, p, ng, wbs, wbd, wo)


def _merge_bwd(dxn, gate, o_sb, o_dn, p, ng, wbs, wbd, wo, name):
    S, D = dxn.shape
    tm = min(256, S)

    def body(dxn_ref, gate_ref, osb_ref, zsb_ref, odn_ref, zdn_ref, msb_ref, mdn_ref, ng_ref, wbs_ref, wbd_ref, wo_ref,
             dosb_ref, dzsb_ref, dodn_ref, dzdn_ref, dmsb_ref, dmdn_ref, dwo_ref, dwbs_ref, dwbd_ref, dgate_ref, dng_ref):
        @pl.when(pl.program_id(0) == 0)
        def _():
            for ref in (dwo_ref, dwbs_ref, dwbd_ref, dgate_ref, dng_ref):
                ref[...] = jnp.zeros_like(ref)

        a, vjp_a = jax.vjp(_gate_sb, osb_ref[...], zsb_ref[...])
        b, vjp_b = jax.vjp(_gate_dn, odn_ref[...], zdn_ref[...], ng_ref[...])
        a16, b16 = a.astype(BF16), b.astype(BF16)
        ps = jnp.dot(a16, wbs_ref[...], preferred_element_type=F32)
        pd = jnp.dot(b16, wbd_ref[...], preferred_element_type=F32)
        ss, sd = _sigmoid(msb_ref[...]), _sigmoid(mdn_ref[...])
        y16 = (ss * ps + sd * pd).astype(BF16)
        out = jnp.dot(y16, wo_ref[...], preferred_element_type=F32)
        dxn_v = dxn_ref[...]
        dgate_ref[...] += jnp.sum(dxn_v * out, axis=0, keepdims=True)
        dout16 = (dxn_v * gate_ref[...]).astype(BF16)
        dwo_ref[...] += _bdot_tn(y16, dout16)
        dy = _bdot_nt(dout16, wo_ref[...])
        dmsb_ref[...] = (dy * ps * ss * (1.0 - ss)).astype(BF16)
        dmdn_ref[...] = (dy * pd * sd * (1.0 - sd)).astype(BF16)
        dps16, dpd16 = (dy * ss).astype(BF16), (dy * sd).astype(BF16)
        dwbs_ref[...] += _bdot_tn(a16, dps16)
        dwbd_ref[...] += _bdot_tn(b16, dpd16)
        dosb, dzsb = vjp_a(_bdot_nt(dps16, wbs_ref[...]))
        dodn, dzdn, dng = vjp_b(_bdot_nt(dpd16, wbd_ref[...]))
        dosb_ref[...] = dosb
        dzsb_ref[...] = dzsb.astype(BF16)
        dodn_ref[...] = dodn
        dzdn_ref[...] = dzdn.astype(BF16)
        dng_ref[...] += dng

    row = lambda w: pl.BlockSpec((tm, w), lambda i: (i, 0))
    full = lambda a, b: pl.BlockSpec((a, b), lambda i: (0, 0))
    sds = jax.ShapeDtypeStruct
    return pl.pallas_call(
        body, name=name, grid=(S // tm,), in_specs=_merge_specs(S, D, tm),
        out_specs=[row(SB_W), row(SB_W), row(DN_W), row(DN_W), row(D), row(D),
                   full(D, D), full(SB_W, D), full(DN_W, D), full(1, D), full(1, DN_DH)],
        out_shape=[sds((S, SB_W), F32), sds((S, SB_W), BF16), sds((S, DN_W), F32), sds((S, DN_W), BF16),
                   sds((S, D), BF16), sds((S, D), BF16), sds((D, D), F32), sds((SB_W, D), F32), sds((DN_W, D), F32),
                   sds((1, D), F32), sds((1, DN_DH), F32)],
        compiler_params=_cp("arbitrary"),
    )(dxn, gate, o_sb, p, o_dn, p, p, p, ng, wbs, wbd, wo)


def _loss_fwd_bwd(y, target, name):
    S, D = y.shape
    tm = min(512, S)

    def body(y_ref, t_ref, l_ref, dy_ref):
        @pl.when(pl.program_id(0) == 0)
        def _():
            l_ref[...] = jnp.zeros_like(l_ref)

        e = y_ref[...] - t_ref[...]
        l_ref[...] += jnp.sum(e * e, axis=0, keepdims=True) * (0.5 / D)
        dy_ref[...] = e * (1.0 / D)

    row = pl.BlockSpec((tm, D), lambda i: (i, 0))
    return pl.pallas_call(
        body, name=name, grid=(S // tm,), in_specs=[row, row],
        out_specs=[pl.BlockSpec((1, D), lambda i: (0, 0)), row],
        out_shape=[jax.ShapeDtypeStruct((1, D), F32), jax.ShapeDtypeStruct((S, D), F32)],
        compiler_params=_cp("arbitrary"),
    )(y, target)


def _mod_fwd(c_all, ada_w, name):
    L, D, n = ada_w.shape

    def body(c_ref, w_ref, o_ref):
        o_ref[0] = _dot(_silu(c_ref[...]), w_ref[0])

    return pl.pallas_call(
        body, name=name, grid=(L,),
        in_specs=[pl.BlockSpec(c_all.shape, lambda l: (0, 0)), pl.BlockSpec((1, D, n), lambda l: (l, 0, 0))],
        out_specs=pl.BlockSpec((1, N_DEV, n), lambda l: (l, 0, 0)),
        out_shape=jax.ShapeDtypeStruct((L, N_DEV, n), F32),
        compiler_params=_cp("parallel"),
    )(c_all, ada_w)


def _mod_bwd_w(c_all_t, dmod, name):
    L, _, n = dmod.shape
    D = c_all_t.shape[0]

    def body(c_ref, d_ref, o_ref):
        o_ref[0] = _dot(_silu(c_ref[...]), d_ref[0])

    return pl.pallas_call(
        body, name=name, grid=(L,),
        in_specs=[pl.BlockSpec(c_all_t.shape, lambda l: (0, 0)), pl.BlockSpec((1, N_DEV, n), lambda l: (l, 0, 0))],
        out_specs=pl.BlockSpec((1, D, n), lambda l: (l, 0, 0)),
        out_shape=jax.ShapeDtypeStruct((L, D, n), F32),
        compiler_params=_cp("parallel"),
    )(c_all_t, dmod)


def _me():
    return lax.axis_index("x"), lax.axis_index("y"), lax.axis_index("c")


def _peer(k):
    x, y, c = _me()
    return (1 - x if k & 4 else x, 1 - y if k & 2 else y, 1 - c if k & 1 else c)


def _lin(dev):
    return 4 * dev[0] + 2 * dev[1] + dev[2]


class _Exchange:
    def __init__(self, arrays, scatter):
        self.n = len(arrays)
        self.scatter = scatter
        self.out_shape = [jax.ShapeDtypeStruct((N_DEV,) + tuple(a.shape[1:] if scatter else a.shape), a.dtype)
                          for a in arrays]
        self.specs = [pl.BlockSpec(memory_space=pl.ANY)] * self.n
        self.scratch = [pltpu.SemaphoreType.DMA((self.n, N_DEV - 1)), pltpu.SemaphoreType.DMA((self.n, N_DEV - 1)),
                        pltpu.SemaphoreType.DMA((self.n,))]

    def _copies(self, ins, outs, sems):
        send_sems, recv_sems, local_sems = sems
        me = _lin(_me())
        local, remote, landed = [], [], []
        for t in range(self.n):
            src_of = (lambda d, t=t: ins[t].at[d]) if self.scatter else (lambda d, t=t: ins[t])
            local.append(pltpu.make_async_copy(src_of(me), outs[t].at[me], local_sems.at[t]))
            for k in range(1, N_DEV):
                peer = _peer(k)
                pair = dict(send_sem=send_sems.at[t, k - 1], recv_sem=recv_sems.at[t, k - 1], device_id=peer,
                            device_id_type=pl.DeviceIdType.MESH)
                remote.append(pltpu.make_async_remote_copy(src_ref=src_of(_lin(peer)), dst_ref=outs[t].at[me], **pair))
                slot = outs[t].at[_lin(peer)]
                landed.append(pltpu.make_async_remote_copy(src_ref=slot, dst_ref=slot, **pair))
        return local, remote, landed

    def start(self, ins, outs, sems):
        local, remote, _ = self._copies(ins, outs, sems)
        for cp in local + remote:
            cp.start()

    def finish(self, ins, outs, sems):
        local, remote, landed = self._copies(ins, outs, sems)
        for cp in landed:
            cp.wait_recv()
        for cp in remote:
            cp.wait_send()
        for cp in local:
            cp.wait()


def _exchange(arrays, scatter, name):
    ex = _Exchange(arrays, scatter)

    def body(*refs):
        ins, outs, sems = refs[:ex.n], refs[ex.n:2 * ex.n], refs[2 * ex.n:]
        ex.start(ins, outs, sems)
        ex.finish(ins, outs, sems)

    return pl.pallas_call(body, name=name, in_specs=ex.specs, out_specs=ex.specs, out_shape=ex.out_shape,
                          scratch_shapes=ex.scratch)(*arrays)


def _sum_slots(a, name):
    _, R, C = a.shape
    tr = SUM_ROWS if R % SUM_ROWS == 0 else R

    def body(a_ref, o_ref):
        acc = a_ref[0].astype(F32)
        for s in range(1, N_DEV):
            acc = acc + a_ref[s].astype(F32)
        o_ref[...] = acc

    return pl.pallas_call(
        body, name=name, grid=(R // tr,),
        in_specs=[pl.BlockSpec((N_DEV, tr, C), lambda i: (0, i, 0))], out_specs=pl.BlockSpec((tr, C), lambda i: (i, 0)),
        out_shape=jax.ShapeDtypeStruct((R, C), F32), compiler_params=_cp("parallel"),
    )(a)


def _adamw(w, g, m, v, name):
    shape = w.shape
    C = shape[-1]
    R = w.size // C
    tr = R
    for cand in (256, 128, 64):
        if R > cand and R % cand == 0:
            tr = cand
            break
    c1 = 1.0 / (1.0 - ADAM_B1 ** ADAM_STEP)
    c2 = 1.0 / (1.0 - ADAM_B2 ** ADAM_STEP)

    def body(w_ref, g_ref, m_ref, v_ref, d_ref, nm_ref, nv_ref):
        gv = g_ref[...]
        nm = ADAM_B1 * m_ref[...] + (1.0 - ADAM_B1) * gv
        nv = ADAM_B2 * v_ref[...] + (1.0 - ADAM_B2) * (gv * gv)
        d_ref[...] = -ADAM_LR * ((nm * c1) / (jnp.sqrt(nv * c2) + ADAM_EPS) + ADAM_WD * w_ref[...])
        nm_ref[...] = nm
        nv_ref[...] = nv

    spec = pl.BlockSpec((tr, C), lambda i: (i, 0))
    sd = jax.ShapeDtypeStruct((R, C), F32)
    outs = pl.pallas_call(
        body, name=name, grid=(R // tr,), in_specs=[spec] * 4, out_specs=[spec] * 3, out_shape=[sd] * 3,
        compiler_params=_cp("parallel"),
    )(*(t.reshape(R, C) for t in (w, g, m, v)))
    return tuple(t.reshape(shape) for t in outs)


def _pad_cols(w_in):
    D = w_in.shape[0]
    pad = jnp.zeros((D, LANES - 2 * DN_HEADS), w_in.dtype)
    return jnp.concatenate([w_in[:, :IN_MAIN], w_in[:, IN_COLS:], w_in[:, IN_MAIN:IN_COLS], pad], axis=1)


def _unpad_cols(dw, D):
    return jnp.concatenate([dw[:, :IN_MAIN], dw[:, IN_MAIN + 2 * D:IN_MAIN + 2 * D + 2 * DN_HEADS],
                            dw[:, IN_MAIN:IN_MAIN + 2 * D]], axis=1)


def _gate_params(a_log, dt_bias):
    z = jnp.zeros((LANES,), F32)
    return jnp.stack([z.at[DN_HEADS:2 * DN_HEADS].set(a_log), z.at[DN_HEADS:2 * DN_HEADS].set(dt_bias)])


def _layer_fwd(l, x, mod, wts, carry=None):
    S, D = x.shape
    tag = f"l{l}_"
    p, h = _inproj_fwd(x, mod, wts["norm_g"], wts["w_in"], tag + "inproj_fwd")
    (o_sb,), carried = _sb_fwd(p, wts["sb_q_g"], wts["sb_k_g"].T, tag + "sb_fwd", carry)
    qkv = _dn_prep_fwd(p, wts["conv_w"], 4 * SB_W, tag + "dn_prep_fwd")
    pv = _gate_params(wts["dn_a_log"], wts["dn_dt_bias"])
    ba_blk = (IN_MAIN + 2 * D) // LANES
    bg = _dn_gate_fwd(p, pv, ba_blk, tag + "dn_gate_fwd")
    a_row = bg[:, DN_HEADS:2 * DN_HEADS].T.reshape(DN_HEADS, S // BLK, BLK)
    o_dn, states = _delta_fwd(qkv, bg, a_row, tag + "delta_fwd")
    gate = mod[:, 2 * D:]
    out = _merge_fwd(x, gate, o_sb, o_dn, p, wts["dn_norm_g"], wts["w_branch_sb"], wts["w_branch_dn"], wts["w_out"],
                     tag + "merge_fwd")
    saved = dict(x=x, mod=mod, p=p, h=h, o_sb=o_sb, qkv=qkv, pv=pv, bg=bg, a_row=a_row, o_dn=o_dn, states=states,
                 gate=gate)
    return out, saved, carried


def _layer_bwd(l, dxn, sv, wts, carry_of=None, late_carry_of=None):
    S, D = dxn.shape
    tag = f"l{l}_"
    (dosb, dzsb, dodn, dzdn, dmsb, dmdn, dwo, dwbs, dwbd, dgate, dng) = _merge_bwd(
        dxn, sv["gate"], sv["o_sb"], sv["o_dn"], sv["p"], wts["dn_norm_g"], wts["w_branch_sb"], wts["w_branch_dn"],
        wts["w_out"], tag + "merge_bwd")
    carry = None if carry_of is None else carry_of(dict(w_out=dwo, w_branch_sb=dwbs, w_branch_dn=dwbd))
    (dq, dk, dv, dgq, dgkt), carried = _sb_bwd(sv["p"], sv["o_sb"], dosb, wts["sb_q_g"], wts["sb_q_g"].T,
                                                wts["sb_k_g"], wts["sb_k_g"].T, tag + "sb_bwd", carry)
    dqkv_n_q, dqkv_n_k, dqkv_n_v, dbg, dar = _delta_bwd(sv["qkv"], sv["bg"], sv["a_row"], sv["states"], dodn,
                                                         tag + "delta_bwd")
    dqkv, dconv = _dn_prep_bwd(sv["p"], wts["conv_w"], 4 * SB_W,
                               jnp.concatenate([dqkv_n_q, dqkv_n_k, dqkv_n_v], axis=1), tag + "dn_prep_bwd")
    dbg = dbg.at[:, DN_HEADS:2 * DN_HEADS].add(dar.reshape(DN_HEADS, S).T)
    ba_blk = (IN_MAIN + 2 * D) // LANES
    dba, dpv = _dn_gate_bwd(sv["p"], sv["pv"], ba_blk, dbg, tag + "dn_gate_bwd")
    dp = jnp.concatenate([dq, dk, dv, dzsb, dqkv, dzdn, dmsb, dmdn, dba], axis=1)
    dw_in = _unpad_cols(_matmul_tn(sv["h"].T, dp, tag + "inproj_bwd_dw"), D)
    late = None if late_carry_of is None else late_carry_of(dict(w_in=dw_in, conv_w=dconv))
    (dx, dmod, dg), carried_late = _inproj_bwd_dx(dp, wts["w_in"], sv["x"], sv["mod"], wts["norm_g"], dxn,
                                                  tag + "inproj_bwd_dx", late)
    dmod = dmod.at[:, 2 * D:].set(dgate)
    grads = dict(w_in=dw_in, w_branch_sb=dwbs, w_branch_dn=dwbd, w_out=dwo, conv_w=dconv,
                 mod=dmod[0], norm_g=dg[0], sb_q_g=jnp.sum(dgq, axis=0)[0], sb_k_g=jnp.sum(dgkt, axis=0)[:, 0],
                 dn_a_log=dpv[0, DN_HEADS:2 * DN_HEADS], dn_dt_bias=dpv[1, DN_HEADS:2 * DN_HEADS], dn_norm_g=dng[0])
    return dx, grads, carried, carried_late


def _pad_rows(a, mult):
    extra = (-a.shape[0]) % mult
    return a if extra == 0 else jnp.concatenate([a, jnp.zeros((extra,) + a.shape[1:], a.dtype)], axis=0)


def _pack_rows(parts, width, mult):
    flat = jnp.concatenate([t.reshape(-1) for t in parts])
    extra = (-flat.shape[0]) % width
    if extra:
        flat = jnp.concatenate([flat, jnp.zeros((extra,), flat.dtype)])
    return _pad_rows(flat.reshape(-1, width), mult)


def _take(flat, off, shape):
    n = math.prod(shape)
    return flat[..., off:off + n].reshape(flat.shape[:-1] + tuple(shape)), off + n


SMALL = ("mod", "norm_g", "sb_q_g", "sb_k_g", "dn_a_log", "dn_dt_bias", "dn_norm_g")


def kernel(x, c, ada_w, ada_b, norm_g, w_in, sb_q_g, sb_k_g, conv_w, dn_a_log, dn_dt_bias, dn_norm_g, w_branch_sb, w_branch_dn, w_out, loss_target, m_ada_w, m_ada_b, m_norm_g, m_w_in, m_sb_q_g, m_sb_k_g, m_conv_w, m_dn_a_log, m_dn_dt_bias, m_dn_norm_g, m_w_branch_sb, m_w_branch_dn, m_w_out, v_ada_w, v_ada_b, v_norm_g, v_w_in, v_sb_q_g, v_sb_k_g, v_conv_w, v_dn_a_log, v_dn_dt_bias, v_dn_norm_g, v_w_branch_sb, v_w_branch_dn, v_w_out):
    L, D = norm_g.shape
    S = x.shape[1]
    n_in = w_in.shape[2]
    n_ada = ada_w.shape[2]
    n_br = w_branch_sb.shape[2]
    n_out = w_out.shape[1]
    n_conv = conv_w.shape[2]
    me = _lin(_me())

    def cat(a):
        return jnp.concatenate([a[d] for d in range(N_DEV)], axis=1)

    c_all, conv_all = _exchange([c, conv_w.reshape(L * CONV_K, n_conv)], False, "gather_small")
    c_all = c_all.reshape(N_DEV, D)
    conv_full = cat(conv_all).reshape(L, CONV_K, N_DEV * n_conv)

    mod_part = _mod_fwd(c_all, ada_w, "mod_fwd")

    def shards16(l):
        return [w_in[l].astype(BF16), w_branch_sb[l].astype(BF16), w_branch_dn[l].astype(BF16), w_out[l].astype(BF16)]

    def whole(l, got):
        wi, wbs, wbd, wo = got
        return dict(norm_g=norm_g[l:l + 1], w_in=_pad_cols(cat(wi)), sb_q_g=sb_q_g[l:l + 1], sb_k_g=sb_k_g[l:l + 1],
                    conv_w=conv_full[l], dn_a_log=dn_a_log[l], dn_dt_bias=dn_dt_bias[l], dn_norm_g=dn_norm_g[l:l + 1],
                    w_branch_sb=cat(wbs), w_branch_dn=cat(wbd), w_out=wo.reshape(N_DEV * n_out, D))

    *got, mod_all = _exchange(shards16(0) + [mod_part.reshape(L * N_DEV, n_ada)], False, "gather_weights")
    mod_full = cat(mod_all).reshape(L, N_DEV, N_DEV * n_ada) + ada_b[:, None, :]
    mod_mine = lax.dynamic_slice_in_dim(mod_full, me, 1, axis=1)

    act = x[0]
    saved, wts = [], []
    for l in range(L):
        wts.append(whole(l, got))
        act, sv, got = _layer_fwd(l, act, mod_mine[l], wts[l], (shards16(l + 1), False) if l + 1 < L else None)
        saved.append(sv)
    loss_cols, dact = _loss_fwd_bwd(act, loss_target[0], "loss")
    loss = lax.psum(jnp.sum(loss_cols), ("x", "y", "c"))

    def blocks(name, g):
        if name == "w_out":
            return g.astype(BF16).reshape(N_DEV, n_out, D)
        n = g.shape[1] // N_DEV
        dtype = F32 if name == "conv_w" else BF16
        return jnp.stack([g[:, d * n:(d + 1) * n].astype(dtype) for d in range(N_DEV)])

    early, late = ("w_out", "w_branch_sb", "w_branch_dn"), ("w_in", "conv_w")
    grads, recv, pending = [None] * L, {}, []
    for l in reversed(range(L)):
        keys = [k for k, _ in pending] + [(n, l) for n in early]

        def carry_of(g_early, pending=pending):
            return [a for _, a in pending] + [blocks(n, g_early[n]) for n in early], True

        last = l == 0
        dact, grads[l], got, got_late = _layer_bwd(
            l, dact, saved[l], wts[l], carry_of, (lambda g: ([blocks(n, g[n]) for n in late], True)) if last else None)
        recv.update(zip(keys, got))
        recv.update(zip([(n, l) for n in late], got_late))
        pending = [] if last else [((n, l), blocks(n, grads[l][n])) for n in late]
    grad_x = dact[None]
    small_g = _pack_rows([grads[l][n] for l in range(L) for n in SMALL], LANES, 8)
    (small_all_g,) = _exchange([small_g], False, "gather_small_grads")
    small_sum = _sum_slots(small_all_g, "sum_small_grads").reshape(-1)
    shard_shapes = dict(w_in=w_in.shape, w_branch_sb=w_branch_sb.shape, w_branch_dn=w_branch_dn.shape,
                        conv_w=conv_w.shape, w_out=w_out.shape)
    g_out = {n: jnp.stack([_sum_slots(recv[(n, l)], f"sum_{n}_l{l}").reshape(shape[1:]) for l in range(L)])
             for n, shape in shard_shapes.items()}
    small_shapes = dict(mod=(3 * D,), norm_g=(D,), sb_q_g=(SB_DH,), sb_k_g=(SB_DH,), dn_a_log=(DN_HEADS,),
                        dn_dt_bias=(DN_HEADS,), dn_norm_g=(DN_DH,))
    off = 0
    off_all = 0
    small_each = small_all_g.reshape(N_DEV, -1)
    per_small = {n: [] for n in SMALL}
    dmod_all = []
    for l in range(L):
        for n in SMALL:
            t, off = _take(small_sum, off, small_shapes[n])
            per_small[n].append(t)
            if n == "mod":
                t_all, _ = _take(small_each, off_all, small_shapes[n])
                dmod_all.append(t_all)
            off_all += math.prod(small_shapes[n])
    for n in SMALL:
        g_out[n if n != "mod" else "ada_b"] = jnp.stack(per_small[n])
    dmod_all = jnp.stack(dmod_all)
    dmod_cols = lax.dynamic_slice_in_dim(dmod_all, me * n_ada, n_ada, axis=2)
    g_out["ada_w"] = _mod_bwd_w(c_all.T, dmod_cols, "mod_bwd_w")

    given = dict(ada_w=(ada_w, m_ada_w, v_ada_w), ada_b=(ada_b, m_ada_b, v_ada_b), norm_g=(norm_g, m_norm_g, v_norm_g),
                 w_in=(w_in, m_w_in, v_w_in), sb_q_g=(sb_q_g, m_sb_q_g, v_sb_q_g), sb_k_g=(sb_k_g, m_sb_k_g, v_sb_k_g),
                 conv_w=(conv_w, m_conv_w, v_conv_w), dn_a_log=(dn_a_log, m_dn_a_log, v_dn_a_log),
                 dn_dt_bias=(dn_dt_bias, m_dn_dt_bias, v_dn_dt_bias), dn_norm_g=(dn_norm_g, m_dn_norm_g, v_dn_norm_g),
                 w_branch_sb=(w_branch_sb, m_w_branch_sb, v_w_branch_sb),
                 w_branch_dn=(w_branch_dn, m_w_branch_dn, v_w_branch_dn), w_out=(w_out, m_w_out, v_w_out))
    order = list(given)
    upd = {n: _adamw(given[n][0], g_out[n], given[n][1], given[n][2], "adamw_" + n) for n in order}
    return (loss, grad_x, *[g_out[n] for n in order], *[upd[n][0] for n in order], *[upd[n][1] for n in order],
            *[upd[n][2] for n in order])
```

```python
import functools
import math

import jax
import jax.numpy as jnp
from jax import lax
from jax.experimental import pallas as pl
from jax.experimental.pallas import tpu as pltpu

F32 = jnp.float32
BF16 = jnp.bfloat16
HI = lax.Precision.HIGHEST

N_DEV = 8
EPS = 1e-6
SB_HEADS, SB_DH = 8, 64
DN_HEADS, DN_DH = 4, 128
SB_W = SB_HEADS * SB_DH
DN_W = DN_HEADS * DN_DH
CONV_K = 4
BLK = 128
SB_KEYS = 512
SB_QB = 512
LANES = 128
IN_MAIN = 4 * SB_W + 4 * DN_W
IN_COLS = IN_MAIN + 2 * DN_HEADS
ADAM_LR, ADAM_B1, ADAM_B2, ADAM_EPS, ADAM_WD, ADAM_STEP = 0.001, 0.9, 0.999, 1e-08, 0.01, 10
VMEM_LIMIT = 56 * 1024 * 1024
SUM_ROWS = 128


def _cp(*sem, vmem=VMEM_LIMIT):
    return pltpu.CompilerParams(dimension_semantics=sem if sem else None, vmem_limit_bytes=vmem)


def _dot(a, b, prec=HI):
    return lax.dot_general(a, b, (((1,), (0,)), ((), ())), precision=prec, preferred_element_type=F32)


def _dot_nt(a, b, prec=HI):
    return lax.dot_general(a, b, (((1,), (1,)), ((), ())), precision=prec, preferred_element_type=F32)


def _bdot(a, b):
    return lax.dot_general(a.astype(BF16), b.astype(BF16), (((1,), (0,)), ((), ())), preferred_element_type=F32)


def _bdot_nt(a, b):
    return lax.dot_general(a.astype(BF16), b.astype(BF16), (((1,), (1,)), ((), ())), preferred_element_type=F32)


def _bdot_tn(a, b):
    return lax.dot_general(a.astype(BF16), b.astype(BF16), (((0,), (0,)), ((), ())), preferred_element_type=F32)


def _split_dot(a, b01):
    hi = a.astype(BF16)
    lo = (a - hi.astype(F32)).astype(BF16)
    return (lax.dot_general(hi, b01, (((1,), (0,)), ((), ())), preferred_element_type=F32)
            + lax.dot_general(lo, b01, (((1,), (0,)), ((), ())), preferred_element_type=F32))


def _sigmoid(x):
    return 1.0 / (1.0 + jnp.exp(-x))


def _silu(x):
    return x * _sigmoid(x)


def _softplus(x):
    return jnp.maximum(x, 0.0) + jnp.log(1.0 + jnp.exp(-jnp.abs(x)))


def _rms(x):
    return x * lax.rsqrt(jnp.mean(x * x, axis=-1, keepdims=True) + EPS)


def _prenorm(x, g, shift, scale):
    return _rms(x) * g * (1.0 + scale) + shift


def _inproj_fwd(x, mod, g, w, name):
    S, D = x.shape
    N = w.shape[1]
    tm = min(512, S)
    tn = 896 if N % 896 == 0 else 128

    def body(x_ref, mod_ref, g_ref, w_ref, p_ref, h_ref):
        @pl.when(pl.program_id(1) == 0)
        def _():
            h = _prenorm(x_ref[...], g_ref[...], mod_ref[:, 0:D], mod_ref[:, D:2 * D])
            h_ref[...] = h.astype(BF16)

        p_ref[...] = jnp.dot(h_ref[...], w_ref[...], preferred_element_type=F32)

    return pl.pallas_call(
        body, name=name, grid=(S // tm, N // tn),
        in_specs=[pl.BlockSpec((tm, D), lambda i, j: (i, 0)), pl.BlockSpec((1, 3 * D), lambda i, j: (0, 0)),
                  pl.BlockSpec((1, D), lambda i, j: (0, 0)), pl.BlockSpec((D, tn), lambda i, j: (0, j))],
        out_specs=[pl.BlockSpec((tm, tn), lambda i, j: (i, j)), pl.BlockSpec((tm, D), lambda i, j: (i, 0))],
        out_shape=[jax.ShapeDtypeStruct((S, N), F32), jax.ShapeDtypeStruct((S, D), BF16)],
        compiler_params=_cp("parallel", "arbitrary"),
    )(x, mod, g, w)


def _inproj_bwd_dx(dp, w, x, mod, g, dxn, name, carry=None):
    S, N = dp.shape
    D = x.shape[1]
    tm = min(512, S)
    tk = 896 if N % 896 == 0 else 128
    nk = N // tk

    def body(dp_ref, w_ref, x_ref, mod_ref, g_ref, dxn_ref, dx_ref, dmod_ref, dg_ref, acc):
        i, k = pl.program_id(0), pl.program_id(1)

        @pl.when(k == 0)
        def _():
            acc[...] = jnp.zeros_like(acc)

        @pl.when((i == 0) & (k == 0))
        def _():
            dmod_ref[...] = jnp.zeros_like(dmod_ref)
            dg_ref[...] = jnp.zeros_like(dg_ref)

        acc[...] += lax.dot_general(dp_ref[...], w_ref[...], (((1,), (1,)), ((), ())), preferred_element_type=F32)

        @pl.when(k == nk - 1)
        def _():
            _, vjp = jax.vjp(_prenorm, x_ref[...], g_ref[...], mod_ref[:, 0:D], mod_ref[:, D:2 * D])
            dx, dg, dshift, dscale = vjp(acc[...])
            dx_ref[...] = dxn_ref[...] + dx
            dg_ref[...] += dg
            dmod_ref[:, 0:D] += dshift
            dmod_ref[:, D:2 * D] += dscale

    return _call_carrying(
        body, name, (S // tm, nk), carry, [dp, w, x, mod, g, dxn],
        in_specs=[pl.BlockSpec((tm, tk), lambda i, k: (i, k)), pl.BlockSpec((D, tk), lambda i, k: (0, k)),
                  pl.BlockSpec((tm, D), lambda i, k: (i, 0)), pl.BlockSpec((1, 3 * D), lambda i, k: (0, 0)),
                  pl.BlockSpec((1, D), lambda i, k: (0, 0)), pl.BlockSpec((tm, D), lambda i, k: (i, 0))],
        out_specs=[pl.BlockSpec((tm, D), lambda i, k: (i, 0)), pl.BlockSpec((1, 3 * D), lambda i, k: (0, 0)),
                   pl.BlockSpec((1, D), lambda i, k: (0, 0))],
        out_shape=[jax.ShapeDtypeStruct((S, D), F32), jax.ShapeDtypeStruct((1, 3 * D), F32),
                   jax.ShapeDtypeStruct((1, D), F32)],
        scratch_shapes=[pltpu.VMEM((tm, D), F32)], vmem=VMEM_LIMIT)


def _matmul_tn(a_t, b, name):
    M, K = a_t.shape
    N = b.shape[1]
    tn = 896 if N % 896 == 0 else (512 if N % 512 == 0 else 128)
    tk = min(512, K)
    nk = K // tk

    def body(a_ref, b_ref, o_ref):
        @pl.when(pl.program_id(1) == 0)
        def _():
            o_ref[...] = jnp.zeros_like(o_ref)

        o_ref[...] += jnp.dot(a_ref[...], b_ref[...], preferred_element_type=F32)

    return pl.pallas_call(
        body, name=name, grid=(N // tn, nk),
        in_specs=[pl.BlockSpec((M, tk), lambda j, k: (0, k)), pl.BlockSpec((tk, tn), lambda j, k: (k, j))],
        out_specs=pl.BlockSpec((M, tn), lambda j, k: (0, j)),
        out_shape=jax.ShapeDtypeStruct((M, N), F32),
        compiler_params=_cp("parallel", "arbitrary"),
    )(a_t, b)


def _qk_norm(t, g, scale):
    return _rms(t) * g * scale


def _qk_norm_t(t, g_col, scale):
    return t * lax.rsqrt(jnp.mean(t * t, axis=0, keepdims=True) + EPS) * g_col * scale


def _suffix_sums(x, tri):
    half = tri.shape[0]
    lo, hi = x[:, :half], x[:, half:]
    hi_sum = jnp.sum(hi, axis=1, keepdims=True)
    y = jnp.concatenate([_split_dot(lo, tri) + hi_sum, _split_dot(hi, tri)], axis=1)
    return y, hi_sum + jnp.sum(lo, axis=1, keepdims=True)


def _sb_step(qi, kat_blk, cl, upper, valid):
    z = jnp.dot(qi, kat_blk, preferred_element_type=F32)
    lk = jnp.minimum(-z, 0.0) - jnp.log(1.0 + jnp.exp(-jnp.abs(z)))
    if valid is not None:
        lk = jnp.where(valid, lk, 0.0)
    later, tot = _suffix_sums(lk, upper)
    w = jnp.exp(z + lk + later + cl)
    if valid is not None:
        w = jnp.where(valid, w, 0.0)
    return z, lk, w, tot


def _sb_masks(kb):
    half = kb // 2
    r = lax.broadcasted_iota(jnp.int32, (half, half), 0)
    c = lax.broadcasted_iota(jnp.int32, (half, half), 1)
    rq = lax.broadcasted_iota(jnp.int32, (SB_QB, kb), 0)
    ck = lax.broadcasted_iota(jnp.int32, (SB_QB, kb), 1)
    return (r > c).astype(BF16), (r >= c).astype(BF16), ck - rq


def _sb_fwd(p, gq, gkt, name, carry=None):
    S, dh = p.shape[0], SB_DH
    kb = min(SB_KEYS, S)
    per = kb // SB_QB
    nb = S // SB_QB
    scale = 1.0 / math.sqrt(dh)
    pairs = SB_W // LANES

    def body(q_ref, k_ref, v_ref, gq_ref, gkt_ref, o_ref, kt2, qa, kat, vb):
        kt2[...] = k_ref[...].T
        upper, _, diff = _sb_masks(kb)
        for hh in range(LANES // dh):
            lanes = slice(hh * dh, (hh + 1) * dh)
            qa[...] = _qk_norm(q_ref[:, lanes], gq_ref[...], scale).astype(BF16)
            kat[...] = _qk_norm_t(kt2[lanes, :], gkt_ref[...], 1.0).astype(BF16)
            vb[...] = v_ref[:, lanes].astype(BF16)

            def qblock(i, _):
                rows = pl.ds(pl.multiple_of(i * SB_QB, SB_QB), SB_QB)
                qi = qa[rows, :]
                sbd = i // per

                def step(sb, cl, acc, valid):
                    cols = pl.ds(pl.multiple_of(sb * kb, kb), kb)
                    _, _, w, tot = _sb_step(qi, kat[:, cols], cl, upper, valid)
                    return cl + tot, acc + jnp.dot(w.astype(BF16), vb[cols, :], preferred_element_type=F32)

                cl, acc = step(sbd, jnp.zeros((SB_QB, 1), F32), jnp.zeros((SB_QB, dh), F32),
                               diff < (i - sbd * per) * SB_QB)
                _, acc = lax.fori_loop(0, sbd, lambda jj, c: step(sbd - 1 - jj, c[0], c[1], None), (cl, acc))
                o_ref[rows, lanes] = acc
                return 0

            lax.fori_loop(0, nb, qblock, 0)

    blk = lambda off: pl.BlockSpec((S, LANES), lambda g: (0, off + g))
    return _call_carrying(
        body, name, pairs, carry, [p, p, p, gq, gkt],
        in_specs=[blk(0), blk(pairs), blk(2 * pairs), pl.BlockSpec((1, dh), lambda g: (0, 0)),
                  pl.BlockSpec((dh, 1), lambda g: (0, 0))],
        out_specs=[blk(0)], out_shape=[jax.ShapeDtypeStruct((S, SB_W), F32)],
        scratch_shapes=[pltpu.VMEM((LANES, S), F32), pltpu.VMEM((S, dh), BF16), pltpu.VMEM((dh, S), BF16),
                        pltpu.VMEM((S, dh), BF16)],
        vmem=VMEM_LIMIT)


def _call_carrying(body, name, grid, carry, operands, in_specs, out_specs, out_shape, scratch_shapes, vmem):
    grid = (grid,) if isinstance(grid, int) else tuple(grid)
    if carry is None:
        res = pl.pallas_call(body, name=name, grid=grid, in_specs=in_specs, out_specs=out_specs,
                             out_shape=out_shape, scratch_shapes=scratch_shapes,
                             compiler_params=_cp(*["arbitrary"] * len(grid), vmem=vmem))(*operands)
        return res, []

    def at(corner):
        hit = pl.program_id(0) == corner(grid[0])
        for axis in range(1, len(grid)):
            hit = jnp.logical_and(hit, pl.program_id(axis) == corner(grid[axis]))
        return hit

    ex = _Exchange(*carry)
    n_in, n_out, n_scr = len(in_specs), len(out_specs), len(scratch_shapes)

    def wrapped(*refs):
        ins, refs = refs[:n_in], refs[n_in:]
        xin, refs = refs[:ex.n], refs[ex.n:]
        outs, refs = refs[:n_out], refs[n_out:]
        xout, refs = refs[:ex.n], refs[ex.n:]
        scr, sems = refs[:n_scr], refs[n_scr:]

        @pl.when(at(lambda n: 0))
        def _():
            ex.start(xin, xout, sems)

        body(*ins, *outs, *scr)

        @pl.when(at(lambda n: n - 1))
        def _():
            ex.finish(xin, xout, sems)

    res = pl.pallas_call(wrapped, name=name, grid=grid, in_specs=in_specs + ex.specs,
                         out_specs=out_specs + ex.specs, out_shape=out_shape + ex.out_shape,
                         scratch_shapes=scratch_shapes + ex.scratch,
                         compiler_params=_cp(*["arbitrary"] * len(grid), vmem=vmem))(*operands, *carry[0])
    return res[:n_out], res[n_out:]


def _sb_bwd(p, o, do, gq, gqt, gk, gkt, name, carry=None):
    S, dh = p.shape[0], SB_DH
    kb = min(SB_KEYS, S)
    per = kb // SB_QB
    nb = S // SB_QB
    scale = 1.0 / math.sqrt(dh)
    pairs = SB_W // LANES
    per_pair = LANES // dh

    def body(q_ref, k_ref, v_ref, o_ref, do_ref, gq_ref, gqt_ref, gk_ref, gkt_ref,
             dq_ref, dk_ref, dv_ref, dgq_ref, dgkt_ref,
             qt2, kt2, vt2, dot2, dkt2, dvt2, qa, qat, ka, kat, vb, vtb, dob, dotb, dqa):
        qt2[...] = q_ref[...].T
        kt2[...] = k_ref[...].T
        vt2[...] = v_ref[...].T
        dot2[...] = do_ref[...].T
        dkt2[...] = jnp.zeros_like(dkt2)
        dvt2[...] = jnp.zeros_like(dvt2)
        upper, lower_incl, diff = _sb_masks(kb)
        for hh in range(per_pair):
            lanes = slice(hh * dh, (hh + 1) * dh)
            qa[...] = _qk_norm(q_ref[:, lanes], gq_ref[...], scale).astype(BF16)
            qat[...] = _qk_norm_t(qt2[lanes, :], gqt_ref[...], scale).astype(BF16)
            ka[...] = _qk_norm(k_ref[:, lanes], gk_ref[...], 1.0).astype(BF16)
            kat[...] = _qk_norm_t(kt2[lanes, :], gkt_ref[...], 1.0).astype(BF16)
            vb[...] = v_ref[:, lanes].astype(BF16)
            vtb[...] = vt2[lanes, :].astype(BF16)
            dob[...] = do_ref[:, lanes].astype(BF16)
            dotb[...] = dot2[lanes, :].astype(BF16)

            def qblock(i, _):
                rows = pl.ds(pl.multiple_of(i * SB_QB, SB_QB), SB_QB)
                qi, qit = qa[rows, :], qat[:, rows]
                doi, doit = dob[rows, :], dotb[:, rows]
                total = jnp.sum(doi.astype(F32) * o_ref[rows, lanes], axis=1, keepdims=True)
                sbd = i // per

                def step(sb, cl, cd, dqi, valid):
                    cols = pl.ds(pl.multiple_of(sb * kb, kb), kb)
                    z, lk, w, tot = _sb_step(qi, kat[:, cols], cl, upper, valid)
                    w16 = w.astype(BF16)
                    dl = jnp.dot(doi, vtb[:, cols], preferred_element_type=F32) * w16.astype(F32)
                    incl, dtot = _suffix_sums(dl, lower_incl)
                    sig = jnp.exp(z + lk)
                    dz = dl - sig * (dl + (total - cd - incl))
                    if valid is not None:
                        dz = jnp.where(valid, dz, 0.0)
                    dz16 = dz.astype(BF16)
                    dqi = dqi + jnp.dot(dz16, ka[cols, :], preferred_element_type=F32)
                    dkt2[lanes, cols] += jnp.dot(qit, dz16, preferred_element_type=F32)
                    dvt2[lanes, cols] += jnp.dot(doit, w16, preferred_element_type=F32)
                    return cl + tot, cd + dtot, dqi

                zero = jnp.zeros((SB_QB, 1), F32)
                first = step(sbd, zero, zero, jnp.zeros((SB_QB, dh), F32), diff < (i - sbd * per) * SB_QB)
                _, _, dqi = lax.fori_loop(0, sbd, lambda jj, c: step(sbd - 1 - jj, c[0], c[1], c[2], None), first)
                dqa[rows, :] = dqi
                return 0

            lax.fori_loop(0, nb, qblock, 0)
            _, vq = jax.vjp(lambda t, g: _qk_norm(t, g, scale), q_ref[:, lanes], gq_ref[...])
            dq, dgq = vq(dqa[...])
            dq_ref[:, lanes] = dq.astype(BF16)
            dgq_ref[hh] = dgq
            _, vk = jax.vjp(lambda t, g: _qk_norm_t(t, g, 1.0), kt2[lanes, :], gkt_ref[...])
            dkt, dgkt = vk(dkt2[lanes, :])
            dkt2[lanes, :] = dkt
            dgkt_ref[hh] = dgkt
        dk_ref[...] = dkt2[...].T.astype(BF16)
        dv_ref[...] = dvt2[...].T.astype(BF16)

    blk = lambda off: pl.BlockSpec((S, LANES), lambda g: (0, off + g))
    once = lambda off: pl.BlockSpec((S, LANES), lambda g: (0, off + g), pipeline_mode=pl.Buffered(1))
    gr = pl.BlockSpec((1, dh), lambda g: (0, 0))
    gc = pl.BlockSpec((dh, 1), lambda g: (0, 0))
    sd = jax.ShapeDtypeStruct((S, SB_W), BF16)
    return _call_carrying(
        body, name, pairs, carry, [p, p, p, o, do, gq, gqt, gk, gkt],
        in_specs=[once(0), once(pairs), once(2 * pairs), once(0), once(0), gr, gc, gr, gc],
        out_specs=[blk(0), blk(0), blk(0), pl.BlockSpec((per_pair, 1, dh), lambda g: (g, 0, 0)),
                   pl.BlockSpec((per_pair, dh, 1), lambda g: (g, 0, 0))],
        out_shape=[sd, sd, sd, jax.ShapeDtypeStruct((SB_HEADS, 1, dh), F32),
                   jax.ShapeDtypeStruct((SB_HEADS, dh, 1), F32)],
        scratch_shapes=[pltpu.VMEM((LANES, S), F32)] * 6 + [pltpu.VMEM((S, dh), BF16), pltpu.VMEM((dh, S), BF16)] * 4
        + [pltpu.VMEM((S, dh), F32)],
        vmem=60 * 1024 * 1024)


def _shift_down(x, s, rows):
    if s == 0:
        return x
    return jnp.where(rows >= s, pltpu.roll(x, s, 0), 0.0)


def _shift_up(x, s, rows, n):
    if s == 0:
        return x
    return jnp.where(rows < n - s, pltpu.roll(x, n - s, 0), 0.0)


def _conv(x, w_ref, rows):
    y = x * w_ref[CONV_K - 1:CONV_K, :]
    for kk in range(CONV_K - 1):
        y = y + _shift_down(x, CONV_K - 1 - kk, rows) * w_ref[kk:kk + 1, :]
    return y


def _act_norm(y, normed):
    s = _silu(y)
    n = s * lax.rsqrt(jnp.sum(s * s, axis=-1, keepdims=True) + EPS)
    return jnp.where(normed, n, s)


def _dn_prep_fwd(p, conv_w, col0, name):
    S = p.shape[0]
    nblk = 3 * DN_HEADS
    b0 = col0 // DN_DH

    def body(x_ref, w_ref, o_ref):
        rows = lax.broadcasted_iota(jnp.int32, (S, DN_DH), 0)
        y = _conv(x_ref[...], w_ref, rows)
        o_ref[...] = _act_norm(y, pl.program_id(0) < 2 * DN_HEADS)

    return pl.pallas_call(
        body, name=name, grid=(nblk,),
        in_specs=[pl.BlockSpec((S, DN_DH), lambda j: (0, b0 + j)), pl.BlockSpec((CONV_K, DN_DH), lambda j: (0, j))],
        out_specs=pl.BlockSpec((S, DN_DH), lambda j: (0, j)),
        out_shape=jax.ShapeDtypeStruct((S, 3 * DN_W), F32),
        compiler_params=_cp("parallel"),
    )(p, conv_w)


def _dn_prep_bwd(p, conv_w, col0, dout, name):
    S = p.shape[0]
    nblk = 3 * DN_HEADS
    b0 = col0 // DN_DH

    def body(x_ref, w_ref, do_ref, dx_ref, dw_ref):
        rows = lax.broadcasted_iota(jnp.int32, (S, DN_DH), 0)
        x = x_ref[...]
        y = _conv(x, w_ref, rows)
        normed = pl.program_id(0) < 2 * DN_HEADS
        _, vjp = jax.vjp(lambda t: _act_norm(t, normed), y)
        (dy,) = vjp(do_ref[...])
        dx = dy * w_ref[CONV_K - 1:CONV_K, :]
        dw_ref[CONV_K - 1:CONV_K, :] = jnp.sum(dy * x, axis=0, keepdims=True)
        for kk in range(CONV_K - 1):
            s = CONV_K - 1 - kk
            dx = dx + _shift_up(dy, s, rows, S) * w_ref[kk:kk + 1, :]
            dw_ref[kk:kk + 1, :] = jnp.sum(dy * _shift_down(x, s, rows), axis=0, keepdims=True)
        dx_ref[...] = dx.astype(BF16)

    return pl.pallas_call(
        body, name=name, grid=(nblk,),
        in_specs=[pl.BlockSpec((S, DN_DH), lambda j: (0, b0 + j)), pl.BlockSpec((CONV_K, DN_DH), lambda j: (0, j)),
                  pl.BlockSpec((S, DN_DH), lambda j: (0, j))],
        out_specs=[pl.BlockSpec((S, DN_DH), lambda j: (0, j)), pl.BlockSpec((CONV_K, DN_DH), lambda j: (0, j))],
        out_shape=[jax.ShapeDtypeStruct((S, 3 * DN_W), BF16), jax.ShapeDtypeStruct((CONV_K, 3 * DN_W), F32)],
        compiler_params=_cp("parallel"),
    )(p, conv_w, dout)


def _gate_fn(x, pv):
    lane = lax.broadcasted_iota(jnp.int32, x.shape, 1)
    decay = -jnp.exp(pv[0:1, :]) * _softplus(x + pv[1:2, :])
    return jnp.where(lane < DN_HEADS, _sigmoid(x), decay)


def _dn_gate_fwd(p, pv, blk, name):
    S = p.shape[0]

    def body(x_ref, pv_ref, o_ref):
        o_ref[...] = _gate_fn(x_ref[...], pv_ref[...])

    return pl.pallas_call(
        body, name=name, grid=(1,),
        in_specs=[pl.BlockSpec((S, LANES), lambda i: (0, blk)), pl.BlockSpec((2, LANES), lambda i: (0, 0))],
        out_specs=pl.BlockSpec((S, LANES), lambda i: (0, 0)),
        out_shape=jax.ShapeDtypeStruct((S, LANES), F32),
        compiler_params=_cp("arbitrary"),
    )(p, pv)


def _dn_gate_bwd(p, pv, blk, dout, name):
    S = p.shape[0]

    def body(x_ref, pv_ref, do_ref, dx_ref, dpv_ref):
        _, vjp = jax.vjp(_gate_fn, x_ref[...], pv_ref[...])
        dx, dpv = vjp(do_ref[...])
        dx_ref[...] = dx.astype(BF16)
        dpv_ref[...] = dpv

    return pl.pallas_call(
        body, name=name, grid=(1,),
        in_specs=[pl.BlockSpec((S, LANES), lambda i: (0, blk)), pl.BlockSpec((2, LANES), lambda i: (0, 0)),
                  pl.BlockSpec((S, LANES), lambda i: (0, 0))],
        out_specs=[pl.BlockSpec((S, LANES), lambda i: (0, 0)), pl.BlockSpec((2, LANES), lambda i: (0, 0))],
        out_shape=[jax.ShapeDtypeStruct((S, LANES), BF16), jax.ShapeDtypeStruct((2, LANES), F32)],
        compiler_params=_cp("arbitrary"),
    )(p, pv, dout)


def _t(x):
    return jnp.swapaxes(x, -1, -2)


def _matmuls(prec, differentiable):
    def mm(a, b):
        return lax.dot_general(a, b, (((2,), (1,)), ((0,), (0,))), precision=prec, preferred_element_type=F32)

    def mm_nt(a, b):
        return lax.dot_general(a, b, (((2,), (2,)), ((0,), (0,))), precision=prec, preferred_element_type=F32)

    if not differentiable:
        return mm, mm_nt
    dmm, dmm_nt = jax.custom_vjp(mm), jax.custom_vjp(mm_nt)
    dmm.defvjp(lambda a, b: (mm(a, b), (a, b)), lambda res, g: (mm_nt(g, res[1]), mm(_t(res[0]), g)))
    dmm_nt.defvjp(lambda a, b: (mm_nt(a, b), (a, b)), lambda res, g: (mm(g, res[1]), mm(_t(g), res[0])))
    return dmm, dmm_nt


def _known_inverse(mm):
    f = jax.custom_vjp(lambda n, inv: inv)
    f.defvjp(lambda n, inv: (inv, inv),
             lambda inv, g: (mm(mm(_t(inv), g), _t(inv)), jnp.zeros_like(inv)))
    return f


def _delta_chunk(state, q, k, v, beta, a_col, a_row, differentiable=False, inv_known=None):
    mm, mm_nt = _matmuls(lax.Precision.HIGH, differentiable)
    mm_sum, _ = _matmuls(HI, differentiable)
    H, C, _ = q.shape
    r = lax.broadcasted_iota(jnp.int32, (H, C, C), 1)
    c = lax.broadcasted_iota(jnp.int32, (H, C, C), 2)
    tril, strict = r >= c, r > c
    eye = (r == c).astype(F32)
    g_c = mm_sum(tril.astype(F32), jnp.broadcast_to(a_col, (H, C, C)))
    g_r = mm_sum(jnp.broadcast_to(a_row, (H, C, C)), (r <= c).astype(F32))
    decay = jnp.where(tril, jnp.exp(jnp.where(tril, g_c - g_r, 0.0)), 0.0)
    eg = jnp.exp(g_c)
    g_last = jnp.sum(jnp.where(r == C - 1, g_c, 0.0), axis=1, keepdims=True)
    qs = q * (float(q.shape[2]) ** -0.5)
    kb = k * beta
    neg_m = jnp.where(strict, -(mm_nt(kb, k) * decay), 0.0)
    if inv_known is None:
        inv = eye + neg_m
        pw = neg_m
        for _ in range(int(math.log2(C)) - 1):
            pw = mm(pw, pw)
            inv = inv + mm(inv, pw)
    else:
        inv = _known_inverse(mm)(neg_m, inv_known)
    u = mm(inv, v * beta)
    w = mm(inv, kb * eg)
    intra = jnp.where(tril, mm_nt(qs, k) * decay, 0.0)
    v_new = u - mm(w, state)
    o = mm(qs * eg, state) + mm(intra, v_new)
    nxt = state * jnp.exp(g_last) + mm(_t(k * jnp.exp(g_last - g_c)), v_new)
    return o, nxt, inv


def _heads(t):
    return jnp.stack([t[:, h * DN_DH:(h + 1) * DN_DH] for h in range(DN_HEADS)])


def _delta_step(state, q, k, v, bg, a_row, differentiable=False, inv_known=None):
    lane = lax.broadcasted_iota(jnp.int32, bg.shape, 1)
    pick = lambda j: jnp.stack([jnp.sum(jnp.where(lane == j + h, bg, 0.0), axis=1, keepdims=True)
                                for h in range(DN_HEADS)])
    o, nxt, inv = _delta_chunk(state, _heads(q), _heads(k), _heads(v), pick(0), pick(DN_HEADS), a_row, differentiable,
                               inv_known)
    return jnp.concatenate([o[h] for h in range(DN_HEADS)], axis=1), nxt, inv


def _delta_fwd(qkv, bg, a_row, name):
    S = qkv.shape[0]
    nc = S // BLK

    def body(q_ref, k_ref, v_ref, bg_ref, ar_ref, o_ref, st_ref, inv_ref, state):
        ci = pl.program_id(0)

        @pl.when(ci == 0)
        def _():
            state[...] = jnp.zeros_like(state)

        st = state[...]
        st_ref[:, 0] = st
        o, nxt, inv = _delta_step(st, q_ref[...], k_ref[...], v_ref[...], bg_ref[...], ar_ref[:, pl.ds(ci, 1), :])
        o_ref[...] = o
        inv_ref[:, 0] = inv
        state[...] = nxt

    part = lambda j: pl.BlockSpec((BLK, DN_W), lambda c: (c, j))
    per_chunk = pl.BlockSpec((DN_HEADS, 1, DN_DH, DN_DH), lambda c: (0, c, 0, 0))
    mats = jax.ShapeDtypeStruct((DN_HEADS, nc, DN_DH, DN_DH), F32)
    return pl.pallas_call(
        body, name=name, grid=(nc,),
        in_specs=[part(0), part(1), part(2), pl.BlockSpec((BLK, LANES), lambda c: (c, 0)),
                  pl.BlockSpec((DN_HEADS, nc, BLK), lambda c: (0, 0, 0))],
        out_specs=[part(0), per_chunk, per_chunk], out_shape=[jax.ShapeDtypeStruct((S, DN_W), F32), mats, mats],
        scratch_shapes=[pltpu.VMEM((DN_HEADS, DN_DH, DN_DH), F32)],
        compiler_params=_cp("arbitrary"),
    )(qkv, qkv, qkv, bg, a_row)


def _delta_bwd(qkv, bg, a_row, states, invs, do, name):
    S = qkv.shape[0]
    nc = S // BLK

    def body(q_ref, k_ref, v_ref, bg_ref, ar_ref, st_ref, inv_ref, do_ref, dq_ref, dk_ref, dv_ref, dbg_ref, dar_ref,
             dstate):
        t = pl.program_id(0)
        ci = nc - 1 - t

        @pl.when(t == 0)
        def _():
            dstate[...] = jnp.zeros_like(dstate)

        step = lambda *a: _delta_step(*a, differentiable=True, inv_known=inv_ref[:, 0])[:2]
        _, vjp = jax.vjp(step, st_ref[:, 0], q_ref[...], k_ref[...], v_ref[...], bg_ref[...],
                         ar_ref[:, pl.ds(ci, 1), :])
        dprev, dq, dk, dv, dbg, dar = vjp((do_ref[...], dstate[...]))
        dq_ref[...] = dq
        dk_ref[...] = dk
        dv_ref[...] = dv
        dbg_ref[...] = dbg
        dar_ref[:, pl.ds(ci, 1), :] = dar
        dstate[...] = dprev

    part = lambda j: pl.BlockSpec((BLK, DN_W), lambda t: (nc - 1 - t, j))
    lanes = pl.BlockSpec((BLK, LANES), lambda t: (nc - 1 - t, 0))
    rows = pl.BlockSpec((DN_HEADS, nc, BLK), lambda t: (0, 0, 0))
    per_chunk = pl.BlockSpec((DN_HEADS, 1, DN_DH, DN_DH), lambda t: (0, nc - 1 - t, 0, 0))
    d3 = jax.ShapeDtypeStruct((S, DN_W), F32)
    return pl.pallas_call(
        body, name=name, grid=(nc,),
        in_specs=[part(0), part(1), part(2), lanes, rows, per_chunk, per_chunk, part(0)],
        out_specs=[part(0), part(0), part(0), lanes, rows],
        out_shape=[d3, d3, d3, jax.ShapeDtypeStruct((S, LANES), F32), jax.ShapeDtypeStruct((DN_HEADS, nc, BLK), F32)],
        scratch_shapes=[pltpu.VMEM((DN_HEADS, DN_DH, DN_DH), F32)],
        compiler_params=_cp("arbitrary"),
    )(qkv, qkv, qkv, bg, a_row, states, invs, do)


def _gate_sb(o, z):
    return o * _silu(z)


def _gate_dn(o, z, g):
    return jnp.concatenate(
        [_rms(o[:, h * DN_DH:(h + 1) * DN_DH]) * g * _silu(z[:, h * DN_DH:(h + 1) * DN_DH]) for h in range(DN_HEADS)],
        axis=1)


def _merge_specs(S, D, tm):
    row = lambda w, blk: pl.BlockSpec((tm, w), lambda i: (i, blk))
    full = lambda a, b: pl.BlockSpec((a, b), lambda i: (0, 0))
    return [row(D, 0), full(1, D), row(SB_W, 0), row(SB_W, 3), row(DN_W, 0), row(DN_W, 7),
            row(D, IN_MAIN // D), row(D, IN_MAIN // D + 1), full(1, DN_DH), full(SB_W, D), full(DN_W, D), full(D, D)]


def _merge_fwd(x, gate, o_sb, o_dn, p, ng, wbs, wbd, wo, name):
    S, D = x.shape
    tm = min(512, S)

    def body(x_ref, gate_ref, osb_ref, zsb_ref, odn_ref, zdn_ref, msb_ref, mdn_ref, ng_ref, wbs_ref, wbd_ref, wo_ref,
             out_ref):
        a = _gate_sb(osb_ref[...], zsb_ref[...])
        b = _gate_dn(odn_ref[...], zdn_ref[...], ng_ref[...])
        y = _sigmoid(msb_ref[...]) * _bdot(a, wbs_ref[...]) + _sigmoid(mdn_ref[...]) * _bdot(b, wbd_ref[...])
        out_ref[...] = x_ref[...] + gate_ref[...] * _bdot(y, wo_ref[...])

    return pl.pallas_call(
        body, name=name, grid=(S // tm,), in_specs=_merge_specs(S, D, tm),
        out_specs=pl.BlockSpec((tm, D), lambda i: (i, 0)), out_shape=jax.ShapeDtypeStruct((S, D), F32),
        compiler_params=_cp("parallel"),
    )(x, gate, o_sb, p, o_dn, p, p, p, ng, wbs, wbd, wo)


def _merge_bwd(dxn, gate, o_sb, o_dn, p, ng, wbs, wbd, wo, name):
    S, D = dxn.shape
    tm = min(256, S)

    def body(dxn_ref, gate_ref, osb_ref, zsb_ref, odn_ref, zdn_ref, msb_ref, mdn_ref, ng_ref, wbs_ref, wbd_ref, wo_ref,
             dosb_ref, dzsb_ref, dodn_ref, dzdn_ref, dmsb_ref, dmdn_ref, dwo_ref, dwbs_ref, dwbd_ref, dgate_ref, dng_ref):
        @pl.when(pl.program_id(0) == 0)
        def _():
            for ref in (dwo_ref, dwbs_ref, dwbd_ref, dgate_ref, dng_ref):
                ref[...] = jnp.zeros_like(ref)

        a, vjp_a = jax.vjp(_gate_sb, osb_ref[...], zsb_ref[...])
        b, vjp_b = jax.vjp(_gate_dn, odn_ref[...], zdn_ref[...], ng_ref[...])
        a16, b16 = a.astype(BF16), b.astype(BF16)
        ps = jnp.dot(a16, wbs_ref[...], preferred_element_type=F32)
        pd = jnp.dot(b16, wbd_ref[...], preferred_element_type=F32)
        ss, sd = _sigmoid(msb_ref[...]), _sigmoid(mdn_ref[...])
        y16 = (ss * ps + sd * pd).astype(BF16)
        out = jnp.dot(y16, wo_ref[...], preferred_element_type=F32)
        dxn_v = dxn_ref[...]
        dgate_ref[...] += jnp.sum(dxn_v * out, axis=0, keepdims=True)
        dout16 = (dxn_v * gate_ref[...]).astype(BF16)
        dwo_ref[...] += _bdot_tn(y16, dout16)
        dy = _bdot_nt(dout16, wo_ref[...])
        dmsb_ref[...] = (dy * ps * ss * (1.0 - ss)).astype(BF16)
        dmdn_ref[...] = (dy * pd * sd * (1.0 - sd)).astype(BF16)
        dps16, dpd16 = (dy * ss).astype(BF16), (dy * sd).astype(BF16)
        dwbs_ref[...] += _bdot_tn(a16, dps16)
        dwbd_ref[...] += _bdot_tn(b16, dpd16)
        dosb, dzsb = vjp_a(_bdot_nt(dps16, wbs_ref[...]))
        dodn, dzdn, dng = vjp_b(_bdot_nt(dpd16, wbd_ref[...]))
        dosb_ref[...] = dosb
        dzsb_ref[...] = dzsb.astype(BF16)
        dodn_ref[...] = dodn
        dzdn_ref[...] = dzdn.astype(BF16)
        dng_ref[...] += dng

    row = lambda w: pl.BlockSpec((tm, w), lambda i: (i, 0))
    full = lambda a, b: pl.BlockSpec((a, b), lambda i: (0, 0))
    sds = jax.ShapeDtypeStruct
    return pl.pallas_call(
        body, name=name, grid=(S // tm,), in_specs=_merge_specs(S, D, tm),
        out_specs=[row(SB_W), row(SB_W), row(DN_W), row(DN_W), row(D), row(D),
                   full(D, D), full(SB_W, D), full(DN_W, D), full(1, D), full(1, DN_DH)],
        out_shape=[sds((S, SB_W), F32), sds((S, SB_W), BF16), sds((S, DN_W), F32), sds((S, DN_W), BF16),
                   sds((S, D), BF16), sds((S, D), BF16), sds((D, D), F32), sds((SB_W, D), F32), sds((DN_W, D), F32),
                   sds((1, D), F32), sds((1, DN_DH), F32)],
        compiler_params=_cp("arbitrary"),
    )(dxn, gate, o_sb, p, o_dn, p, p, p, ng, wbs, wbd, wo)


def _loss_fwd_bwd(y, target, name):
    S, D = y.shape
    tm = min(512, S)

    def body(y_ref, t_ref, l_ref, dy_ref):
        @pl.when(pl.program_id(0) == 0)
        def _():
            l_ref[...] = jnp.zeros_like(l_ref)

        e = y_ref[...] - t_ref[...]
        l_ref[...] += jnp.sum(e * e, axis=0, keepdims=True) * (0.5 / D)
        dy_ref[...] = e * (1.0 / D)

    row = pl.BlockSpec((tm, D), lambda i: (i, 0))
    return pl.pallas_call(
        body, name=name, grid=(S // tm,), in_specs=[row, row],
        out_specs=[pl.BlockSpec((1, D), lambda i: (0, 0)), row],
        out_shape=[jax.ShapeDtypeStruct((1, D), F32), jax.ShapeDtypeStruct((S, D), F32)],
        compiler_params=_cp("arbitrary"),
    )(y, target)


def _mod_fwd(c_all, ada_w, name):
    L, D, n = ada_w.shape

    def body(c_ref, w_ref, o_ref):
        o_ref[0] = _dot(_silu(c_ref[...]), w_ref[0])

    return pl.pallas_call(
        body, name=name, grid=(L,),
        in_specs=[pl.BlockSpec(c_all.shape, lambda l: (0, 0)), pl.BlockSpec((1, D, n), lambda l: (l, 0, 0))],
        out_specs=pl.BlockSpec((1, N_DEV, n), lambda l: (l, 0, 0)),
        out_shape=jax.ShapeDtypeStruct((L, N_DEV, n), F32),
        compiler_params=_cp("parallel"),
    )(c_all, ada_w)


def _mod_bwd_w(c_all_t, dmod, name):
    L, _, n = dmod.shape
    D = c_all_t.shape[0]

    def body(c_ref, d_ref, o_ref):
        o_ref[0] = _dot(_silu(c_ref[...]), d_ref[0])

    return pl.pallas_call(
        body, name=name, grid=(L,),
        in_specs=[pl.BlockSpec(c_all_t.shape, lambda l: (0, 0)), pl.BlockSpec((1, N_DEV, n), lambda l: (l, 0, 0))],
        out_specs=pl.BlockSpec((1, D, n), lambda l: (l, 0, 0)),
        out_shape=jax.ShapeDtypeStruct((L, D, n), F32),
        compiler_params=_cp("parallel"),
    )(c_all_t, dmod)


def _me():
    return lax.axis_index("x"), lax.axis_index("y"), lax.axis_index("c")


def _peer(k):
    x, y, c = _me()
    return (1 - x if k & 4 else x, 1 - y if k & 2 else y, 1 - c if k & 1 else c)


def _lin(dev):
    return 4 * dev[0] + 2 * dev[1] + dev[2]


class _Exchange:
    def __init__(self, arrays, scatter):
        self.n = len(arrays)
        self.scatter = scatter
        self.out_shape = [jax.ShapeDtypeStruct((N_DEV,) + tuple(a.shape[1:] if scatter else a.shape), a.dtype)
                          for a in arrays]
        self.specs = [pl.BlockSpec(memory_space=pl.ANY)] * self.n
        self.scratch = [pltpu.SemaphoreType.DMA((self.n, N_DEV - 1)), pltpu.SemaphoreType.DMA((self.n, N_DEV - 1)),
                        pltpu.SemaphoreType.DMA((self.n,))]

    def _copies(self, ins, outs, sems):
        send_sems, recv_sems, local_sems = sems
        me = _lin(_me())
        local, remote, landed = [], [], []
        for t in range(self.n):
            src_of = (lambda d, t=t: ins[t].at[d]) if self.scatter else (lambda d, t=t: ins[t])
            local.append(pltpu.make_async_copy(src_of(me), outs[t].at[me], local_sems.at[t]))
            for k in range(1, N_DEV):
                peer = _peer(k)
                pair = dict(send_sem=send_sems.at[t, k - 1], recv_sem=recv_sems.at[t, k - 1], device_id=peer,
                            device_id_type=pl.DeviceIdType.MESH)
                remote.append(pltpu.make_async_remote_copy(src_ref=src_of(_lin(peer)), dst_ref=outs[t].at[me], **pair))
                slot = outs[t].at[_lin(peer)]
                landed.append(pltpu.make_async_remote_copy(src_ref=slot, dst_ref=slot, **pair))
        return local, remote, landed

    def start(self, ins, outs, sems):
        local, remote, _ = self._copies(ins, outs, sems)
        for cp in local + remote:
            cp.start()

    def finish(self, ins, outs, sems):
        local, remote, landed = self._copies(ins, outs, sems)
        for cp in landed:
            cp.wait_recv()
        for cp in remote:
            cp.wait_send()
        for cp in local:
            cp.wait()


def _exchange(arrays, scatter, name):
    ex = _Exchange(arrays, scatter)

    def body(*refs):
        ins, outs, sems = refs[:ex.n], refs[ex.n:2 * ex.n], refs[2 * ex.n:]
        ex.start(ins, outs, sems)
        ex.finish(ins, outs, sems)

    return pl.pallas_call(body, name=name, in_specs=ex.specs, out_specs=ex.specs, out_shape=ex.out_shape,
                          scratch_shapes=ex.scratch)(*arrays)


def _sum_slots(a, name):
    _, R, C = a.shape
    tr = SUM_ROWS if R % SUM_ROWS == 0 else R

    def body(a_ref, o_ref):
        acc = a_ref[0].astype(F32)
        for s in range(1, N_DEV):
            acc = acc + a_ref[s].astype(F32)
        o_ref[...] = acc

    return pl.pallas_call(
        body, name=name, grid=(R // tr,),
        in_specs=[pl.BlockSpec((N_DEV, tr, C), lambda i: (0, i, 0))], out_specs=pl.BlockSpec((tr, C), lambda i: (i, 0)),
        out_shape=jax.ShapeDtypeStruct((R, C), F32), compiler_params=_cp("parallel"),
    )(a)


def _adamw(w, g, m, v, name):
    shape = w.shape
    C = shape[-1]
    R = w.size // C
    tr = R
    for cand in (256, 128, 64):
        if R > cand and R % cand == 0:
            tr = cand
            break
    c1 = 1.0 / (1.0 - ADAM_B1 ** ADAM_STEP)
    c2 = 1.0 / (1.0 - ADAM_B2 ** ADAM_STEP)

    def body(w_ref, g_ref, m_ref, v_ref, d_ref, nm_ref, nv_ref):
        gv = g_ref[...]
        nm = ADAM_B1 * m_ref[...] + (1.0 - ADAM_B1) * gv
        nv = ADAM_B2 * v_ref[...] + (1.0 - ADAM_B2) * (gv * gv)
        d_ref[...] = -ADAM_LR * ((nm * c1) / (jnp.sqrt(nv * c2) + ADAM_EPS) + ADAM_WD * w_ref[...])
        nm_ref[...] = nm
        nv_ref[...] = nv

    spec = pl.BlockSpec((tr, C), lambda i: (i, 0))
    sd = jax.ShapeDtypeStruct((R, C), F32)
    outs = pl.pallas_call(
        body, name=name, grid=(R // tr,), in_specs=[spec] * 4, out_specs=[spec] * 3, out_shape=[sd] * 3,
        compiler_params=_cp("parallel"),
    )(*(t.reshape(R, C) for t in (w, g, m, v)))
    return tuple(t.reshape(shape) for t in outs)


def _pad_cols(w_in):
    D = w_in.shape[0]
    pad = jnp.zeros((D, LANES - 2 * DN_HEADS), w_in.dtype)
    return jnp.concatenate([w_in[:, :IN_MAIN], w_in[:, IN_COLS:], w_in[:, IN_MAIN:IN_COLS], pad], axis=1)


def _unpad_cols(dw, D):
    return jnp.concatenate([dw[:, :IN_MAIN], dw[:, IN_MAIN + 2 * D:IN_MAIN + 2 * D + 2 * DN_HEADS],
                            dw[:, IN_MAIN:IN_MAIN + 2 * D]], axis=1)


def _gate_params(a_log, dt_bias):
    z = jnp.zeros((LANES,), F32)
    return jnp.stack([z.at[DN_HEADS:2 * DN_HEADS].set(a_log), z.at[DN_HEADS:2 * DN_HEADS].set(dt_bias)])


def _layer_fwd(l, x, mod, wts, carry=None):
    S, D = x.shape
    tag = f"l{l}_"
    p, h = _inproj_fwd(x, mod, wts["norm_g"], wts["w_in"], tag + "inproj_fwd")
    (o_sb,), carried = _sb_fwd(p, wts["sb_q_g"], wts["sb_k_g"].T, tag + "sb_fwd", carry)
    qkv = _dn_prep_fwd(p, wts["conv_w"], 4 * SB_W, tag + "dn_prep_fwd")
    pv = _gate_params(wts["dn_a_log"], wts["dn_dt_bias"])
    ba_blk = (IN_MAIN + 2 * D) // LANES
    bg = _dn_gate_fwd(p, pv, ba_blk, tag + "dn_gate_fwd")
    a_row = bg[:, DN_HEADS:2 * DN_HEADS].T.reshape(DN_HEADS, S // BLK, BLK)
    o_dn, states, invs = _delta_fwd(qkv, bg, a_row, tag + "delta_fwd")
    gate = mod[:, 2 * D:]
    out = _merge_fwd(x, gate, o_sb, o_dn, p, wts["dn_norm_g"], wts["w_branch_sb"], wts["w_branch_dn"], wts["w_out"],
                     tag + "merge_fwd")
    saved = dict(x=x, mod=mod, p=p, h=h, o_sb=o_sb, qkv=qkv, pv=pv, bg=bg, a_row=a_row, o_dn=o_dn, states=states,
                 invs=invs, gate=gate)
    return out, saved, carried


def _layer_bwd(l, dxn, sv, wts, carry_of=None, late_carry_of=None):
    S, D = dxn.shape
    tag = f"l{l}_"
    (dosb, dzsb, dodn, dzdn, dmsb, dmdn, dwo, dwbs, dwbd, dgate, dng) = _merge_bwd(
        dxn, sv["gate"], sv["o_sb"], sv["o_dn"], sv["p"], wts["dn_norm_g"], wts["w_branch_sb"], wts["w_branch_dn"],
        wts["w_out"], tag + "merge_bwd")
    carry = None if carry_of is None else carry_of(dict(w_out=dwo, w_branch_sb=dwbs, w_branch_dn=dwbd))
    (dq, dk, dv, dgq, dgkt), carried = _sb_bwd(sv["p"], sv["o_sb"], dosb, wts["sb_q_g"], wts["sb_q_g"].T,
                                                wts["sb_k_g"], wts["sb_k_g"].T, tag + "sb_bwd", carry)
    dqkv_n_q, dqkv_n_k, dqkv_n_v, dbg, dar = _delta_bwd(sv["qkv"], sv["bg"], sv["a_row"], sv["states"], sv["invs"],
                                                         dodn, tag + "delta_bwd")
    dqkv, dconv = _dn_prep_bwd(sv["p"], wts["conv_w"], 4 * SB_W,
                               jnp.concatenate([dqkv_n_q, dqkv_n_k, dqkv_n_v], axis=1), tag + "dn_prep_bwd")
    dbg = dbg.at[:, DN_HEADS:2 * DN_HEADS].add(dar.reshape(DN_HEADS, S).T)
    ba_blk = (IN_MAIN + 2 * D) // LANES
    dba, dpv = _dn_gate_bwd(sv["p"], sv["pv"], ba_blk, dbg, tag + "dn_gate_bwd")
    dp = jnp.concatenate([dq, dk, dv, dzsb, dqkv, dzdn, dmsb, dmdn, dba], axis=1)
    dw_in = _unpad_cols(_matmul_tn(sv["h"].T, dp, tag + "inproj_bwd_dw"), D)
    late = None if late_carry_of is None else late_carry_of(dict(w_in=dw_in, conv_w=dconv))
    (dx, dmod, dg), carried_late = _inproj_bwd_dx(dp, wts["w_in"], sv["x"], sv["mod"], wts["norm_g"], dxn,
                                                  tag + "inproj_bwd_dx", late)
    dmod = dmod.at[:, 2 * D:].set(dgate)
    grads = dict(w_in=dw_in, w_branch_sb=dwbs, w_branch_dn=dwbd, w_out=dwo, conv_w=dconv,
                 mod=dmod[0], norm_g=dg[0], sb_q_g=jnp.sum(dgq, axis=0)[0], sb_k_g=jnp.sum(dgkt, axis=0)[:, 0],
                 dn_a_log=dpv[0, DN_HEADS:2 * DN_HEADS], dn_dt_bias=dpv[1, DN_HEADS:2 * DN_HEADS], dn_norm_g=dng[0])
    return dx, grads, carried, carried_late


def _pad_rows(a, mult):
    extra = (-a.shape[0]) % mult
    return a if extra == 0 else jnp.concatenate([a, jnp.zeros((extra,) + a.shape[1:], a.dtype)], axis=0)


def _pack_rows(parts, width, mult):
    flat = jnp.concatenate([t.reshape(-1) for t in parts])
    extra = (-flat.shape[0]) % width
    if extra:
        flat = jnp.concatenate([flat, jnp.zeros((extra,), flat.dtype)])
    return _pad_rows(flat.reshape(-1, width), mult)


def _take(flat, off, shape):
    n = math.prod(shape)
    return flat[..., off:off + n].reshape(flat.shape[:-1] + tuple(shape)), off + n


SMALL = ("mod", "norm_g", "sb_q_g", "sb_k_g", "dn_a_log", "dn_dt_bias", "dn_norm_g")


def kernel(x, c, ada_w, ada_b, norm_g, w_in, sb_q_g, sb_k_g, conv_w, dn_a_log, dn_dt_bias, dn_norm_g, w_branch_sb, w_branch_dn, w_out, loss_target, m_ada_w, m_ada_b, m_norm_g, m_w_in, m_sb_q_g, m_sb_k_g, m_conv_w, m_dn_a_log, m_dn_dt_bias, m_dn_norm_g, m_w_branch_sb, m_w_branch_dn, m_w_out, v_ada_w, v_ada_b, v_norm_g, v_w_in, v_sb_q_g, v_sb_k_g, v_conv_w, v_dn_a_log, v_dn_dt_bias, v_dn_norm_g, v_w_branch_sb, v_w_branch_dn, v_w_out):
    L, D = norm_g.shape
    S = x.shape[1]
    n_in = w_in.shape[2]
    n_ada = ada_w.shape[2]
    n_br = w_branch_sb.shape[2]
    n_out = w_out.shape[1]
    n_conv = conv_w.shape[2]
    me = _lin(_me())

    def cat(a):
        return jnp.concatenate([a[d] for d in range(N_DEV)], axis=1)

    c_all, conv_all = _exchange([c, conv_w.reshape(L * CONV_K, n_conv)], False, "gather_small")
    c_all = c_all.reshape(N_DEV, D)
    conv_full = cat(conv_all).reshape(L, CONV_K, N_DEV * n_conv)

    mod_part = _mod_fwd(c_all, ada_w, "mod_fwd")

    def shards16(l):
        return [w_in[l].astype(BF16), w_branch_sb[l].astype(BF16), w_branch_dn[l].astype(BF16), w_out[l].astype(BF16)]

    def whole(l, got):
        wi, wbs, wbd, wo = got
        return dict(norm_g=norm_g[l:l + 1], w_in=_pad_cols(cat(wi)), sb_q_g=sb_q_g[l:l + 1], sb_k_g=sb_k_g[l:l + 1],
                    conv_w=conv_full[l], dn_a_log=dn_a_log[l], dn_dt_bias=dn_dt_bias[l], dn_norm_g=dn_norm_g[l:l + 1],
                    w_branch_sb=cat(wbs), w_branch_dn=cat(wbd), w_out=wo.reshape(N_DEV * n_out, D))

    *got, mod_all = _exchange(shards16(0) + [mod_part.reshape(L * N_DEV, n_ada)], False, "gather_weights")
    mod_full = cat(mod_all).reshape(L, N_DEV, N_DEV * n_ada) + ada_b[:, None, :]
    mod_mine = lax.dynamic_slice_in_dim(mod_full, me, 1, axis=1)

    act = x[0]
    saved, wts = [], []
    for l in range(L):
        wts.append(whole(l, got))
        act, sv, got = _layer_fwd(l, act, mod_mine[l], wts[l], (shards16(l + 1), False) if l + 1 < L else None)
        saved.append(sv)
    loss_cols, dact = _loss_fwd_bwd(act, loss_target[0], "loss")
    loss = lax.psum(jnp.sum(loss_cols), ("x", "y", "c"))

    def blocks(name, g):
        if name == "w_out":
            return g.astype(BF16).reshape(N_DEV, n_out, D)
        n = g.shape[1] // N_DEV
        dtype = F32 if name == "conv_w" else BF16
        return jnp.stack([g[:, d * n:(d + 1) * n].astype(dtype) for d in range(N_DEV)])

    early, late = ("w_out", "w_branch_sb", "w_branch_dn"), ("w_in", "conv_w")
    grads, recv, pending = [None] * L, {}, []
    for l in reversed(range(L)):
        keys = [k for k, _ in pending] + [(n, l) for n in early]

        def carry_of(g_early, pending=pending):
            return [a for _, a in pending] + [blocks(n, g_early[n]) for n in early], True

        last = l == 0
        dact, grads[l], got, got_late = _layer_bwd(
            l, dact, saved[l], wts[l], carry_of, (lambda g: ([blocks(n, g[n]) for n in late], True)) if last else None)
        recv.update(zip(keys, got))
        recv.update(zip([(n, l) for n in late], got_late))
        pending = [] if last else [((n, l), blocks(n, grads[l][n])) for n in late]
    grad_x = dact[None]
    small_g = _pack_rows([grads[l][n] for l in range(L) for n in SMALL], LANES, 8)
    (small_all_g,) = _exchange([small_g], False, "gather_small_grads")
    small_sum = _sum_slots(small_all_g, "sum_small_grads").reshape(-1)
    shard_shapes = dict(w_in=w_in.shape, w_branch_sb=w_branch_sb.shape, w_branch_dn=w_branch_dn.shape,
                        conv_w=conv_w.shape, w_out=w_out.shape)
    g_out = {n: jnp.stack([_sum_slots(recv[(n, l)], f"sum_{n}_l{l}").reshape(shape[1:]) for l in range(L)])
             for n, shape in shard_shapes.items()}
    small_shapes = dict(mod=(3 * D,), norm_g=(D,), sb_q_g=(SB_DH,), sb_k_g=(SB_DH,), dn_a_log=(DN_HEADS,),
                        dn_dt_bias=(DN_HEADS,), dn_norm_g=(DN_DH,))
    off = 0
    off_all = 0
    small_each = small_all_g.reshape(N_DEV, -1)
    per_small = {n: [] for n in SMALL}
    dmod_all = []
    for l in range(L):
        for n in SMALL:
            t, off = _take(small_sum, off, small_shapes[n])
            per_small[n].append(t)
            if n == "mod":
                t_all, _ = _take(small_each, off_all, small_shapes[n])
                dmod_all.append(t_all)
            off_all += math.prod(small_shapes[n])
    for n in SMALL:
        g_out[n if n != "mod" else "ada_b"] = jnp.stack(per_small[n])
    dmod_all = jnp.stack(dmod_all)
    dmod_cols = lax.dynamic_slice_in_dim(dmod_all, me * n_ada, n_ada, axis=2)
    g_out["ada_w"] = _mod_bwd_w(c_all.T, dmod_cols, "mod_bwd_w")

    given = dict(ada_w=(ada_w, m_ada_w, v_ada_w), ada_b=(ada_b, m_ada_b, v_ada_b), norm_g=(norm_g, m_norm_g, v_norm_g),
                 w_in=(w_in, m_w_in, v_w_in), sb_q_g=(sb_q_g, m_sb_q_g, v_sb_q_g), sb_k_g=(sb_k_g, m_sb_k_g, v_sb_k_g),
                 conv_w=(conv_w, m_conv_w, v_conv_w), dn_a_log=(dn_a_log, m_dn_a_log, v_dn_a_log),
                 dn_dt_bias=(dn_dt_bias, m_dn_dt_bias, v_dn_dt_bias), dn_norm_g=(dn_norm_g, m_dn_norm_g, v_dn_norm_g),
                 w_branch_sb=(w_branch_sb, m_w_branch_sb, v_w_branch_sb),
                 w_branch_dn=(w_branch_dn, m_w_branch_dn, v_w_branch_dn), w_out=(w_out, m_w_out, v_w_out))
    order = list(given)
    upd = {n: _adamw(given[n][0], g_out[n], given[n][1], given[n][2], "adamw_" + n) for n in order}
    return (loss, grad_x, *[g_out[n] for n in order], *[upd[n][0] for n in order], *[upd[n][1] for n in order],
            *[upd[n][2] for n in order])
```

```python
import functools
import math

import jax
import jax.numpy as jnp
from jax import lax
from jax.experimental import pallas as pl
from jax.experimental.pallas import tpu as pltpu

F32 = jnp.float32
BF16 = jnp.bfloat16
HI = lax.Precision.HIGHEST

N_DEV = 8
EPS = 1e-6
SB_HEADS, SB_DH = 8, 64
DN_HEADS, DN_DH = 4, 128
SB_W = SB_HEADS * SB_DH
DN_W = DN_HEADS * DN_DH
CONV_K = 4
BLK = 128
SB_KEYS = 512
SB_QB = 512
LANES = 128
IN_MAIN = 4 * SB_W + 4 * DN_W
IN_COLS = IN_MAIN + 2 * DN_HEADS
ADAM_LR, ADAM_B1, ADAM_B2, ADAM_EPS, ADAM_WD, ADAM_STEP = 0.001, 0.9, 0.999, 1e-08, 0.01, 10
VMEM_LIMIT = 56 * 1024 * 1024
LOG2E = 1.4426950408889634
SUM_ROWS = 128


def _cp(*sem, vmem=VMEM_LIMIT):
    return pltpu.CompilerParams(dimension_semantics=sem if sem else None, vmem_limit_bytes=vmem)


def _dot(a, b, prec=HI):
    return lax.dot_general(a, b, (((1,), (0,)), ((), ())), precision=prec, preferred_element_type=F32)


def _dot_nt(a, b, prec=HI):
    return lax.dot_general(a, b, (((1,), (1,)), ((), ())), precision=prec, preferred_element_type=F32)


def _bdot(a, b):
    return lax.dot_general(a.astype(BF16), b.astype(BF16), (((1,), (0,)), ((), ())), preferred_element_type=F32)


def _bdot_nt(a, b):
    return lax.dot_general(a.astype(BF16), b.astype(BF16), (((1,), (1,)), ((), ())), preferred_element_type=F32)


def _bdot_tn(a, b):
    return lax.dot_general(a.astype(BF16), b.astype(BF16), (((0,), (0,)), ((), ())), preferred_element_type=F32)


def _split_dot(a, b01_twice):
    hi = a.astype(BF16)
    lo = (a - hi.astype(F32)).astype(BF16)
    return jnp.dot(jnp.concatenate([hi, lo], axis=1), b01_twice, preferred_element_type=F32)


def _sigmoid(x):
    return 1.0 / (1.0 + jnp.exp(-x))


def _silu(x):
    return x * _sigmoid(x)


def _softplus(x):
    return jnp.maximum(x, 0.0) + jnp.log(1.0 + jnp.exp(-jnp.abs(x)))


def _rms(x):
    return x * lax.rsqrt(jnp.mean(x * x, axis=-1, keepdims=True) + EPS)


def _prenorm(x, g, shift, scale):
    return _rms(x) * g * (1.0 + scale) + shift


def _inproj_fwd(x, mod, g, w, name):
    S, D = x.shape
    N = w.shape[1]
    tm = min(512, S)
    tn = 896 if N % 896 == 0 else 128

    def body(x_ref, mod_ref, g_ref, w_ref, p_ref, h_ref):
        @pl.when(pl.program_id(1) == 0)
        def _():
            h = _prenorm(x_ref[...], g_ref[...], mod_ref[:, 0:D], mod_ref[:, D:2 * D])
            h_ref[...] = h.astype(BF16)

        p_ref[...] = jnp.dot(h_ref[...], w_ref[...], preferred_element_type=F32)

    return pl.pallas_call(
        body, name=name, grid=(S // tm, N // tn),
        in_specs=[pl.BlockSpec((tm, D), lambda i, j: (i, 0)), pl.BlockSpec((1, 3 * D), lambda i, j: (0, 0)),
                  pl.BlockSpec((1, D), lambda i, j: (0, 0)), pl.BlockSpec((D, tn), lambda i, j: (0, j))],
        out_specs=[pl.BlockSpec((tm, tn), lambda i, j: (i, j)), pl.BlockSpec((tm, D), lambda i, j: (i, 0))],
        out_shape=[jax.ShapeDtypeStruct((S, N), F32), jax.ShapeDtypeStruct((S, D), BF16)],
        compiler_params=_cp("parallel", "arbitrary"),
    )(x, mod, g, w)


def _inproj_bwd_dx(dp, w, x, mod, g, dxn, name, carry=None):
    S, N = dp.shape
    D = x.shape[1]
    tm = min(512, S)
    tk = 896 if N % 896 == 0 else 128
    nk = N // tk

    def body(dp_ref, w_ref, x_ref, mod_ref, g_ref, dxn_ref, dx_ref, dmod_ref, dg_ref, acc):
        i, k = pl.program_id(0), pl.program_id(1)

        @pl.when(k == 0)
        def _():
            acc[...] = jnp.zeros_like(acc)

        @pl.when((i == 0) & (k == 0))
        def _():
            dmod_ref[...] = jnp.zeros_like(dmod_ref)
            dg_ref[...] = jnp.zeros_like(dg_ref)

        acc[...] += lax.dot_general(dp_ref[...], w_ref[...], (((1,), (1,)), ((), ())), preferred_element_type=F32)

        @pl.when(k == nk - 1)
        def _():
            _, vjp = jax.vjp(_prenorm, x_ref[...], g_ref[...], mod_ref[:, 0:D], mod_ref[:, D:2 * D])
            dx, dg, dshift, dscale = vjp(acc[...])
            dx_ref[...] = dxn_ref[...] + dx
            dg_ref[...] += dg
            dmod_ref[:, 0:D] += dshift
            dmod_ref[:, D:2 * D] += dscale

    return _call_carrying(
        body, name, (S // tm, nk), carry, [dp, w, x, mod, g, dxn],
        in_specs=[pl.BlockSpec((tm, tk), lambda i, k: (i, k)), pl.BlockSpec((D, tk), lambda i, k: (0, k)),
                  pl.BlockSpec((tm, D), lambda i, k: (i, 0)), pl.BlockSpec((1, 3 * D), lambda i, k: (0, 0)),
                  pl.BlockSpec((1, D), lambda i, k: (0, 0)), pl.BlockSpec((tm, D), lambda i, k: (i, 0))],
        out_specs=[pl.BlockSpec((tm, D), lambda i, k: (i, 0)), pl.BlockSpec((1, 3 * D), lambda i, k: (0, 0)),
                   pl.BlockSpec((1, D), lambda i, k: (0, 0))],
        out_shape=[jax.ShapeDtypeStruct((S, D), F32), jax.ShapeDtypeStruct((1, 3 * D), F32),
                   jax.ShapeDtypeStruct((1, D), F32)],
        scratch_shapes=[pltpu.VMEM((tm, D), F32)], vmem=VMEM_LIMIT)


def _matmul_tn(a_t, b, name):
    M, K = a_t.shape
    N = b.shape[1]
    tn = 896 if N % 896 == 0 else (512 if N % 512 == 0 else 128)
    tk = min(512, K)
    nk = K // tk

    def body(a_ref, b_ref, o_ref):
        @pl.when(pl.program_id(1) == 0)
        def _():
            o_ref[...] = jnp.zeros_like(o_ref)

        o_ref[...] += jnp.dot(a_ref[...], b_ref[...], preferred_element_type=F32)

    return pl.pallas_call(
        body, name=name, grid=(N // tn, nk),
        in_specs=[pl.BlockSpec((M, tk), lambda j, k: (0, k)), pl.BlockSpec((tk, tn), lambda j, k: (k, j))],
        out_specs=pl.BlockSpec((M, tn), lambda j, k: (0, j)),
        out_shape=jax.ShapeDtypeStruct((M, N), F32),
        compiler_params=_cp("parallel", "arbitrary"),
    )(a_t, b)


def _qk_norm(t, g, scale):
    return _rms(t) * g * scale


def _qk_norm_t(t, g_col, scale):
    return t * lax.rsqrt(jnp.mean(t * t, axis=0, keepdims=True) + EPS) * g_col * scale


def _suffix_sums(x, tri):
    half = tri.shape[1]
    lo, hi = x[:, :half], x[:, half:]
    hi_sum = jnp.sum(hi, axis=1, keepdims=True)
    y = jnp.concatenate([_split_dot(lo, tri) + hi_sum, _split_dot(hi, tri)], axis=1)
    return y, hi_sum + jnp.sum(lo, axis=1, keepdims=True)


def _sb_step(qi, kat_blk, cl, upper, valid):
    z = jnp.dot(qi, kat_blk, preferred_element_type=F32)
    nz = -z
    lk = jnp.minimum(nz, 0.0) - jnp.log(1.0 + jnp.exp2(jnp.minimum(z, nz))) * LOG2E
    if valid is not None:
        lk = jnp.where(valid, lk, 0.0)
    later, tot = _suffix_sums(lk, upper)
    w = jnp.exp2(z + lk + later + cl)
    if valid is not None:
        w = jnp.where(valid, w, 0.0)
    return z, lk, w, tot


def _sb_masks(kb):
    half = kb // 2
    r = lax.broadcasted_iota(jnp.int32, (half, half), 0)
    c = lax.broadcasted_iota(jnp.int32, (half, half), 1)
    rq = lax.broadcasted_iota(jnp.int32, (SB_QB, kb), 0)
    ck = lax.broadcasted_iota(jnp.int32, (SB_QB, kb), 1)
    twice = lambda m: jnp.concatenate([m, m], axis=0).astype(BF16)
    return twice((r > c).astype(F32)), twice((r >= c).astype(F32)), ck - rq


def _sb_fwd(p, gq, gkt, name, carry=None):
    S, dh = p.shape[0], SB_DH
    kb = min(SB_KEYS, S)
    per = kb // SB_QB
    nb = S // SB_QB
    scale = 1.0 / math.sqrt(dh)
    pairs = SB_W // LANES

    def body(q_ref, k_ref, v_ref, gq_ref, gkt_ref, o_ref, kt2, qa, kat, vb):
        kt2[...] = k_ref[...].T
        upper, _, diff = _sb_masks(kb)
        for hh in range(LANES // dh):
            lanes = slice(hh * dh, (hh + 1) * dh)
            qa[...] = _qk_norm(q_ref[:, lanes], gq_ref[...], scale * LOG2E).astype(BF16)
            kat[...] = _qk_norm_t(kt2[lanes, :], gkt_ref[...], 1.0).astype(BF16)
            vb[...] = v_ref[:, lanes].astype(BF16)

            def qblock(i, _):
                rows = pl.ds(pl.multiple_of(i * SB_QB, SB_QB), SB_QB)
                qi = qa[rows, :]
                sbd = i // per

                def step(sb, cl, acc, valid):
                    cols = pl.ds(pl.multiple_of(sb * kb, kb), kb)
                    _, _, w, tot = _sb_step(qi, kat[:, cols], cl, upper, valid)
                    return cl + tot, acc + jnp.dot(w.astype(BF16), vb[cols, :], preferred_element_type=F32)

                cl, acc = step(sbd, jnp.zeros((SB_QB, 1), F32), jnp.zeros((SB_QB, dh), F32),
                               diff < (i - sbd * per) * SB_QB)
                _, acc = lax.fori_loop(0, sbd, lambda jj, c: step(sbd - 1 - jj, c[0], c[1], None), (cl, acc))
                o_ref[rows, lanes] = acc
                return 0

            lax.fori_loop(0, nb, qblock, 0)

    blk = lambda off: pl.BlockSpec((S, LANES), lambda g: (0, off + g))
    return _call_carrying(
        body, name, pairs, carry, [p, p, p, gq, gkt],
        in_specs=[blk(0), blk(pairs), blk(2 * pairs), pl.BlockSpec((1, dh), lambda g: (0, 0)),
                  pl.BlockSpec((dh, 1), lambda g: (0, 0))],
        out_specs=[blk(0)], out_shape=[jax.ShapeDtypeStruct((S, SB_W), F32)],
        scratch_shapes=[pltpu.VMEM((LANES, S), F32), pltpu.VMEM((S, dh), BF16), pltpu.VMEM((dh, S), BF16),
                        pltpu.VMEM((S, dh), BF16)],
        vmem=VMEM_LIMIT)


def _call_carrying(body, name, grid, carry, operands, in_specs, out_specs, out_shape, scratch_shapes, vmem):
    grid = (grid,) if isinstance(grid, int) else tuple(grid)
    if carry is None:
        res = pl.pallas_call(body, name=name, grid=grid, in_specs=in_specs, out_specs=out_specs,
                             out_shape=out_shape, scratch_shapes=scratch_shapes,
                             compiler_params=_cp(*["arbitrary"] * len(grid), vmem=vmem))(*operands)
        return res, []

    def at(corner):
        hit = pl.program_id(0) == corner(grid[0])
        for axis in range(1, len(grid)):
            hit = jnp.logical_and(hit, pl.program_id(axis) == corner(grid[axis]))
        return hit

    ex = _Exchange(*carry)
    n_in, n_out, n_scr = len(in_specs), len(out_specs), len(scratch_shapes)

    def wrapped(*refs):
        ins, refs = refs[:n_in], refs[n_in:]
        xin, refs = refs[:ex.n], refs[ex.n:]
        outs, refs = refs[:n_out], refs[n_out:]
        xout, refs = refs[:ex.n], refs[ex.n:]
        scr, sems = refs[:n_scr], refs[n_scr:]

        @pl.when(at(lambda n: 0))
        def _():
            ex.start(xin, xout, sems)

        body(*ins, *outs, *scr)

        @pl.when(at(lambda n: n - 1))
        def _():
            ex.finish(xin, xout, sems)

    res = pl.pallas_call(wrapped, name=name, grid=grid, in_specs=in_specs + ex.specs,
                         out_specs=out_specs + ex.specs, out_shape=out_shape + ex.out_shape,
                         scratch_shapes=scratch_shapes + ex.scratch,
                         compiler_params=_cp(*["arbitrary"] * len(grid), vmem=vmem))(*operands, *carry[0])
    return res[:n_out], res[n_out:]


def _sb_bwd(p, o, do, gq, gqt, gk, gkt, name, carry=None):
    S, dh = p.shape[0], SB_DH
    kb = min(SB_KEYS, S)
    per = kb // SB_QB
    nb = S // SB_QB
    scale = 1.0 / math.sqrt(dh)
    pairs = SB_W // LANES
    per_pair = LANES // dh

    def body(q_ref, k_ref, v_ref, o_ref, do_ref, gq_ref, gqt_ref, gk_ref, gkt_ref,
             dq_ref, dk_ref, dv_ref, dgq_ref, dgkt_ref,
             qt2, kt2, vt2, dot2, dkt2, dvt2, qa, qat, ka, kat, vb, vtb, dob, dotb, dqa):
        qt2[...] = q_ref[...].T
        kt2[...] = k_ref[...].T
        vt2[...] = v_ref[...].T
        dot2[...] = do_ref[...].T
        dkt2[...] = jnp.zeros_like(dkt2)
        dvt2[...] = jnp.zeros_like(dvt2)
        upper, lower_incl, diff = _sb_masks(kb)
        for hh in range(per_pair):
            lanes = slice(hh * dh, (hh + 1) * dh)
            qa[...] = _qk_norm(q_ref[:, lanes], gq_ref[...], scale * LOG2E).astype(BF16)
            qat[...] = _qk_norm_t(qt2[lanes, :], gqt_ref[...], scale * LOG2E).astype(BF16)
            ka[...] = _qk_norm(k_ref[:, lanes], gk_ref[...], 1.0).astype(BF16)
            kat[...] = _qk_norm_t(kt2[lanes, :], gkt_ref[...], 1.0).astype(BF16)
            vb[...] = v_ref[:, lanes].astype(BF16)
            vtb[...] = vt2[lanes, :].astype(BF16)
            dob[...] = do_ref[:, lanes].astype(BF16)
            dotb[...] = dot2[lanes, :].astype(BF16)

            def qblock(i, _):
                rows = pl.ds(pl.multiple_of(i * SB_QB, SB_QB), SB_QB)
                qi, qit = qa[rows, :], qat[:, rows]
                doi, doit = dob[rows, :], dotb[:, rows]
                total = jnp.sum(doi.astype(F32) * o_ref[rows, lanes], axis=1, keepdims=True)
                sbd = i // per

                def step(sb, cl, cd, dqi, valid):
                    cols = pl.ds(pl.multiple_of(sb * kb, kb), kb)
                    z, lk, w, tot = _sb_step(qi, kat[:, cols], cl, upper, valid)
                    w16 = w.astype(BF16)
                    dl = jnp.dot(doi, vtb[:, cols], preferred_element_type=F32) * w16.astype(F32)
                    incl, dtot = _suffix_sums(dl, lower_incl)
                    sig = jnp.exp2(z + lk)
                    dz = dl - sig * (dl + (total - cd - incl))
                    if valid is not None:
                        dz = jnp.where(valid, dz, 0.0)
                    dz16 = dz.astype(BF16)
                    dqi = dqi + jnp.dot(dz16, ka[cols, :], preferred_element_type=F32)
                    dkt2[lanes, cols] += jnp.dot(qit, dz16, preferred_element_type=F32)
                    dvt2[lanes, cols] += jnp.dot(doit, w16, preferred_element_type=F32)
                    return cl + tot, cd + dtot, dqi

                zero = jnp.zeros((SB_QB, 1), F32)
                first = step(sbd, zero, zero, jnp.zeros((SB_QB, dh), F32), diff < (i - sbd * per) * SB_QB)
                _, _, dqi = lax.fori_loop(0, sbd, lambda jj, c: step(sbd - 1 - jj, c[0], c[1], c[2], None), first)
                dqa[rows, :] = dqi
                return 0

            lax.fori_loop(0, nb, qblock, 0)
            _, vq = jax.vjp(lambda t, g: _qk_norm(t, g, scale), q_ref[:, lanes], gq_ref[...])
            dq, dgq = vq(dqa[...])
            dq_ref[:, lanes] = dq.astype(BF16)
            dgq_ref[hh] = dgq
            _, vk = jax.vjp(lambda t, g: _qk_norm_t(t, g, 1.0), kt2[lanes, :], gkt_ref[...])
            dkt, dgkt = vk(dkt2[lanes, :] * (1.0 / LOG2E))
            dkt2[lanes, :] = dkt
            dgkt_ref[hh] = dgkt
        dk_ref[...] = dkt2[...].T.astype(BF16)
        dv_ref[...] = dvt2[...].T.astype(BF16)

    blk = lambda off: pl.BlockSpec((S, LANES), lambda g: (0, off + g))
    once = lambda off: pl.BlockSpec((S, LANES), lambda g: (0, off + g), pipeline_mode=pl.Buffered(1))
    gr = pl.BlockSpec((1, dh), lambda g: (0, 0))
    gc = pl.BlockSpec((dh, 1), lambda g: (0, 0))
    sd = jax.ShapeDtypeStruct((S, SB_W), BF16)
    return _call_carrying(
        body, name, pairs, carry, [p, p, p, o, do, gq, gqt, gk, gkt],
        in_specs=[once(0), once(pairs), once(2 * pairs), once(0), once(0), gr, gc, gr, gc],
        out_specs=[blk(0), blk(0), blk(0), pl.BlockSpec((per_pair, 1, dh), lambda g: (g, 0, 0)),
                   pl.BlockSpec((per_pair, dh, 1), lambda g: (g, 0, 0))],
        out_shape=[sd, sd, sd, jax.ShapeDtypeStruct((SB_HEADS, 1, dh), F32),
                   jax.ShapeDtypeStruct((SB_HEADS, dh, 1), F32)],
        scratch_shapes=[pltpu.VMEM((LANES, S), F32)] * 6 + [pltpu.VMEM((S, dh), BF16), pltpu.VMEM((dh, S), BF16)] * 4
        + [pltpu.VMEM((S, dh), F32)],
        vmem=60 * 1024 * 1024)


def _shift_down(x, s, rows):
    if s == 0:
        return x
    return jnp.where(rows >= s, pltpu.roll(x, s, 0), 0.0)


def _shift_up(x, s, rows, n):
    if s == 0:
        return x
    return jnp.where(rows < n - s, pltpu.roll(x, n - s, 0), 0.0)


def _conv(x, w_ref, rows):
    y = x * w_ref[CONV_K - 1:CONV_K, :]
    for kk in range(CONV_K - 1):
        y = y + _shift_down(x, CONV_K - 1 - kk, rows) * w_ref[kk:kk + 1, :]
    return y


def _act_norm(y, normed):
    s = _silu(y)
    n = s * lax.rsqrt(jnp.sum(s * s, axis=-1, keepdims=True) + EPS)
    return jnp.where(normed, n, s)


def _dn_prep_fwd(p, conv_w, col0, name):
    S = p.shape[0]
    nblk = 3 * DN_HEADS
    b0 = col0 // DN_DH

    def body(x_ref, w_ref, o_ref):
        rows = lax.broadcasted_iota(jnp.int32, (S, DN_DH), 0)
        y = _conv(x_ref[...], w_ref, rows)
        o_ref[...] = _act_norm(y, pl.program_id(0) < 2 * DN_HEADS)

    return pl.pallas_call(
        body, name=name, grid=(nblk,),
        in_specs=[pl.BlockSpec((S, DN_DH), lambda j: (0, b0 + j)), pl.BlockSpec((CONV_K, DN_DH), lambda j: (0, j))],
        out_specs=pl.BlockSpec((S, DN_DH), lambda j: (0, j)),
        out_shape=jax.ShapeDtypeStruct((S, 3 * DN_W), F32),
        compiler_params=_cp("parallel"),
    )(p, conv_w)


def _dn_prep_bwd(p, conv_w, col0, dout, name):
    S = p.shape[0]
    nblk = 3 * DN_HEADS
    b0 = col0 // DN_DH

    def body(x_ref, w_ref, do_ref, dx_ref, dw_ref):
        rows = lax.broadcasted_iota(jnp.int32, (S, DN_DH), 0)
        x = x_ref[...]
        y = _conv(x, w_ref, rows)
        normed = pl.program_id(0) < 2 * DN_HEADS
        _, vjp = jax.vjp(lambda t: _act_norm(t, normed), y)
        (dy,) = vjp(do_ref[...])
        dx = dy * w_ref[CONV_K - 1:CONV_K, :]
        dw_ref[CONV_K - 1:CONV_K, :] = jnp.sum(dy * x, axis=0, keepdims=True)
        for kk in range(CONV_K - 1):
            s = CONV_K - 1 - kk
            dx = dx + _shift_up(dy, s, rows, S) * w_ref[kk:kk + 1, :]
            dw_ref[kk:kk + 1, :] = jnp.sum(dy * _shift_down(x, s, rows), axis=0, keepdims=True)
        dx_ref[...] = dx.astype(BF16)

    return pl.pallas_call(
        body, name=name, grid=(nblk,),
        in_specs=[pl.BlockSpec((S, DN_DH), lambda j: (0, b0 + j)), pl.BlockSpec((CONV_K, DN_DH), lambda j: (0, j)),
                  pl.BlockSpec((S, DN_DH), lambda j: (0, j))],
        out_specs=[pl.BlockSpec((S, DN_DH), lambda j: (0, j)), pl.BlockSpec((CONV_K, DN_DH), lambda j: (0, j))],
        out_shape=[jax.ShapeDtypeStruct((S, 3 * DN_W), BF16), jax.ShapeDtypeStruct((CONV_K, 3 * DN_W), F32)],
        compiler_params=_cp("parallel"),
    )(p, conv_w, dout)


def _gate_fn(x, pv):
    lane = lax.broadcasted_iota(jnp.int32, x.shape, 1)
    decay = -jnp.exp(pv[0:1, :]) * _softplus(x + pv[1:2, :])
    return jnp.where(lane < DN_HEADS, _sigmoid(x), decay)


def _dn_gate_fwd(p, pv, blk, name):
    S = p.shape[0]

    def body(x_ref, pv_ref, o_ref):
        o_ref[...] = _gate_fn(x_ref[...], pv_ref[...])

    return pl.pallas_call(
        body, name=name, grid=(1,),
        in_specs=[pl.BlockSpec((S, LANES), lambda i: (0, blk)), pl.BlockSpec((2, LANES), lambda i: (0, 0))],
        out_specs=pl.BlockSpec((S, LANES), lambda i: (0, 0)),
        out_shape=jax.ShapeDtypeStruct((S, LANES), F32),
        compiler_params=_cp("arbitrary"),
    )(p, pv)


def _dn_gate_bwd(p, pv, blk, dout, name):
    S = p.shape[0]

    def body(x_ref, pv_ref, do_ref, dx_ref, dpv_ref):
        _, vjp = jax.vjp(_gate_fn, x_ref[...], pv_ref[...])
        dx, dpv = vjp(do_ref[...])
        dx_ref[...] = dx.astype(BF16)
        dpv_ref[...] = dpv

    return pl.pallas_call(
        body, name=name, grid=(1,),
        in_specs=[pl.BlockSpec((S, LANES), lambda i: (0, blk)), pl.BlockSpec((2, LANES), lambda i: (0, 0)),
                  pl.BlockSpec((S, LANES), lambda i: (0, 0))],
        out_specs=[pl.BlockSpec((S, LANES), lambda i: (0, 0)), pl.BlockSpec((2, LANES), lambda i: (0, 0))],
        out_shape=[jax.ShapeDtypeStruct((S, LANES), BF16), jax.ShapeDtypeStruct((2, LANES), F32)],
        compiler_params=_cp("arbitrary"),
    )(p, pv, dout)


def _t(x):
    return jnp.swapaxes(x, -1, -2)


def _matmuls(prec, differentiable):
    def mm(a, b):
        return lax.dot_general(a, b, (((2,), (1,)), ((0,), (0,))), precision=prec, preferred_element_type=F32)

    def mm_nt(a, b):
        return lax.dot_general(a, b, (((2,), (2,)), ((0,), (0,))), precision=prec, preferred_element_type=F32)

    if not differentiable:
        return mm, mm_nt
    dmm, dmm_nt = jax.custom_vjp(mm), jax.custom_vjp(mm_nt)
    dmm.defvjp(lambda a, b: (mm(a, b), (a, b)), lambda res, g: (mm_nt(g, res[1]), mm(_t(res[0]), g)))
    dmm_nt.defvjp(lambda a, b: (mm_nt(a, b), (a, b)), lambda res, g: (mm(g, res[1]), mm(_t(g), res[0])))
    return dmm, dmm_nt


def _known_inverse(mm):
    f = jax.custom_vjp(lambda n, inv: inv)
    f.defvjp(lambda n, inv: (inv, inv),
             lambda inv, g: (mm(mm(_t(inv), g), _t(inv)), jnp.zeros_like(inv)))
    return f


def _delta_chunk(state, q, k, v, beta, a_col, a_row, differentiable=False, inv_known=None):
    mm, mm_nt = _matmuls(lax.Precision.HIGH, differentiable)
    mm_sum, _ = _matmuls(HI, differentiable)
    H, C, _ = q.shape
    r = lax.broadcasted_iota(jnp.int32, (H, C, C), 1)
    c = lax.broadcasted_iota(jnp.int32, (H, C, C), 2)
    tril, strict = r >= c, r > c
    eye = (r == c).astype(F32)
    g_c = mm_sum(tril.astype(F32), jnp.broadcast_to(a_col, (H, C, C)))
    g_r = mm_sum(jnp.broadcast_to(a_row, (H, C, C)), (r <= c).astype(F32))
    decay = jnp.where(tril, jnp.exp(jnp.where(tril, g_c - g_r, 0.0)), 0.0)
    eg = jnp.exp(g_c)
    g_last = jnp.sum(jnp.where(r == C - 1, g_c, 0.0), axis=1, keepdims=True)
    qs = q * (float(q.shape[2]) ** -0.5)
    kb = k * beta
    neg_m = jnp.where(strict, -(mm_nt(kb, k) * decay), 0.0)
    if inv_known is None:
        inv = eye + neg_m
        pw = neg_m
        for _ in range(int(math.log2(C)) - 1):
            pw = mm(pw, pw)
            inv = inv + mm(inv, pw)
    else:
        inv = _known_inverse(mm)(neg_m, inv_known)
    u = mm(inv, v * beta)
    w = mm(inv, kb * eg)
    intra = jnp.where(tril, mm_nt(qs, k) * decay, 0.0)
    v_new = u - mm(w, state)
    o = mm(qs * eg, state) + mm(intra, v_new)
    nxt = state * jnp.exp(g_last) + mm(_t(k * jnp.exp(g_last - g_c)), v_new)
    return o, nxt, inv


def _heads(t):
    return jnp.stack([t[:, h * DN_DH:(h + 1) * DN_DH] for h in range(DN_HEADS)])


def _delta_step(state, q, k, v, bg, a_row, differentiable=False, inv_known=None):
    lane = lax.broadcasted_iota(jnp.int32, bg.shape, 1)
    pick = lambda j: jnp.stack([jnp.sum(jnp.where(lane == j + h, bg, 0.0), axis=1, keepdims=True)
                                for h in range(DN_HEADS)])
    o, nxt, inv = _delta_chunk(state, _heads(q), _heads(k), _heads(v), pick(0), pick(DN_HEADS), a_row, differentiable,
                               inv_known)
    return jnp.concatenate([o[h] for h in range(DN_HEADS)], axis=1), nxt, inv


def _delta_fwd(qkv, bg, a_row, name):
    S = qkv.shape[0]
    nc = S // BLK

    def body(q_ref, k_ref, v_ref, bg_ref, ar_ref, o_ref, st_ref, inv_ref, state):
        ci = pl.program_id(0)

        @pl.when(ci == 0)
        def _():
            state[...] = jnp.zeros_like(state)

        st = state[...]
        st_ref[:, 0] = st
        o, nxt, inv = _delta_step(st, q_ref[...], k_ref[...], v_ref[...], bg_ref[...], ar_ref[:, pl.ds(ci, 1), :])
        o_ref[...] = o
        inv_ref[:, 0] = inv
        state[...] = nxt

    part = lambda j: pl.BlockSpec((BLK, DN_W), lambda c: (c, j))
    per_chunk = pl.BlockSpec((DN_HEADS, 1, DN_DH, DN_DH), lambda c: (0, c, 0, 0))
    mats = jax.ShapeDtypeStruct((DN_HEADS, nc, DN_DH, DN_DH), F32)
    return pl.pallas_call(
        body, name=name, grid=(nc,),
        in_specs=[part(0), part(1), part(2), pl.BlockSpec((BLK, LANES), lambda c: (c, 0)),
                  pl.BlockSpec((DN_HEADS, nc, BLK), lambda c: (0, 0, 0))],
        out_specs=[part(0), per_chunk, per_chunk], out_shape=[jax.ShapeDtypeStruct((S, DN_W), F32), mats, mats],
        scratch_shapes=[pltpu.VMEM((DN_HEADS, DN_DH, DN_DH), F32)],
        compiler_params=_cp("arbitrary"),
    )(qkv, qkv, qkv, bg, a_row)


def _delta_bwd(qkv, bg, a_row, states, invs, do, name):
    S = qkv.shape[0]
    nc = S // BLK

    def body(q_ref, k_ref, v_ref, bg_ref, ar_ref, st_ref, inv_ref, do_ref, dq_ref, dk_ref, dv_ref, dbg_ref, dar_ref,
             dstate):
        t = pl.program_id(0)
        ci = nc - 1 - t

        @pl.when(t == 0)
        def _():
            dstate[...] = jnp.zeros_like(dstate)

        step = lambda *a: _delta_step(*a, differentiable=True, inv_known=inv_ref[:, 0])[:2]
        _, vjp = jax.vjp(step, st_ref[:, 0], q_ref[...], k_ref[...], v_ref[...], bg_ref[...],
                         ar_ref[:, pl.ds(ci, 1), :])
        dprev, dq, dk, dv, dbg, dar = vjp((do_ref[...], dstate[...]))
        dq_ref[...] = dq
        dk_ref[...] = dk
        dv_ref[...] = dv
        dbg_ref[...] = dbg
        dar_ref[:, pl.ds(ci, 1), :] = dar
        dstate[...] = dprev

    part = lambda j: pl.BlockSpec((BLK, DN_W), lambda t: (nc - 1 - t, j))
    lanes = pl.BlockSpec((BLK, LANES), lambda t: (nc - 1 - t, 0))
    rows = pl.BlockSpec((DN_HEADS, nc, BLK), lambda t: (0, 0, 0))
    per_chunk = pl.BlockSpec((DN_HEADS, 1, DN_DH, DN_DH), lambda t: (0, nc - 1 - t, 0, 0))
    d3 = jax.ShapeDtypeStruct((S, DN_W), F32)
    return pl.pallas_call(
        body, name=name, grid=(nc,),
        in_specs=[part(0), part(1), part(2), lanes, rows, per_chunk, per_chunk, part(0)],
        out_specs=[part(0), part(0), part(0), lanes, rows],
        out_shape=[d3, d3, d3, jax.ShapeDtypeStruct((S, LANES), F32), jax.ShapeDtypeStruct((DN_HEADS, nc, BLK), F32)],
        scratch_shapes=[pltpu.VMEM((DN_HEADS, DN_DH, DN_DH), F32)],
        compiler_params=_cp("arbitrary"),
    )(qkv, qkv, qkv, bg, a_row, states, invs, do)


def _gate_sb(o, z):
    return o * _silu(z)


def _gate_dn(o, z, g):
    return jnp.concatenate(
        [_rms(o[:, h * DN_DH:(h + 1) * DN_DH]) * g * _silu(z[:, h * DN_DH:(h + 1) * DN_DH]) for h in range(DN_HEADS)],
        axis=1)


def _merge_specs(S, D, tm):
    row = lambda w, blk: pl.BlockSpec((tm, w), lambda i: (i, blk))
    full = lambda a, b: pl.BlockSpec((a, b), lambda i: (0, 0))
    return [row(D, 0), full(1, D), row(SB_W, 0), row(SB_W, 3), row(DN_W, 0), row(DN_W, 7),
            row(D, IN_MAIN // D), row(D, IN_MAIN // D + 1), full(1, DN_DH), full(SB_W, D), full(DN_W, D), full(D, D)]


def _merge_fwd(x, gate, o_sb, o_dn, p, ng, wbs, wbd, wo, name):
    S, D = x.shape
    tm = min(512, S)

    def body(x_ref, gate_ref, osb_ref, zsb_ref, odn_ref, zdn_ref, msb_ref, mdn_ref, ng_ref, wbs_ref, wbd_ref, wo_ref,
             out_ref):
        a = _gate_sb(osb_ref[...], zsb_ref[...])
        b = _gate_dn(odn_ref[...], zdn_ref[...], ng_ref[...])
        y = _sigmoid(msb_ref[...]) * _bdot(a, wbs_ref[...]) + _sigmoid(mdn_ref[...]) * _bdot(b, wbd_ref[...])
        out_ref[...] = x_ref[...] + gate_ref[...] * _bdot(y, wo_ref[...])

    return pl.pallas_call(
        body, name=name, grid=(S // tm,), in_specs=_merge_specs(S, D, tm),
        out_specs=pl.BlockSpec((tm, D), lambda i: (i, 0)), out_shape=jax.ShapeDtypeStruct((S, D), F32),
        compiler_params=_cp("parallel"),
    )(x, gate, o_sb, p, o_dn, p, p, p, ng, wbs, wbd, wo)


def _merge_bwd(dxn, gate, o_sb, o_dn, p, ng, wbs, wbd, wo, name):
    S, D = dxn.shape
    tm = min(256, S)

    def body(dxn_ref, gate_ref, osb_ref, zsb_ref, odn_ref, zdn_ref, msb_ref, mdn_ref, ng_ref, wbs_ref, wbd_ref, wo_ref,
             dosb_ref, dzsb_ref, dodn_ref, dzdn_ref, dmsb_ref, dmdn_ref, dwo_ref, dwbs_ref, dwbd_ref, dgate_ref, dng_ref):
        @pl.when(pl.program_id(0) == 0)
        def _():
            for ref in (dwo_ref, dwbs_ref, dwbd_ref, dgate_ref, dng_ref):
                ref[...] = jnp.zeros_like(ref)

        a, vjp_a = jax.vjp(_gate_sb, osb_ref[...], zsb_ref[...])
        b, vjp_b = jax.vjp(_gate_dn, odn_ref[...], zdn_ref[...], ng_ref[...])
        a16, b16 = a.astype(BF16), b.astype(BF16)
        ps = jnp.dot(a16, wbs_ref[...], preferred_element_type=F32)
        pd = jnp.dot(b16, wbd_ref[...], preferred_element_type=F32)
        ss, sd = _sigmoid(msb_ref[...]), _sigmoid(mdn_ref[...])
        y16 = (ss * ps + sd * pd).astype(BF16)
        out = jnp.dot(y16, wo_ref[...], preferred_element_type=F32)
        dxn_v = dxn_ref[...]
        dgate_ref[...] += jnp.sum(dxn_v * out, axis=0, keepdims=True)
        dout16 = (dxn_v * gate_ref[...]).astype(BF16)
        dwo_ref[...] += _bdot_tn(y16, dout16)
        dy = _bdot_nt(dout16, wo_ref[...])
        dmsb_ref[...] = (dy * ps * ss * (1.0 - ss)).astype(BF16)
        dmdn_ref[...] = (dy * pd * sd * (1.0 - sd)).astype(BF16)
        dps16, dpd16 = (dy * ss).astype(BF16), (dy * sd).astype(BF16)
        dwbs_ref[...] += _bdot_tn(a16, dps16)
        dwbd_ref[...] += _bdot_tn(b16, dpd16)
        dosb, dzsb = vjp_a(_bdot_nt(dps16, wbs_ref[...]))
        dodn, dzdn, dng = vjp_b(_bdot_nt(dpd16, wbd_ref[...]))
        dosb_ref[...] = dosb
        dzsb_ref[...] = dzsb.astype(BF16)
        dodn_ref[...] = dodn
        dzdn_ref[...] = dzdn.astype(BF16)
        dng_ref[...] += dng

    row = lambda w: pl.BlockSpec((tm, w), lambda i: (i, 0))
    full = lambda a, b: pl.BlockSpec((a, b), lambda i: (0, 0))
    sds = jax.ShapeDtypeStruct
    return pl.pallas_call(
        body, name=name, grid=(S // tm,), in_specs=_merge_specs(S, D, tm),
        out_specs=[row(SB_W), row(SB_W), row(DN_W), row(DN_W), row(D), row(D),
                   full(D, D), full(SB_W, D), full(DN_W, D), full(1, D), full(1, DN_DH)],
        out_shape=[sds((S, SB_W), F32), sds((S, SB_W), BF16), sds((S, DN_W), F32), sds((S, DN_W), BF16),
                   sds((S, D), BF16), sds((S, D), BF16), sds((D, D), F32), sds((SB_W, D), F32), sds((DN_W, D), F32),
                   sds((1, D), F32), sds((1, DN_DH), F32)],
        compiler_params=_cp("arbitrary"),
    )(dxn, gate, o_sb, p, o_dn, p, p, p, ng, wbs, wbd, wo)


def _loss_fwd_bwd(y, target, name):
    S, D = y.shape
    tm = min(512, S)

    def body(y_ref, t_ref, l_ref, dy_ref):
        @pl.when(pl.program_id(0) == 0)
        def _():
            l_ref[...] = jnp.zeros_like(l_ref)

        e = y_ref[...] - t_ref[...]
        l_ref[...] += jnp.sum(e * e, axis=0, keepdims=True) * (0.5 / D)
        dy_ref[...] = e * (1.0 / D)

    row = pl.BlockSpec((tm, D), lambda i: (i, 0))
    return pl.pallas_call(
        body, name=name, grid=(S // tm,), in_specs=[row, row],
        out_specs=[pl.BlockSpec((1, D), lambda i: (0, 0)), row],
        out_shape=[jax.ShapeDtypeStruct((1, D), F32), jax.ShapeDtypeStruct((S, D), F32)],
        compiler_params=_cp("arbitrary"),
    )(y, target)


def _mod_fwd(c_all, ada_w, name):
    L, D, n = ada_w.shape

    def body(c_ref, w_ref, o_ref):
        o_ref[0] = _dot(_silu(c_ref[...]), w_ref[0])

    return pl.pallas_call(
        body, name=name, grid=(L,),
        in_specs=[pl.BlockSpec(c_all.shape, lambda l: (0, 0)), pl.BlockSpec((1, D, n), lambda l: (l, 0, 0))],
        out_specs=pl.BlockSpec((1, N_DEV, n), lambda l: (l, 0, 0)),
        out_shape=jax.ShapeDtypeStruct((L, N_DEV, n), F32),
        compiler_params=_cp("parallel"),
    )(c_all, ada_w)


def _mod_bwd_w(c_all_t, dmod, name):
    L, _, n = dmod.shape
    D = c_all_t.shape[0]

    def body(c_ref, d_ref, o_ref):
        o_ref[0] = _dot(_silu(c_ref[...]), d_ref[0])

    return pl.pallas_call(
        body, name=name, grid=(L,),
        in_specs=[pl.BlockSpec(c_all_t.shape, lambda l: (0, 0)), pl.BlockSpec((1, N_DEV, n), lambda l: (l, 0, 0))],
        out_specs=pl.BlockSpec((1, D, n), lambda l: (l, 0, 0)),
        out_shape=jax.ShapeDtypeStruct((L, D, n), F32),
        compiler_params=_cp("parallel"),
    )(c_all_t, dmod)


def _me():
    return lax.axis_index("x"), lax.axis_index("y"), lax.axis_index("c")


def _peer(k):
    x, y, c = _me()
    return (1 - x if k & 4 else x, 1 - y if k & 2 else y, 1 - c if k & 1 else c)


def _lin(dev):
    return 4 * dev[0] + 2 * dev[1] + dev[2]


class _Exchange:
    def __init__(self, arrays, scatter):
        self.n = len(arrays)
        self.scatter = scatter
        self.out_shape = [jax.ShapeDtypeStruct((N_DEV,) + tuple(a.shape[1:] if scatter else a.shape), a.dtype)
                          for a in arrays]
        self.specs = [pl.BlockSpec(memory_space=pl.ANY)] * self.n
        self.scratch = [pltpu.SemaphoreType.DMA((self.n, N_DEV - 1)), pltpu.SemaphoreType.DMA((self.n, N_DEV - 1)),
                        pltpu.SemaphoreType.DMA((self.n,))]

    def _copies(self, ins, outs, sems):
        send_sems, recv_sems, local_sems = sems
        me = _lin(_me())
        local, remote, landed = [], [], []
        for t in range(self.n):
            src_of = (lambda d, t=t: ins[t].at[d]) if self.scatter else (lambda d, t=t: ins[t])
            local.append(pltpu.make_async_copy(src_of(me), outs[t].at[me], local_sems.at[t]))
            for k in range(1, N_DEV):
                peer = _peer(k)
                pair = dict(send_sem=send_sems.at[t, k - 1], recv_sem=recv_sems.at[t, k - 1], device_id=peer,
                            device_id_type=pl.DeviceIdType.MESH)
                remote.append(pltpu.make_async_remote_copy(src_ref=src_of(_lin(peer)), dst_ref=outs[t].at[me], **pair))
                slot = outs[t].at[_lin(peer)]
                landed.append(pltpu.make_async_remote_copy(src_ref=slot, dst_ref=slot, **pair))
        return local, remote, landed

    def start(self, ins, outs, sems):
        local, remote, _ = self._copies(ins, outs, sems)
        for cp in local + remote:
            cp.start()

    def finish(self, ins, outs, sems):
        local, remote, landed = self._copies(ins, outs, sems)
        for cp in landed:
            cp.wait_recv()
        for cp in remote:
            cp.wait_send()
        for cp in local:
            cp.wait()


def _exchange(arrays, scatter, name):
    ex = _Exchange(arrays, scatter)

    def body(*refs):
        ins, outs, sems = refs[:ex.n], refs[ex.n:2 * ex.n], refs[2 * ex.n:]
        ex.start(ins, outs, sems)
        ex.finish(ins, outs, sems)

    return pl.pallas_call(body, name=name, in_specs=ex.specs, out_specs=ex.specs, out_shape=ex.out_shape,
                          scratch_shapes=ex.scratch)(*arrays)


def _sum_slots(a, name):
    _, R, C = a.shape
    tr = SUM_ROWS if R % SUM_ROWS == 0 else R

    def body(a_ref, o_ref):
        acc = a_ref[0].astype(F32)
        for s in range(1, N_DEV):
            acc = acc + a_ref[s].astype(F32)
        o_ref[...] = acc

    return pl.pallas_call(
        body, name=name, grid=(R // tr,),
        in_specs=[pl.BlockSpec((N_DEV, tr, C), lambda i: (0, i, 0))], out_specs=pl.BlockSpec((tr, C), lambda i: (i, 0)),
        out_shape=jax.ShapeDtypeStruct((R, C), F32), compiler_params=_cp("parallel"),
    )(a)


def _adamw(w, g, m, v, name):
    shape = w.shape
    C = shape[-1]
    R = w.size // C
    tr = R
    for cand in (256, 128, 64):
        if R > cand and R % cand == 0:
            tr = cand
            break
    c1 = 1.0 / (1.0 - ADAM_B1 ** ADAM_STEP)
    c2 = 1.0 / (1.0 - ADAM_B2 ** ADAM_STEP)

    def body(w_ref, g_ref, m_ref, v_ref, d_ref, nm_ref, nv_ref):
        gv = g_ref[...]
        nm = ADAM_B1 * m_ref[...] + (1.0 - ADAM_B1) * gv
        nv = ADAM_B2 * v_ref[...] + (1.0 - ADAM_B2) * (gv * gv)
        d_ref[...] = -ADAM_LR * ((nm * c1) / (jnp.sqrt(nv * c2) + ADAM_EPS) + ADAM_WD * w_ref[...])
        nm_ref[...] = nm
        nv_ref[...] = nv

    spec = pl.BlockSpec((tr, C), lambda i: (i, 0))
    sd = jax.ShapeDtypeStruct((R, C), F32)
    outs = pl.pallas_call(
        body, name=name, grid=(R // tr,), in_specs=[spec] * 4, out_specs=[spec] * 3, out_shape=[sd] * 3,
        compiler_params=_cp("parallel"),
    )(*(t.reshape(R, C) for t in (w, g, m, v)))
    return tuple(t.reshape(shape) for t in outs)


def _pad_cols(w_in):
    D = w_in.shape[0]
    pad = jnp.zeros((D, LANES - 2 * DN_HEADS), w_in.dtype)
    return jnp.concatenate([w_in[:, :IN_MAIN], w_in[:, IN_COLS:], w_in[:, IN_MAIN:IN_COLS], pad], axis=1)


def _unpad_cols(dw, D):
    return jnp.concatenate([dw[:, :IN_MAIN], dw[:, IN_MAIN + 2 * D:IN_MAIN + 2 * D + 2 * DN_HEADS],
                            dw[:, IN_MAIN:IN_MAIN + 2 * D]], axis=1)


def _gate_params(a_log, dt_bias):
    z = jnp.zeros((LANES,), F32)
    return jnp.stack([z.at[DN_HEADS:2 * DN_HEADS].set(a_log), z.at[DN_HEADS:2 * DN_HEADS].set(dt_bias)])


def _layer_fwd(l, x, mod, wts, carry=None):
    S, D = x.shape
    tag = f"l{l}_"
    p, h = _inproj_fwd(x, mod, wts["norm_g"], wts["w_in"], tag + "inproj_fwd")
    (o_sb,), carried = _sb_fwd(p, wts["sb_q_g"], wts["sb_k_g"].T, tag + "sb_fwd", carry)
    qkv = _dn_prep_fwd(p, wts["conv_w"], 4 * SB_W, tag + "dn_prep_fwd")
    pv = _gate_params(wts["dn_a_log"], wts["dn_dt_bias"])
    ba_blk = (IN_MAIN + 2 * D) // LANES
    bg = _dn_gate_fwd(p, pv, ba_blk, tag + "dn_gate_fwd")
    a_row = bg[:, DN_HEADS:2 * DN_HEADS].T.reshape(DN_HEADS, S // BLK, BLK)
    o_dn, states, invs = _delta_fwd(qkv, bg, a_row, tag + "delta_fwd")
    gate = mod[:, 2 * D:]
    out = _merge_fwd(x, gate, o_sb, o_dn, p, wts["dn_norm_g"], wts["w_branch_sb"], wts["w_branch_dn"], wts["w_out"],
                     tag + "merge_fwd")
    saved = dict(x=x, mod=mod, p=p, h=h, o_sb=o_sb, qkv=qkv, pv=pv, bg=bg, a_row=a_row, o_dn=o_dn, states=states,
                 invs=invs, gate=gate)
    return out, saved, carried


def _layer_bwd(l, dxn, sv, wts, carry_of=None, late_carry_of=None):
    S, D = dxn.shape
    tag = f"l{l}_"
    (dosb, dzsb, dodn, dzdn, dmsb, dmdn, dwo, dwbs, dwbd, dgate, dng) = _merge_bwd(
        dxn, sv["gate"], sv["o_sb"], sv["o_dn"], sv["p"], wts["dn_norm_g"], wts["w_branch_sb"], wts["w_branch_dn"],
        wts["w_out"], tag + "merge_bwd")
    carry = None if carry_of is None else carry_of(dict(w_out=dwo, w_branch_sb=dwbs, w_branch_dn=dwbd))
    (dq, dk, dv, dgq, dgkt), carried = _sb_bwd(sv["p"], sv["o_sb"], dosb, wts["sb_q_g"], wts["sb_q_g"].T,
                                                wts["sb_k_g"], wts["sb_k_g"].T, tag + "sb_bwd", carry)
    dqkv_n_q, dqkv_n_k, dqkv_n_v, dbg, dar = _delta_bwd(sv["qkv"], sv["bg"], sv["a_row"], sv["states"], sv["invs"],
                                                         dodn, tag + "delta_bwd")
    dqkv, dconv = _dn_prep_bwd(sv["p"], wts["conv_w"], 4 * SB_W,
                               jnp.concatenate([dqkv_n_q, dqkv_n_k, dqkv_n_v], axis=1), tag + "dn_prep_bwd")
    dbg = dbg.at[:, DN_HEADS:2 * DN_HEADS].add(dar.reshape(DN_HEADS, S).T)
    ba_blk = (IN_MAIN + 2 * D) // LANES
    dba, dpv = _dn_gate_bwd(sv["p"], sv["pv"], ba_blk, dbg, tag + "dn_gate_bwd")
    dp = jnp.concatenate([dq, dk, dv, dzsb, dqkv, dzdn, dmsb, dmdn, dba], axis=1)
    dw_in = _unpad_cols(_matmul_tn(sv["h"].T, dp, tag + "inproj_bwd_dw"), D)
    late = None if late_carry_of is None else late_carry_of(dict(w_in=dw_in, conv_w=dconv))
    (dx, dmod, dg), carried_late = _inproj_bwd_dx(dp, wts["w_in"], sv["x"], sv["mod"], wts["norm_g"], dxn,
                                                  tag + "inproj_bwd_dx", late)
    dmod = dmod.at[:, 2 * D:].set(dgate)
    grads = dict(w_in=dw_in, w_branch_sb=dwbs, w_branch_dn=dwbd, w_out=dwo, conv_w=dconv,
                 mod=dmod[0], norm_g=dg[0], sb_q_g=jnp.sum(dgq, axis=0)[0], sb_k_g=jnp.sum(dgkt, axis=0)[:, 0],
                 dn_a_log=dpv[0, DN_HEADS:2 * DN_HEADS], dn_dt_bias=dpv[1, DN_HEADS:2 * DN_HEADS], dn_norm_g=dng[0])
    return dx, grads, carried, carried_late


def _pad_rows(a, mult):
    extra = (-a.shape[0]) % mult
    return a if extra == 0 else jnp.concatenate([a, jnp.zeros((extra,) + a.shape[1:], a.dtype)], axis=0)


def _pack_rows(parts, width, mult):
    flat = jnp.concatenate([t.reshape(-1) for t in parts])
    extra = (-flat.shape[0]) % width
    if extra:
        flat = jnp.concatenate([flat, jnp.zeros((extra,), flat.dtype)])
    return _pad_rows(flat.reshape(-1, width), mult)


def _take(flat, off, shape):
    n = math.prod(shape)
    return flat[..., off:off + n].reshape(flat.shape[:-1] + tuple(shape)), off + n


SMALL = ("mod", "norm_g", "sb_q_g", "sb_k_g", "dn_a_log", "dn_dt_bias", "dn_norm_g")


def kernel(x, c, ada_w, ada_b, norm_g, w_in, sb_q_g, sb_k_g, conv_w, dn_a_log, dn_dt_bias, dn_norm_g, w_branch_sb, w_branch_dn, w_out, loss_target, m_ada_w, m_ada_b, m_norm_g, m_w_in, m_sb_q_g, m_sb_k_g, m_conv_w, m_dn_a_log, m_dn_dt_bias, m_dn_norm_g, m_w_branch_sb, m_w_branch_dn, m_w_out, v_ada_w, v_ada_b, v_norm_g, v_w_in, v_sb_q_g, v_sb_k_g, v_conv_w, v_dn_a_log, v_dn_dt_bias, v_dn_norm_g, v_w_branch_sb, v_w_branch_dn, v_w_out):
    L, D = norm_g.shape
    S = x.shape[1]
    n_in = w_in.shape[2]
    n_ada = ada_w.shape[2]
    n_br = w_branch_sb.shape[2]
    n_out = w_out.shape[1]
    n_conv = conv_w.shape[2]
    me = _lin(_me())

    def cat(a):
        return jnp.concatenate([a[d] for d in range(N_DEV)], axis=1)

    c_all, conv_all = _exchange([c, conv_w.reshape(L * CONV_K, n_conv)], False, "gather_small")
    c_all = c_all.reshape(N_DEV, D)
    conv_full = cat(conv_all).reshape(L, CONV_K, N_DEV * n_conv)

    mod_part = _mod_fwd(c_all, ada_w, "mod_fwd")

    def shards16(l):
        return [w_in[l].astype(BF16), w_branch_sb[l].astype(BF16), w_branch_dn[l].astype(BF16), w_out[l].astype(BF16)]

    def whole(l, got):
        wi, wbs, wbd, wo = got
        return dict(norm_g=norm_g[l:l + 1], w_in=_pad_cols(cat(wi)), sb_q_g=sb_q_g[l:l + 1], sb_k_g=sb_k_g[l:l + 1],
                    conv_w=conv_full[l], dn_a_log=dn_a_log[l], dn_dt_bias=dn_dt_bias[l], dn_norm_g=dn_norm_g[l:l + 1],
                    w_branch_sb=cat(wbs), w_branch_dn=cat(wbd), w_out=wo.reshape(N_DEV * n_out, D))

    *got, mod_all = _exchange(shards16(0) + [mod_part.reshape(L * N_DEV, n_ada)], False, "gather_weights")
    mod_full = cat(mod_all).reshape(L, N_DEV, N_DEV * n_ada) + ada_b[:, None, :]
    mod_mine = lax.dynamic_slice_in_dim(mod_full, me, 1, axis=1)

    act = x[0]
    saved, wts = [], []
    for l in range(L):
        wts.append(whole(l, got))
        act, sv, got = _layer_fwd(l, act, mod_mine[l], wts[l], (shards16(l + 1), False) if l + 1 < L else None)
        saved.append(sv)
    loss_cols, dact = _loss_fwd_bwd(act, loss_target[0], "loss")
    loss = lax.psum(jnp.sum(loss_cols), ("x", "y", "c"))

    def blocks(name, g):
        if name == "w_out":
            return g.astype(BF16).reshape(N_DEV, n_out, D)
        n = g.shape[1] // N_DEV
        dtype = F32 if name == "conv_w" else BF16
        return jnp.stack([g[:, d * n:(d + 1) * n].astype(dtype) for d in range(N_DEV)])

    early, late = ("w_out", "w_branch_sb", "w_branch_dn"), ("w_in", "conv_w")
    grads, recv, pending = [None] * L, {}, []
    for l in reversed(range(L)):
        keys = [k for k, _ in pending] + [(n, l) for n in early]

        def carry_of(g_early, pending=pending):
            return [a for _, a in pending] + [blocks(n, g_early[n]) for n in early], True

        last = l == 0
        dact, grads[l], got, got_late = _layer_bwd(
            l, dact, saved[l], wts[l], carry_of, (lambda g: ([blocks(n, g[n]) for n in late], True)) if last else None)
        recv.update(zip(keys, got))
        recv.update(zip([(n, l) for n in late], got_late))
        pending = [] if last else [((n, l), blocks(n, grads[l][n])) for n in late]
    grad_x = dact[None]
    small_g = _pack_rows([grads[l][n] for l in range(L) for n in SMALL], LANES, 8)
    (small_all_g,) = _exchange([small_g], False, "gather_small_grads")
    small_sum = _sum_slots(small_all_g, "sum_small_grads").reshape(-1)
    shard_shapes = dict(w_in=w_in.shape, w_branch_sb=w_branch_sb.shape, w_branch_dn=w_branch_dn.shape,
                        conv_w=conv_w.shape, w_out=w_out.shape)
    g_out = {n: jnp.stack([_sum_slots(recv[(n, l)], f"sum_{n}_l{l}").reshape(shape[1:]) for l in range(L)])
             for n, shape in shard_shapes.items()}
    small_shapes = dict(mod=(3 * D,), norm_g=(D,), sb_q_g=(SB_DH,), sb_k_g=(SB_DH,), dn_a_log=(DN_HEADS,),
                        dn_dt_bias=(DN_HEADS,), dn_norm_g=(DN_DH,))
    off = 0
    off_all = 0
    small_each = small_all_g.reshape(N_DEV, -1)
    per_small = {n: [] for n in SMALL}
    dmod_all = []
    for l in range(L):
        for n in SMALL:
            t, off = _take(small_sum, off, small_shapes[n])
            per_small[n].append(t)
            if n == "mod":
                t_all, _ = _take(small_each, off_all, small_shapes[n])
                dmod_all.append(t_all)
            off_all += math.prod(small_shapes[n])
    for n in SMALL:
        g_out[n if n != "mod" else "ada_b"] = jnp.stack(per_small[n])
    dmod_all = jnp.stack(dmod_all)
    dmod_cols = lax.dynamic_slice_in_dim(dmod_all, me * n_ada, n_ada, axis=2)
    g_out["ada_w"] = _mod_bwd_w(c_all.T, dmod_cols, "mod_bwd_w")

    given = dict(ada_w=(ada_w, m_ada_w, v_ada_w), ada_b=(ada_b, m_ada_b, v_ada_b), norm_g=(norm_g, m_norm_g, v_norm_g),
                 w_in=(w_in, m_w_in, v_w_in), sb_q_g=(sb_q_g, m_sb_q_g, v_sb_q_g), sb_k_g=(sb_k_g, m_sb_k_g, v_sb_k_g),
                 conv_w=(conv_w, m_conv_w, v_conv_w), dn_a_log=(dn_a_log, m_dn_a_log, v_dn_a_log),
                 dn_dt_bias=(dn_dt_bias, m_dn_dt_bias, v_dn_dt_bias), dn_norm_g=(dn_norm_g, m_dn_norm_g, v_dn_norm_g),
                 w_branch_sb=(w_branch_sb, m_w_branch_sb, v_w_branch_sb),
                 w_branch_dn=(w_branch_dn, m_w_branch_dn, v_w_branch_dn), w_out=(w_out, m_w_out, v_w_out))
    order = list(given)
    upd = {n: _adamw(given[n][0], g_out[n], given[n][1], given[n][2], "adamw_" + n) for n in order}
    return (loss, grad_x, *[g_out[n] for n in order], *[upd[n][0] for n in order], *[upd[n][1] for n in order],
            *[upd[n][2] for n in order])
```

```python
import functools
import math

import jax
import jax.numpy as jnp
from jax import lax
from jax.experimental import pallas as pl
from jax.experimental.pallas import tpu as pltpu

F32 = jnp.float32
BF16 = jnp.bfloat16
HI = lax.Precision.HIGHEST

N_DEV = 8
EPS = 1e-6
SB_HEADS, SB_DH = 8, 64
DN_HEADS, DN_DH = 4, 128
SB_W = SB_HEADS * SB_DH
DN_W = DN_HEADS * DN_DH
CONV_K = 4
BLK = 128
SB_KEYS = 512
SB_QB = 512
LANES = 128
IN_MAIN = 4 * SB_W + 4 * DN_W
IN_COLS = IN_MAIN + 2 * DN_HEADS
ADAM_LR, ADAM_B1, ADAM_B2, ADAM_EPS, ADAM_WD, ADAM_STEP = 0.001, 0.9, 0.999, 1e-08, 0.01, 10
VMEM_LIMIT = 56 * 1024 * 1024
LOG2E = 1.4426950408889634
SUM_ROWS = 128


def _cp(*sem, vmem=VMEM_LIMIT):
    return pltpu.CompilerParams(dimension_semantics=sem if sem else None, vmem_limit_bytes=vmem)


def _dot(a, b, prec=HI):
    return lax.dot_general(a, b, (((1,), (0,)), ((), ())), precision=prec, preferred_element_type=F32)


def _dot_nt(a, b, prec=HI):
    return lax.dot_general(a, b, (((1,), (1,)), ((), ())), precision=prec, preferred_element_type=F32)


def _bdot(a, b):
    return lax.dot_general(a.astype(BF16), b.astype(BF16), (((1,), (0,)), ((), ())), preferred_element_type=F32)


def _bdot_nt(a, b):
    return lax.dot_general(a.astype(BF16), b.astype(BF16), (((1,), (1,)), ((), ())), preferred_element_type=F32)


def _bdot_tn(a, b):
    return lax.dot_general(a.astype(BF16), b.astype(BF16), (((0,), (0,)), ((), ())), preferred_element_type=F32)


def _split_dot(a, b01_twice):
    hi = a.astype(BF16)
    lo = (a - hi.astype(F32)).astype(BF16)
    return jnp.dot(jnp.concatenate([hi, lo], axis=1), b01_twice, preferred_element_type=F32)


def _sigmoid(x):
    return 1.0 / (1.0 + jnp.exp(-x))


def _silu(x):
    return x * _sigmoid(x)


def _softplus(x):
    return jnp.maximum(x, 0.0) + jnp.log(1.0 + jnp.exp(-jnp.abs(x)))


def _rms(x):
    return x * lax.rsqrt(jnp.mean(x * x, axis=-1, keepdims=True) + EPS)


def _prenorm(x, g, shift, scale):
    return _rms(x) * g * (1.0 + scale) + shift


def _inproj_fwd(x, mod, g, w, name):
    S, D = x.shape
    N = w.shape[1]
    tm = min(512, S)
    tn = 896 if N % 896 == 0 else 128

    def body(x_ref, mod_ref, g_ref, w_ref, p_ref, h_ref):
        @pl.when(pl.program_id(1) == 0)
        def _():
            h = _prenorm(x_ref[...], g_ref[...], mod_ref[:, 0:D], mod_ref[:, D:2 * D])
            h_ref[...] = h.astype(BF16)

        p_ref[...] = jnp.dot(h_ref[...], w_ref[...], preferred_element_type=F32)

    return pl.pallas_call(
        body, name=name, grid=(S // tm, N // tn),
        in_specs=[pl.BlockSpec((tm, D), lambda i, j: (i, 0)), pl.BlockSpec((1, 3 * D), lambda i, j: (0, 0)),
                  pl.BlockSpec((1, D), lambda i, j: (0, 0)), pl.BlockSpec((D, tn), lambda i, j: (0, j))],
        out_specs=[pl.BlockSpec((tm, tn), lambda i, j: (i, j)), pl.BlockSpec((tm, D), lambda i, j: (i, 0))],
        out_shape=[jax.ShapeDtypeStruct((S, N), F32), jax.ShapeDtypeStruct((S, D), BF16)],
        compiler_params=_cp("parallel", "arbitrary"),
    )(x, mod, g, w)


def _inproj_bwd_dx(dp, w, x, mod, g, dxn, name, carry=None):
    S, N = dp.shape
    D = x.shape[1]
    tm = min(512, S)
    tk = 896 if N % 896 == 0 else 128
    nk = N // tk

    def body(dp_ref, w_ref, x_ref, mod_ref, g_ref, dxn_ref, dx_ref, dmod_ref, dg_ref, acc):
        i, k = pl.program_id(0), pl.program_id(1)

        @pl.when(k == 0)
        def _():
            acc[...] = jnp.zeros_like(acc)

        @pl.when((i == 0) & (k == 0))
        def _():
            dmod_ref[...] = jnp.zeros_like(dmod_ref)
            dg_ref[...] = jnp.zeros_like(dg_ref)

        acc[...] += lax.dot_general(dp_ref[...], w_ref[...], (((1,), (1,)), ((), ())), preferred_element_type=F32)

        @pl.when(k == nk - 1)
        def _():
            _, vjp = jax.vjp(_prenorm, x_ref[...], g_ref[...], mod_ref[:, 0:D], mod_ref[:, D:2 * D])
            dx, dg, dshift, dscale = vjp(acc[...])
            dx_ref[...] = dxn_ref[...] + dx
            dg_ref[...] += dg
            dmod_ref[:, 0:D] += dshift
            dmod_ref[:, D:2 * D] += dscale

    return _call_carrying(
        body, name, (S // tm, nk), carry, [dp, w, x, mod, g, dxn],
        in_specs=[pl.BlockSpec((tm, tk), lambda i, k: (i, k)), pl.BlockSpec((D, tk), lambda i, k: (0, k)),
                  pl.BlockSpec((tm, D), lambda i, k: (i, 0)), pl.BlockSpec((1, 3 * D), lambda i, k: (0, 0)),
                  pl.BlockSpec((1, D), lambda i, k: (0, 0)), pl.BlockSpec((tm, D), lambda i, k: (i, 0))],
        out_specs=[pl.BlockSpec((tm, D), lambda i, k: (i, 0)), pl.BlockSpec((1, 3 * D), lambda i, k: (0, 0)),
                   pl.BlockSpec((1, D), lambda i, k: (0, 0))],
        out_shape=[jax.ShapeDtypeStruct((S, D), F32), jax.ShapeDtypeStruct((1, 3 * D), F32),
                   jax.ShapeDtypeStruct((1, D), F32)],
        scratch_shapes=[pltpu.VMEM((tm, D), F32)], vmem=VMEM_LIMIT)


def _matmul_tn(a_t, b, name):
    M, K = a_t.shape
    N = b.shape[1]
    tn = 896 if N % 896 == 0 else (512 if N % 512 == 0 else 128)
    tk = min(512, K)
    nk = K // tk

    def body(a_ref, b_ref, o_ref):
        @pl.when(pl.program_id(1) == 0)
        def _():
            o_ref[...] = jnp.zeros_like(o_ref)

        o_ref[...] += jnp.dot(a_ref[...], b_ref[...], preferred_element_type=F32)

    return pl.pallas_call(
        body, name=name, grid=(N // tn, nk),
        in_specs=[pl.BlockSpec((M, tk), lambda j, k: (0, k)), pl.BlockSpec((tk, tn), lambda j, k: (k, j))],
        out_specs=pl.BlockSpec((M, tn), lambda j, k: (0, j)),
        out_shape=jax.ShapeDtypeStruct((M, N), F32),
        compiler_params=_cp("parallel", "arbitrary"),
    )(a_t, b)


def _qk_norm(t, g, scale):
    return _rms(t) * g * scale


def _qk_norm_t(t, g_col, scale):
    return t * lax.rsqrt(jnp.mean(t * t, axis=0, keepdims=True) + EPS) * g_col * scale


def _suffix_sums(x, tri):
    half = tri.shape[1]
    lo, hi = x[:, :half], x[:, half:]
    hi_sum = jnp.sum(hi, axis=1, keepdims=True)
    y = jnp.concatenate([_split_dot(lo, tri) + hi_sum, _split_dot(hi, tri)], axis=1)
    return y, hi_sum + jnp.sum(lo, axis=1, keepdims=True)


def _sb_step(qi, kat_blk, cl, from_here, valid):
    z = jnp.dot(qi, kat_blk, preferred_element_type=F32)
    nz = -z
    lk = jnp.minimum(nz, 0.0) - jnp.log(1.0 + jnp.exp2(jnp.minimum(z, nz))) * LOG2E
    if valid is not None:
        lk = jnp.where(valid, lk, 0.0)
    later, tot = _suffix_sums(lk, from_here)
    w = jnp.exp2(z + later + cl)
    if valid is not None:
        w = jnp.where(valid, w, 0.0)
    return z, lk, w, tot


def _sb_masks(kb):
    half = kb // 2
    r = lax.broadcasted_iota(jnp.int32, (half, half), 0)
    c = lax.broadcasted_iota(jnp.int32, (half, half), 1)
    rq = lax.broadcasted_iota(jnp.int32, (SB_QB, kb), 0)
    ck = lax.broadcasted_iota(jnp.int32, (SB_QB, kb), 1)
    twice = lambda m: jnp.concatenate([m, m], axis=0).astype(BF16)
    return twice((r >= c).astype(F32)), ck - rq


def _sb_fwd(p, gq, gkt, name, carry=None):
    S, dh = p.shape[0], SB_DH
    kb = min(SB_KEYS, S)
    per = kb // SB_QB
    nb = S // SB_QB
    scale = 1.0 / math.sqrt(dh)
    pairs = SB_W // LANES

    def body(q_ref, k_ref, v_ref, gq_ref, gkt_ref, o_ref, kt2, qa, kat, vb):
        kt2[...] = k_ref[...].T
        from_here, diff = _sb_masks(kb)
        for hh in range(LANES // dh):
            lanes = slice(hh * dh, (hh + 1) * dh)
            qa[...] = _qk_norm(q_ref[:, lanes], gq_ref[...], scale * LOG2E).astype(BF16)
            kat[...] = _qk_norm_t(kt2[lanes, :], gkt_ref[...], 1.0).astype(BF16)
            vb[...] = v_ref[:, lanes].astype(BF16)

            def qblock(i, _):
                rows = pl.ds(pl.multiple_of(i * SB_QB, SB_QB), SB_QB)
                qi = qa[rows, :]
                sbd = i // per

                def step(sb, cl, acc, valid):
                    cols = pl.ds(pl.multiple_of(sb * kb, kb), kb)
                    _, _, w, tot = _sb_step(qi, kat[:, cols], cl, from_here, valid)
                    return cl + tot, acc + jnp.dot(w.astype(BF16), vb[cols, :], preferred_element_type=F32)

                cl, acc = step(sbd, jnp.zeros((SB_QB, 1), F32), jnp.zeros((SB_QB, dh), F32),
                               diff < (i - sbd * per) * SB_QB)
                _, acc = lax.fori_loop(0, sbd, lambda jj, c: step(sbd - 1 - jj, c[0], c[1], None), (cl, acc))
                o_ref[rows, lanes] = acc
                return 0

            lax.fori_loop(0, nb, qblock, 0)

    blk = lambda off: pl.BlockSpec((S, LANES), lambda g: (0, off + g))
    return _call_carrying(
        body, name, pairs, carry, [p, p, p, gq, gkt],
        in_specs=[blk(0), blk(pairs), blk(2 * pairs), pl.BlockSpec((1, dh), lambda g: (0, 0)),
                  pl.BlockSpec((dh, 1), lambda g: (0, 0))],
        out_specs=[blk(0)], out_shape=[jax.ShapeDtypeStruct((S, SB_W), F32)],
        scratch_shapes=[pltpu.VMEM((LANES, S), F32), pltpu.VMEM((S, dh), BF16), pltpu.VMEM((dh, S), BF16),
                        pltpu.VMEM((S, dh), BF16)],
        vmem=VMEM_LIMIT)


def _call_carrying(body, name, grid, carry, operands, in_specs, out_specs, out_shape, scratch_shapes, vmem):
    grid = (grid,) if isinstance(grid, int) else tuple(grid)
    if carry is None:
        res = pl.pallas_call(body, name=name, grid=grid, in_specs=in_specs, out_specs=out_specs,
                             out_shape=out_shape, scratch_shapes=scratch_shapes,
                             compiler_params=_cp(*["arbitrary"] * len(grid), vmem=vmem))(*operands)
        return res, []

    def at(corner):
        hit = pl.program_id(0) == corner(grid[0])
        for axis in range(1, len(grid)):
            hit = jnp.logical_and(hit, pl.program_id(axis) == corner(grid[axis]))
        return hit

    ex = _Exchange(*carry)
    n_in, n_out, n_scr = len(in_specs), len(out_specs), len(scratch_shapes)

    def wrapped(*refs):
        ins, refs = refs[:n_in], refs[n_in:]
        xin, refs = refs[:ex.n], refs[ex.n:]
        outs, refs = refs[:n_out], refs[n_out:]
        xout, refs = refs[:ex.n], refs[ex.n:]
        scr, sems = refs[:n_scr], refs[n_scr:]

        @pl.when(at(lambda n: 0))
        def _():
            ex.start(xin, xout, sems)

        body(*ins, *outs, *scr)

        @pl.when(at(lambda n: n - 1))
        def _():
            ex.finish(xin, xout, sems)

    res = pl.pallas_call(wrapped, name=name, grid=grid, in_specs=in_specs + ex.specs,
                         out_specs=out_specs + ex.specs, out_shape=out_shape + ex.out_shape,
                         scratch_shapes=scratch_shapes + ex.scratch,
                         compiler_params=_cp(*["arbitrary"] * len(grid), vmem=vmem))(*operands, *carry[0])
    return res[:n_out], res[n_out:]


def _sb_bwd(p, o, do, gq, gqt, gk, gkt, name, carry=None):
    S, dh = p.shape[0], SB_DH
    kb = min(SB_KEYS, S)
    per = kb // SB_QB
    nb = S // SB_QB
    scale = 1.0 / math.sqrt(dh)
    pairs = SB_W // LANES
    per_pair = LANES // dh

    def body(q_ref, k_ref, v_ref, o_ref, do_ref, gq_ref, gqt_ref, gk_ref, gkt_ref,
             dq_ref, dk_ref, dv_ref, dgq_ref, dgkt_ref,
             qt2, kt2, vt2, dot2, dkt2, dvt2, qa, qat, ka, kat, vb, vtb, dob, dotb, dqa):
        qt2[...] = q_ref[...].T
        kt2[...] = k_ref[...].T
        vt2[...] = v_ref[...].T
        dot2[...] = do_ref[...].T
        dkt2[...] = jnp.zeros_like(dkt2)
        dvt2[...] = jnp.zeros_like(dvt2)
        from_here, diff = _sb_masks(kb)
        for hh in range(per_pair):
            lanes = slice(hh * dh, (hh + 1) * dh)
            qa[...] = _qk_norm(q_ref[:, lanes], gq_ref[...], scale * LOG2E).astype(BF16)
            qat[...] = _qk_norm_t(qt2[lanes, :], gqt_ref[...], scale * LOG2E).astype(BF16)
            ka[...] = _qk_norm(k_ref[:, lanes], gk_ref[...], 1.0).astype(BF16)
            kat[...] = _qk_norm_t(kt2[lanes, :], gkt_ref[...], 1.0).astype(BF16)
            vb[...] = v_ref[:, lanes].astype(BF16)
            vtb[...] = vt2[lanes, :].astype(BF16)
            dob[...] = do_ref[:, lanes].astype(BF16)
            dotb[...] = dot2[lanes, :].astype(BF16)

            def qblock(i, _):
                rows = pl.ds(pl.multiple_of(i * SB_QB, SB_QB), SB_QB)
                qi, qit = qa[rows, :], qat[:, rows]
                doi, doit = dob[rows, :], dotb[:, rows]
                total = jnp.sum(doi.astype(F32) * o_ref[rows, lanes], axis=1, keepdims=True)
                sbd = i // per

                def step(sb, cl, cd, dqi, valid):
                    cols = pl.ds(pl.multiple_of(sb * kb, kb), kb)
                    z, lk, w, tot = _sb_step(qi, kat[:, cols], cl, from_here, valid)
                    w16 = w.astype(BF16)
                    dl = jnp.dot(doi, vtb[:, cols], preferred_element_type=F32) * w16.astype(F32)
                    incl, dtot = _suffix_sums(dl, from_here)
                    sig = jnp.exp2(z + lk)
                    dz = dl - sig * (dl + (total - cd - incl))
                    if valid is not None:
                        dz = jnp.where(valid, dz, 0.0)
                    dz16 = dz.astype(BF16)
                    dqi = dqi + jnp.dot(dz16, ka[cols, :], preferred_element_type=F32)
                    dkt2[lanes, cols] += jnp.dot(qit, dz16, preferred_element_type=F32)
                    dvt2[lanes, cols] += jnp.dot(doit, w16, preferred_element_type=F32)
                    return cl + tot, cd + dtot, dqi

                zero = jnp.zeros((SB_QB, 1), F32)
                first = step(sbd, zero, zero, jnp.zeros((SB_QB, dh), F32), diff < (i - sbd * per) * SB_QB)
                _, _, dqi = lax.fori_loop(0, sbd, lambda jj, c: step(sbd - 1 - jj, c[0], c[1], c[2], None), first)
                dqa[rows, :] = dqi
                return 0

            lax.fori_loop(0, nb, qblock, 0)
            _, vq = jax.vjp(lambda t, g: _qk_norm(t, g, scale), q_ref[:, lanes], gq_ref[...])
            dq, dgq = vq(dqa[...])
            dq_ref[:, lanes] = dq.astype(BF16)
            dgq_ref[hh] = dgq
            _, vk = jax.vjp(lambda t, g: _qk_norm_t(t, g, 1.0), kt2[lanes, :], gkt_ref[...])
            dkt, dgkt = vk(dkt2[lanes, :] * (1.0 / LOG2E))
            dkt2[lanes, :] = dkt
            dgkt_ref[hh] = dgkt
        dk_ref[...] = dkt2[...].T.astype(BF16)
        dv_ref[...] = dvt2[...].T.astype(BF16)

    blk = lambda off: pl.BlockSpec((S, LANES), lambda g: (0, off + g))
    once = lambda off: pl.BlockSpec((S, LANES), lambda g: (0, off + g), pipeline_mode=pl.Buffered(1))
    gr = pl.BlockSpec((1, dh), lambda g: (0, 0))
    gc = pl.BlockSpec((dh, 1), lambda g: (0, 0))
    sd = jax.ShapeDtypeStruct((S, SB_W), BF16)
    return _call_carrying(
        body, name, pairs, carry, [p, p, p, o, do, gq, gqt, gk, gkt],
        in_specs=[once(0), once(pairs), once(2 * pairs), once(0), once(0), gr, gc, gr, gc],
        out_specs=[blk(0), blk(0), blk(0), pl.BlockSpec((per_pair, 1, dh), lambda g: (g, 0, 0)),
                   pl.BlockSpec((per_pair, dh, 1), lambda g: (g, 0, 0))],
        out_shape=[sd, sd, sd, jax.ShapeDtypeStruct((SB_HEADS, 1, dh), F32),
                   jax.ShapeDtypeStruct((SB_HEADS, dh, 1), F32)],
        scratch_shapes=[pltpu.VMEM((LANES, S), F32)] * 6 + [pltpu.VMEM((S, dh), BF16), pltpu.VMEM((dh, S), BF16)] * 4
        + [pltpu.VMEM((S, dh), F32)],
        vmem=60 * 1024 * 1024)


def _shift_down(x, s, rows):
    if s == 0:
        return x
    return jnp.where(rows >= s, pltpu.roll(x, s, 0), 0.0)


def _shift_up(x, s, rows, n):
    if s == 0:
        return x
    return jnp.where(rows < n - s, pltpu.roll(x, n - s, 0), 0.0)


def _conv(x, w_ref, rows):
    y = x * w_ref[CONV_K - 1:CONV_K, :]
    for kk in range(CONV_K - 1):
        y = y + _shift_down(x, CONV_K - 1 - kk, rows) * w_ref[kk:kk + 1, :]
    return y


def _act_norm(y, normed):
    s = _silu(y)
    n = s * lax.rsqrt(jnp.sum(s * s, axis=-1, keepdims=True) + EPS)
    return jnp.where(normed, n, s)


def _dn_prep_fwd(p, conv_w, col0, name):
    S = p.shape[0]
    nblk = 3 * DN_HEADS
    b0 = col0 // DN_DH

    def body(x_ref, w_ref, o_ref):
        rows = lax.broadcasted_iota(jnp.int32, (S, DN_DH), 0)
        y = _conv(x_ref[...], w_ref, rows)
        o_ref[...] = _act_norm(y, pl.program_id(0) < 2 * DN_HEADS)

    return pl.pallas_call(
        body, name=name, grid=(nblk,),
        in_specs=[pl.BlockSpec((S, DN_DH), lambda j: (0, b0 + j)), pl.BlockSpec((CONV_K, DN_DH), lambda j: (0, j))],
        out_specs=pl.BlockSpec((S, DN_DH), lambda j: (0, j)),
        out_shape=jax.ShapeDtypeStruct((S, 3 * DN_W), F32),
        compiler_params=_cp("parallel"),
    )(p, conv_w)


def _dn_prep_bwd(p, conv_w, col0, dout, name):
    S = p.shape[0]
    nblk = 3 * DN_HEADS
    b0 = col0 // DN_DH

    def body(x_ref, w_ref, do_ref, dx_ref, dw_ref):
        rows = lax.broadcasted_iota(jnp.int32, (S, DN_DH), 0)
        x = x_ref[...]
        y = _conv(x, w_ref, rows)
        normed = pl.program_id(0) < 2 * DN_HEADS
        _, vjp = jax.vjp(lambda t: _act_norm(t, normed), y)
        (dy,) = vjp(do_ref[...])
        dx = dy * w_ref[CONV_K - 1:CONV_K, :]
        dw_ref[CONV_K - 1:CONV_K, :] = jnp.sum(dy * x, axis=0, keepdims=True)
        for kk in range(CONV_K - 1):
            s = CONV_K - 1 - kk
            dx = dx + _shift_up(dy, s, rows, S) * w_ref[kk:kk + 1, :]
            dw_ref[kk:kk + 1, :] = jnp.sum(dy * _shift_down(x, s, rows), axis=0, keepdims=True)
        dx_ref[...] = dx.astype(BF16)

    return pl.pallas_call(
        body, name=name, grid=(nblk,),
        in_specs=[pl.BlockSpec((S, DN_DH), lambda j: (0, b0 + j)), pl.BlockSpec((CONV_K, DN_DH), lambda j: (0, j)),
                  pl.BlockSpec((S, DN_DH), lambda j: (0, j))],
        out_specs=[pl.BlockSpec((S, DN_DH), lambda j: (0, j)), pl.BlockSpec((CONV_K, DN_DH), lambda j: (0, j))],
        out_shape=[jax.ShapeDtypeStruct((S, 3 * DN_W), BF16), jax.ShapeDtypeStruct((CONV_K, 3 * DN_W), F32)],
        compiler_params=_cp("parallel"),
    )(p, conv_w, dout)


def _gate_fn(x, pv):
    lane = lax.broadcasted_iota(jnp.int32, x.shape, 1)
    decay = -jnp.exp(pv[0:1, :]) * _softplus(x + pv[1:2, :])
    return jnp.where(lane < DN_HEADS, _sigmoid(x), decay)


def _dn_gate_fwd(p, pv, blk, name):
    S = p.shape[0]

    def body(x_ref, pv_ref, o_ref):
        o_ref[...] = _gate_fn(x_ref[...], pv_ref[...])

    return pl.pallas_call(
        body, name=name, grid=(1,),
        in_specs=[pl.BlockSpec((S, LANES), lambda i: (0, blk)), pl.BlockSpec((2, LANES), lambda i: (0, 0))],
        out_specs=pl.BlockSpec((S, LANES), lambda i: (0, 0)),
        out_shape=jax.ShapeDtypeStruct((S, LANES), F32),
        compiler_params=_cp("arbitrary"),
    )(p, pv)


def _dn_gate_bwd(p, pv, blk, dout, name):
    S = p.shape[0]

    def body(x_ref, pv_ref, do_ref, dx_ref, dpv_ref):
        _, vjp = jax.vjp(_gate_fn, x_ref[...], pv_ref[...])
        dx, dpv = vjp(do_ref[...])
        dx_ref[...] = dx.astype(BF16)
        dpv_ref[...] = dpv

    return pl.pallas_call(
        body, name=name, grid=(1,),
        in_specs=[pl.BlockSpec((S, LANES), lambda i: (0, blk)), pl.BlockSpec((2, LANES), lambda i: (0, 0)),
                  pl.BlockSpec((S, LANES), lambda i: (0, 0))],
        out_specs=[pl.BlockSpec((S, LANES), lambda i: (0, 0)), pl.BlockSpec((2, LANES), lambda i: (0, 0))],
        out_shape=[jax.ShapeDtypeStruct((S, LANES), BF16), jax.ShapeDtypeStruct((2, LANES), F32)],
        compiler_params=_cp("arbitrary"),
    )(p, pv, dout)


def _t(x):
    return jnp.swapaxes(x, -1, -2)


def _matmuls(prec, differentiable):
    def mm(a, b):
        return lax.dot_general(a, b, (((2,), (1,)), ((0,), (0,))), precision=prec, preferred_element_type=F32)

    def mm_nt(a, b):
        return lax.dot_general(a, b, (((2,), (2,)), ((0,), (0,))), precision=prec, preferred_element_type=F32)

    if not differentiable:
        return mm, mm_nt
    dmm, dmm_nt = jax.custom_vjp(mm), jax.custom_vjp(mm_nt)
    dmm.defvjp(lambda a, b: (mm(a, b), (a, b)), lambda res, g: (mm_nt(g, res[1]), mm(_t(res[0]), g)))
    dmm_nt.defvjp(lambda a, b: (mm_nt(a, b), (a, b)), lambda res, g: (mm(g, res[1]), mm(_t(g), res[0])))
    return dmm, dmm_nt


def _known_inverse(mm):
    f = jax.custom_vjp(lambda n, inv: inv)
    f.defvjp(lambda n, inv: (inv, inv),
             lambda inv, g: (mm(mm(_t(inv), g), _t(inv)), jnp.zeros_like(inv)))
    return f


def _delta_chunk(state, q, k, v, beta, a_col, a_row, differentiable=False, inv_known=None):
    mm, mm_nt = _matmuls(lax.Precision.HIGH, differentiable)
    mm_sum, _ = _matmuls(HI, differentiable)
    H, C, _ = q.shape
    r = lax.broadcasted_iota(jnp.int32, (H, C, C), 1)
    c = lax.broadcasted_iota(jnp.int32, (H, C, C), 2)
    tril, strict = r >= c, r > c
    eye = (r == c).astype(F32)
    g_c = mm_sum(tril.astype(F32), jnp.broadcast_to(a_col, (H, C, C)))
    g_r = mm_sum(jnp.broadcast_to(a_row, (H, C, C)), (r <= c).astype(F32))
    decay = jnp.where(tril, jnp.exp(jnp.where(tril, g_c - g_r, 0.0)), 0.0)
    eg = jnp.exp(g_c)
    g_last = jnp.sum(jnp.where(r == C - 1, g_c, 0.0), axis=1, keepdims=True)
    qs = q * (float(q.shape[2]) ** -0.5)
    kb = k * beta
    neg_m = jnp.where(strict, -(mm_nt(kb, k) * decay), 0.0)
    if inv_known is None:
        inv = eye + neg_m
        pw = neg_m
        for _ in range(int(math.log2(C)) - 1):
            pw = mm(pw, pw)
            inv = inv + mm(inv, pw)
    else:
        inv = _known_inverse(mm)(neg_m, inv_known)
    u = mm(inv, v * beta)
    w = mm(inv, kb * eg)
    intra = jnp.where(tril, mm_nt(qs, k) * decay, 0.0)
    v_new = u - mm(w, state)
    o = mm(qs * eg, state) + mm(intra, v_new)
    nxt = state * jnp.exp(g_last) + mm(_t(k * jnp.exp(g_last - g_c)), v_new)
    return o, nxt, inv


def _heads(t):
    return jnp.stack([t[:, h * DN_DH:(h + 1) * DN_DH] for h in range(DN_HEADS)])


def _delta_step(state, q, k, v, bg, a_row, differentiable=False, inv_known=None):
    lane = lax.broadcasted_iota(jnp.int32, bg.shape, 1)
    pick = lambda j: jnp.stack([jnp.sum(jnp.where(lane == j + h, bg, 0.0), axis=1, keepdims=True)
                                for h in range(DN_HEADS)])
    o, nxt, inv = _delta_chunk(state, _heads(q), _heads(k), _heads(v), pick(0), pick(DN_HEADS), a_row, differentiable,
                               inv_known)
    return jnp.concatenate([o[h] for h in range(DN_HEADS)], axis=1), nxt, inv


def _delta_fwd(qkv, bg, a_row, name):
    S = qkv.shape[0]
    nc = S // BLK

    def body(q_ref, k_ref, v_ref, bg_ref, ar_ref, o_ref, st_ref, inv_ref, state):
        ci = pl.program_id(0)

        @pl.when(ci == 0)
        def _():
            state[...] = jnp.zeros_like(state)

        st = state[...]
        st_ref[:, 0] = st
        o, nxt, inv = _delta_step(st, q_ref[...], k_ref[...], v_ref[...], bg_ref[...], ar_ref[:, pl.ds(ci, 1), :])
        o_ref[...] = o
        inv_ref[:, 0] = inv
        state[...] = nxt

    part = lambda j: pl.BlockSpec((BLK, DN_W), lambda c: (c, j))
    per_chunk = pl.BlockSpec((DN_HEADS, 1, DN_DH, DN_DH), lambda c: (0, c, 0, 0))
    mats = jax.ShapeDtypeStruct((DN_HEADS, nc, DN_DH, DN_DH), F32)
    return pl.pallas_call(
        body, name=name, grid=(nc,),
        in_specs=[part(0), part(1), part(2), pl.BlockSpec((BLK, LANES), lambda c: (c, 0)),
                  pl.BlockSpec((DN_HEADS, nc, BLK), lambda c: (0, 0, 0))],
        out_specs=[part(0), per_chunk, per_chunk], out_shape=[jax.ShapeDtypeStruct((S, DN_W), F32), mats, mats],
        scratch_shapes=[pltpu.VMEM((DN_HEADS, DN_DH, DN_DH), F32)],
        compiler_params=_cp("arbitrary"),
    )(qkv, qkv, qkv, bg, a_row)


def _delta_bwd(qkv, bg, a_row, states, invs, do, name):
    S = qkv.shape[0]
    nc = S // BLK

    def body(q_ref, k_ref, v_ref, bg_ref, ar_ref, st_ref, inv_ref, do_ref, dqkv_ref, dbg_ref, dar_ref, dstate):
        t = pl.program_id(0)
        ci = nc - 1 - t

        @pl.when(t == 0)
        def _():
            dstate[...] = jnp.zeros_like(dstate)

        step = lambda *a: _delta_step(*a, differentiable=True, inv_known=inv_ref[:, 0])[:2]
        _, vjp = jax.vjp(step, st_ref[:, 0], q_ref[...], k_ref[...], v_ref[...], bg_ref[...],
                         ar_ref[:, pl.ds(ci, 1), :])
        dprev, dq, dk, dv, dbg, dar = vjp((do_ref[...], dstate[...]))
        dqkv_ref[:, 0:DN_W] = dq
        dqkv_ref[:, DN_W:2 * DN_W] = dk
        dqkv_ref[:, 2 * DN_W:3 * DN_W] = dv
        dbg_ref[...] = dbg
        dar_ref[:, pl.ds(ci, 1), :] = dar
        dstate[...] = dprev

    part = lambda j: pl.BlockSpec((BLK, DN_W), lambda t: (nc - 1 - t, j))
    lanes = pl.BlockSpec((BLK, LANES), lambda t: (nc - 1 - t, 0))
    rows = pl.BlockSpec((DN_HEADS, nc, BLK), lambda t: (0, 0, 0))
    per_chunk = pl.BlockSpec((DN_HEADS, 1, DN_DH, DN_DH), lambda t: (0, nc - 1 - t, 0, 0))
    return pl.pallas_call(
        body, name=name, grid=(nc,),
        in_specs=[part(0), part(1), part(2), lanes, rows, per_chunk, per_chunk, part(0)],
        out_specs=[pl.BlockSpec((BLK, 3 * DN_W), lambda t: (nc - 1 - t, 0)), lanes, rows],
        out_shape=[jax.ShapeDtypeStruct((S, 3 * DN_W), F32), jax.ShapeDtypeStruct((S, LANES), F32),
                   jax.ShapeDtypeStruct((DN_HEADS, nc, BLK), F32)],
        scratch_shapes=[pltpu.VMEM((DN_HEADS, DN_DH, DN_DH), F32)],
        compiler_params=_cp("arbitrary"),
    )(qkv, qkv, qkv, bg, a_row, states, invs, do)


def _gate_sb(o, z):
    return o * _silu(z)


def _gate_dn(o, z, g):
    return jnp.concatenate(
        [_rms(o[:, h * DN_DH:(h + 1) * DN_DH]) * g * _silu(z[:, h * DN_DH:(h + 1) * DN_DH]) for h in range(DN_HEADS)],
        axis=1)


def _merge_specs(S, D, tm):
    row = lambda w, blk: pl.BlockSpec((tm, w), lambda i: (i, blk))
    full = lambda a, b: pl.BlockSpec((a, b), lambda i: (0, 0))
    return [row(D, 0), full(1, D), row(SB_W, 0), row(SB_W, 3), row(DN_W, 0), row(DN_W, 7),
            row(D, IN_MAIN // D), row(D, IN_MAIN // D + 1), full(1, DN_DH), full(SB_W, D), full(DN_W, D), full(D, D)]


def _merge_fwd(x, gate, o_sb, o_dn, p, ng, wbs, wbd, wo, name):
    S, D = x.shape
    tm = min(512, S)

    def body(x_ref, gate_ref, osb_ref, zsb_ref, odn_ref, zdn_ref, msb_ref, mdn_ref, ng_ref, wbs_ref, wbd_ref, wo_ref,
             out_ref):
        a = _gate_sb(osb_ref[...], zsb_ref[...])
        b = _gate_dn(odn_ref[...], zdn_ref[...], ng_ref[...])
        y = _sigmoid(msb_ref[...]) * _bdot(a, wbs_ref[...]) + _sigmoid(mdn_ref[...]) * _bdot(b, wbd_ref[...])
        out_ref[...] = x_ref[...] + gate_ref[...] * _bdot(y, wo_ref[...])

    return pl.pallas_call(
        body, name=name, grid=(S // tm,), in_specs=_merge_specs(S, D, tm),
        out_specs=pl.BlockSpec((tm, D), lambda i: (i, 0)), out_shape=jax.ShapeDtypeStruct((S, D), F32),
        compiler_params=_cp("parallel"),
    )(x, gate, o_sb, p, o_dn, p, p, p, ng, wbs, wbd, wo)


def _merge_bwd(dxn, gate, o_sb, o_dn, p, ng, wbs, wbd, wo, name):
    S, D = dxn.shape
    tm = min(256, S)

    def body(dxn_ref, gate_ref, osb_ref, zsb_ref, odn_ref, zdn_ref, msb_ref, mdn_ref, ng_ref, wbs_ref, wbd_ref, wo_ref,
             dosb_ref, dzsb_ref, dodn_ref, dzdn_ref, dmsb_ref, dmdn_ref, dwo_ref, dwbs_ref, dwbd_ref, dgate_ref, dng_ref):
        @pl.when(pl.program_id(0) == 0)
        def _():
            for ref in (dwo_ref, dwbs_ref, dwbd_ref, dgate_ref, dng_ref):
                ref[...] = jnp.zeros_like(ref)

        a, vjp_a = jax.vjp(_gate_sb, osb_ref[...], zsb_ref[...])
        b, vjp_b = jax.vjp(_gate_dn, odn_ref[...], zdn_ref[...], ng_ref[...])
        a16, b16 = a.astype(BF16), b.astype(BF16)
        ps = jnp.dot(a16, wbs_ref[...], preferred_element_type=F32)
        pd = jnp.dot(b16, wbd_ref[...], preferred_element_type=F32)
        ss, sd = _sigmoid(msb_ref[...]), _sigmoid(mdn_ref[...])
        y16 = (ss * ps + sd * pd).astype(BF16)
        out = jnp.dot(y16, wo_ref[...], preferred_element_type=F32)
        dxn_v = dxn_ref[...]
        dgate_ref[...] += jnp.sum(dxn_v * out, axis=0, keepdims=True)
        dout16 = (dxn_v * gate_ref[...]).astype(BF16)
        dwo_ref[...] += _bdot_tn(y16, dout16)
        dy = _bdot_nt(dout16, wo_ref[...])
        dmsb_ref[...] = (dy * ps * ss * (1.0 - ss)).astype(BF16)
        dmdn_ref[...] = (dy * pd * sd * (1.0 - sd)).astype(BF16)
        dps16, dpd16 = (dy * ss).astype(BF16), (dy * sd).astype(BF16)
        dwbs_ref[...] += _bdot_tn(a16, dps16)
        dwbd_ref[...] += _bdot_tn(b16, dpd16)
        dosb, dzsb = vjp_a(_bdot_nt(dps16, wbs_ref[...]))
        dodn, dzdn, dng = vjp_b(_bdot_nt(dpd16, wbd_ref[...]))
        dosb_ref[...] = dosb
        dzsb_ref[...] = dzsb.astype(BF16)
        dodn_ref[...] = dodn
        dzdn_ref[...] = dzdn.astype(BF16)
        dng_ref[...] += dng

    row = lambda w: pl.BlockSpec((tm, w), lambda i: (i, 0))
    full = lambda a, b: pl.BlockSpec((a, b), lambda i: (0, 0))
    sds = jax.ShapeDtypeStruct
    return pl.pallas_call(
        body, name=name, grid=(S // tm,), in_specs=_merge_specs(S, D, tm),
        out_specs=[row(SB_W), row(SB_W), row(DN_W), row(DN_W), row(D), row(D),
                   full(D, D), full(SB_W, D), full(DN_W, D), full(1, D), full(1, DN_DH)],
        out_shape=[sds((S, SB_W), F32), sds((S, SB_W), BF16), sds((S, DN_W), F32), sds((S, DN_W), BF16),
                   sds((S, D), BF16), sds((S, D), BF16), sds((D, D), F32), sds((SB_W, D), F32), sds((DN_W, D), F32),
                   sds((1, D), F32), sds((1, DN_DH), F32)],
        compiler_params=_cp("arbitrary"),
    )(dxn, gate, o_sb, p, o_dn, p, p, p, ng, wbs, wbd, wo)


def _loss_fwd_bwd(y, target, name):
    S, D = y.shape
    tm = min(512, S)

    def body(y_ref, t_ref, l_ref, dy_ref):
        @pl.when(pl.program_id(0) == 0)
        def _():
            l_ref[...] = jnp.zeros_like(l_ref)

        e = y_ref[...] - t_ref[...]
        l_ref[...] += jnp.sum(e * e, axis=0, keepdims=True) * (0.5 / D)
        dy_ref[...] = e * (1.0 / D)

    row = pl.BlockSpec((tm, D), lambda i: (i, 0))
    return pl.pallas_call(
        body, name=name, grid=(S // tm,), in_specs=[row, row],
        out_specs=[pl.BlockSpec((1, D), lambda i: (0, 0)), row],
        out_shape=[jax.ShapeDtypeStruct((1, D), F32), jax.ShapeDtypeStruct((S, D), F32)],
        compiler_params=_cp("arbitrary"),
    )(y, target)


def _mod_fwd(c_all, ada_w, name):
    L, D, n = ada_w.shape

    def body(c_ref, w_ref, o_ref):
        o_ref[0] = _dot(_silu(c_ref[...]), w_ref[0])

    return pl.pallas_call(
        body, name=name, grid=(L,),
        in_specs=[pl.BlockSpec(c_all.shape, lambda l: (0, 0)), pl.BlockSpec((1, D, n), lambda l: (l, 0, 0))],
        out_specs=pl.BlockSpec((1, N_DEV, n), lambda l: (l, 0, 0)),
        out_shape=jax.ShapeDtypeStruct((L, N_DEV, n), F32),
        compiler_params=_cp("parallel"),
    )(c_all, ada_w)


def _mod_bwd_w(c_all_t, dmod, name):
    L, _, n = dmod.shape
    D = c_all_t.shape[0]

    def body(c_ref, d_ref, o_ref):
        o_ref[0] = _dot(_silu(c_ref[...]), d_ref[0])

    return pl.pallas_call(
        body, name=name, grid=(L,),
        in_specs=[pl.BlockSpec(c_all_t.shape, lambda l: (0, 0)), pl.BlockSpec((1, N_DEV, n), lambda l: (l, 0, 0))],
        out_specs=pl.BlockSpec((1, D, n), lambda l: (l, 0, 0)),
        out_shape=jax.ShapeDtypeStruct((L, D, n), F32),
        compiler_params=_cp("parallel"),
    )(c_all_t, dmod)


def _me():
    return lax.axis_index("x"), lax.axis_index("y"), lax.axis_index("c")


def _peer(k):
    x, y, c = _me()
    return (1 - x if k & 4 else x, 1 - y if k & 2 else y, 1 - c if k & 1 else c)


def _lin(dev):
    return 4 * dev[0] + 2 * dev[1] + dev[2]


class _Exchange:
    def __init__(self, arrays, scatter):
        self.n = len(arrays)
        self.scatter = scatter
        self.out_shape = [jax.ShapeDtypeStruct((N_DEV,) + tuple(a.shape[1:] if scatter else a.shape), a.dtype)
                          for a in arrays]
        self.specs = [pl.BlockSpec(memory_space=pl.ANY)] * self.n
        self.scratch = [pltpu.SemaphoreType.DMA((self.n, N_DEV - 1)), pltpu.SemaphoreType.DMA((self.n, N_DEV - 1)),
                        pltpu.SemaphoreType.DMA((self.n,))]

    def _copies(self, ins, outs, sems):
        send_sems, recv_sems, local_sems = sems
        me = _lin(_me())
        local, direct, passed, landed = [], [], [], []
        for t in range(self.n):
            src_of = (lambda d, t=t: ins[t].at[d]) if self.scatter else (lambda d, t=t: ins[t])
            local.append(pltpu.make_async_copy(src_of(me), outs[t].at[me], local_sems.at[t]))
            for k in range(1, N_DEV):
                peer = _peer(k)
                pair = dict(send_sem=send_sems.at[t, k - 1], recv_sem=recv_sems.at[t, k - 1],
                            device_id_type=pl.DeviceIdType.MESH)
                slot = outs[t].at[_lin(peer)]
                landed.append(pltpu.make_async_remote_copy(src_ref=slot, dst_ref=slot, device_id=peer, **pair))
                if self.scatter or k in (1, 2, 4, 6):
                    direct.append(pltpu.make_async_remote_copy(src_ref=src_of(_lin(peer)), dst_ref=outs[t].at[me],
                                                               device_id=peer, **pair))
                else:
                    came = outs[t].at[_lin(_peer(k - 1))]
                    passed.append((landed[-2], pltpu.make_async_remote_copy(src_ref=came, dst_ref=came,
                                                                            device_id=_peer(1), **pair)))
        return local, direct, passed, landed

    def start(self, ins, outs, sems):
        local, direct, _, _ = self._copies(ins, outs, sems)
        for cp in local + direct:
            cp.start()

    def finish(self, ins, outs, sems):
        local, direct, passed, landed = self._copies(ins, outs, sems)
        arrived = set()
        for came, onward in passed:
            came.wait_recv()
            arrived.add(id(came))
            onward.start()
        for cp in landed:
            if id(cp) not in arrived:
                cp.wait_recv()
        for cp in direct + [onward for _, onward in passed]:
            cp.wait_send()
        for cp in local:
            cp.wait()


def _exchange(arrays, scatter, name):
    ex = _Exchange(arrays, scatter)

    def body(*refs):
        ins, outs, sems = refs[:ex.n], refs[ex.n:2 * ex.n], refs[2 * ex.n:]
        ex.start(ins, outs, sems)
        ex.finish(ins, outs, sems)

    return pl.pallas_call(body, name=name, in_specs=ex.specs, out_specs=ex.specs, out_shape=ex.out_shape,
                          scratch_shapes=ex.scratch)(*arrays)


def _sum_slots(a, name):
    _, R, C = a.shape
    tr = SUM_ROWS if R % SUM_ROWS == 0 else R

    def body(a_ref, o_ref):
        acc = a_ref[0].astype(F32)
        for s in range(1, N_DEV):
            acc = acc + a_ref[s].astype(F32)
        o_ref[...] = acc

    return pl.pallas_call(
        body, name=name, grid=(R // tr,),
        in_specs=[pl.BlockSpec((N_DEV, tr, C), lambda i: (0, i, 0))], out_specs=pl.BlockSpec((tr, C), lambda i: (i, 0)),
        out_shape=jax.ShapeDtypeStruct((R, C), F32), compiler_params=_cp("parallel"),
    )(a)


def _adamw(w, g, m, v, name):
    shape = w.shape
    C = shape[-1]
    R = w.size // C
    tr = R
    for cand in (256, 128, 64):
        if R > cand and R % cand == 0:
            tr = cand
            break
    c1 = 1.0 / (1.0 - ADAM_B1 ** ADAM_STEP)
    c2 = 1.0 / (1.0 - ADAM_B2 ** ADAM_STEP)

    def body(w_ref, g_ref, m_ref, v_ref, d_ref, nm_ref, nv_ref):
        gv = g_ref[...]
        nm = ADAM_B1 * m_ref[...] + (1.0 - ADAM_B1) * gv
        nv = ADAM_B2 * v_ref[...] + (1.0 - ADAM_B2) * (gv * gv)
        d_ref[...] = -ADAM_LR * ((nm * c1) / (jnp.sqrt(nv * c2) + ADAM_EPS) + ADAM_WD * w_ref[...])
        nm_ref[...] = nm
        nv_ref[...] = nv

    spec = pl.BlockSpec((tr, C), lambda i: (i, 0))
    sd = jax.ShapeDtypeStruct((R, C), F32)
    outs = pl.pallas_call(
        body, name=name, grid=(R // tr,), in_specs=[spec] * 4, out_specs=[spec] * 3, out_shape=[sd] * 3,
        compiler_params=_cp("parallel"),
    )(*(t.reshape(R, C) for t in (w, g, m, v)))
    return tuple(t.reshape(shape) for t in outs)


def _col_segments(D):
    return [(0, IN_MAIN, 0), (IN_COLS, IN_COLS + 2 * D, IN_MAIN), (IN_MAIN, IN_COLS, IN_MAIN + 2 * D)]


def _pad_cols_of_blocks(wi):
    _, D, n = wi.shape
    pieces = []
    for lo, hi, _ in _col_segments(D):
        for d in range(N_DEV):
            a, b = max(lo, d * n), min(hi, (d + 1) * n)
            if a < b:
                pieces.append(wi[d][:, a - d * n:b - d * n])
    return jnp.concatenate(pieces + [jnp.zeros((D, LANES - 2 * DN_HEADS), wi.dtype)], axis=1)


def _blocks_of_padded(dw, n):
    D = dw.shape[0]
    out = []
    for d in range(N_DEV):
        pieces = []
        for lo, hi, at in sorted(_col_segments(D)):
            a, b = max(lo, d * n), min(hi, (d + 1) * n)
            if a < b:
                pieces.append(dw[:, at + a - lo:at + b - lo])
        out.append(pieces[0] if len(pieces) == 1 else jnp.concatenate(pieces, axis=1))
    return jnp.stack(out)


def _gate_params(a_log, dt_bias):
    z = jnp.zeros((LANES,), F32)
    return jnp.stack([z.at[DN_HEADS:2 * DN_HEADS].set(a_log), z.at[DN_HEADS:2 * DN_HEADS].set(dt_bias)])


def _layer_fwd(l, x, mod, wts, carry=None):
    S, D = x.shape
    tag = f"l{l}_"
    p, h = _inproj_fwd(x, mod, wts["norm_g"], wts["w_in"], tag + "inproj_fwd")
    (o_sb,), carried = _sb_fwd(p, wts["sb_q_g"], wts["sb_k_g"].T, tag + "sb_fwd", carry)
    qkv = _dn_prep_fwd(p, wts["conv_w"], 4 * SB_W, tag + "dn_prep_fwd")
    pv = _gate_params(wts["dn_a_log"], wts["dn_dt_bias"])
    ba_blk = (IN_MAIN + 2 * D) // LANES
    bg = _dn_gate_fwd(p, pv, ba_blk, tag + "dn_gate_fwd")
    a_row = bg[:, DN_HEADS:2 * DN_HEADS].T.reshape(DN_HEADS, S // BLK, BLK)
    o_dn, states, invs = _delta_fwd(qkv, bg, a_row, tag + "delta_fwd")
    gate = mod[:, 2 * D:]
    out = _merge_fwd(x, gate, o_sb, o_dn, p, wts["dn_norm_g"], wts["w_branch_sb"], wts["w_branch_dn"], wts["w_out"],
                     tag + "merge_fwd")
    saved = dict(x=x, mod=mod, p=p, h=h, o_sb=o_sb, qkv=qkv, pv=pv, bg=bg, a_row=a_row, o_dn=o_dn, states=states,
                 invs=invs, gate=gate)
    return out, saved, carried


def _layer_bwd(l, dxn, sv, wts, carry_of=None, late_carry_of=None):
    S, D = dxn.shape
    tag = f"l{l}_"
    (dosb, dzsb, dodn, dzdn, dmsb, dmdn, dwo, dwbs, dwbd, dgate, dng) = _merge_bwd(
        dxn, sv["gate"], sv["o_sb"], sv["o_dn"], sv["p"], wts["dn_norm_g"], wts["w_branch_sb"], wts["w_branch_dn"],
        wts["w_out"], tag + "merge_bwd")
    carry = None if carry_of is None else carry_of(dict(w_out=dwo, w_branch_sb=dwbs, w_branch_dn=dwbd))
    (dq, dk, dv, dgq, dgkt), carried = _sb_bwd(sv["p"], sv["o_sb"], dosb, wts["sb_q_g"], wts["sb_q_g"].T,
                                                wts["sb_k_g"], wts["sb_k_g"].T, tag + "sb_bwd", carry)
    dqkv_n, dbg, dar = _delta_bwd(sv["qkv"], sv["bg"], sv["a_row"], sv["states"], sv["invs"], dodn, tag + "delta_bwd")
    dqkv, dconv = _dn_prep_bwd(sv["p"], wts["conv_w"], 4 * SB_W, dqkv_n, tag + "dn_prep_bwd")
    dbg = dbg.at[:, DN_HEADS:2 * DN_HEADS].add(dar.reshape(DN_HEADS, S).T)
    ba_blk = (IN_MAIN + 2 * D) // LANES
    dba, dpv = _dn_gate_bwd(sv["p"], sv["pv"], ba_blk, dbg, tag + "dn_gate_bwd")
    dp = jnp.concatenate([dq, dk, dv, dzsb, dqkv, dzdn, dmsb, dmdn, dba], axis=1)
    dw_in = _matmul_tn(sv["h"].T, dp, tag + "inproj_bwd_dw")
    late = None if late_carry_of is None else late_carry_of(dict(w_in=dw_in, conv_w=dconv))
    (dx, dmod, dg), carried_late = _inproj_bwd_dx(dp, wts["w_in"], sv["x"], sv["mod"], wts["norm_g"], dxn,
                                                  tag + "inproj_bwd_dx", late)
    dmod = dmod.at[:, 2 * D:].set(dgate)
    grads = dict(w_in=dw_in, w_branch_sb=dwbs, w_branch_dn=dwbd, w_out=dwo, conv_w=dconv,
                 mod=dmod[0], norm_g=dg[0], sb_q_g=jnp.sum(dgq, axis=0)[0], sb_k_g=jnp.sum(dgkt, axis=0)[:, 0],
                 dn_a_log=dpv[0, DN_HEADS:2 * DN_HEADS], dn_dt_bias=dpv[1, DN_HEADS:2 * DN_HEADS], dn_norm_g=dng[0])
    return dx, grads, carried, carried_late


def _pad_rows(a, mult):
    extra = (-a.shape[0]) % mult
    return a if extra == 0 else jnp.concatenate([a, jnp.zeros((extra,) + a.shape[1:], a.dtype)], axis=0)


def _pack_rows(parts, width, mult):
    flat = jnp.concatenate([t.reshape(-1) for t in parts])
    extra = (-flat.shape[0]) % width
    if extra:
        flat = jnp.concatenate([flat, jnp.zeros((extra,), flat.dtype)])
    return _pad_rows(flat.reshape(-1, width), mult)


def _take(flat, off, shape):
    n = math.prod(shape)
    return flat[..., off:off + n].reshape(flat.shape[:-1] + tuple(shape)), off + n


SMALL = ("mod", "norm_g", "sb_q_g", "sb_k_g", "dn_a_log", "dn_dt_bias", "dn_norm_g")


def kernel(x, c, ada_w, ada_b, norm_g, w_in, sb_q_g, sb_k_g, conv_w, dn_a_log, dn_dt_bias, dn_norm_g, w_branch_sb, w_branch_dn, w_out, loss_target, m_ada_w, m_ada_b, m_norm_g, m_w_in, m_sb_q_g, m_sb_k_g, m_conv_w, m_dn_a_log, m_dn_dt_bias, m_dn_norm_g, m_w_branch_sb, m_w_branch_dn, m_w_out, v_ada_w, v_ada_b, v_norm_g, v_w_in, v_sb_q_g, v_sb_k_g, v_conv_w, v_dn_a_log, v_dn_dt_bias, v_dn_norm_g, v_w_branch_sb, v_w_branch_dn, v_w_out):
    L, D = norm_g.shape
    S = x.shape[1]
    n_in = w_in.shape[2]
    n_ada = ada_w.shape[2]
    n_br = w_branch_sb.shape[2]
    n_out = w_out.shape[1]
    n_conv = conv_w.shape[2]
    me = _lin(_me())

    def cat(a):
        return jnp.concatenate([a[d] for d in range(N_DEV)], axis=1)

    c_all, conv_all = _exchange([c, conv_w.reshape(L * CONV_K, n_conv)], False, "gather_small")
    c_all = c_all.reshape(N_DEV, D)
    conv_full = cat(conv_all).reshape(L, CONV_K, N_DEV * n_conv)

    mod_part = _mod_fwd(c_all, ada_w, "mod_fwd")

    def shards16(l):
        return [w_in[l].astype(BF16), w_branch_sb[l].astype(BF16), w_branch_dn[l].astype(BF16), w_out[l].astype(BF16)]

    def whole(l, got):
        wi, wbs, wbd, wo = got
        return dict(norm_g=norm_g[l:l + 1], w_in=_pad_cols_of_blocks(wi), sb_q_g=sb_q_g[l:l + 1], sb_k_g=sb_k_g[l:l + 1],
                    conv_w=conv_full[l], dn_a_log=dn_a_log[l], dn_dt_bias=dn_dt_bias[l], dn_norm_g=dn_norm_g[l:l + 1],
                    w_branch_sb=cat(wbs), w_branch_dn=cat(wbd), w_out=wo.reshape(N_DEV * n_out, D))

    *got, mod_all = _exchange(shards16(0) + [mod_part.reshape(L * N_DEV, n_ada)], False, "gather_weights")
    mod_full = cat(mod_all).reshape(L, N_DEV, N_DEV * n_ada) + ada_b[:, None, :]
    mod_mine = lax.dynamic_slice_in_dim(mod_full, me, 1, axis=1)

    act = x[0]
    saved, wts = [], []
    for l in range(L):
        wts.append(whole(l, got))
        act, sv, got = _layer_fwd(l, act, mod_mine[l], wts[l], (shards16(l + 1), False) if l + 1 < L else None)
        saved.append(sv)
    loss_cols, dact = _loss_fwd_bwd(act, loss_target[0], "loss")
    loss = lax.psum(jnp.sum(loss_cols), ("x", "y", "c"))

    def blocks(name, g):
        if name == "w_out":
            return g.astype(BF16).reshape(N_DEV, n_out, D)
        if name == "w_in":
            return _blocks_of_padded(g, n_in).astype(BF16)
        n = g.shape[1] // N_DEV
        dtype = F32 if name == "conv_w" else BF16
        return jnp.stack([g[:, d * n:(d + 1) * n].astype(dtype) for d in range(N_DEV)])

    early, late = ("w_out", "w_branch_sb", "w_branch_dn"), ("w_in", "conv_w")
    grads, recv, pending = [None] * L, {}, []
    for l in reversed(range(L)):
        keys = [k for k, _ in pending] + [(n, l) for n in early]

        def carry_of(g_early, pending=pending):
            return [a for _, a in pending] + [blocks(n, g_early[n]) for n in early], True

        last = l == 0
        dact, grads[l], got, got_late = _layer_bwd(
            l, dact, saved[l], wts[l], carry_of, (lambda g: ([blocks(n, g[n]) for n in late], True)) if last else None)
        recv.update(zip(keys, got))
        recv.update(zip([(n, l) for n in late], got_late))
        pending = [] if last else [((n, l), blocks(n, grads[l][n])) for n in late]
    grad_x = dact[None]
    small_g = _pack_rows([grads[l][n] for l in range(L) for n in SMALL], LANES, 8)
    (small_all_g,) = _exchange([small_g], False, "gather_small_grads")
    small_sum = _sum_slots(small_all_g, "sum_small_grads").reshape(-1)
    shard_shapes = dict(w_in=w_in.shape, w_branch_sb=w_branch_sb.shape, w_branch_dn=w_branch_dn.shape,
                        conv_w=conv_w.shape, w_out=w_out.shape)
    g_out = {n: jnp.stack([_sum_slots(recv[(n, l)], f"sum_{n}_l{l}").reshape(shape[1:]) for l in range(L)])
             for n, shape in shard_shapes.items()}
    small_shapes = dict(mod=(3 * D,), norm_g=(D,), sb_q_g=(SB_DH,), sb_k_g=(SB_DH,), dn_a_log=(DN_HEADS,),
                        dn_dt_bias=(DN_HEADS,), dn_norm_g=(DN_DH,))
    off = 0
    off_all = 0
    small_each = small_all_g.reshape(N_DEV, -1)
    per_small = {n: [] for n in SMALL}
    dmod_all = []
    for l in range(L):
        for n in SMALL:
            t, off = _take(small_sum, off, small_shapes[n])
            per_small[n].append(t)
            if n == "mod":
                t_all, _ = _take(small_each, off_all, small_shapes[n])
                dmod_all.append(t_all)
            off_all += math.prod(small_shapes[n])
    for n in SMALL:
        g_out[n if n != "mod" else "ada_b"] = jnp.stack(per_small[n])
    dmod_all = jnp.stack(dmod_all)
    dmod_cols = lax.dynamic_slice_in_dim(dmod_all, me * n_ada, n_ada, axis=2)
    g_out["ada_w"] = _mod_bwd_w(c_all.T, dmod_cols, "mod_bwd_w")

    given = dict(ada_w=(ada_w, m_ada_w, v_ada_w), ada_b=(ada_b, m_ada_b, v_ada_b), norm_g=(norm_g, m_norm_g, v_norm_g),
                 w_in=(w_in, m_w_in, v_w_in), sb_q_g=(sb_q_g, m_sb_q_g, v_sb_q_g), sb_k_g=(sb_k_g, m_sb_k_g, v_sb_k_g),
                 conv_w=(conv_w, m_conv_w, v_conv_w), dn_a_log=(dn_a_log, m_dn_a_log, v_dn_a_log),
                 dn_dt_bias=(dn_dt_bias, m_dn_dt_bias, v_dn_dt_bias), dn_norm_g=(dn_norm_g, m_dn_norm_g, v_dn_norm_g),
                 w_branch_sb=(w_branch_sb, m_w_branch_sb, v_w_branch_sb),
                 w_branch_dn=(w_branch_dn, m_w_branch_dn, v_w_branch_dn), w_out=(w_out, m_w_out, v_w_out))
    order = list(given)
    upd = {n: _adamw(given[n][0], g_out[n], given[n][1], given[n][2], "adamw_" + n) for n in order}
    return (loss, grad_x, *[g_out[n] for n in order], *[upd[n][0] for n in order], *[upd[n][1] for n in order],
            *[upd[n][2] for n in order])
```

```python
import functools
import math

import jax
import jax.numpy as jnp
from jax import lax
from jax.experimental import pallas as pl
from jax.experimental.pallas import tpu as pltpu

F32 = jnp.float32
BF16 = jnp.bfloat16
HI = lax.Precision.HIGHEST

N_DEV = 8
EPS = 1e-6
SB_HEADS, SB_DH = 8, 64
DN_HEADS, DN_DH = 4, 128
SB_W = SB_HEADS * SB_DH
DN_W = DN_HEADS * DN_DH
CONV_K = 4
BLK = 128
SB_KEYS = 512
SB_QB = 512
LANES = 128
IN_MAIN = 4 * SB_W + 4 * DN_W
IN_COLS = IN_MAIN + 2 * DN_HEADS
ADAM_LR, ADAM_B1, ADAM_B2, ADAM_EPS, ADAM_WD, ADAM_STEP = 0.001, 0.9, 0.999, 1e-08, 0.01, 10
VMEM_LIMIT = 56 * 1024 * 1024
LOG2E = 1.4426950408889634
SUM_ROWS = 128


def _cp(*sem, vmem=VMEM_LIMIT):
    return pltpu.CompilerParams(dimension_semantics=sem if sem else None, vmem_limit_bytes=vmem)


def _dot(a, b, prec=HI):
    return lax.dot_general(a, b, (((1,), (0,)), ((), ())), precision=prec, preferred_element_type=F32)


def _dot_nt(a, b, prec=HI):
    return lax.dot_general(a, b, (((1,), (1,)), ((), ())), precision=prec, preferred_element_type=F32)


def _bdot(a, b):
    return lax.dot_general(a.astype(BF16), b.astype(BF16), (((1,), (0,)), ((), ())), preferred_element_type=F32)


def _bdot_nt(a, b):
    return lax.dot_general(a.astype(BF16), b.astype(BF16), (((1,), (1,)), ((), ())), preferred_element_type=F32)


def _bdot_tn(a, b):
    return lax.dot_general(a.astype(BF16), b.astype(BF16), (((0,), (0,)), ((), ())), preferred_element_type=F32)


def _split_dot(a, b01_twice):
    hi = a.astype(BF16)
    lo = (a - hi.astype(F32)).astype(BF16)
    return jnp.dot(jnp.concatenate([hi, lo], axis=1), b01_twice, preferred_element_type=F32)


def _sigmoid(x):
    return 1.0 / (1.0 + jnp.exp(-x))


def _silu(x):
    return x * _sigmoid(x)


def _softplus(x):
    return jnp.maximum(x, 0.0) + jnp.log(1.0 + jnp.exp(-jnp.abs(x)))


def _rms(x):
    return x * lax.rsqrt(jnp.mean(x * x, axis=-1, keepdims=True) + EPS)


def _prenorm(x, g, shift, scale):
    return _rms(x) * g * (1.0 + scale) + shift


def _inproj_fwd(x, mod, g, w, name):
    S, D = x.shape
    N = w.shape[1]
    tm = min(512, S)
    tn = 896 if N % 896 == 0 else 128

    def body(x_ref, mod_ref, g_ref, w_ref, p_ref, h_ref):
        @pl.when(pl.program_id(1) == 0)
        def _():
            h = _prenorm(x_ref[...], g_ref[...], mod_ref[:, 0:D], mod_ref[:, D:2 * D])
            h_ref[...] = h.astype(BF16)

        p_ref[...] = jnp.dot(h_ref[...], w_ref[...], preferred_element_type=F32)

    return pl.pallas_call(
        body, name=name, grid=(S // tm, N // tn),
        in_specs=[pl.BlockSpec((tm, D), lambda i, j: (i, 0)), pl.BlockSpec((1, 3 * D), lambda i, j: (0, 0)),
                  pl.BlockSpec((1, D), lambda i, j: (0, 0)), pl.BlockSpec((D, tn), lambda i, j: (0, j))],
        out_specs=[pl.BlockSpec((tm, tn), lambda i, j: (i, j)), pl.BlockSpec((tm, D), lambda i, j: (i, 0))],
        out_shape=[jax.ShapeDtypeStruct((S, N), F32), jax.ShapeDtypeStruct((S, D), BF16)],
        compiler_params=_cp("parallel", "arbitrary"),
    )(x, mod, g, w)


def _inproj_bwd_dx(dp, w, x, mod, g, dxn, name, carry=None):
    S, N = dp.shape
    D = x.shape[1]
    tm = min(512, S)
    tk = 896 if N % 896 == 0 else 128
    nk = N // tk

    def body(dp_ref, w_ref, x_ref, mod_ref, g_ref, dxn_ref, dx_ref, dmod_ref, dg_ref, acc):
        i, k = pl.program_id(0), pl.program_id(1)

        @pl.when(k == 0)
        def _():
            acc[...] = jnp.zeros_like(acc)

        @pl.when((i == 0) & (k == 0))
        def _():
            dmod_ref[...] = jnp.zeros_like(dmod_ref)
            dg_ref[...] = jnp.zeros_like(dg_ref)

        acc[...] += lax.dot_general(dp_ref[...], w_ref[...], (((1,), (1,)), ((), ())), preferred_element_type=F32)

        @pl.when(k == nk - 1)
        def _():
            _, vjp = jax.vjp(_prenorm, x_ref[...], g_ref[...], mod_ref[:, 0:D], mod_ref[:, D:2 * D])
            dx, dg, dshift, dscale = vjp(acc[...])
            dx_ref[...] = dxn_ref[...] + dx
            dg_ref[...] += dg
            dmod_ref[:, 0:D] += dshift
            dmod_ref[:, D:2 * D] += dscale

    return _call_carrying(
        body, name, (S // tm, nk), carry, [dp, w, x, mod, g, dxn],
        in_specs=[pl.BlockSpec((tm, tk), lambda i, k: (i, k)), pl.BlockSpec((D, tk), lambda i, k: (0, k)),
                  pl.BlockSpec((tm, D), lambda i, k: (i, 0)), pl.BlockSpec((1, 3 * D), lambda i, k: (0, 0)),
                  pl.BlockSpec((1, D), lambda i, k: (0, 0)), pl.BlockSpec((tm, D), lambda i, k: (i, 0))],
        out_specs=[pl.BlockSpec((tm, D), lambda i, k: (i, 0)), pl.BlockSpec((1, 3 * D), lambda i, k: (0, 0)),
                   pl.BlockSpec((1, D), lambda i, k: (0, 0))],
        out_shape=[jax.ShapeDtypeStruct((S, D), F32), jax.ShapeDtypeStruct((1, 3 * D), F32),
                   jax.ShapeDtypeStruct((1, D), F32)],
        scratch_shapes=[pltpu.VMEM((tm, D), F32)], vmem=VMEM_LIMIT)


def _matmul_tn(a_t, b, name):
    M, K = a_t.shape
    N = b.shape[1]
    tn = 896 if N % 896 == 0 else (512 if N % 512 == 0 else 128)
    tk = min(512, K)
    nk = K // tk

    def body(a_ref, b_ref, o_ref):
        @pl.when(pl.program_id(1) == 0)
        def _():
            o_ref[...] = jnp.zeros_like(o_ref)

        o_ref[...] += jnp.dot(a_ref[...], b_ref[...], preferred_element_type=F32)

    return pl.pallas_call(
        body, name=name, grid=(N // tn, nk),
        in_specs=[pl.BlockSpec((M, tk), lambda j, k: (0, k)), pl.BlockSpec((tk, tn), lambda j, k: (k, j))],
        out_specs=pl.BlockSpec((M, tn), lambda j, k: (0, j)),
        out_shape=jax.ShapeDtypeStruct((M, N), F32),
        compiler_params=_cp("parallel", "arbitrary"),
    )(a_t, b)


def _qk_norm(t, g, scale):
    return _rms(t) * g * scale


def _qk_norm_t(t, g_col, scale):
    return t * lax.rsqrt(jnp.mean(t * t, axis=0, keepdims=True) + EPS) * g_col * scale


def _suffix_sums(x, tri):
    half = tri.shape[1]
    lo, hi = x[:, :half], x[:, half:]
    hi_sum = jnp.sum(hi, axis=1, keepdims=True)
    y = jnp.concatenate([_split_dot(lo, tri) + hi_sum, _split_dot(hi, tri)], axis=1)
    return y, hi_sum + jnp.sum(lo, axis=1, keepdims=True)


def _sb_step(qi, kat_blk, cl, from_here, valid):
    z = jnp.dot(qi, kat_blk, preferred_element_type=F32)
    nz = -z
    lk = jnp.minimum(nz, 0.0) - jnp.log(1.0 + jnp.exp2(jnp.minimum(z, nz))) * LOG2E
    if valid is not None:
        lk = jnp.where(valid, lk, 0.0)
    later, tot = _suffix_sums(lk, from_here)
    w = jnp.exp2(z + later + cl)
    if valid is not None:
        w = jnp.where(valid, w, 0.0)
    return z, lk, w, tot


def _sb_masks(kb):
    half = kb // 2
    r = lax.broadcasted_iota(jnp.int32, (half, half), 0)
    c = lax.broadcasted_iota(jnp.int32, (half, half), 1)
    rq = lax.broadcasted_iota(jnp.int32, (SB_QB, kb), 0)
    ck = lax.broadcasted_iota(jnp.int32, (SB_QB, kb), 1)
    twice = lambda m: jnp.concatenate([m, m], axis=0).astype(BF16)
    return twice((r >= c).astype(F32)), ck - rq


def _sb_fwd(p, gq, gkt, name, carry=None):
    S, dh = p.shape[0], SB_DH
    kb = min(SB_KEYS, S)
    per = kb // SB_QB
    nb = S // SB_QB
    scale = 1.0 / math.sqrt(dh)
    pairs = SB_W // LANES

    def body(q_ref, k_ref, v_ref, gq_ref, gkt_ref, o_ref, kt2, qa, kat, vb):
        kt2[...] = k_ref[...].T
        from_here, diff = _sb_masks(kb)
        for hh in range(LANES // dh):
            lanes = slice(hh * dh, (hh + 1) * dh)
            qa[...] = _qk_norm(q_ref[:, lanes], gq_ref[...], scale * LOG2E).astype(BF16)
            kat[...] = _qk_norm_t(kt2[lanes, :], gkt_ref[...], 1.0).astype(BF16)
            vb[...] = v_ref[:, lanes].astype(BF16)

            def qblock(i, _):
                rows = pl.ds(pl.multiple_of(i * SB_QB, SB_QB), SB_QB)
                qi = qa[rows, :]
                sbd = i // per

                def step(sb, cl, acc, valid):
                    cols = pl.ds(pl.multiple_of(sb * kb, kb), kb)
                    _, _, w, tot = _sb_step(qi, kat[:, cols], cl, from_here, valid)
                    return cl + tot, acc + jnp.dot(w.astype(BF16), vb[cols, :], preferred_element_type=F32)

                cl, acc = step(sbd, jnp.zeros((SB_QB, 1), F32), jnp.zeros((SB_QB, dh), F32),
                               diff < (i - sbd * per) * SB_QB)
                _, acc = lax.fori_loop(0, sbd, lambda jj, c: step(sbd - 1 - jj, c[0], c[1], None), (cl, acc))
                o_ref[rows, lanes] = acc
                return 0

            lax.fori_loop(0, nb, qblock, 0)

    blk = lambda off: pl.BlockSpec((S, LANES), lambda g: (0, off + g))
    return _call_carrying(
        body, name, pairs, carry, [p, p, p, gq, gkt],
        in_specs=[blk(0), blk(pairs), blk(2 * pairs), pl.BlockSpec((1, dh), lambda g: (0, 0)),
                  pl.BlockSpec((dh, 1), lambda g: (0, 0))],
        out_specs=[blk(0)], out_shape=[jax.ShapeDtypeStruct((S, SB_W), F32)],
        scratch_shapes=[pltpu.VMEM((LANES, S), F32), pltpu.VMEM((S, dh), BF16), pltpu.VMEM((dh, S), BF16),
                        pltpu.VMEM((S, dh), BF16)],
        vmem=VMEM_LIMIT)


def _call_carrying(body, name, grid, carry, operands, in_specs, out_specs, out_shape, scratch_shapes, vmem):
    grid = (grid,) if isinstance(grid, int) else tuple(grid)
    if carry is None:
        res = pl.pallas_call(body, name=name, grid=grid, in_specs=in_specs, out_specs=out_specs,
                             out_shape=out_shape, scratch_shapes=scratch_shapes,
                             compiler_params=_cp(*["arbitrary"] * len(grid), vmem=vmem))(*operands)
        return res, []

    def at(corner):
        hit = pl.program_id(0) == corner(grid[0])
        for axis in range(1, len(grid)):
            hit = jnp.logical_and(hit, pl.program_id(axis) == corner(grid[axis]))
        return hit

    ex = _Exchange(*carry)
    n_in, n_out, n_scr = len(in_specs), len(out_specs), len(scratch_shapes)

    def wrapped(*refs):
        ins, refs = refs[:n_in], refs[n_in:]
        xin, refs = refs[:ex.n], refs[ex.n:]
        outs, refs = refs[:n_out], refs[n_out:]
        xout, refs = refs[:ex.n], refs[ex.n:]
        scr, sems = refs[:n_scr], refs[n_scr:]

        @pl.when(at(lambda n: 0))
        def _():
            ex.start(xin, xout, sems)

        body(*ins, *outs, *scr)

        @pl.when(at(lambda n: n - 1))
        def _():
            ex.finish(xin, xout, sems)

    res = pl.pallas_call(wrapped, name=name, grid=grid, in_specs=in_specs + ex.specs,
                         out_specs=out_specs + ex.specs, out_shape=out_shape + ex.out_shape,
                         scratch_shapes=scratch_shapes + ex.scratch,
                         compiler_params=_cp(*["arbitrary"] * len(grid), vmem=vmem))(*operands, *carry[0])
    return res[:n_out], res[n_out:]


def _sb_bwd(p, o, do, gq, gqt, gk, gkt, name, carry=None):
    S, dh = p.shape[0], SB_DH
    kb = min(SB_KEYS, S)
    per = kb // SB_QB
    nb = S // SB_QB
    scale = 1.0 / math.sqrt(dh)
    pairs = SB_W // LANES
    per_pair = LANES // dh

    def body(q_ref, k_ref, v_ref, o_ref, do_ref, gq_ref, gqt_ref, gk_ref, gkt_ref,
             dq_ref, dk_ref, dv_ref, dgq_ref, dgkt_ref,
             qt2, kt2, vt2, dot2, dkt2, dvt2, qa, qat, ka, kat, vb, vtb, dob, dotb, dqa):
        qt2[...] = q_ref[...].T
        kt2[...] = k_ref[...].T
        vt2[...] = v_ref[...].T
        dot2[...] = do_ref[...].T
        dkt2[...] = jnp.zeros_like(dkt2)
        dvt2[...] = jnp.zeros_like(dvt2)
        from_here, diff = _sb_masks(kb)
        for hh in range(per_pair):
            lanes = slice(hh * dh, (hh + 1) * dh)
            qa[...] = _qk_norm(q_ref[:, lanes], gq_ref[...], scale * LOG2E).astype(BF16)
            qat[...] = _qk_norm_t(qt2[lanes, :], gqt_ref[...], scale * LOG2E).astype(BF16)
            ka[...] = _qk_norm(k_ref[:, lanes], gk_ref[...], 1.0).astype(BF16)
            kat[...] = _qk_norm_t(kt2[lanes, :], gkt_ref[...], 1.0).astype(BF16)
            vb[...] = v_ref[:, lanes].astype(BF16)
            vtb[...] = vt2[lanes, :].astype(BF16)
            dob[...] = do_ref[:, lanes].astype(BF16)
            dotb[...] = dot2[lanes, :].astype(BF16)

            def qblock(i, _):
                rows = pl.ds(pl.multiple_of(i * SB_QB, SB_QB), SB_QB)
                qi, qit = qa[rows, :], qat[:, rows]
                doi, doit = dob[rows, :], dotb[:, rows]
                total = jnp.sum(doi.astype(F32) * o_ref[rows, lanes], axis=1, keepdims=True)
                sbd = i // per

                def step(sb, cl, cd, dqi, valid):
                    cols = pl.ds(pl.multiple_of(sb * kb, kb), kb)
                    z, lk, w, tot = _sb_step(qi, kat[:, cols], cl, from_here, valid)
                    w16 = w.astype(BF16)
                    dl = jnp.dot(doi, vtb[:, cols], preferred_element_type=F32) * w16.astype(F32)
                    incl, dtot = _suffix_sums(dl, from_here)
                    sig = jnp.exp2(z + lk)
                    dz = dl - sig * (dl + (total - cd - incl))
                    if valid is not None:
                        dz = jnp.where(valid, dz, 0.0)
                    dz16 = dz.astype(BF16)
                    dqi = dqi + jnp.dot(dz16, ka[cols, :], preferred_element_type=F32)
                    dkt2[lanes, cols] += jnp.dot(qit, dz16, preferred_element_type=F32)
                    dvt2[lanes, cols] += jnp.dot(doit, w16, preferred_element_type=F32)
                    return cl + tot, cd + dtot, dqi

                zero = jnp.zeros((SB_QB, 1), F32)
                first = step(sbd, zero, zero, jnp.zeros((SB_QB, dh), F32), diff < (i - sbd * per) * SB_QB)
                _, _, dqi = lax.fori_loop(0, sbd, lambda jj, c: step(sbd - 1 - jj, c[0], c[1], c[2], None), first)
                dqa[rows, :] = dqi
                return 0

            lax.fori_loop(0, nb, qblock, 0)
            _, vq = jax.vjp(lambda t, g: _qk_norm(t, g, scale), q_ref[:, lanes], gq_ref[...])
            dq, dgq = vq(dqa[...])
            dq_ref[:, lanes] = dq.astype(BF16)
            dgq_ref[hh] = dgq
            _, vk = jax.vjp(lambda t, g: _qk_norm_t(t, g, 1.0), kt2[lanes, :], gkt_ref[...])
            dkt, dgkt = vk(dkt2[lanes, :] * (1.0 / LOG2E))
            dkt2[lanes, :] = dkt
            dgkt_ref[hh] = dgkt
        dk_ref[...] = dkt2[...].T.astype(BF16)
        dv_ref[...] = dvt2[...].T.astype(BF16)

    blk = lambda off: pl.BlockSpec((S, LANES), lambda g: (0, off + g))
    once = lambda off: pl.BlockSpec((S, LANES), lambda g: (0, off + g), pipeline_mode=pl.Buffered(1))
    gr = pl.BlockSpec((1, dh), lambda g: (0, 0))
    gc = pl.BlockSpec((dh, 1), lambda g: (0, 0))
    sd = jax.ShapeDtypeStruct((S, SB_W), BF16)
    return _call_carrying(
        body, name, pairs, carry, [p, p, p, o, do, gq, gqt, gk, gkt],
        in_specs=[once(0), once(pairs), once(2 * pairs), once(0), once(0), gr, gc, gr, gc],
        out_specs=[blk(0), blk(0), blk(0), pl.BlockSpec((per_pair, 1, dh), lambda g: (g, 0, 0)),
                   pl.BlockSpec((per_pair, dh, 1), lambda g: (g, 0, 0))],
        out_shape=[sd, sd, sd, jax.ShapeDtypeStruct((SB_HEADS, 1, dh), F32),
                   jax.ShapeDtypeStruct((SB_HEADS, dh, 1), F32)],
        scratch_shapes=[pltpu.VMEM((LANES, S), F32)] * 6 + [pltpu.VMEM((S, dh), BF16), pltpu.VMEM((dh, S), BF16)] * 4
        + [pltpu.VMEM((S, dh), F32)],
        vmem=60 * 1024 * 1024)


def _shift_down(x, s, rows):
    if s == 0:
        return x
    return jnp.where(rows >= s, pltpu.roll(x, s, 0), 0.0)


def _shift_up(x, s, rows, n):
    if s == 0:
        return x
    return jnp.where(rows < n - s, pltpu.roll(x, n - s, 0), 0.0)


def _conv(x, w_ref, rows):
    y = x * w_ref[CONV_K - 1:CONV_K, :]
    for kk in range(CONV_K - 1):
        y = y + _shift_down(x, CONV_K - 1 - kk, rows) * w_ref[kk:kk + 1, :]
    return y


def _act_norm(y, normed):
    s = _silu(y)
    return s * lax.rsqrt(jnp.sum(s * s, axis=-1, keepdims=True) + EPS) if normed else s


def _by_block_kind(fn):
    pl.when(pl.program_id(0) < 2 * DN_HEADS)(lambda: fn(True))
    pl.when(pl.program_id(0) >= 2 * DN_HEADS)(lambda: fn(False))


def _dn_prep_fwd(p, conv_w, col0, name):
    S = p.shape[0]
    nblk = 3 * DN_HEADS
    b0 = col0 // DN_DH

    def body(x_ref, w_ref, o_ref):
        rows = lax.broadcasted_iota(jnp.int32, (S, DN_DH), 0)
        y = _conv(x_ref[...], w_ref, rows)

        def write(normed):
            o_ref[...] = _act_norm(y, normed)

        _by_block_kind(write)

    return pl.pallas_call(
        body, name=name, grid=(nblk,),
        in_specs=[pl.BlockSpec((S, DN_DH), lambda j: (0, b0 + j)), pl.BlockSpec((CONV_K, DN_DH), lambda j: (0, j))],
        out_specs=pl.BlockSpec((S, DN_DH), lambda j: (0, j)),
        out_shape=jax.ShapeDtypeStruct((S, 3 * DN_W), F32),
        compiler_params=_cp("parallel"),
    )(p, conv_w)


def _dn_prep_bwd(p, conv_w, col0, dout, name):
    S = p.shape[0]
    nblk = 3 * DN_HEADS
    b0 = col0 // DN_DH

    def body(x_ref, w_ref, do_ref, dx_ref, dw_ref):
        rows = lax.broadcasted_iota(jnp.int32, (S, DN_DH), 0)
        x = x_ref[...]
        y = _conv(x, w_ref, rows)

        def back(normed):
            _, vjp = jax.vjp(lambda t: _act_norm(t, normed), y)
            (dy,) = vjp(do_ref[...])
            dx = dy * w_ref[CONV_K - 1:CONV_K, :]
            dw_ref[CONV_K - 1:CONV_K, :] = jnp.sum(dy * x, axis=0, keepdims=True)
            for kk in range(CONV_K - 1):
                s = CONV_K - 1 - kk
                dx = dx + _shift_up(dy, s, rows, S) * w_ref[kk:kk + 1, :]
                dw_ref[kk:kk + 1, :] = jnp.sum(dy * _shift_down(x, s, rows), axis=0, keepdims=True)
            dx_ref[...] = dx.astype(BF16)

        _by_block_kind(back)

    return pl.pallas_call(
        body, name=name, grid=(nblk,),
        in_specs=[pl.BlockSpec((S, DN_DH), lambda j: (0, b0 + j)), pl.BlockSpec((CONV_K, DN_DH), lambda j: (0, j)),
                  pl.BlockSpec((S, DN_DH), lambda j: (0, j))],
        out_specs=[pl.BlockSpec((S, DN_DH), lambda j: (0, j)), pl.BlockSpec((CONV_K, DN_DH), lambda j: (0, j))],
        out_shape=[jax.ShapeDtypeStruct((S, 3 * DN_W), BF16), jax.ShapeDtypeStruct((CONV_K, 3 * DN_W), F32)],
        compiler_params=_cp("parallel"),
    )(p, conv_w, dout)


def _gate_fn(x, pv):
    lane = lax.broadcasted_iota(jnp.int32, x.shape, 1)
    decay = -jnp.exp(pv[0:1, :]) * _softplus(x + pv[1:2, :])
    return jnp.where(lane < DN_HEADS, _sigmoid(x), decay)


def _dn_gate_fwd(p, pv, blk, name):
    S = p.shape[0]

    def body(x_ref, pv_ref, o_ref):
        o_ref[...] = _gate_fn(x_ref[...], pv_ref[...])

    return pl.pallas_call(
        body, name=name, grid=(1,),
        in_specs=[pl.BlockSpec((S, LANES), lambda i: (0, blk)), pl.BlockSpec((2, LANES), lambda i: (0, 0))],
        out_specs=pl.BlockSpec((S, LANES), lambda i: (0, 0)),
        out_shape=jax.ShapeDtypeStruct((S, LANES), F32),
        compiler_params=_cp("arbitrary"),
    )(p, pv)


def _dn_gate_bwd(p, pv, blk, dout, name):
    S = p.shape[0]

    def body(x_ref, pv_ref, do_ref, dx_ref, dpv_ref):
        _, vjp = jax.vjp(_gate_fn, x_ref[...], pv_ref[...])
        dx, dpv = vjp(do_ref[...])
        dx_ref[...] = dx.astype(BF16)
        dpv_ref[...] = dpv

    return pl.pallas_call(
        body, name=name, grid=(1,),
        in_specs=[pl.BlockSpec((S, LANES), lambda i: (0, blk)), pl.BlockSpec((2, LANES), lambda i: (0, 0)),
                  pl.BlockSpec((S, LANES), lambda i: (0, 0))],
        out_specs=[pl.BlockSpec((S, LANES), lambda i: (0, 0)), pl.BlockSpec((2, LANES), lambda i: (0, 0))],
        out_shape=[jax.ShapeDtypeStruct((S, LANES), BF16), jax.ShapeDtypeStruct((2, LANES), F32)],
        compiler_params=_cp("arbitrary"),
    )(p, pv, dout)


def _t(x):
    return jnp.swapaxes(x, -1, -2)


def _matmuls(prec, differentiable):
    def mm(a, b):
        return lax.dot_general(a, b, (((2,), (1,)), ((0,), (0,))), precision=prec, preferred_element_type=F32)

    def mm_nt(a, b):
        return lax.dot_general(a, b, (((2,), (2,)), ((0,), (0,))), precision=prec, preferred_element_type=F32)

    if not differentiable:
        return mm, mm_nt
    dmm, dmm_nt = jax.custom_vjp(mm), jax.custom_vjp(mm_nt)
    dmm.defvjp(lambda a, b: (mm(a, b), (a, b)), lambda res, g: (mm_nt(g, res[1]), mm(_t(res[0]), g)))
    dmm_nt.defvjp(lambda a, b: (mm_nt(a, b), (a, b)), lambda res, g: (mm(g, res[1]), mm(_t(g), res[0])))
    return dmm, dmm_nt


def _known_inverse(mm):
    f = jax.custom_vjp(lambda n, inv: inv)
    f.defvjp(lambda n, inv: (inv, inv),
             lambda inv, g: (mm(mm(_t(inv), g), _t(inv)), jnp.zeros_like(inv)))
    return f


def _delta_chunk(state, q, k, v, beta, a_col, a_row, differentiable=False, inv_known=None):
    mm, mm_nt = _matmuls(lax.Precision.HIGH, differentiable)
    mm_sum, _ = _matmuls(HI, differentiable)
    H, C, _ = q.shape
    r = lax.broadcasted_iota(jnp.int32, (H, C, C), 1)
    c = lax.broadcasted_iota(jnp.int32, (H, C, C), 2)
    tril, strict = r >= c, r > c
    eye = (r == c).astype(F32)
    g_c = mm_sum(tril.astype(F32), jnp.broadcast_to(a_col, (H, C, C)))
    g_r = mm_sum(jnp.broadcast_to(a_row, (H, C, C)), (r <= c).astype(F32))
    decay = jnp.where(tril, jnp.exp(jnp.where(tril, g_c - g_r, 0.0)), 0.0)
    eg = jnp.exp(g_c)
    g_last = jnp.sum(jnp.where(r == C - 1, g_c, 0.0), axis=1, keepdims=True)
    qs = q * (float(q.shape[2]) ** -0.5)
    kb = k * beta
    neg_m = jnp.where(strict, -(mm_nt(kb, k) * decay), 0.0)
    if inv_known is None:
        inv = eye + neg_m
        pw = neg_m
        for _ in range(int(math.log2(C)) - 1):
            pw = mm(pw, pw)
            inv = inv + mm(inv, pw)
    else:
        inv = _known_inverse(mm)(neg_m, inv_known)
    u = mm(inv, v * beta)
    w = mm(inv, kb * eg)
    intra = jnp.where(tril, mm_nt(qs, k) * decay, 0.0)
    v_new = u - mm(w, state)
    o = mm(qs * eg, state) + mm(intra, v_new)
    nxt = state * jnp.exp(g_last) + mm(_t(k * jnp.exp(g_last - g_c)), v_new)
    return o, nxt, inv


def _heads(t):
    return jnp.stack([t[:, h * DN_DH:(h + 1) * DN_DH] for h in range(DN_HEADS)])


def _delta_step(state, q, k, v, bg, a_row, differentiable=False, inv_known=None):
    lane = lax.broadcasted_iota(jnp.int32, bg.shape, 1)
    pick = lambda j: jnp.stack([jnp.sum(jnp.where(lane == j + h, bg, 0.0), axis=1, keepdims=True)
                                for h in range(DN_HEADS)])
    o, nxt, inv = _delta_chunk(state, _heads(q), _heads(k), _heads(v), pick(0), pick(DN_HEADS), a_row, differentiable,
                               inv_known)
    return jnp.concatenate([o[h] for h in range(DN_HEADS)], axis=1), nxt, inv


def _delta_fwd(qkv, bg, a_row, name):
    S = qkv.shape[0]
    nc = S // BLK

    def body(q_ref, k_ref, v_ref, bg_ref, ar_ref, o_ref, st_ref, inv_ref, state):
        ci = pl.program_id(0)

        @pl.when(ci == 0)
        def _():
            state[...] = jnp.zeros_like(state)

        st = state[...]
        st_ref[:, 0] = st
        o, nxt, inv = _delta_step(st, q_ref[...], k_ref[...], v_ref[...], bg_ref[...], ar_ref[:, pl.ds(ci, 1), :])
        o_ref[...] = o
        inv_ref[:, 0] = inv
        state[...] = nxt

    part = lambda j: pl.BlockSpec((BLK, DN_W), lambda c: (c, j))
    per_chunk = pl.BlockSpec((DN_HEADS, 1, DN_DH, DN_DH), lambda c: (0, c, 0, 0))
    mats = jax.ShapeDtypeStruct((DN_HEADS, nc, DN_DH, DN_DH), F32)
    return pl.pallas_call(
        body, name=name, grid=(nc,),
        in_specs=[part(0), part(1), part(2), pl.BlockSpec((BLK, LANES), lambda c: (c, 0)),
                  pl.BlockSpec((DN_HEADS, nc, BLK), lambda c: (0, 0, 0))],
        out_specs=[part(0), per_chunk, per_chunk], out_shape=[jax.ShapeDtypeStruct((S, DN_W), F32), mats, mats],
        scratch_shapes=[pltpu.VMEM((DN_HEADS, DN_DH, DN_DH), F32)],
        compiler_params=_cp("arbitrary"),
    )(qkv, qkv, qkv, bg, a_row)


def _delta_bwd(qkv, bg, a_row, states, invs, do, name):
    S = qkv.shape[0]
    nc = S // BLK

    def body(q_ref, k_ref, v_ref, bg_ref, ar_ref, st_ref, inv_ref, do_ref, dqkv_ref, dbg_ref, dar_ref, dstate):
        t = pl.program_id(0)
        ci = nc - 1 - t

        @pl.when(t == 0)
        def _():
            dstate[...] = jnp.zeros_like(dstate)

        step = lambda *a: _delta_step(*a, differentiable=True, inv_known=inv_ref[:, 0])[:2]
        _, vjp = jax.vjp(step, st_ref[:, 0], q_ref[...], k_ref[...], v_ref[...], bg_ref[...],
                         ar_ref[:, pl.ds(ci, 1), :])
        dprev, dq, dk, dv, dbg, dar = vjp((do_ref[...], dstate[...]))
        dqkv_ref[:, 0:DN_W] = dq
        dqkv_ref[:, DN_W:2 * DN_W] = dk
        dqkv_ref[:, 2 * DN_W:3 * DN_W] = dv
        dbg_ref[...] = dbg
        dar_ref[:, pl.ds(ci, 1), :] = dar
        dstate[...] = dprev

    part = lambda j: pl.BlockSpec((BLK, DN_W), lambda t: (nc - 1 - t, j))
    lanes = pl.BlockSpec((BLK, LANES), lambda t: (nc - 1 - t, 0))
    rows = pl.BlockSpec((DN_HEADS, nc, BLK), lambda t: (0, 0, 0))
    per_chunk = pl.BlockSpec((DN_HEADS, 1, DN_DH, DN_DH), lambda t: (0, nc - 1 - t, 0, 0))
    return pl.pallas_call(
        body, name=name, grid=(nc,),
        in_specs=[part(0), part(1), part(2), lanes, rows, per_chunk, per_chunk, part(0)],
        out_specs=[pl.BlockSpec((BLK, 3 * DN_W), lambda t: (nc - 1 - t, 0)), lanes, rows],
        out_shape=[jax.ShapeDtypeStruct((S, 3 * DN_W), F32), jax.ShapeDtypeStruct((S, LANES), F32),
                   jax.ShapeDtypeStruct((DN_HEADS, nc, BLK), F32)],
        scratch_shapes=[pltpu.VMEM((DN_HEADS, DN_DH, DN_DH), F32)],
        compiler_params=_cp("arbitrary"),
    )(qkv, qkv, qkv, bg, a_row, states, invs, do)


def _gate_sb(o, z):
    return o * _silu(z)


def _gate_dn(o, z, g):
    return jnp.concatenate(
        [_rms(o[:, h * DN_DH:(h + 1) * DN_DH]) * g * _silu(z[:, h * DN_DH:(h + 1) * DN_DH]) for h in range(DN_HEADS)],
        axis=1)


def _merge_specs(S, D, tm):
    row = lambda w, blk: pl.BlockSpec((tm, w), lambda i: (i, blk))
    full = lambda a, b: pl.BlockSpec((a, b), lambda i: (0, 0))
    return [row(D, 0), full(1, D), row(SB_W, 0), row(SB_W, 3), row(DN_W, 0), row(DN_W, 7),
            row(D, IN_MAIN // D), row(D, IN_MAIN // D + 1), full(1, DN_DH), full(SB_W, D), full(DN_W, D), full(D, D)]


def _merge_fwd(x, gate, o_sb, o_dn, p, ng, wbs, wbd, wo, name):
    S, D = x.shape
    tm = min(512, S)

    def body(x_ref, gate_ref, osb_ref, zsb_ref, odn_ref, zdn_ref, msb_ref, mdn_ref, ng_ref, wbs_ref, wbd_ref, wo_ref,
             out_ref):
        a = _gate_sb(osb_ref[...], zsb_ref[...])
        b = _gate_dn(odn_ref[...], zdn_ref[...], ng_ref[...])
        y = _sigmoid(msb_ref[...]) * _bdot(a, wbs_ref[...]) + _sigmoid(mdn_ref[...]) * _bdot(b, wbd_ref[...])
        out_ref[...] = x_ref[...] + gate_ref[...] * _bdot(y, wo_ref[...])

    return pl.pallas_call(
        body, name=name, grid=(S // tm,), in_specs=_merge_specs(S, D, tm),
        out_specs=pl.BlockSpec((tm, D), lambda i: (i, 0)), out_shape=jax.ShapeDtypeStruct((S, D), F32),
        compiler_params=_cp("parallel"),
    )(x, gate, o_sb, p, o_dn, p, p, p, ng, wbs, wbd, wo)


def _merge_bwd(dxn, gate, o_sb, o_dn, p, ng, wbs, wbd, wo, name):
    S, D = dxn.shape
    tm = min(256, S)

    def body(dxn_ref, gate_ref, osb_ref, zsb_ref, odn_ref, zdn_ref, msb_ref, mdn_ref, ng_ref, wbs_ref, wbd_ref, wo_ref,
             dosb_ref, dzsb_ref, dodn_ref, dzdn_ref, dmsb_ref, dmdn_ref, dwo_ref, dwbs_ref, dwbd_ref, dgate_ref, dng_ref):
        @pl.when(pl.program_id(0) == 0)
        def _():
            for ref in (dwo_ref, dwbs_ref, dwbd_ref, dgate_ref, dng_ref):
                ref[...] = jnp.zeros_like(ref)

        a, vjp_a = jax.vjp(_gate_sb, osb_ref[...], zsb_ref[...])
        b, vjp_b = jax.vjp(_gate_dn, odn_ref[...], zdn_ref[...], ng_ref[...])
        a16, b16 = a.astype(BF16), b.astype(BF16)
        ps = jnp.dot(a16, wbs_ref[...], preferred_element_type=F32)
        pd = jnp.dot(b16, wbd_ref[...], preferred_element_type=F32)
        ss, sd = _sigmoid(msb_ref[...]), _sigmoid(mdn_ref[...])
        y16 = (ss * ps + sd * pd).astype(BF16)
        out = jnp.dot(y16, wo_ref[...], preferred_element_type=F32)
        dxn_v = dxn_ref[...]
        dgate_ref[...] += jnp.sum(dxn_v * out, axis=0, keepdims=True)
        dout16 = (dxn_v * gate_ref[...]).astype(BF16)
        dwo_ref[...] += _bdot_tn(y16, dout16)
        dy = _bdot_nt(dout16, wo_ref[...])
        dmsb_ref[...] = (dy * ps * ss * (1.0 - ss)).astype(BF16)
        dmdn_ref[...] = (dy * pd * sd * (1.0 - sd)).astype(BF16)
        dps16, dpd16 = (dy * ss).astype(BF16), (dy * sd).astype(BF16)
        dwbs_ref[...] += _bdot_tn(a16, dps16)
        dwbd_ref[...] += _bdot_tn(b16, dpd16)
        dosb, dzsb = vjp_a(_bdot_nt(dps16, wbs_ref[...]))
        dodn, dzdn, dng = vjp_b(_bdot_nt(dpd16, wbd_ref[...]))
        dosb_ref[...] = dosb
        dzsb_ref[...] = dzsb.astype(BF16)
        dodn_ref[...] = dodn
        dzdn_ref[...] = dzdn.astype(BF16)
        dng_ref[...] += dng

    row = lambda w: pl.BlockSpec((tm, w), lambda i: (i, 0))
    full = lambda a, b: pl.BlockSpec((a, b), lambda i: (0, 0))
    sds = jax.ShapeDtypeStruct
    return pl.pallas_call(
        body, name=name, grid=(S // tm,), in_specs=_merge_specs(S, D, tm),
        out_specs=[row(SB_W), row(SB_W), row(DN_W), row(DN_W), row(D), row(D),
                   full(D, D), full(SB_W, D), full(DN_W, D), full(1, D), full(1, DN_DH)],
        out_shape=[sds((S, SB_W), F32), sds((S, SB_W), BF16), sds((S, DN_W), F32), sds((S, DN_W), BF16),
                   sds((S, D), BF16), sds((S, D), BF16), sds((D, D), F32), sds((SB_W, D), F32), sds((DN_W, D), F32),
                   sds((1, D), F32), sds((1, DN_DH), F32)],
        compiler_params=_cp("arbitrary"),
    )(dxn, gate, o_sb, p, o_dn, p, p, p, ng, wbs, wbd, wo)


def _loss_fwd_bwd(y, target, name):
    S, D = y.shape
    tm = min(512, S)

    def body(y_ref, t_ref, l_ref, dy_ref):
        @pl.when(pl.program_id(0) == 0)
        def _():
            l_ref[...] = jnp.zeros_like(l_ref)

        e = y_ref[...] - t_ref[...]
        l_ref[...] += jnp.sum(e * e, axis=0, keepdims=True) * (0.5 / D)
        dy_ref[...] = e * (1.0 / D)

    row = pl.BlockSpec((tm, D), lambda i: (i, 0))
    return pl.pallas_call(
        body, name=name, grid=(S // tm,), in_specs=[row, row],
        out_specs=[pl.BlockSpec((1, D), lambda i: (0, 0)), row],
        out_shape=[jax.ShapeDtypeStruct((1, D), F32), jax.ShapeDtypeStruct((S, D), F32)],
        compiler_params=_cp("arbitrary"),
    )(y, target)


def _mod_fwd(c_all, ada_w, name):
    L, D, n = ada_w.shape

    def body(c_ref, w_ref, o_ref):
        o_ref[0] = _dot(_silu(c_ref[...]), w_ref[0])

    return pl.pallas_call(
        body, name=name, grid=(L,),
        in_specs=[pl.BlockSpec(c_all.shape, lambda l: (0, 0)), pl.BlockSpec((1, D, n), lambda l: (l, 0, 0))],
        out_specs=pl.BlockSpec((1, N_DEV, n), lambda l: (l, 0, 0)),
        out_shape=jax.ShapeDtypeStruct((L, N_DEV, n), F32),
        compiler_params=_cp("parallel"),
    )(c_all, ada_w)


def _mod_bwd_w(c_all_t, dmod, name):
    L, _, n = dmod.shape
    D = c_all_t.shape[0]

    def body(c_ref, d_ref, o_ref):
        o_ref[0] = _dot(_silu(c_ref[...]), d_ref[0])

    return pl.pallas_call(
        body, name=name, grid=(L,),
        in_specs=[pl.BlockSpec(c_all_t.shape, lambda l: (0, 0)), pl.BlockSpec((1, N_DEV, n), lambda l: (l, 0, 0))],
        out_specs=pl.BlockSpec((1, D, n), lambda l: (l, 0, 0)),
        out_shape=jax.ShapeDtypeStruct((L, D, n), F32),
        compiler_params=_cp("parallel"),
    )(c_all_t, dmod)


def _me():
    return lax.axis_index("x"), lax.axis_index("y"), lax.axis_index("c")


def _peer(k):
    x, y, c = _me()
    return (1 - x if k & 4 else x, 1 - y if k & 2 else y, 1 - c if k & 1 else c)


def _lin(dev):
    return 4 * dev[0] + 2 * dev[1] + dev[2]


class _Exchange:
    def __init__(self, arrays, scatter):
        self.n = len(arrays)
        self.scatter = scatter
        self.out_shape = [jax.ShapeDtypeStruct((N_DEV,) + tuple(a.shape[1:] if scatter else a.shape), a.dtype)
                          for a in arrays]
        self.specs = [pl.BlockSpec(memory_space=pl.ANY)] * self.n
        self.scratch = [pltpu.SemaphoreType.DMA((self.n, N_DEV - 1)), pltpu.SemaphoreType.DMA((self.n, N_DEV - 1)),
                        pltpu.SemaphoreType.DMA((self.n,))]

    def _copies(self, ins, outs, sems):
        send_sems, recv_sems, local_sems = sems
        me = _lin(_me())
        local, direct, passed, landed = [], [], [], []
        for t in range(self.n):
            src_of = (lambda d, t=t: ins[t].at[d]) if self.scatter else (lambda d, t=t: ins[t])
            local.append(pltpu.make_async_copy(src_of(me), outs[t].at[me], local_sems.at[t]))
            for k in range(1, N_DEV):
                peer = _peer(k)
                pair = dict(send_sem=send_sems.at[t, k - 1], recv_sem=recv_sems.at[t, k - 1],
                            device_id_type=pl.DeviceIdType.MESH)
                slot = outs[t].at[_lin(peer)]
                landed.append(pltpu.make_async_remote_copy(src_ref=slot, dst_ref=slot, device_id=peer, **pair))
                if self.scatter or k in (1, 2, 4, 6):
                    direct.append(pltpu.make_async_remote_copy(src_ref=src_of(_lin(peer)), dst_ref=outs[t].at[me],
                                                               device_id=peer, **pair))
                else:
                    came = outs[t].at[_lin(_peer(k - 1))]
                    passed.append((landed[-2], pltpu.make_async_remote_copy(src_ref=came, dst_ref=came,
                                                                            device_id=_peer(1), **pair)))
        return local, direct, passed, landed

    def start(self, ins, outs, sems):
        local, direct, _, _ = self._copies(ins, outs, sems)
        for cp in local + direct:
            cp.start()

    def finish(self, ins, outs, sems):
        local, direct, passed, landed = self._copies(ins, outs, sems)
        arrived = set()
        for came, onward in passed:
            came.wait_recv()
            arrived.add(id(came))
            onward.start()
        for cp in landed:
            if id(cp) not in arrived:
                cp.wait_recv()
        for cp in direct + [onward for _, onward in passed]:
            cp.wait_send()
        for cp in local:
            cp.wait()


def _exchange(arrays, scatter, name):
    ex = _Exchange(arrays, scatter)

    def body(*refs):
        ins, outs, sems = refs[:ex.n], refs[ex.n:2 * ex.n], refs[2 * ex.n:]
        ex.start(ins, outs, sems)
        ex.finish(ins, outs, sems)

    return pl.pallas_call(body, name=name, in_specs=ex.specs, out_specs=ex.specs, out_shape=ex.out_shape,
                          scratch_shapes=ex.scratch)(*arrays)


def _sum_slots(a, name):
    _, R, C = a.shape
    tr = SUM_ROWS if R % SUM_ROWS == 0 else R

    def body(a_ref, o_ref):
        acc = a_ref[0].astype(F32)
        for s in range(1, N_DEV):
            acc = acc + a_ref[s].astype(F32)
        o_ref[...] = acc

    return pl.pallas_call(
        body, name=name, grid=(R // tr,),
        in_specs=[pl.BlockSpec((N_DEV, tr, C), lambda i: (0, i, 0))], out_specs=pl.BlockSpec((tr, C), lambda i: (i, 0)),
        out_shape=jax.ShapeDtypeStruct((R, C), F32), compiler_params=_cp("parallel"),
    )(a)


def _adamw(w, g, m, v, name):
    shape = w.shape
    C = shape[-1]
    R = w.size // C
    tr = R
    for cand in (256, 128, 64):
        if R > cand and R % cand == 0:
            tr = cand
            break
    c1 = 1.0 / (1.0 - ADAM_B1 ** ADAM_STEP)
    c2 = 1.0 / (1.0 - ADAM_B2 ** ADAM_STEP)

    def body(w_ref, g_ref, m_ref, v_ref, d_ref, nm_ref, nv_ref):
        gv = g_ref[...]
        nm = ADAM_B1 * m_ref[...] + (1.0 - ADAM_B1) * gv
        nv = ADAM_B2 * v_ref[...] + (1.0 - ADAM_B2) * (gv * gv)
        d_ref[...] = -ADAM_LR * ((nm * c1) / (jnp.sqrt(nv * c2) + ADAM_EPS) + ADAM_WD * w_ref[...])
        nm_ref[...] = nm
        nv_ref[...] = nv

    spec = pl.BlockSpec((tr, C), lambda i: (i, 0))
    sd = jax.ShapeDtypeStruct((R, C), F32)
    outs = pl.pallas_call(
        body, name=name, grid=(R // tr,), in_specs=[spec] * 4, out_specs=[spec] * 3, out_shape=[sd] * 3,
        compiler_params=_cp("parallel"),
    )(*(t.reshape(R, C) for t in (w, g, m, v)))
    return tuple(t.reshape(shape) for t in outs)


def _col_segments(D):
    return [(0, IN_MAIN, 0), (IN_COLS, IN_COLS + 2 * D, IN_MAIN), (IN_MAIN, IN_COLS, IN_MAIN + 2 * D)]


def _pad_cols_of_blocks(wi):
    _, D, n = wi.shape
    pieces = []
    for lo, hi, _ in _col_segments(D):
        for d in range(N_DEV):
            a, b = max(lo, d * n), min(hi, (d + 1) * n)
            if a < b:
                pieces.append(wi[d][:, a - d * n:b - d * n])
    return jnp.concatenate(pieces + [jnp.zeros((D, LANES - 2 * DN_HEADS), wi.dtype)], axis=1)


def _blocks_of_padded(dw, n):
    D = dw.shape[0]
    out = []
    for d in range(N_DEV):
        pieces = []
        for lo, hi, at in sorted(_col_segments(D)):
            a, b = max(lo, d * n), min(hi, (d + 1) * n)
            if a < b:
                pieces.append(dw[:, at + a - lo:at + b - lo])
        out.append(pieces[0] if len(pieces) == 1 else jnp.concatenate(pieces, axis=1))
    return jnp.stack(out)


def _gate_params(a_log, dt_bias):
    z = jnp.zeros((LANES,), F32)
    return jnp.stack([z.at[DN_HEADS:2 * DN_HEADS].set(a_log), z.at[DN_HEADS:2 * DN_HEADS].set(dt_bias)])


def _layer_fwd(l, x, mod, wts, carry=None):
    S, D = x.shape
    tag = f"l{l}_"
    p, h = _inproj_fwd(x, mod, wts["norm_g"], wts["w_in"], tag + "inproj_fwd")
    (o_sb,), carried = _sb_fwd(p, wts["sb_q_g"], wts["sb_k_g"].T, tag + "sb_fwd", carry)
    qkv = _dn_prep_fwd(p, wts["conv_w"], 4 * SB_W, tag + "dn_prep_fwd")
    pv = _gate_params(wts["dn_a_log"], wts["dn_dt_bias"])
    ba_blk = (IN_MAIN + 2 * D) // LANES
    bg = _dn_gate_fwd(p, pv, ba_blk, tag + "dn_gate_fwd")
    a_row = bg[:, DN_HEADS:2 * DN_HEADS].T.reshape(DN_HEADS, S // BLK, BLK)
    o_dn, states, invs = _delta_fwd(qkv, bg, a_row, tag + "delta_fwd")
    gate = mod[:, 2 * D:]
    out = _merge_fwd(x, gate, o_sb, o_dn, p, wts["dn_norm_g"], wts["w_branch_sb"], wts["w_branch_dn"], wts["w_out"],
                     tag + "merge_fwd")
    saved = dict(x=x, mod=mod, p=p, h=h, o_sb=o_sb, qkv=qkv, pv=pv, bg=bg, a_row=a_row, o_dn=o_dn, states=states,
                 invs=invs, gate=gate)
    return out, saved, carried


def _layer_bwd(l, dxn, sv, wts, carry_of=None, late_carry_of=None):
    S, D = dxn.shape
    tag = f"l{l}_"
    (dosb, dzsb, dodn, dzdn, dmsb, dmdn, dwo, dwbs, dwbd, dgate, dng) = _merge_bwd(
        dxn, sv["gate"], sv["o_sb"], sv["o_dn"], sv["p"], wts["dn_norm_g"], wts["w_branch_sb"], wts["w_branch_dn"],
        wts["w_out"], tag + "merge_bwd")
    carry = None if carry_of is None else carry_of(dict(w_out=dwo, w_branch_sb=dwbs, w_branch_dn=dwbd))
    (dq, dk, dv, dgq, dgkt), carried = _sb_bwd(sv["p"], sv["o_sb"], dosb, wts["sb_q_g"], wts["sb_q_g"].T,
                                                wts["sb_k_g"], wts["sb_k_g"].T, tag + "sb_bwd", carry)
    dqkv_n, dbg, dar = _delta_bwd(sv["qkv"], sv["bg"], sv["a_row"], sv["states"], sv["invs"], dodn, tag + "delta_bwd")
    dqkv, dconv = _dn_prep_bwd(sv["p"], wts["conv_w"], 4 * SB_W, dqkv_n, tag + "dn_prep_bwd")
    dbg = dbg.at[:, DN_HEADS:2 * DN_HEADS].add(dar.reshape(DN_HEADS, S).T)
    ba_blk = (IN_MAIN + 2 * D) // LANES
    dba, dpv = _dn_gate_bwd(sv["p"], sv["pv"], ba_blk, dbg, tag + "dn_gate_bwd")
    dp = jnp.concatenate([dq, dk, dv, dzsb, dqkv, dzdn, dmsb, dmdn, dba], axis=1)
    dw_in = _matmul_tn(sv["h"].T, dp, tag + "inproj_bwd_dw")
    late = None if late_carry_of is None else late_carry_of(dict(w_in=dw_in, conv_w=dconv))
    (dx, dmod, dg), carried_late = _inproj_bwd_dx(dp, wts["w_in"], sv["x"], sv["mod"], wts["norm_g"], dxn,
                                                  tag + "inproj_bwd_dx", late)
    dmod = dmod.at[:, 2 * D:].set(dgate)
    grads = dict(w_in=dw_in, w_branch_sb=dwbs, w_branch_dn=dwbd, w_out=dwo, conv_w=dconv,
                 mod=dmod[0], norm_g=dg[0], sb_q_g=jnp.sum(dgq, axis=0)[0], sb_k_g=jnp.sum(dgkt, axis=0)[:, 0],
                 dn_a_log=dpv[0, DN_HEADS:2 * DN_HEADS], dn_dt_bias=dpv[1, DN_HEADS:2 * DN_HEADS], dn_norm_g=dng[0])
    return dx, grads, carried, carried_late


def _pad_rows(a, mult):
    extra = (-a.shape[0]) % mult
    return a if extra == 0 else jnp.concatenate([a, jnp.zeros((extra,) + a.shape[1:], a.dtype)], axis=0)


def _pack_rows(parts, width, mult):
    flat = jnp.concatenate([t.reshape(-1) for t in parts])
    extra = (-flat.shape[0]) % width
    if extra:
        flat = jnp.concatenate([flat, jnp.zeros((extra,), flat.dtype)])
    return _pad_rows(flat.reshape(-1, width), mult)


def _take(flat, off, shape):
    n = math.prod(shape)
    return flat[..., off:off + n].reshape(flat.shape[:-1] + tuple(shape)), off + n


SMALL = ("mod", "norm_g", "sb_q_g", "sb_k_g", "dn_a_log", "dn_dt_bias", "dn_norm_g")


def kernel(x, c, ada_w, ada_b, norm_g, w_in, sb_q_g, sb_k_g, conv_w, dn_a_log, dn_dt_bias, dn_norm_g, w_branch_sb, w_branch_dn, w_out, loss_target, m_ada_w, m_ada_b, m_norm_g, m_w_in, m_sb_q_g, m_sb_k_g, m_conv_w, m_dn_a_log, m_dn_dt_bias, m_dn_norm_g, m_w_branch_sb, m_w_branch_dn, m_w_out, v_ada_w, v_ada_b, v_norm_g, v_w_in, v_sb_q_g, v_sb_k_g, v_conv_w, v_dn_a_log, v_dn_dt_bias, v_dn_norm_g, v_w_branch_sb, v_w_branch_dn, v_w_out):
    L, D = norm_g.shape
    S = x.shape[1]
    n_in = w_in.shape[2]
    n_ada = ada_w.shape[2]
    n_br = w_branch_sb.shape[2]
    n_out = w_out.shape[1]
    n_conv = conv_w.shape[2]
    me = _lin(_me())

    def cat(a):
        return a.transpose(1, 0, 2).reshape(a.shape[1], N_DEV * a.shape[2])

    c_all, conv_all = _exchange([c, conv_w.reshape(L * CONV_K, n_conv)], False, "gather_small")
    c_all = c_all.reshape(N_DEV, D)
    conv_full = cat(conv_all).reshape(L, CONV_K, N_DEV * n_conv)

    mod_part = _mod_fwd(c_all, ada_w, "mod_fwd")

    def shards16(l):
        return [w_in[l].astype(BF16), w_branch_sb[l].astype(BF16), w_branch_dn[l].astype(BF16), w_out[l].astype(BF16)]

    def whole(l, got):
        wi, wbs, wbd, wo = got
        return dict(norm_g=norm_g[l:l + 1], w_in=_pad_cols_of_blocks(wi), sb_q_g=sb_q_g[l:l + 1], sb_k_g=sb_k_g[l:l + 1],
                    conv_w=conv_full[l], dn_a_log=dn_a_log[l], dn_dt_bias=dn_dt_bias[l], dn_norm_g=dn_norm_g[l:l + 1],
                    w_branch_sb=cat(wbs), w_branch_dn=cat(wbd), w_out=wo.reshape(N_DEV * n_out, D))

    *got, mod_all = _exchange(shards16(0) + [mod_part.reshape(L * N_DEV, n_ada)], False, "gather_weights")
    mod_full = cat(mod_all).reshape(L, N_DEV, N_DEV * n_ada) + ada_b[:, None, :]
    mod_mine = lax.dynamic_slice_in_dim(mod_full, me, 1, axis=1)

    act = x[0]
    saved, wts = [], []
    for l in range(L):
        wts.append(whole(l, got))
        act, sv, got = _layer_fwd(l, act, mod_mine[l], wts[l], (shards16(l + 1), False) if l + 1 < L else None)
        saved.append(sv)
    loss_cols, dact = _loss_fwd_bwd(act, loss_target[0], "loss")
    loss = lax.psum(jnp.sum(loss_cols), ("x", "y", "c"))

    def blocks(name, g):
        if name == "w_out":
            return g.astype(BF16).reshape(N_DEV, n_out, D)
        if name == "w_in":
            return _blocks_of_padded(g, n_in).astype(BF16)
        n = g.shape[1] // N_DEV
        dtype = F32 if name == "conv_w" else BF16
        return g.astype(dtype).reshape(g.shape[0], N_DEV, n).transpose(1, 0, 2)

    early, late = ("w_out", "w_branch_sb", "w_branch_dn"), ("w_in", "conv_w")
    grads, recv, pending = [None] * L, {}, []
    for l in reversed(range(L)):
        keys = [k for k, _ in pending] + [(n, l) for n in early]

        def carry_of(g_early, pending=pending):
            return [a for _, a in pending] + [blocks(n, g_early[n]) for n in early], True

        last = l == 0
        dact, grads[l], got, got_late = _layer_bwd(
            l, dact, saved[l], wts[l], carry_of, (lambda g: ([blocks(n, g[n]) for n in late], True)) if last else None)
        recv.update(zip(keys, got))
        recv.update(zip([(n, l) for n in late], got_late))
        pending = [] if last else [((n, l), blocks(n, grads[l][n])) for n in late]
    grad_x = dact[None]
    small_g = _pack_rows([grads[l][n] for l in range(L) for n in SMALL], LANES, 8)
    (small_all_g,) = _exchange([small_g], False, "gather_small_grads")
    small_sum = _sum_slots(small_all_g, "sum_small_grads").reshape(-1)
    shard_shapes = dict(w_in=w_in.shape, w_branch_sb=w_branch_sb.shape, w_branch_dn=w_branch_dn.shape,
                        conv_w=conv_w.shape, w_out=w_out.shape)
    g_out = {n: jnp.stack([_sum_slots(recv[(n, l)], f"sum_{n}_l{l}").reshape(shape[1:]) for l in range(L)])
             for n, shape in shard_shapes.items()}
    small_shapes = dict(mod=(3 * D,), norm_g=(D,), sb_q_g=(SB_DH,), sb_k_g=(SB_DH,), dn_a_log=(DN_HEADS,),
                        dn_dt_bias=(DN_HEADS,), dn_norm_g=(DN_DH,))
    off = 0
    off_all = 0
    small_each = small_all_g.reshape(N_DEV, -1)
    per_small = {n: [] for n in SMALL}
    dmod_all = []
    for l in range(L):
        for n in SMALL:
            t, off = _take(small_sum, off, small_shapes[n])
            per_small[n].append(t)
            if n == "mod":
                t_all, _ = _take(small_each, off_all, small_shapes[n])
                dmod_all.append(t_all)
            off_all += math.prod(small_shapes[n])
    for n in SMALL:
        g_out[n if n != "mod" else "ada_b"] = jnp.stack(per_small[n])
    dmod_all = jnp.stack(dmod_all)
    dmod_cols = lax.dynamic_slice_in_dim(dmod_all, me * n_ada, n_ada, axis=2)
    g_out["ada_w"] = _mod_bwd_w(c_all.T, dmod_cols, "mod_bwd_w")

    given = dict(ada_w=(ada_w, m_ada_w, v_ada_w), ada_b=(ada_b, m_ada_b, v_ada_b), norm_g=(norm_g, m_norm_g, v_norm_g),
                 w_in=(w_in, m_w_in, v_w_in), sb_q_g=(sb_q_g, m_sb_q_g, v_sb_q_g), sb_k_g=(sb_k_g, m_sb_k_g, v_sb_k_g),
                 conv_w=(conv_w, m_conv_w, v_conv_w), dn_a_log=(dn_a_log, m_dn_a_log, v_dn_a_log),
                 dn_dt_bias=(dn_dt_bias, m_dn_dt_bias, v_dn_dt_bias), dn_norm_g=(dn_norm_g, m_dn_norm_g, v_dn_norm_g),
                 w_branch_sb=(w_branch_sb, m_w_branch_sb, v_w_branch_sb),
                 w_branch_dn=(w_branch_dn, m_w_branch_dn, v_w_branch_dn), w_out=(w_out, m_w_out, v_w_out))
    order = list(given)
    upd = {n: _adamw(given[n][0], g_out[n], given[n][1], given[n][2], "adamw_" + n) for n in order}
    return (loss, grad_x, *[g_out[n] for n in order], *[upd[n][0] for n in order], *[upd[n][1] for n in order],
            *[upd[n][2] for n in order])
```

```python
import functools
import math

import jax
import jax.numpy as jnp
from jax import lax
from jax.experimental import pallas as pl
from jax.experimental.pallas import tpu as pltpu

F32 = jnp.float32
BF16 = jnp.bfloat16
HI = lax.Precision.HIGHEST

N_DEV = 8
EPS = 1e-6
SB_HEADS, SB_DH = 8, 64
DN_HEADS, DN_DH = 4, 128
SB_W = SB_HEADS * SB_DH
DN_W = DN_HEADS * DN_DH
CONV_K = 4
BLK = 128
SB_KEYS = 512
SB_QB = 512
LANES = 128
IN_MAIN = 4 * SB_W + 4 * DN_W
IN_COLS = IN_MAIN + 2 * DN_HEADS
ADAM_LR, ADAM_B1, ADAM_B2, ADAM_EPS, ADAM_WD, ADAM_STEP = 0.001, 0.9, 0.999, 1e-08, 0.01, 10
VMEM_LIMIT = 56 * 1024 * 1024
LOG2E = 1.4426950408889634
SUM_ROWS = 128


def _cp(*sem, vmem=VMEM_LIMIT):
    return pltpu.CompilerParams(dimension_semantics=sem if sem else None, vmem_limit_bytes=vmem)


def _dot(a, b, prec=HI):
    return lax.dot_general(a, b, (((1,), (0,)), ((), ())), precision=prec, preferred_element_type=F32)


def _dot_nt(a, b, prec=HI):
    return lax.dot_general(a, b, (((1,), (1,)), ((), ())), precision=prec, preferred_element_type=F32)


def _bdot(a, b):
    return lax.dot_general(a.astype(BF16), b.astype(BF16), (((1,), (0,)), ((), ())), preferred_element_type=F32)


def _bdot_nt(a, b):
    return lax.dot_general(a.astype(BF16), b.astype(BF16), (((1,), (1,)), ((), ())), preferred_element_type=F32)


def _bdot_tn(a, b):
    return lax.dot_general(a.astype(BF16), b.astype(BF16), (((0,), (0,)), ((), ())), preferred_element_type=F32)


def _split_dot(a, b01_twice):
    hi = a.astype(BF16)
    lo = (a - hi.astype(F32)).astype(BF16)
    return jnp.dot(jnp.concatenate([hi, lo], axis=1), b01_twice, preferred_element_type=F32)


def _sigmoid(x):
    return 1.0 / (1.0 + jnp.exp(-x))


def _silu(x):
    return x * _sigmoid(x)


def _softplus(x):
    return jnp.maximum(x, 0.0) + jnp.log(1.0 + jnp.exp(-jnp.abs(x)))


def _rms(x):
    return x * lax.rsqrt(jnp.mean(x * x, axis=-1, keepdims=True) + EPS)


def _prenorm(x, g, shift, scale):
    return _rms(x) * g * (1.0 + scale) + shift


def _inproj_fwd(x, mod, g, w, name):
    S, D = x.shape
    N = w.shape[1]
    tm = min(512, S)
    tn = 896 if N % 896 == 0 else 128

    def body(x_ref, mod_ref, g_ref, w_ref, p_ref, h_ref):
        @pl.when(pl.program_id(1) == 0)
        def _():
            h = _prenorm(x_ref[...], g_ref[...], mod_ref[:, 0:D], mod_ref[:, D:2 * D])
            h_ref[...] = h.astype(BF16)

        p_ref[...] = jnp.dot(h_ref[...], w_ref[...], preferred_element_type=F32)

    return pl.pallas_call(
        body, name=name, grid=(S // tm, N // tn),
        in_specs=[pl.BlockSpec((tm, D), lambda i, j: (i, 0)), pl.BlockSpec((1, 3 * D), lambda i, j: (0, 0)),
                  pl.BlockSpec((1, D), lambda i, j: (0, 0)), pl.BlockSpec((D, tn), lambda i, j: (0, j))],
        out_specs=[pl.BlockSpec((tm, tn), lambda i, j: (i, j)), pl.BlockSpec((tm, D), lambda i, j: (i, 0))],
        out_shape=[jax.ShapeDtypeStruct((S, N), F32), jax.ShapeDtypeStruct((S, D), BF16)],
        compiler_params=_cp("parallel", "arbitrary"),
    )(x, mod, g, w)


def _inproj_bwd_dx(dp, w, x, mod, g, dxn, name, carry=None):
    S, N = dp.shape
    D = x.shape[1]
    tm = min(512, S)
    tk = 896 if N % 896 == 0 else 128
    nk = N // tk

    def body(dp_ref, w_ref, x_ref, mod_ref, g_ref, dxn_ref, dx_ref, dmod_ref, dg_ref, acc):
        i, k = pl.program_id(0), pl.program_id(1)

        @pl.when(k == 0)
        def _():
            acc[...] = jnp.zeros_like(acc)

        @pl.when((i == 0) & (k == 0))
        def _():
            dmod_ref[...] = jnp.zeros_like(dmod_ref)
            dg_ref[...] = jnp.zeros_like(dg_ref)

        acc[...] += lax.dot_general(dp_ref[...], w_ref[...], (((1,), (1,)), ((), ())), preferred_element_type=F32)

        @pl.when(k == nk - 1)
        def _():
            _, vjp = jax.vjp(_prenorm, x_ref[...], g_ref[...], mod_ref[:, 0:D], mod_ref[:, D:2 * D])
            dx, dg, dshift, dscale = vjp(acc[...])
            dx_ref[...] = dxn_ref[...] + dx
            dg_ref[...] += dg
            dmod_ref[:, 0:D] += dshift
            dmod_ref[:, D:2 * D] += dscale

    return _call_carrying(
        body, name, (S // tm, nk), carry, [dp, w, x, mod, g, dxn],
        in_specs=[pl.BlockSpec((tm, tk), lambda i, k: (i, k)), pl.BlockSpec((D, tk), lambda i, k: (0, k)),
                  pl.BlockSpec((tm, D), lambda i, k: (i, 0)), pl.BlockSpec((1, 3 * D), lambda i, k: (0, 0)),
                  pl.BlockSpec((1, D), lambda i, k: (0, 0)), pl.BlockSpec((tm, D), lambda i, k: (i, 0))],
        out_specs=[pl.BlockSpec((tm, D), lambda i, k: (i, 0)), pl.BlockSpec((1, 3 * D), lambda i, k: (0, 0)),
                   pl.BlockSpec((1, D), lambda i, k: (0, 0))],
        out_shape=[jax.ShapeDtypeStruct((S, D), F32), jax.ShapeDtypeStruct((1, 3 * D), F32),
                   jax.ShapeDtypeStruct((1, D), F32)],
        scratch_shapes=[pltpu.VMEM((tm, D), F32)], vmem=VMEM_LIMIT)


def _matmul_tn(a_t, b, name):
    M, K = a_t.shape
    N = b.shape[1]
    tn = 896 if N % 896 == 0 else (512 if N % 512 == 0 else 128)
    tk = min(512, K)
    nk = K // tk

    def body(a_ref, b_ref, o_ref):
        @pl.when(pl.program_id(1) == 0)
        def _():
            o_ref[...] = jnp.zeros_like(o_ref)

        o_ref[...] += jnp.dot(a_ref[...], b_ref[...], preferred_element_type=F32)

    return pl.pallas_call(
        body, name=name, grid=(N // tn, nk),
        in_specs=[pl.BlockSpec((M, tk), lambda j, k: (0, k)), pl.BlockSpec((tk, tn), lambda j, k: (k, j))],
        out_specs=pl.BlockSpec((M, tn), lambda j, k: (0, j)),
        out_shape=jax.ShapeDtypeStruct((M, N), F32),
        compiler_params=_cp("parallel", "arbitrary"),
    )(a_t, b)


def _qk_norm(t, g, scale):
    return _rms(t) * g * scale


def _qk_norm_t(t, g_col, scale):
    return t * lax.rsqrt(jnp.mean(t * t, axis=0, keepdims=True) + EPS) * g_col * scale


def _suffix_sums(x, tri):
    half = tri.shape[1]
    lo, hi = x[:, :half], x[:, half:]
    hi_sum = jnp.sum(hi, axis=1, keepdims=True)
    y = jnp.concatenate([_split_dot(lo, tri) + hi_sum, _split_dot(hi, tri)], axis=1)
    return y, hi_sum + jnp.sum(lo, axis=1, keepdims=True)


def _sb_step(qi, kat_blk, cl, from_here, valid):
    z = jnp.dot(qi, kat_blk, preferred_element_type=F32)
    nz = -z
    lk = jnp.minimum(nz, 0.0) - jnp.log(1.0 + jnp.exp2(jnp.minimum(z, nz))) * LOG2E
    if valid is not None:
        lk = jnp.where(valid, lk, 0.0)
    later, tot = _suffix_sums(lk, from_here)
    w = jnp.exp2(z + later + cl)
    if valid is not None:
        w = jnp.where(valid, w, 0.0)
    return z, lk, w, tot


def _sb_masks(kb):
    half = kb // 2
    r = lax.broadcasted_iota(jnp.int32, (half, half), 0)
    c = lax.broadcasted_iota(jnp.int32, (half, half), 1)
    rq = lax.broadcasted_iota(jnp.int32, (SB_QB, kb), 0)
    ck = lax.broadcasted_iota(jnp.int32, (SB_QB, kb), 1)
    twice = lambda m: jnp.concatenate([m, m], axis=0).astype(BF16)
    return twice((r >= c).astype(F32)), ck - rq


def _sb_fwd(p, gq, gkt, name, carry=None):
    S, dh = p.shape[0], SB_DH
    kb = min(SB_KEYS, S)
    per = kb // SB_QB
    nb = S // SB_QB
    scale = 1.0 / math.sqrt(dh)
    pairs = SB_W // LANES

    def body(q_ref, k_ref, v_ref, gq_ref, gkt_ref, o_ref, kt2, qa, kat, vb):
        kt2[...] = k_ref[...].T
        from_here, diff = _sb_masks(kb)
        for hh in range(LANES // dh):
            lanes = slice(hh * dh, (hh + 1) * dh)
            qa[...] = _qk_norm(q_ref[:, lanes], gq_ref[...], scale * LOG2E).astype(BF16)
            kat[...] = _qk_norm_t(kt2[lanes, :], gkt_ref[...], 1.0).astype(BF16)
            vb[...] = v_ref[:, lanes].astype(BF16)

            def qblock(i, _):
                rows = pl.ds(pl.multiple_of(i * SB_QB, SB_QB), SB_QB)
                qi = qa[rows, :]
                sbd = i // per

                def step(sb, cl, acc, valid):
                    cols = pl.ds(pl.multiple_of(sb * kb, kb), kb)
                    _, _, w, tot = _sb_step(qi, kat[:, cols], cl, from_here, valid)
                    return cl + tot, acc + jnp.dot(w.astype(BF16), vb[cols, :], preferred_element_type=F32)

                cl, acc = step(sbd, jnp.zeros((SB_QB, 1), F32), jnp.zeros((SB_QB, dh), F32),
                               diff < (i - sbd * per) * SB_QB)
                _, acc = lax.fori_loop(0, sbd, lambda jj, c: step(sbd - 1 - jj, c[0], c[1], None), (cl, acc))
                o_ref[rows, lanes] = acc
                return 0

            lax.fori_loop(0, nb, qblock, 0)

    blk = lambda off: pl.BlockSpec((S, LANES), lambda g: (0, off + g))
    return _call_carrying(
        body, name, pairs, carry, [p, p, p, gq, gkt],
        in_specs=[blk(0), blk(pairs), blk(2 * pairs), pl.BlockSpec((1, dh), lambda g: (0, 0)),
                  pl.BlockSpec((dh, 1), lambda g: (0, 0))],
        out_specs=[blk(0)], out_shape=[jax.ShapeDtypeStruct((S, SB_W), F32)],
        scratch_shapes=[pltpu.VMEM((LANES, S), F32), pltpu.VMEM((S, dh), BF16), pltpu.VMEM((dh, S), BF16),
                        pltpu.VMEM((S, dh), BF16)],
        vmem=VMEM_LIMIT)


def _call_carrying(body, name, grid, carry, operands, in_specs, out_specs, out_shape, scratch_shapes, vmem):
    grid = (grid,) if isinstance(grid, int) else tuple(grid)
    if carry is None:
        res = pl.pallas_call(body, name=name, grid=grid, in_specs=in_specs, out_specs=out_specs,
                             out_shape=out_shape, scratch_shapes=scratch_shapes,
                             compiler_params=_cp(*["arbitrary"] * len(grid), vmem=vmem))(*operands)
        return res, []

    def at(corner):
        hit = pl.program_id(0) == corner(grid[0])
        for axis in range(1, len(grid)):
            hit = jnp.logical_and(hit, pl.program_id(axis) == corner(grid[axis]))
        return hit

    ex = _Exchange(*carry)
    n_in, n_out, n_scr = len(in_specs), len(out_specs), len(scratch_shapes)

    def wrapped(*refs):
        ins, refs = refs[:n_in], refs[n_in:]
        xin, refs = refs[:ex.n], refs[ex.n:]
        outs, refs = refs[:n_out], refs[n_out:]
        xout, refs = refs[:ex.n], refs[ex.n:]
        scr, sems = refs[:n_scr], refs[n_scr:]

        @pl.when(at(lambda n: 0))
        def _():
            ex.start(xin, xout, sems)

        body(*ins, *outs, *scr)

        @pl.when(at(lambda n: n - 1))
        def _():
            ex.finish(xin, xout, sems)

    res = pl.pallas_call(wrapped, name=name, grid=grid, in_specs=in_specs + ex.specs,
                         out_specs=out_specs + ex.specs, out_shape=out_shape + ex.out_shape,
                         scratch_shapes=scratch_shapes + ex.scratch,
                         compiler_params=_cp(*["arbitrary"] * len(grid), vmem=vmem))(*operands, *carry[0])
    return res[:n_out], res[n_out:]


def _sb_bwd(p, o, do, gq, gqt, gk, gkt, name, carry=None):
    S, dh = p.shape[0], SB_DH
    kb = min(SB_KEYS, S)
    per = kb // SB_QB
    nb = S // SB_QB
    scale = 1.0 / math.sqrt(dh)
    pairs = SB_W // LANES
    per_pair = LANES // dh

    def body(q_ref, k_ref, v_ref, o_ref, do_ref, gq_ref, gqt_ref, gk_ref, gkt_ref,
             dq_ref, dk_ref, dv_ref, dgq_ref, dgkt_ref,
             qt2, kt2, vt2, dot2, dkt2, dvt2, qa, qat, ka, kat, vb, vtb, dob, dotb, dqa):
        qt2[...] = q_ref[...].T
        kt2[...] = k_ref[...].T
        vt2[...] = v_ref[...].T
        dot2[...] = do_ref[...].T
        dkt2[...] = jnp.zeros_like(dkt2)
        dvt2[...] = jnp.zeros_like(dvt2)
        from_here, diff = _sb_masks(kb)
        for hh in range(per_pair):
            lanes = slice(hh * dh, (hh + 1) * dh)
            qa[...] = _qk_norm(q_ref[:, lanes], gq_ref[...], scale * LOG2E).astype(BF16)
            qat[...] = _qk_norm_t(qt2[lanes, :], gqt_ref[...], scale * LOG2E).astype(BF16)
            ka[...] = _qk_norm(k_ref[:, lanes], gk_ref[...], 1.0).astype(BF16)
            kat[...] = _qk_norm_t(kt2[lanes, :], gkt_ref[...], 1.0).astype(BF16)
            vb[...] = v_ref[:, lanes].astype(BF16)
            vtb[...] = vt2[lanes, :].astype(BF16)
            dob[...] = do_ref[:, lanes].astype(BF16)
            dotb[...] = dot2[lanes, :].astype(BF16)

            def qblock(i, _):
                rows = pl.ds(pl.multiple_of(i * SB_QB, SB_QB), SB_QB)
                qi, qit = qa[rows, :], qat[:, rows]
                doi, doit = dob[rows, :], dotb[:, rows]
                total = jnp.sum(doi.astype(F32) * o_ref[rows, lanes], axis=1, keepdims=True)
                sbd = i // per

                def step(sb, cl, cd, dqi, valid):
                    cols = pl.ds(pl.multiple_of(sb * kb, kb), kb)
                    z, lk, w, tot = _sb_step(qi, kat[:, cols], cl, from_here, valid)
                    w16 = w.astype(BF16)
                    dl = jnp.dot(doi, vtb[:, cols], preferred_element_type=F32) * w16.astype(F32)
                    incl, dtot = _suffix_sums(dl, from_here)
                    sig = jnp.exp2(z + lk)
                    dz = dl - sig * (dl + (total - cd - incl))
                    if valid is not None:
                        dz = jnp.where(valid, dz, 0.0)
                    dz16 = dz.astype(BF16)
                    dqi = dqi + jnp.dot(dz16, ka[cols, :], preferred_element_type=F32)
                    dkt2[lanes, cols] += jnp.dot(qit, dz16, preferred_element_type=F32)
                    dvt2[lanes, cols] += jnp.dot(doit, w16, preferred_element_type=F32)
                    return cl + tot, cd + dtot, dqi

                zero = jnp.zeros((SB_QB, 1), F32)
                first = step(sbd, zero, zero, jnp.zeros((SB_QB, dh), F32), diff < (i - sbd * per) * SB_QB)
                _, _, dqi = lax.fori_loop(0, sbd, lambda jj, c: step(sbd - 1 - jj, c[0], c[1], c[2], None), first)
                dqa[rows, :] = dqi
                return 0

            lax.fori_loop(0, nb, qblock, 0)
            _, vq = jax.vjp(lambda t, g: _qk_norm(t, g, scale), q_ref[:, lanes], gq_ref[...])
            dq, dgq = vq(dqa[...])
            dq_ref[:, lanes] = dq.astype(BF16)
            dgq_ref[hh] = dgq
            _, vk = jax.vjp(lambda t, g: _qk_norm_t(t, g, 1.0), kt2[lanes, :], gkt_ref[...])
            dkt, dgkt = vk(dkt2[lanes, :] * (1.0 / LOG2E))
            dkt2[lanes, :] = dkt
            dgkt_ref[hh] = dgkt
        dk_ref[...] = dkt2[...].T.astype(BF16)
        dv_ref[...] = dvt2[...].T.astype(BF16)

    blk = lambda off: pl.BlockSpec((S, LANES), lambda g: (0, off + g))
    once = lambda off: pl.BlockSpec((S, LANES), lambda g: (0, off + g), pipeline_mode=pl.Buffered(1))
    gr = pl.BlockSpec((1, dh), lambda g: (0, 0))
    gc = pl.BlockSpec((dh, 1), lambda g: (0, 0))
    sd = jax.ShapeDtypeStruct((S, SB_W), BF16)
    return _call_carrying(
        body, name, pairs, carry, [p, p, p, o, do, gq, gqt, gk, gkt],
        in_specs=[once(0), once(pairs), once(2 * pairs), once(0), once(0), gr, gc, gr, gc],
        out_specs=[blk(0), blk(0), blk(0), pl.BlockSpec((per_pair, 1, dh), lambda g: (g, 0, 0)),
                   pl.BlockSpec((per_pair, dh, 1), lambda g: (g, 0, 0))],
        out_shape=[sd, sd, sd, jax.ShapeDtypeStruct((SB_HEADS, 1, dh), F32),
                   jax.ShapeDtypeStruct((SB_HEADS, dh, 1), F32)],
        scratch_shapes=[pltpu.VMEM((LANES, S), F32)] * 6 + [pltpu.VMEM((S, dh), BF16), pltpu.VMEM((dh, S), BF16)] * 4
        + [pltpu.VMEM((S, dh), F32)],
        vmem=60 * 1024 * 1024)


def _shift_down(x, s, rows):
    if s == 0:
        return x
    return jnp.where(rows >= s, pltpu.roll(x, s, 0), 0.0)


def _shift_up(x, s, rows, n):
    if s == 0:
        return x
    return jnp.where(rows < n - s, pltpu.roll(x, n - s, 0), 0.0)


def _conv(x, w_ref, rows):
    y = x * w_ref[CONV_K - 1:CONV_K, :]
    for kk in range(CONV_K - 1):
        y = y + _shift_down(x, CONV_K - 1 - kk, rows) * w_ref[kk:kk + 1, :]
    return y


def _act_norm(y, normed):
    s = _silu(y)
    n = s * lax.rsqrt(jnp.sum(s * s, axis=-1, keepdims=True) + EPS)
    return jnp.where(normed, n, s)


def _dn_prep_fwd(p, conv_w, col0, name):
    S = p.shape[0]
    nblk = 3 * DN_HEADS
    b0 = col0 // DN_DH

    def body(x_ref, w_ref, o_ref):
        rows = lax.broadcasted_iota(jnp.int32, (S, DN_DH), 0)
        y = _conv(x_ref[...], w_ref, rows)
        o_ref[...] = _act_norm(y, pl.program_id(0) < 2 * DN_HEADS)

    return pl.pallas_call(
        body, name=name, grid=(nblk,),
        in_specs=[pl.BlockSpec((S, DN_DH), lambda j: (0, b0 + j)), pl.BlockSpec((CONV_K, DN_DH), lambda j: (0, j))],
        out_specs=pl.BlockSpec((S, DN_DH), lambda j: (0, j)),
        out_shape=jax.ShapeDtypeStruct((S, 3 * DN_W), F32),
        compiler_params=_cp("parallel"),
    )(p, conv_w)


def _dn_prep_bwd(p, conv_w, col0, dout, name):
    S = p.shape[0]
    nblk = 3 * DN_HEADS
    b0 = col0 // DN_DH

    def body(x_ref, w_ref, do_ref, dx_ref, dw_ref):
        rows = lax.broadcasted_iota(jnp.int32, (S, DN_DH), 0)
        x = x_ref[...]
        y = _conv(x, w_ref, rows)
        normed = pl.program_id(0) < 2 * DN_HEADS
        _, vjp = jax.vjp(lambda t: _act_norm(t, normed), y)
        (dy,) = vjp(do_ref[...])
        dx = dy * w_ref[CONV_K - 1:CONV_K, :]
        dw_ref[CONV_K - 1:CONV_K, :] = jnp.sum(dy * x, axis=0, keepdims=True)
        for kk in range(CONV_K - 1):
            s = CONV_K - 1 - kk
            dx = dx + _shift_up(dy, s, rows, S) * w_ref[kk:kk + 1, :]
            dw_ref[kk:kk + 1, :] = jnp.sum(dy * _shift_down(x, s, rows), axis=0, keepdims=True)
        dx_ref[...] = dx.astype(BF16)

    return pl.pallas_call(
        body, name=name, grid=(nblk,),
        in_specs=[pl.BlockSpec((S, DN_DH), lambda j: (0, b0 + j)), pl.BlockSpec((CONV_K, DN_DH), lambda j: (0, j)),
                  pl.BlockSpec((S, DN_DH), lambda j: (0, j))],
        out_specs=[pl.BlockSpec((S, DN_DH), lambda j: (0, j)), pl.BlockSpec((CONV_K, DN_DH), lambda j: (0, j))],
        out_shape=[jax.ShapeDtypeStruct((S, 3 * DN_W), BF16), jax.ShapeDtypeStruct((CONV_K, 3 * DN_W), F32)],
        compiler_params=_cp("parallel"),
    )(p, conv_w, dout)


def _gate_fn(x, pv):
    lane = lax.broadcasted_iota(jnp.int32, x.shape, 1)
    decay = -jnp.exp(pv[0:1, :]) * _softplus(x + pv[1:2, :])
    return jnp.where(lane < DN_HEADS, _sigmoid(x), decay)


def _dn_gate_fwd(p, pv, blk, name):
    S = p.shape[0]

    def body(x_ref, pv_ref, o_ref):
        o_ref[...] = _gate_fn(x_ref[...], pv_ref[...])

    return pl.pallas_call(
        body, name=name, grid=(1,),
        in_specs=[pl.BlockSpec((S, LANES), lambda i: (0, blk)), pl.BlockSpec((2, LANES), lambda i: (0, 0))],
        out_specs=pl.BlockSpec((S, LANES), lambda i: (0, 0)),
        out_shape=jax.ShapeDtypeStruct((S, LANES), F32),
        compiler_params=_cp("arbitrary"),
    )(p, pv)


def _dn_gate_bwd(p, pv, blk, dout, name):
    S = p.shape[0]

    def body(x_ref, pv_ref, do_ref, dx_ref, dpv_ref):
        _, vjp = jax.vjp(_gate_fn, x_ref[...], pv_ref[...])
        dx, dpv = vjp(do_ref[...])
        dx_ref[...] = dx.astype(BF16)
        dpv_ref[...] = dpv

    return pl.pallas_call(
        body, name=name, grid=(1,),
        in_specs=[pl.BlockSpec((S, LANES), lambda i: (0, blk)), pl.BlockSpec((2, LANES), lambda i: (0, 0)),
                  pl.BlockSpec((S, LANES), lambda i: (0, 0))],
        out_specs=[pl.BlockSpec((S, LANES), lambda i: (0, 0)), pl.BlockSpec((2, LANES), lambda i: (0, 0))],
        out_shape=[jax.ShapeDtypeStruct((S, LANES), BF16), jax.ShapeDtypeStruct((2, LANES), F32)],
        compiler_params=_cp("arbitrary"),
    )(p, pv, dout)


def _t(x):
    return jnp.swapaxes(x, -1, -2)


def _matmuls(prec, differentiable):
    cast = (lambda t: t.astype(BF16)) if prec is None else (lambda t: t)

    def mm(a, b):
        return lax.dot_general(cast(a), cast(b), (((2,), (1,)), ((0,), (0,))), precision=prec,
                               preferred_element_type=F32)

    def mm_nt(a, b):
        return lax.dot_general(cast(a), cast(b), (((2,), (2,)), ((0,), (0,))), precision=prec,
                               preferred_element_type=F32)

    if not differentiable:
        return mm, mm_nt
    dmm, dmm_nt = jax.custom_vjp(mm), jax.custom_vjp(mm_nt)
    dmm.defvjp(lambda a, b: (mm(a, b), (a, b)), lambda res, g: (mm_nt(g, res[1]), mm(_t(res[0]), g)))
    dmm_nt.defvjp(lambda a, b: (mm_nt(a, b), (a, b)), lambda res, g: (mm(g, res[1]), mm(_t(g), res[0])))
    return dmm, dmm_nt


def _known_inverse(mm):
    f = jax.custom_vjp(lambda n, inv: inv)
    f.defvjp(lambda n, inv: (inv, inv),
             lambda inv, g: (mm(mm(_t(inv), g), _t(inv)), jnp.zeros_like(inv)))
    return f


def _delta_chunk(state, q, k, v, beta, a_col, a_row, differentiable=False, inv_known=None):
    mm, mm_nt = _matmuls(lax.Precision.HIGH, differentiable)
    mm_sum, _ = _matmuls(HI, differentiable)
    ein, ein_nt = _matmuls(None, differentiable)
    H, C, _ = q.shape
    r = lax.broadcasted_iota(jnp.int32, (H, C, C), 1)
    c = lax.broadcasted_iota(jnp.int32, (H, C, C), 2)
    tril, strict = r >= c, r > c
    eye = (r == c).astype(F32)
    g_c = mm_sum(tril.astype(F32), jnp.broadcast_to(a_col, (H, C, C)))
    g_r = mm_sum(jnp.broadcast_to(a_row, (H, C, C)), (r <= c).astype(F32))
    decay = jnp.where(tril, jnp.exp(jnp.where(tril, g_c - g_r, 0.0)), 0.0)
    eg = jnp.exp(g_c)
    g_last = jnp.sum(jnp.where(r == C - 1, g_c, 0.0), axis=1, keepdims=True)
    qs = q * (float(q.shape[2]) ** -0.5)
    kb = k * beta
    neg_m = jnp.where(strict, -(mm_nt(kb, k) * decay), 0.0)
    if inv_known is None:
        inv = eye + neg_m
        pw = neg_m
        for _ in range(int(math.log2(C)) - 1):
            pw = mm(pw, pw)
            inv = inv + mm(inv, pw)
    else:
        inv = _known_inverse(mm)(neg_m, inv_known)
    u = mm(inv, v * beta)
    w = mm(inv, kb * eg)
    intra = jnp.where(tril, ein_nt(qs, k) * decay, 0.0)
    v_new = u - ein(w, state)
    o = ein(qs * eg, state) + ein(intra, v_new)
    nxt = state * jnp.exp(g_last) + ein(_t(k * jnp.exp(g_last - g_c)), v_new)
    return o, nxt, inv


def _heads(t):
    return jnp.stack([t[:, h * DN_DH:(h + 1) * DN_DH] for h in range(DN_HEADS)])


def _delta_step(state, q, k, v, bg, a_row, differentiable=False, inv_known=None):
    lane = lax.broadcasted_iota(jnp.int32, bg.shape, 1)
    pick = lambda j: jnp.stack([jnp.sum(jnp.where(lane == j + h, bg, 0.0), axis=1, keepdims=True)
                                for h in range(DN_HEADS)])
    o, nxt, inv = _delta_chunk(state, _heads(q), _heads(k), _heads(v), pick(0), pick(DN_HEADS), a_row, differentiable,
                               inv_known)
    return jnp.concatenate([o[h] for h in range(DN_HEADS)], axis=1), nxt, inv


def _delta_fwd(qkv, bg, a_row, name):
    S = qkv.shape[0]
    nc = S // BLK

    def body(q_ref, k_ref, v_ref, bg_ref, ar_ref, o_ref, st_ref, inv_ref, state):
        ci = pl.program_id(0)

        @pl.when(ci == 0)
        def _():
            state[...] = jnp.zeros_like(state)

        st = state[...]
        st_ref[:, 0] = st
        o, nxt, inv = _delta_step(st, q_ref[...], k_ref[...], v_ref[...], bg_ref[...], ar_ref[:, pl.ds(ci, 1), :])
        o_ref[...] = o
        inv_ref[:, 0] = inv
        state[...] = nxt

    part = lambda j: pl.BlockSpec((BLK, DN_W), lambda c: (c, j))
    per_chunk = pl.BlockSpec((DN_HEADS, 1, DN_DH, DN_DH), lambda c: (0, c, 0, 0))
    mats = jax.ShapeDtypeStruct((DN_HEADS, nc, DN_DH, DN_DH), F32)
    return pl.pallas_call(
        body, name=name, grid=(nc,),
        in_specs=[part(0), part(1), part(2), pl.BlockSpec((BLK, LANES), lambda c: (c, 0)),
                  pl.BlockSpec((DN_HEADS, nc, BLK), lambda c: (0, 0, 0))],
        out_specs=[part(0), per_chunk, per_chunk], out_shape=[jax.ShapeDtypeStruct((S, DN_W), F32), mats, mats],
        scratch_shapes=[pltpu.VMEM((DN_HEADS, DN_DH, DN_DH), F32)],
        compiler_params=_cp("arbitrary"),
    )(qkv, qkv, qkv, bg, a_row)


def _delta_bwd(qkv, bg, a_row, states, invs, do, name):
    S = qkv.shape[0]
    nc = S // BLK

    def body(q_ref, k_ref, v_ref, bg_ref, ar_ref, st_ref, inv_ref, do_ref, dqkv_ref, dbg_ref, dar_ref, dstate):
        t = pl.program_id(0)
        ci = nc - 1 - t

        @pl.when(t == 0)
        def _():
            dstate[...] = jnp.zeros_like(dstate)

        step = lambda *a: _delta_step(*a, differentiable=True, inv_known=inv_ref[:, 0])[:2]
        _, vjp = jax.vjp(step, st_ref[:, 0], q_ref[...], k_ref[...], v_ref[...], bg_ref[...],
                         ar_ref[:, pl.ds(ci, 1), :])
        dprev, dq, dk, dv, dbg, dar = vjp((do_ref[...], dstate[...]))
        dqkv_ref[:, 0:DN_W] = dq
        dqkv_ref[:, DN_W:2 * DN_W] = dk
        dqkv_ref[:, 2 * DN_W:3 * DN_W] = dv
        dbg_ref[...] = dbg
        dar_ref[:, pl.ds(ci, 1), :] = dar
        dstate[...] = dprev

    part = lambda j: pl.BlockSpec((BLK, DN_W), lambda t: (nc - 1 - t, j))
    lanes = pl.BlockSpec((BLK, LANES), lambda t: (nc - 1 - t, 0))
    rows = pl.BlockSpec((DN_HEADS, nc, BLK), lambda t: (0, 0, 0))
    per_chunk = pl.BlockSpec((DN_HEADS, 1, DN_DH, DN_DH), lambda t: (0, nc - 1 - t, 0, 0))
    return pl.pallas_call(
        body, name=name, grid=(nc,),
        in_specs=[part(0), part(1), part(2), lanes, rows, per_chunk, per_chunk, part(0)],
        out_specs=[pl.BlockSpec((BLK, 3 * DN_W), lambda t: (nc - 1 - t, 0)), lanes, rows],
        out_shape=[jax.ShapeDtypeStruct((S, 3 * DN_W), F32), jax.ShapeDtypeStruct((S, LANES), F32),
                   jax.ShapeDtypeStruct((DN_HEADS, nc, BLK), F32)],
        scratch_shapes=[pltpu.VMEM((DN_HEADS, DN_DH, DN_DH), F32)],
        compiler_params=_cp("arbitrary"),
    )(qkv, qkv, qkv, bg, a_row, states, invs, do)


def _gate_sb(o, z):
    return o * _silu(z)


def _gate_dn(o, z, g):
    return jnp.concatenate(
        [_rms(o[:, h * DN_DH:(h + 1) * DN_DH]) * g * _silu(z[:, h * DN_DH:(h + 1) * DN_DH]) for h in range(DN_HEADS)],
        axis=1)


def _merge_specs(S, D, tm):
    row = lambda w, blk: pl.BlockSpec((tm, w), lambda i: (i, blk))
    full = lambda a, b: pl.BlockSpec((a, b), lambda i: (0, 0))
    return [row(D, 0), full(1, D), row(SB_W, 0), row(SB_W, 3), row(DN_W, 0), row(DN_W, 7),
            row(D, IN_MAIN // D), row(D, IN_MAIN // D + 1), full(1, DN_DH), full(SB_W, D), full(DN_W, D), full(D, D)]


def _merge_fwd(x, gate, o_sb, o_dn, p, ng, wbs, wbd, wo, name):
    S, D = x.shape
    tm = min(512, S)

    def body(x_ref, gate_ref, osb_ref, zsb_ref, odn_ref, zdn_ref, msb_ref, mdn_ref, ng_ref, wbs_ref, wbd_ref, wo_ref,
             out_ref):
        a = _gate_sb(osb_ref[...], zsb_ref[...])
        b = _gate_dn(odn_ref[...], zdn_ref[...], ng_ref[...])
        y = _sigmoid(msb_ref[...]) * _bdot(a, wbs_ref[...]) + _sigmoid(mdn_ref[...]) * _bdot(b, wbd_ref[...])
        out_ref[...] = x_ref[...] + gate_ref[...] * _bdot(y, wo_ref[...])

    return pl.pallas_call(
        body, name=name, grid=(S // tm,), in_specs=_merge_specs(S, D, tm),
        out_specs=pl.BlockSpec((tm, D), lambda i: (i, 0)), out_shape=jax.ShapeDtypeStruct((S, D), F32),
        compiler_params=_cp("parallel"),
    )(x, gate, o_sb, p, o_dn, p, p, p, ng, wbs, wbd, wo)


def _merge_bwd(dxn, gate, o_sb, o_dn, p, ng, wbs, wbd, wo, name):
    S, D = dxn.shape
    tm = min(256, S)

    def body(dxn_ref, gate_ref, osb_ref, zsb_ref, odn_ref, zdn_ref, msb_ref, mdn_ref, ng_ref, wbs_ref, wbd_ref, wo_ref,
             dosb_ref, dzsb_ref, dodn_ref, dzdn_ref, dmsb_ref, dmdn_ref, dwo_ref, dwbs_ref, dwbd_ref, dgate_ref, dng_ref):
        @pl.when(pl.program_id(0) == 0)
        def _():
            for ref in (dwo_ref, dwbs_ref, dwbd_ref, dgate_ref, dng_ref):
                ref[...] = jnp.zeros_like(ref)

        a, vjp_a = jax.vjp(_gate_sb, osb_ref[...], zsb_ref[...])
        b, vjp_b = jax.vjp(_gate_dn, odn_ref[...], zdn_ref[...], ng_ref[...])
        a16, b16 = a.astype(BF16), b.astype(BF16)
        ps = jnp.dot(a16, wbs_ref[...], preferred_element_type=F32)
        pd = jnp.dot(b16, wbd_ref[...], preferred_element_type=F32)
        ss, sd = _sigmoid(msb_ref[...]), _sigmoid(mdn_ref[...])
        y16 = (ss * ps + sd * pd).astype(BF16)
        out = jnp.dot(y16, wo_ref[...], preferred_element_type=F32)
        dxn_v = dxn_ref[...]
        dgate_ref[...] += jnp.sum(dxn_v * out, axis=0, keepdims=True)
        dout16 = (dxn_v * gate_ref[...]).astype(BF16)
        dwo_ref[...] += _bdot_tn(y16, dout16)
        dy = _bdot_nt(dout16, wo_ref[...])
        dmsb_ref[...] = (dy * ps * ss * (1.0 - ss)).astype(BF16)
        dmdn_ref[...] = (dy * pd * sd * (1.0 - sd)).astype(BF16)
        dps16, dpd16 = (dy * ss).astype(BF16), (dy * sd).astype(BF16)
        dwbs_ref[...] += _bdot_tn(a16, dps16)
        dwbd_ref[...] += _bdot_tn(b16, dpd16)
        dosb, dzsb = vjp_a(_bdot_nt(dps16, wbs_ref[...]))
        dodn, dzdn, dng = vjp_b(_bdot_nt(dpd16, wbd_ref[...]))
        dosb_ref[...] = dosb
        dzsb_ref[...] = dzsb.astype(BF16)
        dodn_ref[...] = dodn
        dzdn_ref[...] = dzdn.astype(BF16)
        dng_ref[...] += dng

    row = lambda w: pl.BlockSpec((tm, w), lambda i: (i, 0))
    full = lambda a, b: pl.BlockSpec((a, b), lambda i: (0, 0))
    sds = jax.ShapeDtypeStruct
    return pl.pallas_call(
        body, name=name, grid=(S // tm,), in_specs=_merge_specs(S, D, tm),
        out_specs=[row(SB_W), row(SB_W), row(DN_W), row(DN_W), row(D), row(D),
                   full(D, D), full(SB_W, D), full(DN_W, D), full(1, D), full(1, DN_DH)],
        out_shape=[sds((S, SB_W), F32), sds((S, SB_W), BF16), sds((S, DN_W), F32), sds((S, DN_W), BF16),
                   sds((S, D), BF16), sds((S, D), BF16), sds((D, D), F32), sds((SB_W, D), F32), sds((DN_W, D), F32),
                   sds((1, D), F32), sds((1, DN_DH), F32)],
        compiler_params=_cp("arbitrary"),
    )(dxn, gate, o_sb, p, o_dn, p, p, p, ng, wbs, wbd, wo)


def _loss_fwd_bwd(y, target, name):
    S, D = y.shape
    tm = min(512, S)

    def body(y_ref, t_ref, l_ref, dy_ref):
        @pl.when(pl.program_id(0) == 0)
        def _():
            l_ref[...] = jnp.zeros_like(l_ref)

        e = y_ref[...] - t_ref[...]
        l_ref[...] += jnp.sum(e * e, axis=0, keepdims=True) * (0.5 / D)
        dy_ref[...] = e * (1.0 / D)

    row = pl.BlockSpec((tm, D), lambda i: (i, 0))
    return pl.pallas_call(
        body, name=name, grid=(S // tm,), in_specs=[row, row],
        out_specs=[pl.BlockSpec((1, D), lambda i: (0, 0)), row],
        out_shape=[jax.ShapeDtypeStruct((1, D), F32), jax.ShapeDtypeStruct((S, D), F32)],
        compiler_params=_cp("arbitrary"),
    )(y, target)


def _mod_fwd(c_all, ada_w, name):
    L, D, n = ada_w.shape

    def body(c_ref, w_ref, o_ref):
        o_ref[0] = _dot(_silu(c_ref[...]), w_ref[0])

    return pl.pallas_call(
        body, name=name, grid=(L,),
        in_specs=[pl.BlockSpec(c_all.shape, lambda l: (0, 0)), pl.BlockSpec((1, D, n), lambda l: (l, 0, 0))],
        out_specs=pl.BlockSpec((1, N_DEV, n), lambda l: (l, 0, 0)),
        out_shape=jax.ShapeDtypeStruct((L, N_DEV, n), F32),
        compiler_params=_cp("parallel"),
    )(c_all, ada_w)


def _mod_bwd_w(c_all_t, dmod, name):
    L, _, n = dmod.shape
    D = c_all_t.shape[0]

    def body(c_ref, d_ref, o_ref):
        o_ref[0] = _dot(_silu(c_ref[...]), d_ref[0])

    return pl.pallas_call(
        body, name=name, grid=(L,),
        in_specs=[pl.BlockSpec(c_all_t.shape, lambda l: (0, 0)), pl.BlockSpec((1, N_DEV, n), lambda l: (l, 0, 0))],
        out_specs=pl.BlockSpec((1, D, n), lambda l: (l, 0, 0)),
        out_shape=jax.ShapeDtypeStruct((L, D, n), F32),
        compiler_params=_cp("parallel"),
    )(c_all_t, dmod)


def _me():
    return lax.axis_index("x"), lax.axis_index("y"), lax.axis_index("c")


def _peer(k):
    x, y, c = _me()
    return (1 - x if k & 4 else x, 1 - y if k & 2 else y, 1 - c if k & 1 else c)


def _lin(dev):
    return 4 * dev[0] + 2 * dev[1] + dev[2]


class _Exchange:
    def __init__(self, arrays, scatter):
        self.n = len(arrays)
        self.scatter = scatter
        self.out_shape = [jax.ShapeDtypeStruct((N_DEV,) + tuple(a.shape[1:] if scatter else a.shape), a.dtype)
                          for a in arrays]
        self.specs = [pl.BlockSpec(memory_space=pl.ANY)] * self.n
        self.scratch = [pltpu.SemaphoreType.DMA((self.n, N_DEV - 1)), pltpu.SemaphoreType.DMA((self.n, N_DEV - 1)),
                        pltpu.SemaphoreType.DMA((self.n,))]

    def _copies(self, ins, outs, sems):
        send_sems, recv_sems, local_sems = sems
        me = _lin(_me())
        local, direct, passed, landed = [], [], [], []
        for t in range(self.n):
            src_of = (lambda d, t=t: ins[t].at[d]) if self.scatter else (lambda d, t=t: ins[t])
            local.append(pltpu.make_async_copy(src_of(me), outs[t].at[me], local_sems.at[t]))
            for k in range(1, N_DEV):
                peer = _peer(k)
                pair = dict(send_sem=send_sems.at[t, k - 1], recv_sem=recv_sems.at[t, k - 1],
                            device_id_type=pl.DeviceIdType.MESH)
                slot = outs[t].at[_lin(peer)]
                landed.append(pltpu.make_async_remote_copy(src_ref=slot, dst_ref=slot, device_id=peer, **pair))
                if self.scatter or k in (1, 2, 4, 6):
                    direct.append(pltpu.make_async_remote_copy(src_ref=src_of(_lin(peer)), dst_ref=outs[t].at[me],
                                                               device_id=peer, **pair))
                else:
                    came = outs[t].at[_lin(_peer(k - 1))]
                    passed.append((landed[-2], pltpu.make_async_remote_copy(src_ref=came, dst_ref=came,
                                                                            device_id=_peer(1), **pair)))
        return local, direct, passed, landed

    def start(self, ins, outs, sems):
        local, direct, _, _ = self._copies(ins, outs, sems)
        for cp in local + direct:
            cp.start()

    def finish(self, ins, outs, sems):
        local, direct, passed, landed = self._copies(ins, outs, sems)
        arrived = set()
        for came, onward in passed:
            came.wait_recv()
            arrived.add(id(came))
            onward.start()
        for cp in landed:
            if id(cp) not in arrived:
                cp.wait_recv()
        for cp in direct + [onward for _, onward in passed]:
            cp.wait_send()
        for cp in local:
            cp.wait()


def _exchange(arrays, scatter, name):
    ex = _Exchange(arrays, scatter)

    def body(*refs):
        ins, outs, sems = refs[:ex.n], refs[ex.n:2 * ex.n], refs[2 * ex.n:]
        ex.start(ins, outs, sems)
        ex.finish(ins, outs, sems)

    return pl.pallas_call(body, name=name, in_specs=ex.specs, out_specs=ex.specs, out_shape=ex.out_shape,
                          scratch_shapes=ex.scratch)(*arrays)


def _sum_slots(a, name):
    _, R, C = a.shape
    tr = SUM_ROWS if R % SUM_ROWS == 0 else R

    def body(a_ref, o_ref):
        acc = a_ref[0].astype(F32)
        for s in range(1, N_DEV):
            acc = acc + a_ref[s].astype(F32)
        o_ref[...] = acc

    return pl.pallas_call(
        body, name=name, grid=(R // tr,),
        in_specs=[pl.BlockSpec((N_DEV, tr, C), lambda i: (0, i, 0))], out_specs=pl.BlockSpec((tr, C), lambda i: (i, 0)),
        out_shape=jax.ShapeDtypeStruct((R, C), F32), compiler_params=_cp("parallel"),
    )(a)


def _adamw(w, g, m, v, name):
    shape = w.shape
    C = shape[-1]
    R = w.size // C
    tr = R
    for cand in (256, 128, 64):
        if R > cand and R % cand == 0:
            tr = cand
            break
    c1 = 1.0 / (1.0 - ADAM_B1 ** ADAM_STEP)
    c2 = 1.0 / (1.0 - ADAM_B2 ** ADAM_STEP)

    def body(w_ref, g_ref, m_ref, v_ref, d_ref, nm_ref, nv_ref):
        gv = g_ref[...]
        nm = ADAM_B1 * m_ref[...] + (1.0 - ADAM_B1) * gv
        nv = ADAM_B2 * v_ref[...] + (1.0 - ADAM_B2) * (gv * gv)
        d_ref[...] = -ADAM_LR * ((nm * c1) / (jnp.sqrt(nv * c2) + ADAM_EPS) + ADAM_WD * w_ref[...])
        nm_ref[...] = nm
        nv_ref[...] = nv

    spec = pl.BlockSpec((tr, C), lambda i: (i, 0))
    sd = jax.ShapeDtypeStruct((R, C), F32)
    outs = pl.pallas_call(
        body, name=name, grid=(R // tr,), in_specs=[spec] * 4, out_specs=[spec] * 3, out_shape=[sd] * 3,
        compiler_params=_cp("parallel"),
    )(*(t.reshape(R, C) for t in (w, g, m, v)))
    return tuple(t.reshape(shape) for t in outs)


def _col_segments(D):
    return [(0, IN_MAIN, 0), (IN_COLS, IN_COLS + 2 * D, IN_MAIN), (IN_MAIN, IN_COLS, IN_MAIN + 2 * D)]


def _pad_cols_of_blocks(wi):
    _, D, n = wi.shape
    pieces = []
    for lo, hi, _ in _col_segments(D):
        for d in range(N_DEV):
            a, b = max(lo, d * n), min(hi, (d + 1) * n)
            if a < b:
                pieces.append(wi[d][:, a - d * n:b - d * n])
    return jnp.concatenate(pieces + [jnp.zeros((D, LANES - 2 * DN_HEADS), wi.dtype)], axis=1)


def _blocks_of_padded(dw, n):
    D = dw.shape[0]
    out = []
    for d in range(N_DEV):
        pieces = []
        for lo, hi, at in sorted(_col_segments(D)):
            a, b = max(lo, d * n), min(hi, (d + 1) * n)
            if a < b:
                pieces.append(dw[:, at + a - lo:at + b - lo])
        out.append(pieces[0] if len(pieces) == 1 else jnp.concatenate(pieces, axis=1))
    return jnp.stack(out)


def _gate_params(a_log, dt_bias):
    z = jnp.zeros((LANES,), F32)
    return jnp.stack([z.at[DN_HEADS:2 * DN_HEADS].set(a_log), z.at[DN_HEADS:2 * DN_HEADS].set(dt_bias)])


def _layer_fwd(l, x, mod, wts, carry=None):
    S, D = x.shape
    tag = f"l{l}_"
    p, h = _inproj_fwd(x, mod, wts["norm_g"], wts["w_in"], tag + "inproj_fwd")
    (o_sb,), carried = _sb_fwd(p, wts["sb_q_g"], wts["sb_k_g"].T, tag + "sb_fwd", carry)
    qkv = _dn_prep_fwd(p, wts["conv_w"], 4 * SB_W, tag + "dn_prep_fwd")
    pv = _gate_params(wts["dn_a_log"], wts["dn_dt_bias"])
    ba_blk = (IN_MAIN + 2 * D) // LANES
    bg = _dn_gate_fwd(p, pv, ba_blk, tag + "dn_gate_fwd")
    a_row = bg[:, DN_HEADS:2 * DN_HEADS].T.reshape(DN_HEADS, S // BLK, BLK)
    o_dn, states, invs = _delta_fwd(qkv, bg, a_row, tag + "delta_fwd")
    gate = mod[:, 2 * D:]
    out = _merge_fwd(x, gate, o_sb, o_dn, p, wts["dn_norm_g"], wts["w_branch_sb"], wts["w_branch_dn"], wts["w_out"],
                     tag + "merge_fwd")
    saved = dict(x=x, mod=mod, p=p, h=h, o_sb=o_sb, qkv=qkv, pv=pv, bg=bg, a_row=a_row, o_dn=o_dn, states=states,
                 invs=invs, gate=gate)
    return out, saved, carried


def _layer_bwd(l, dxn, sv, wts, carry_of=None, late_carry_of=None):
    S, D = dxn.shape
    tag = f"l{l}_"
    (dosb, dzsb, dodn, dzdn, dmsb, dmdn, dwo, dwbs, dwbd, dgate, dng) = _merge_bwd(
        dxn, sv["gate"], sv["o_sb"], sv["o_dn"], sv["p"], wts["dn_norm_g"], wts["w_branch_sb"], wts["w_branch_dn"],
        wts["w_out"], tag + "merge_bwd")
    carry = None if carry_of is None else carry_of(dict(w_out=dwo, w_branch_sb=dwbs, w_branch_dn=dwbd))
    (dq, dk, dv, dgq, dgkt), carried = _sb_bwd(sv["p"], sv["o_sb"], dosb, wts["sb_q_g"], wts["sb_q_g"].T,
                                                wts["sb_k_g"], wts["sb_k_g"].T, tag + "sb_bwd", carry)
    dqkv_n, dbg, dar = _delta_bwd(sv["qkv"], sv["bg"], sv["a_row"], sv["states"], sv["invs"], dodn, tag + "delta_bwd")
    dqkv, dconv = _dn_prep_bwd(sv["p"], wts["conv_w"], 4 * SB_W, dqkv_n, tag + "dn_prep_bwd")
    dbg = dbg.at[:, DN_HEADS:2 * DN_HEADS].add(dar.reshape(DN_HEADS, S).T)
    ba_blk = (IN_MAIN + 2 * D) // LANES
    dba, dpv = _dn_gate_bwd(sv["p"], sv["pv"], ba_blk, dbg, tag + "dn_gate_bwd")
    dp = jnp.concatenate([dq, dk, dv, dzsb, dqkv, dzdn, dmsb, dmdn, dba], axis=1)
    dw_in = _matmul_tn(sv["h"].T, dp, tag + "inproj_bwd_dw")
    late = None if late_carry_of is None else late_carry_of(dict(w_in=dw_in, conv_w=dconv))
    (dx, dmod, dg), carried_late = _inproj_bwd_dx(dp, wts["w_in"], sv["x"], sv["mod"], wts["norm_g"], dxn,
                                                  tag + "inproj_bwd_dx", late)
    dmod = dmod.at[:, 2 * D:].set(dgate)
    grads = dict(w_in=dw_in, w_branch_sb=dwbs, w_branch_dn=dwbd, w_out=dwo, conv_w=dconv,
                 mod=dmod[0], norm_g=dg[0], sb_q_g=jnp.sum(dgq, axis=0)[0], sb_k_g=jnp.sum(dgkt, axis=0)[:, 0],
                 dn_a_log=dpv[0, DN_HEADS:2 * DN_HEADS], dn_dt_bias=dpv[1, DN_HEADS:2 * DN_HEADS], dn_norm_g=dng[0])
    return dx, grads, carried, carried_late


def _pad_rows(a, mult):
    extra = (-a.shape[0]) % mult
    return a if extra == 0 else jnp.concatenate([a, jnp.zeros((extra,) + a.shape[1:], a.dtype)], axis=0)


def _pack_rows(parts, width, mult):
    flat = jnp.concatenate([t.reshape(-1) for t in parts])
    extra = (-flat.shape[0]) % width
    if extra:
        flat = jnp.concatenate([flat, jnp.zeros((extra,), flat.dtype)])
    return _pad_rows(flat.reshape(-1, width), mult)


def _take(flat, off, shape):
    n = math.prod(shape)
    return flat[..., off:off + n].reshape(flat.shape[:-1] + tuple(shape)), off + n


SMALL = ("mod", "norm_g", "sb_q_g", "sb_k_g", "dn_a_log", "dn_dt_bias", "dn_norm_g")


def kernel(x, c, ada_w, ada_b, norm_g, w_in, sb_q_g, sb_k_g, conv_w, dn_a_log, dn_dt_bias, dn_norm_g, w_branch_sb, w_branch_dn, w_out, loss_target, m_ada_w, m_ada_b, m_norm_g, m_w_in, m_sb_q_g, m_sb_k_g, m_conv_w, m_dn_a_log, m_dn_dt_bias, m_dn_norm_g, m_w_branch_sb, m_w_branch_dn, m_w_out, v_ada_w, v_ada_b, v_norm_g, v_w_in, v_sb_q_g, v_sb_k_g, v_conv_w, v_dn_a_log, v_dn_dt_bias, v_dn_norm_g, v_w_branch_sb, v_w_branch_dn, v_w_out):
    L, D = norm_g.shape
    S = x.shape[1]
    n_in = w_in.shape[2]
    n_ada = ada_w.shape[2]
    n_br = w_branch_sb.shape[2]
    n_out = w_out.shape[1]
    n_conv = conv_w.shape[2]
    me = _lin(_me())

    def cat(a):
        return jnp.concatenate([a[d] for d in range(N_DEV)], axis=1)

    c_all, conv_all = _exchange([c, conv_w.reshape(L * CONV_K, n_conv)], False, "gather_small")
    c_all = c_all.reshape(N_DEV, D)
    conv_full = cat(conv_all).reshape(L, CONV_K, N_DEV * n_conv)

    mod_part = _mod_fwd(c_all, ada_w, "mod_fwd")

    def shards16(l):
        return [w_in[l].astype(BF16), w_branch_sb[l].astype(BF16), w_branch_dn[l].astype(BF16), w_out[l].astype(BF16)]

    def whole(l, got):
        wi, wbs, wbd, wo = got
        return dict(norm_g=norm_g[l:l + 1], w_in=_pad_cols_of_blocks(wi), sb_q_g=sb_q_g[l:l + 1], sb_k_g=sb_k_g[l:l + 1],
                    conv_w=conv_full[l], dn_a_log=dn_a_log[l], dn_dt_bias=dn_dt_bias[l], dn_norm_g=dn_norm_g[l:l + 1],
                    w_branch_sb=cat(wbs), w_branch_dn=cat(wbd), w_out=wo.reshape(N_DEV * n_out, D))

    *got, mod_all = _exchange(shards16(0) + [mod_part.reshape(L * N_DEV, n_ada)], False, "gather_weights")
    mod_full = cat(mod_all).reshape(L, N_DEV, N_DEV * n_ada) + ada_b[:, None, :]
    mod_mine = lax.dynamic_slice_in_dim(mod_full, me, 1, axis=1)

    act = x[0]
    saved, wts = [], []
    for l in range(L):
        wts.append(whole(l, got))
        act, sv, got = _layer_fwd(l, act, mod_mine[l], wts[l], (shards16(l + 1), False) if l + 1 < L else None)
        saved.append(sv)
    loss_cols, dact = _loss_fwd_bwd(act, loss_target[0], "loss")
    loss = lax.psum(jnp.sum(loss_cols), ("x", "y", "c"))

    def blocks(name, g):
        if name == "w_out":
            return g.astype(BF16).reshape(N_DEV, n_out, D)
        if name == "w_in":
            return _blocks_of_padded(g, n_in).astype(BF16)
        n = g.shape[1] // N_DEV
        dtype = F32 if name == "conv_w" else BF16
        return jnp.stack([g[:, d * n:(d + 1) * n].astype(dtype) for d in range(N_DEV)])

    early, late = ("w_out", "w_branch_sb", "w_branch_dn"), ("w_in", "conv_w")
    grads, recv, pending = [None] * L, {}, []
    for l in reversed(range(L)):
        keys = [k for k, _ in pending] + [(n, l) for n in early]

        def carry_of(g_early, pending=pending):
            return [a for _, a in pending] + [blocks(n, g_early[n]) for n in early], True

        last = l == 0
        dact, grads[l], got, got_late = _layer_bwd(
            l, dact, saved[l], wts[l], carry_of, (lambda g: ([blocks(n, g[n]) for n in late], True)) if last else None)
        recv.update(zip(keys, got))
        recv.update(zip([(n, l) for n in late], got_late))
        pending = [] if last else [((n, l), blocks(n, grads[l][n])) for n in late]
    grad_x = dact[None]
    small_g = _pack_rows([grads[l][n] for l in range(L) for n in SMALL], LANES, 8)
    (small_all_g,) = _exchange([small_g], False, "gather_small_grads")
    small_sum = _sum_slots(small_all_g, "sum_small_grads").reshape(-1)
    shard_shapes = dict(w_in=w_in.shape, w_branch_sb=w_branch_sb.shape, w_branch_dn=w_branch_dn.shape,
                        conv_w=conv_w.shape, w_out=w_out.shape)
    g_out = {n: jnp.stack([_sum_slots(recv[(n, l)], f"sum_{n}_l{l}").reshape(shape[1:]) for l in range(L)])
             for n, shape in shard_shapes.items()}
    small_shapes = dict(mod=(3 * D,), norm_g=(D,), sb_q_g=(SB_DH,), sb_k_g=(SB_DH,), dn_a_log=(DN_HEADS,),
                        dn_dt_bias=(DN_HEADS,), dn_norm_g=(DN_DH,))
    off = 0
    off_all = 0
    small_each = small_all_g.reshape(N_DEV, -1)
    per_small = {n: [] for n in SMALL}
    dmod_all = []
    for l in range(L):
        for n in SMALL:
            t, off = _take(small_sum, off, small_shapes[n])
            per_small[n].append(t)
            if n == "mod":
                t_all, _ = _take(small_each, off_all, small_shapes[n])
                dmod_all.append(t_all)
            off_all += math.prod(small_shapes[n])
    for n in SMALL:
        g_out[n if n != "mod" else "ada_b"] = jnp.stack(per_small[n])
    dmod_all = jnp.stack(dmod_all)
    dmod_cols = lax.dynamic_slice_in_dim(dmod_all, me * n_ada, n_ada, axis=2)
    g_out["ada_w"] = _mod_bwd_w(c_all.T, dmod_cols, "mod_bwd_w")

    given = dict(ada_w=(ada_w, m_ada_w, v_ada_w), ada_b=(ada_b, m_ada_b, v_ada_b), norm_g=(norm_g, m_norm_g, v_norm_g),
                 w_in=(w_in, m_w_in, v_w_in), sb_q_g=(sb_q_g, m_sb_q_g, v_sb_q_g), sb_k_g=(sb_k_g, m_sb_k_g, v_sb_k_g),
                 conv_w=(conv_w, m_conv_w, v_conv_w), dn_a_log=(dn_a_log, m_dn_a_log, v_dn_a_log),
                 dn_dt_bias=(dn_dt_bias, m_dn_dt_bias, v_dn_dt_bias), dn_norm_g=(dn_norm_g, m_dn_norm_g, v_dn_norm_g),
                 w_branch_sb=(w_branch_sb, m_w_branch_sb, v_w_branch_sb),
                 w_branch_dn=(w_branch_dn, m_w_branch_dn, v_w_branch_dn), w_out=(w_out, m_w_out, v_w_out))
    order = list(given)
    upd = {n: _adamw(given[n][0], g_out[n], given[n][1], given[n][2], "adamw_" + n) for n in order}
    return (loss, grad_x, *[g_out[n] for n in order], *[upd[n][0] for n in order], *[upd[n][1] for n in order],
            *[upd[n][2] for n in order])
```

```python
import functools
import math

import jax
import jax.numpy as jnp
from jax import lax
from jax.experimental import pallas as pl
from jax.experimental.pallas import tpu as pltpu

F32 = jnp.float32
BF16 = jnp.bfloat16
HI = lax.Precision.HIGHEST

N_DEV = 8
EPS = 1e-6
SB_HEADS, SB_DH = 8, 64
DN_HEADS, DN_DH = 4, 128
SB_W = SB_HEADS * SB_DH
DN_W = DN_HEADS * DN_DH
CONV_K = 4
BLK = 128
SB_KEYS = 512
SB_QB = 512
LANES = 128
IN_MAIN = 4 * SB_W + 4 * DN_W
IN_COLS = IN_MAIN + 2 * DN_HEADS
ADAM_LR, ADAM_B1, ADAM_B2, ADAM_EPS, ADAM_WD, ADAM_STEP = 0.001, 0.9, 0.999, 1e-08, 0.01, 10
VMEM_LIMIT = 56 * 1024 * 1024
LOG2E = 1.4426950408889634
SUM_ROWS = 128


def _cp(*sem, vmem=VMEM_LIMIT):
    return pltpu.CompilerParams(dimension_semantics=sem if sem else None, vmem_limit_bytes=vmem)


def _dot(a, b, prec=HI):
    return lax.dot_general(a, b, (((1,), (0,)), ((), ())), precision=prec, preferred_element_type=F32)


def _dot_nt(a, b, prec=HI):
    return lax.dot_general(a, b, (((1,), (1,)), ((), ())), precision=prec, preferred_element_type=F32)


def _bdot(a, b):
    return lax.dot_general(a.astype(BF16), b.astype(BF16), (((1,), (0,)), ((), ())), preferred_element_type=F32)


def _bdot_nt(a, b):
    return lax.dot_general(a.astype(BF16), b.astype(BF16), (((1,), (1,)), ((), ())), preferred_element_type=F32)


def _bdot_tn(a, b):
    return lax.dot_general(a.astype(BF16), b.astype(BF16), (((0,), (0,)), ((), ())), preferred_element_type=F32)


def _split_dot(a, b01_twice):
    hi = a.astype(BF16)
    lo = (a - hi.astype(F32)).astype(BF16)
    return jnp.dot(jnp.concatenate([hi, lo], axis=1), b01_twice, preferred_element_type=F32)


def _sigmoid(x):
    return 1.0 / (1.0 + jnp.exp(-x))


def _silu(x):
    return x * _sigmoid(x)


def _softplus(x):
    return jnp.maximum(x, 0.0) + jnp.log(1.0 + jnp.exp(-jnp.abs(x)))


def _rms(x):
    return x * lax.rsqrt(jnp.mean(x * x, axis=-1, keepdims=True) + EPS)


def _prenorm(x, g, shift, scale):
    return _rms(x) * g * (1.0 + scale) + shift


def _inproj_fwd(x, mod, g, w, name):
    S, D = x.shape
    N = w.shape[1]
    tm = min(512, S)
    tn = 896 if N % 896 == 0 else 128

    def body(x_ref, mod_ref, g_ref, w_ref, p_ref, h_ref):
        @pl.when(pl.program_id(1) == 0)
        def _():
            h = _prenorm(x_ref[...], g_ref[...], mod_ref[:, 0:D], mod_ref[:, D:2 * D])
            h_ref[...] = h.astype(BF16)

        p_ref[...] = jnp.dot(h_ref[...], w_ref[...], preferred_element_type=F32)

    return pl.pallas_call(
        body, name=name, grid=(S // tm, N // tn),
        in_specs=[pl.BlockSpec((tm, D), lambda i, j: (i, 0)), pl.BlockSpec((1, 3 * D), lambda i, j: (0, 0)),
                  pl.BlockSpec((1, D), lambda i, j: (0, 0)), pl.BlockSpec((D, tn), lambda i, j: (0, j))],
        out_specs=[pl.BlockSpec((tm, tn), lambda i, j: (i, j)), pl.BlockSpec((tm, D), lambda i, j: (i, 0))],
        out_shape=[jax.ShapeDtypeStruct((S, N), F32), jax.ShapeDtypeStruct((S, D), BF16)],
        compiler_params=_cp("parallel", "arbitrary"),
    )(x, mod, g, w)


def _inproj_bwd_dx(dp, w, x, mod, g, dxn, name, carry=None):
    S, N = dp.shape
    D = x.shape[1]
    tm = min(512, S)
    tk = 896 if N % 896 == 0 else 128
    nk = N // tk

    def body(dp_ref, w_ref, x_ref, mod_ref, g_ref, dxn_ref, dx_ref, dmod_ref, dg_ref, acc):
        i, k = pl.program_id(0), pl.program_id(1)

        @pl.when(k == 0)
        def _():
            acc[...] = jnp.zeros_like(acc)

        @pl.when((i == 0) & (k == 0))
        def _():
            dmod_ref[...] = jnp.zeros_like(dmod_ref)
            dg_ref[...] = jnp.zeros_like(dg_ref)

        acc[...] += lax.dot_general(dp_ref[...], w_ref[...], (((1,), (1,)), ((), ())), preferred_element_type=F32)

        @pl.when(k == nk - 1)
        def _():
            _, vjp = jax.vjp(_prenorm, x_ref[...], g_ref[...], mod_ref[:, 0:D], mod_ref[:, D:2 * D])
            dx, dg, dshift, dscale = vjp(acc[...])
            dx_ref[...] = dxn_ref[...] + dx
            dg_ref[...] += dg
            dmod_ref[:, 0:D] += dshift
            dmod_ref[:, D:2 * D] += dscale

    return _call_carrying(
        body, name, (S // tm, nk), carry, [dp, w, x, mod, g, dxn],
        in_specs=[pl.BlockSpec((tm, tk), lambda i, k: (i, k)), pl.BlockSpec((D, tk), lambda i, k: (0, k)),
                  pl.BlockSpec((tm, D), lambda i, k: (i, 0)), pl.BlockSpec((1, 3 * D), lambda i, k: (0, 0)),
                  pl.BlockSpec((1, D), lambda i, k: (0, 0)), pl.BlockSpec((tm, D), lambda i, k: (i, 0))],
        out_specs=[pl.BlockSpec((tm, D), lambda i, k: (i, 0)), pl.BlockSpec((1, 3 * D), lambda i, k: (0, 0)),
                   pl.BlockSpec((1, D), lambda i, k: (0, 0))],
        out_shape=[jax.ShapeDtypeStruct((S, D), F32), jax.ShapeDtypeStruct((1, 3 * D), F32),
                   jax.ShapeDtypeStruct((1, D), F32)],
        scratch_shapes=[pltpu.VMEM((tm, D), F32)], vmem=VMEM_LIMIT)


def _matmul_tn(a_t, b, name):
    M, K = a_t.shape
    N = b.shape[1]
    tn = 896 if N % 896 == 0 else (512 if N % 512 == 0 else 128)
    tk = min(512, K)
    nk = K // tk

    def body(a_ref, b_ref, o_ref):
        @pl.when(pl.program_id(1) == 0)
        def _():
            o_ref[...] = jnp.zeros_like(o_ref)

        o_ref[...] += jnp.dot(a_ref[...], b_ref[...], preferred_element_type=F32)

    return pl.pallas_call(
        body, name=name, grid=(N // tn, nk),
        in_specs=[pl.BlockSpec((M, tk), lambda j, k: (0, k)), pl.BlockSpec((tk, tn), lambda j, k: (k, j))],
        out_specs=pl.BlockSpec((M, tn), lambda j, k: (0, j)),
        out_shape=jax.ShapeDtypeStruct((M, N), F32),
        compiler_params=_cp("parallel", "arbitrary"),
    )(a_t, b)


def _qk_norm(t, g, scale):
    return _rms(t) * g * scale


def _qk_norm_t(t, g_col, scale):
    return t * lax.rsqrt(jnp.mean(t * t, axis=0, keepdims=True) + EPS) * g_col * scale


def _suffix_sums(x, tri):
    half = tri.shape[1]
    lo, hi = x[:, :half], x[:, half:]
    hi_sum = jnp.sum(hi, axis=1, keepdims=True)
    y = jnp.concatenate([_split_dot(lo, tri) + hi_sum, _split_dot(hi, tri)], axis=1)
    return y, hi_sum + jnp.sum(lo, axis=1, keepdims=True)


def _sb_step(qi, kat_blk, cl, from_here, valid):
    z = jnp.dot(qi, kat_blk, preferred_element_type=F32)
    nz = -z
    lk = jnp.minimum(nz, 0.0) - jnp.log(1.0 + jnp.exp2(jnp.minimum(z, nz))) * LOG2E
    if valid is not None:
        lk = jnp.where(valid, lk, 0.0)
    later, tot = _suffix_sums(lk, from_here)
    w = jnp.exp2(z + later + cl)
    if valid is not None:
        w = jnp.where(valid, w, 0.0)
    return z, lk, w, tot


def _sb_masks(kb):
    half = kb // 2
    r = lax.broadcasted_iota(jnp.int32, (half, half), 0)
    c = lax.broadcasted_iota(jnp.int32, (half, half), 1)
    rq = lax.broadcasted_iota(jnp.int32, (SB_QB, kb), 0)
    ck = lax.broadcasted_iota(jnp.int32, (SB_QB, kb), 1)
    twice = lambda m: jnp.concatenate([m, m], axis=0).astype(BF16)
    return twice((r >= c).astype(F32)), ck - rq


def _sb_fwd(p, gq, gkt, name, carry=None):
    S, dh = p.shape[0], SB_DH
    kb = min(SB_KEYS, S)
    per = kb // SB_QB
    nb = S // SB_QB
    scale = 1.0 / math.sqrt(dh)
    pairs = SB_W // LANES

    def body(q_ref, k_ref, v_ref, gq_ref, gkt_ref, o_ref, kt2, qa, kat, vb):
        kt2[...] = k_ref[...].T
        from_here, diff = _sb_masks(kb)
        for hh in range(LANES // dh):
            lanes = slice(hh * dh, (hh + 1) * dh)
            qa[...] = _qk_norm(q_ref[:, lanes], gq_ref[...], scale * LOG2E).astype(BF16)
            kat[...] = _qk_norm_t(kt2[lanes, :], gkt_ref[...], 1.0).astype(BF16)
            vb[...] = v_ref[:, lanes].astype(BF16)

            def qblock(i, _):
                rows = pl.ds(pl.multiple_of(i * SB_QB, SB_QB), SB_QB)
                qi = qa[rows, :]
                sbd = i // per

                def step(sb, cl, acc, valid):
                    cols = pl.ds(pl.multiple_of(sb * kb, kb), kb)
                    _, _, w, tot = _sb_step(qi, kat[:, cols], cl, from_here, valid)
                    return cl + tot, acc + jnp.dot(w.astype(BF16), vb[cols, :], preferred_element_type=F32)

                cl, acc = step(sbd, jnp.zeros((SB_QB, 1), F32), jnp.zeros((SB_QB, dh), F32),
                               diff < (i - sbd * per) * SB_QB)
                _, acc = lax.fori_loop(0, sbd, lambda jj, c: step(sbd - 1 - jj, c[0], c[1], None), (cl, acc))
                o_ref[rows, lanes] = acc
                return 0

            lax.fori_loop(0, nb, qblock, 0)

    blk = lambda off: pl.BlockSpec((S, LANES), lambda g: (0, off + g))
    return _call_carrying(
        body, name, pairs, carry, [p, p, p, gq, gkt],
        in_specs=[blk(0), blk(pairs), blk(2 * pairs), pl.BlockSpec((1, dh), lambda g: (0, 0)),
                  pl.BlockSpec((dh, 1), lambda g: (0, 0))],
        out_specs=[blk(0)], out_shape=[jax.ShapeDtypeStruct((S, SB_W), F32)],
        scratch_shapes=[pltpu.VMEM((LANES, S), F32), pltpu.VMEM((S, dh), BF16), pltpu.VMEM((dh, S), BF16),
                        pltpu.VMEM((S, dh), BF16)],
        vmem=VMEM_LIMIT)


def _call_carrying(body, name, grid, carry, operands, in_specs, out_specs, out_shape, scratch_shapes, vmem):
    grid = (grid,) if isinstance(grid, int) else tuple(grid)
    if carry is None:
        res = pl.pallas_call(body, name=name, grid=grid, in_specs=in_specs, out_specs=out_specs,
                             out_shape=out_shape, scratch_shapes=scratch_shapes,
                             compiler_params=_cp(*["arbitrary"] * len(grid), vmem=vmem))(*operands)
        return res, []

    def at(corner):
        hit = pl.program_id(0) == corner(grid[0])
        for axis in range(1, len(grid)):
            hit = jnp.logical_and(hit, pl.program_id(axis) == corner(grid[axis]))
        return hit

    ex = _Exchange(*carry)
    n_in, n_out, n_scr = len(in_specs), len(out_specs), len(scratch_shapes)

    def wrapped(*refs):
        ins, refs = refs[:n_in], refs[n_in:]
        xin, refs = refs[:ex.n], refs[ex.n:]
        outs, refs = refs[:n_out], refs[n_out:]
        xout, refs = refs[:ex.n], refs[ex.n:]
        scr, sems = refs[:n_scr], refs[n_scr:]

        @pl.when(at(lambda n: 0))
        def _():
            ex.start(xin, xout, sems)

        body(*ins, *outs, *scr)

        @pl.when(at(lambda n: n - 1))
        def _():
            ex.finish(xin, xout, sems)

    res = pl.pallas_call(wrapped, name=name, grid=grid, in_specs=in_specs + ex.specs,
                         out_specs=out_specs + ex.specs, out_shape=out_shape + ex.out_shape,
                         scratch_shapes=scratch_shapes + ex.scratch,
                         compiler_params=_cp(*["arbitrary"] * len(grid), vmem=vmem))(*operands, *carry[0])
    return res[:n_out], res[n_out:]


def _sb_bwd(p, o, do, gq, gqt, gk, gkt, name, carry=None):
    S, dh = p.shape[0], SB_DH
    kb = min(SB_KEYS, S)
    per = kb // SB_QB
    nb = S // SB_QB
    scale = 1.0 / math.sqrt(dh)
    pairs = SB_W // LANES
    per_pair = LANES // dh

    def body(q_ref, k_ref, v_ref, o_ref, do_ref, gq_ref, gqt_ref, gk_ref, gkt_ref,
             dq_ref, dk_ref, dv_ref, dgq_ref, dgkt_ref,
             qt2, kt2, vt2, dot2, dkt2, dvt2, qa, qat, ka, kat, vb, vtb, dob, dotb, dqa):
        qt2[...] = q_ref[...].T
        kt2[...] = k_ref[...].T
        vt2[...] = v_ref[...].T
        dot2[...] = do_ref[...].T
        dkt2[...] = jnp.zeros_like(dkt2)
        dvt2[...] = jnp.zeros_like(dvt2)
        from_here, diff = _sb_masks(kb)
        for hh in range(per_pair):
            lanes = slice(hh * dh, (hh + 1) * dh)
            qa[...] = _qk_norm(q_ref[:, lanes], gq_ref[...], scale * LOG2E).astype(BF16)
            qat[...] = _qk_norm_t(qt2[lanes, :], gqt_ref[...], scale * LOG2E).astype(BF16)
            ka[...] = _qk_norm(k_ref[:, lanes], gk_ref[...], 1.0).astype(BF16)
            kat[...] = _qk_norm_t(kt2[lanes, :], gkt_ref[...], 1.0).astype(BF16)
            vb[...] = v_ref[:, lanes].astype(BF16)
            vtb[...] = vt2[lanes, :].astype(BF16)
            dob[...] = do_ref[:, lanes].astype(BF16)
            dotb[...] = dot2[lanes, :].astype(BF16)

            def qblock(i, _):
                rows = pl.ds(pl.multiple_of(i * SB_QB, SB_QB), SB_QB)
                qi, qit = qa[rows, :], qat[:, rows]
                doi, doit = dob[rows, :], dotb[:, rows]
                total = jnp.sum(doi.astype(F32) * o_ref[rows, lanes], axis=1, keepdims=True)
                sbd = i // per

                def step(sb, cl, cd, dqi, valid):
                    cols = pl.ds(pl.multiple_of(sb * kb, kb), kb)
                    z, lk, w, tot = _sb_step(qi, kat[:, cols], cl, from_here, valid)
                    w16 = w.astype(BF16)
                    dl = jnp.dot(doi, vtb[:, cols], preferred_element_type=F32) * w16.astype(F32)
                    incl, dtot = _suffix_sums(dl, from_here)
                    sig = jnp.exp2(z + lk)
                    dz = dl - sig * (dl + (total - cd - incl))
                    if valid is not None:
                        dz = jnp.where(valid, dz, 0.0)
                    dz16 = dz.astype(BF16)
                    dqi = dqi + jnp.dot(dz16, ka[cols, :], preferred_element_type=F32)
                    dkt2[lanes, cols] += jnp.dot(qit, dz16, preferred_element_type=F32)
                    dvt2[lanes, cols] += jnp.dot(doit, w16, preferred_element_type=F32)
                    return cl + tot, cd + dtot, dqi

                zero = jnp.zeros((SB_QB, 1), F32)
                first = step(sbd, zero, zero, jnp.zeros((SB_QB, dh), F32), diff < (i - sbd * per) * SB_QB)
                _, _, dqi = lax.fori_loop(0, sbd, lambda jj, c: step(sbd - 1 - jj, c[0], c[1], c[2], None), first)
                dqa[rows, :] = dqi
                return 0

            lax.fori_loop(0, nb, qblock, 0)
            _, vq = jax.vjp(lambda t, g: _qk_norm(t, g, scale), q_ref[:, lanes], gq_ref[...])
            dq, dgq = vq(dqa[...])
            dq_ref[:, lanes] = dq.astype(BF16)
            dgq_ref[hh] = dgq
            _, vk = jax.vjp(lambda t, g: _qk_norm_t(t, g, 1.0), kt2[lanes, :], gkt_ref[...])
            dkt, dgkt = vk(dkt2[lanes, :] * (1.0 / LOG2E))
            dkt2[lanes, :] = dkt
            dgkt_ref[hh] = dgkt
        dk_ref[...] = dkt2[...].T.astype(BF16)
        dv_ref[...] = dvt2[...].T.astype(BF16)

    blk = lambda off: pl.BlockSpec((S, LANES), lambda g: (0, off + g))
    once = lambda off: pl.BlockSpec((S, LANES), lambda g: (0, off + g), pipeline_mode=pl.Buffered(1))
    gr = pl.BlockSpec((1, dh), lambda g: (0, 0))
    gc = pl.BlockSpec((dh, 1), lambda g: (0, 0))
    sd = jax.ShapeDtypeStruct((S, SB_W), BF16)
    return _call_carrying(
        body, name, pairs, carry, [p, p, p, o, do, gq, gqt, gk, gkt],
        in_specs=[once(0), once(pairs), once(2 * pairs), once(0), once(0), gr, gc, gr, gc],
        out_specs=[blk(0), blk(0), blk(0), pl.BlockSpec((per_pair, 1, dh), lambda g: (g, 0, 0)),
                   pl.BlockSpec((per_pair, dh, 1), lambda g: (g, 0, 0))],
        out_shape=[sd, sd, sd, jax.ShapeDtypeStruct((SB_HEADS, 1, dh), F32),
                   jax.ShapeDtypeStruct((SB_HEADS, dh, 1), F32)],
        scratch_shapes=[pltpu.VMEM((LANES, S), F32)] * 6 + [pltpu.VMEM((S, dh), BF16), pltpu.VMEM((dh, S), BF16)] * 4
        + [pltpu.VMEM((S, dh), F32)],
        vmem=60 * 1024 * 1024)


def _shift_down(x, s, rows):
    if s == 0:
        return x
    return jnp.where(rows >= s, pltpu.roll(x, s, 0), 0.0)


def _shift_up(x, s, rows, n):
    if s == 0:
        return x
    return jnp.where(rows < n - s, pltpu.roll(x, n - s, 0), 0.0)


def _conv(x, w_ref, rows):
    y = x * w_ref[CONV_K - 1:CONV_K, :]
    for kk in range(CONV_K - 1):
        y = y + _shift_down(x, CONV_K - 1 - kk, rows) * w_ref[kk:kk + 1, :]
    return y


def _act_norm(y, normed):
    s = _silu(y)
    n = s * lax.rsqrt(jnp.sum(s * s, axis=-1, keepdims=True) + EPS)
    return jnp.where(normed, n, s)


def _dn_prep_fwd(p, conv_w, col0, name):
    S = p.shape[0]
    nblk = 3 * DN_HEADS
    b0 = col0 // DN_DH

    def body(x_ref, w_ref, o_ref):
        rows = lax.broadcasted_iota(jnp.int32, (S, DN_DH), 0)
        y = _conv(x_ref[...], w_ref, rows)
        o_ref[...] = _act_norm(y, pl.program_id(0) < 2 * DN_HEADS)

    return pl.pallas_call(
        body, name=name, grid=(nblk,),
        in_specs=[pl.BlockSpec((S, DN_DH), lambda j: (0, b0 + j)), pl.BlockSpec((CONV_K, DN_DH), lambda j: (0, j))],
        out_specs=pl.BlockSpec((S, DN_DH), lambda j: (0, j)),
        out_shape=jax.ShapeDtypeStruct((S, 3 * DN_W), F32),
        compiler_params=_cp("parallel"),
    )(p, conv_w)


def _dn_prep_bwd(p, conv_w, col0, dout, name):
    S = p.shape[0]
    nblk = 3 * DN_HEADS
    b0 = col0 // DN_DH

    def body(x_ref, w_ref, do_ref, dx_ref, dw_ref):
        rows = lax.broadcasted_iota(jnp.int32, (S, DN_DH), 0)
        x = x_ref[...]
        y = _conv(x, w_ref, rows)
        normed = pl.program_id(0) < 2 * DN_HEADS
        _, vjp = jax.vjp(lambda t: _act_norm(t, normed), y)
        (dy,) = vjp(do_ref[...])
        dx = dy * w_ref[CONV_K - 1:CONV_K, :]
        dw_ref[CONV_K - 1:CONV_K, :] = jnp.sum(dy * x, axis=0, keepdims=True)
        for kk in range(CONV_K - 1):
            s = CONV_K - 1 - kk
            dx = dx + _shift_up(dy, s, rows, S) * w_ref[kk:kk + 1, :]
            dw_ref[kk:kk + 1, :] = jnp.sum(dy * _shift_down(x, s, rows), axis=0, keepdims=True)
        dx_ref[...] = dx.astype(BF16)

    return pl.pallas_call(
        body, name=name, grid=(nblk,),
        in_specs=[pl.BlockSpec((S, DN_DH), lambda j: (0, b0 + j)), pl.BlockSpec((CONV_K, DN_DH), lambda j: (0, j)),
                  pl.BlockSpec((S, DN_DH), lambda j: (0, j))],
        out_specs=[pl.BlockSpec((S, DN_DH), lambda j: (0, j)), pl.BlockSpec((CONV_K, DN_DH), lambda j: (0, j))],
        out_shape=[jax.ShapeDtypeStruct((S, 3 * DN_W), BF16), jax.ShapeDtypeStruct((CONV_K, 3 * DN_W), F32)],
        compiler_params=_cp("parallel"),
    )(p, conv_w, dout)


def _gate_fn(x, pv):
    lane = lax.broadcasted_iota(jnp.int32, x.shape, 1)
    decay = -jnp.exp(pv[0:1, :]) * _softplus(x + pv[1:2, :])
    return jnp.where(lane < DN_HEADS, _sigmoid(x), decay)


def _dn_gate_fwd(p, pv, blk, name):
    S = p.shape[0]

    def body(x_ref, pv_ref, o_ref):
        o_ref[...] = _gate_fn(x_ref[...], pv_ref[...])

    return pl.pallas_call(
        body, name=name, grid=(1,),
        in_specs=[pl.BlockSpec((S, LANES), lambda i: (0, blk)), pl.BlockSpec((2, LANES), lambda i: (0, 0))],
        out_specs=pl.BlockSpec((S, LANES), lambda i: (0, 0)),
        out_shape=jax.ShapeDtypeStruct((S, LANES), F32),
        compiler_params=_cp("arbitrary"),
    )(p, pv)


def _dn_gate_bwd(p, pv, blk, dout, name):
    S = p.shape[0]

    def body(x_ref, pv_ref, do_ref, dx_ref, dpv_ref):
        _, vjp = jax.vjp(_gate_fn, x_ref[...], pv_ref[...])
        dx, dpv = vjp(do_ref[...])
        dx_ref[...] = dx.astype(BF16)
        dpv_ref[...] = dpv

    return pl.pallas_call(
        body, name=name, grid=(1,),
        in_specs=[pl.BlockSpec((S, LANES), lambda i: (0, blk)), pl.BlockSpec((2, LANES), lambda i: (0, 0)),
                  pl.BlockSpec((S, LANES), lambda i: (0, 0))],
        out_specs=[pl.BlockSpec((S, LANES), lambda i: (0, 0)), pl.BlockSpec((2, LANES), lambda i: (0, 0))],
        out_shape=[jax.ShapeDtypeStruct((S, LANES), BF16), jax.ShapeDtypeStruct((2, LANES), F32)],
        compiler_params=_cp("arbitrary"),
    )(p, pv, dout)


def _t(x):
    return jnp.swapaxes(x, -1, -2)


def _matmuls(prec, differentiable):
    cast = (lambda t: t.astype(BF16)) if prec is None else (lambda t: t)

    def mm(a, b):
        return lax.dot_general(cast(a), cast(b), (((2,), (1,)), ((0,), (0,))), precision=prec,
                               preferred_element_type=F32)

    def mm_nt(a, b):
        return lax.dot_general(cast(a), cast(b), (((2,), (2,)), ((0,), (0,))), precision=prec,
                               preferred_element_type=F32)

    if not differentiable:
        return mm, mm_nt
    dmm, dmm_nt = jax.custom_vjp(mm), jax.custom_vjp(mm_nt)
    dmm.defvjp(lambda a, b: (mm(a, b), (a, b)), lambda res, g: (mm_nt(g, res[1]), mm(_t(res[0]), g)))
    dmm_nt.defvjp(lambda a, b: (mm_nt(a, b), (a, b)), lambda res, g: (mm(g, res[1]), mm(_t(g), res[0])))
    return dmm, dmm_nt


def _exact_sums(differentiable):
    def three(x, axis):
        a = x.astype(BF16)
        r = x - a.astype(F32)
        b = r.astype(BF16)
        return jnp.concatenate([a, b, (r - b.astype(F32)).astype(BF16)], axis=axis)

    def left(tri, x):
        return lax.dot_general(jnp.concatenate([tri] * 3, axis=2).astype(BF16), three(x, 1),
                               (((2,), (1,)), ((0,), (0,))), preferred_element_type=F32)

    def right(x, tri):
        return lax.dot_general(three(x, 2), jnp.concatenate([tri] * 3, axis=1).astype(BF16),
                               (((2,), (1,)), ((0,), (0,))), preferred_element_type=F32)

    if not differentiable:
        return left, right
    dleft, dright = jax.custom_vjp(left), jax.custom_vjp(right)
    dleft.defvjp(lambda tri, x: (left(tri, x), tri), lambda tri, g: (jnp.zeros_like(tri), left(_t(tri), g)))
    dright.defvjp(lambda x, tri: (right(x, tri), tri), lambda tri, g: (right(g, _t(tri)), jnp.zeros_like(tri)))
    return dleft, dright


def _known_inverse(mm):
    f = jax.custom_vjp(lambda n, inv: inv)
    f.defvjp(lambda n, inv: (inv, inv),
             lambda inv, g: (mm(mm(_t(inv), g), _t(inv)), jnp.zeros_like(inv)))
    return f


def _delta_chunk(state, q, k, v, beta, a_col, a_row, differentiable=False, inv_known=None):
    mm, mm_nt = _matmuls(lax.Precision.HIGH, differentiable)
    sum_left, sum_right = _exact_sums(differentiable)
    ein, ein_nt = _matmuls(None, differentiable)
    H, C, _ = q.shape
    r = lax.broadcasted_iota(jnp.int32, (H, C, C), 1)
    c = lax.broadcasted_iota(jnp.int32, (H, C, C), 2)
    tril, strict = r >= c, r > c
    eye = (r == c).astype(F32)
    g_c = sum_left(tril.astype(F32), jnp.broadcast_to(a_col, (H, C, C)))
    g_r = sum_right(jnp.broadcast_to(a_row, (H, C, C)), (r <= c).astype(F32))
    decay = jnp.where(tril, jnp.exp(jnp.where(tril, g_c - g_r, 0.0)), 0.0)
    eg = jnp.exp(g_c)
    g_last = jnp.sum(jnp.where(r == C - 1, g_c, 0.0), axis=1, keepdims=True)
    qs = q * (float(q.shape[2]) ** -0.5)
    kb = k * beta
    neg_m = jnp.where(strict, -(mm_nt(kb, k) * decay), 0.0)
    if inv_known is None:
        inv = eye + neg_m
        pw = neg_m
        for _ in range(int(math.log2(C)) - 1):
            pw = mm(pw, pw)
            inv = inv + mm(inv, pw)
    else:
        inv = _known_inverse(mm)(neg_m, inv_known)
    u = mm(inv, v * beta)
    w = mm(inv, kb * eg)
    intra = jnp.where(tril, ein_nt(qs, k) * decay, 0.0)
    v_new = u - ein(w, state)
    o = ein(qs * eg, state) + ein(intra, v_new)
    nxt = state * jnp.exp(g_last) + ein(_t(k * jnp.exp(g_last - g_c)), v_new)
    return o, nxt, inv


def _heads(t):
    return jnp.stack([t[:, h * DN_DH:(h + 1) * DN_DH] for h in range(DN_HEADS)])


def _delta_step(state, q, k, v, bg, a_row, differentiable=False, inv_known=None):
    lane = lax.broadcasted_iota(jnp.int32, bg.shape, 1)
    pick = lambda j: jnp.stack([jnp.sum(jnp.where(lane == j + h, bg, 0.0), axis=1, keepdims=True)
                                for h in range(DN_HEADS)])
    o, nxt, inv = _delta_chunk(state, _heads(q), _heads(k), _heads(v), pick(0), pick(DN_HEADS), a_row, differentiable,
                               inv_known)
    return jnp.concatenate([o[h] for h in range(DN_HEADS)], axis=1), nxt, inv


def _delta_fwd(qkv, bg, a_row, name):
    S = qkv.shape[0]
    nc = S // BLK

    def body(q_ref, k_ref, v_ref, bg_ref, ar_ref, o_ref, st_ref, inv_ref, state):
        ci = pl.program_id(0)

        @pl.when(ci == 0)
        def _():
            state[...] = jnp.zeros_like(state)

        st = state[...]
        st_ref[:, 0] = st
        o, nxt, inv = _delta_step(st, q_ref[...], k_ref[...], v_ref[...], bg_ref[...], ar_ref[:, pl.ds(ci, 1), :])
        o_ref[...] = o
        inv_ref[:, 0] = inv
        state[...] = nxt

    part = lambda j: pl.BlockSpec((BLK, DN_W), lambda c: (c, j))
    per_chunk = pl.BlockSpec((DN_HEADS, 1, DN_DH, DN_DH), lambda c: (0, c, 0, 0))
    mats = jax.ShapeDtypeStruct((DN_HEADS, nc, DN_DH, DN_DH), F32)
    return pl.pallas_call(
        body, name=name, grid=(nc,),
        in_specs=[part(0), part(1), part(2), pl.BlockSpec((BLK, LANES), lambda c: (c, 0)),
                  pl.BlockSpec((DN_HEADS, nc, BLK), lambda c: (0, 0, 0))],
        out_specs=[part(0), per_chunk, per_chunk], out_shape=[jax.ShapeDtypeStruct((S, DN_W), F32), mats, mats],
        scratch_shapes=[pltpu.VMEM((DN_HEADS, DN_DH, DN_DH), F32)],
        compiler_params=_cp("arbitrary"),
    )(qkv, qkv, qkv, bg, a_row)


def _delta_bwd(qkv, bg, a_row, states, invs, do, name):
    S = qkv.shape[0]
    nc = S // BLK

    def body(q_ref, k_ref, v_ref, bg_ref, ar_ref, st_ref, inv_ref, do_ref, dqkv_ref, dbg_ref, dar_ref, dstate):
        t = pl.program_id(0)
        ci = nc - 1 - t

        @pl.when(t == 0)
        def _():
            dstate[...] = jnp.zeros_like(dstate)

        step = lambda *a: _delta_step(*a, differentiable=True, inv_known=inv_ref[:, 0])[:2]
        _, vjp = jax.vjp(step, st_ref[:, 0], q_ref[...], k_ref[...], v_ref[...], bg_ref[...],
                         ar_ref[:, pl.ds(ci, 1), :])
        dprev, dq, dk, dv, dbg, dar = vjp((do_ref[...], dstate[...]))
        dqkv_ref[:, 0:DN_W] = dq
        dqkv_ref[:, DN_W:2 * DN_W] = dk
        dqkv_ref[:, 2 * DN_W:3 * DN_W] = dv
        dbg_ref[...] = dbg
        dar_ref[:, pl.ds(ci, 1), :] = dar
        dstate[...] = dprev

    part = lambda j: pl.BlockSpec((BLK, DN_W), lambda t: (nc - 1 - t, j))
    lanes = pl.BlockSpec((BLK, LANES), lambda t: (nc - 1 - t, 0))
    rows = pl.BlockSpec((DN_HEADS, nc, BLK), lambda t: (0, 0, 0))
    per_chunk = pl.BlockSpec((DN_HEADS, 1, DN_DH, DN_DH), lambda t: (0, nc - 1 - t, 0, 0))
    return pl.pallas_call(
        body, name=name, grid=(nc,),
        in_specs=[part(0), part(1), part(2), lanes, rows, per_chunk, per_chunk, part(0)],
        out_specs=[pl.BlockSpec((BLK, 3 * DN_W), lambda t: (nc - 1 - t, 0)), lanes, rows],
        out_shape=[jax.ShapeDtypeStruct((S, 3 * DN_W), F32), jax.ShapeDtypeStruct((S, LANES), F32),
                   jax.ShapeDtypeStruct((DN_HEADS, nc, BLK), F32)],
        scratch_shapes=[pltpu.VMEM((DN_HEADS, DN_DH, DN_DH), F32)],
        compiler_params=_cp("arbitrary"),
    )(qkv, qkv, qkv, bg, a_row, states, invs, do)


def _gate_sb(o, z):
    return o * _silu(z)


def _gate_dn(o, z, g):
    return jnp.concatenate(
        [_rms(o[:, h * DN_DH:(h + 1) * DN_DH]) * g * _silu(z[:, h * DN_DH:(h + 1) * DN_DH]) for h in range(DN_HEADS)],
        axis=1)


def _merge_specs(S, D, tm):
    row = lambda w, blk: pl.BlockSpec((tm, w), lambda i: (i, blk))
    full = lambda a, b: pl.BlockSpec((a, b), lambda i: (0, 0))
    return [row(D, 0), full(1, D), row(SB_W, 0), row(SB_W, 3), row(DN_W, 0), row(DN_W, 7),
            row(D, IN_MAIN // D), row(D, IN_MAIN // D + 1), full(1, DN_DH), full(SB_W, D), full(DN_W, D), full(D, D)]


def _merge_fwd(x, gate, o_sb, o_dn, p, ng, wbs, wbd, wo, name):
    S, D = x.shape
    tm = min(512, S)

    def body(x_ref, gate_ref, osb_ref, zsb_ref, odn_ref, zdn_ref, msb_ref, mdn_ref, ng_ref, wbs_ref, wbd_ref, wo_ref,
             out_ref):
        a = _gate_sb(osb_ref[...], zsb_ref[...])
        b = _gate_dn(odn_ref[...], zdn_ref[...], ng_ref[...])
        y = _sigmoid(msb_ref[...]) * _bdot(a, wbs_ref[...]) + _sigmoid(mdn_ref[...]) * _bdot(b, wbd_ref[...])
        out_ref[...] = x_ref[...] + gate_ref[...] * _bdot(y, wo_ref[...])

    return pl.pallas_call(
        body, name=name, grid=(S // tm,), in_specs=_merge_specs(S, D, tm),
        out_specs=pl.BlockSpec((tm, D), lambda i: (i, 0)), out_shape=jax.ShapeDtypeStruct((S, D), F32),
        compiler_params=_cp("parallel"),
    )(x, gate, o_sb, p, o_dn, p, p, p, ng, wbs, wbd, wo)


def _merge_bwd(dxn, gate, o_sb, o_dn, p, ng, wbs, wbd, wo, name):
    S, D = dxn.shape
    tm = min(256, S)

    def body(dxn_ref, gate_ref, osb_ref, zsb_ref, odn_ref, zdn_ref, msb_ref, mdn_ref, ng_ref, wbs_ref, wbd_ref, wo_ref,
             dosb_ref, dzsb_ref, dodn_ref, dzdn_ref, dmsb_ref, dmdn_ref, dwo_ref, dwbs_ref, dwbd_ref, dgate_ref, dng_ref):
        @pl.when(pl.program_id(0) == 0)
        def _():
            for ref in (dwo_ref, dwbs_ref, dwbd_ref, dgate_ref, dng_ref):
                ref[...] = jnp.zeros_like(ref)

        a, vjp_a = jax.vjp(_gate_sb, osb_ref[...], zsb_ref[...])
        b, vjp_b = jax.vjp(_gate_dn, odn_ref[...], zdn_ref[...], ng_ref[...])
        a16, b16 = a.astype(BF16), b.astype(BF16)
        ps = jnp.dot(a16, wbs_ref[...], preferred_element_type=F32)
        pd = jnp.dot(b16, wbd_ref[...], preferred_element_type=F32)
        ss, sd = _sigmoid(msb_ref[...]), _sigmoid(mdn_ref[...])
        y16 = (ss * ps + sd * pd).astype(BF16)
        out = jnp.dot(y16, wo_ref[...], preferred_element_type=F32)
        dxn_v = dxn_ref[...]
        dgate_ref[...] += jnp.sum(dxn_v * out, axis=0, keepdims=True)
        dout16 = (dxn_v * gate_ref[...]).astype(BF16)
        dwo_ref[...] += _bdot_tn(y16, dout16)
        dy = _bdot_nt(dout16, wo_ref[...])
        dmsb_ref[...] = (dy * ps * ss * (1.0 - ss)).astype(BF16)
        dmdn_ref[...] = (dy * pd * sd * (1.0 - sd)).astype(BF16)
        dps16, dpd16 = (dy * ss).astype(BF16), (dy * sd).astype(BF16)
        dwbs_ref[...] += _bdot_tn(a16, dps16)
        dwbd_ref[...] += _bdot_tn(b16, dpd16)
        dosb, dzsb = vjp_a(_bdot_nt(dps16, wbs_ref[...]))
        dodn, dzdn, dng = vjp_b(_bdot_nt(dpd16, wbd_ref[...]))
        dosb_ref[...] = dosb
        dzsb_ref[...] = dzsb.astype(BF16)
        dodn_ref[...] = dodn
        dzdn_ref[...] = dzdn.astype(BF16)
        dng_ref[...] += dng

    row = lambda w: pl.BlockSpec((tm, w), lambda i: (i, 0))
    full = lambda a, b: pl.BlockSpec((a, b), lambda i: (0, 0))
    sds = jax.ShapeDtypeStruct
    return pl.pallas_call(
        body, name=name, grid=(S // tm,), in_specs=_merge_specs(S, D, tm),
        out_specs=[row(SB_W), row(SB_W), row(DN_W), row(DN_W), row(D), row(D),
                   full(D, D), full(SB_W, D), full(DN_W, D), full(1, D), full(1, DN_DH)],
        out_shape=[sds((S, SB_W), F32), sds((S, SB_W), BF16), sds((S, DN_W), F32), sds((S, DN_W), BF16),
                   sds((S, D), BF16), sds((S, D), BF16), sds((D, D), F32), sds((SB_W, D), F32), sds((DN_W, D), F32),
                   sds((1, D), F32), sds((1, DN_DH), F32)],
        compiler_params=_cp("arbitrary"),
    )(dxn, gate, o_sb, p, o_dn, p, p, p, ng, wbs, wbd, wo)


def _loss_fwd_bwd(y, target, name):
    S, D = y.shape
    tm = min(512, S)

    def body(y_ref, t_ref, l_ref, dy_ref):
        @pl.when(pl.program_id(0) == 0)
        def _():
            l_ref[...] = jnp.zeros_like(l_ref)

        e = y_ref[...] - t_ref[...]
        l_ref[...] += jnp.sum(e * e, axis=0, keepdims=True) * (0.5 / D)
        dy_ref[...] = e * (1.0 / D)

    row = pl.BlockSpec((tm, D), lambda i: (i, 0))
    return pl.pallas_call(
        body, name=name, grid=(S // tm,), in_specs=[row, row],
        out_specs=[pl.BlockSpec((1, D), lambda i: (0, 0)), row],
        out_shape=[jax.ShapeDtypeStruct((1, D), F32), jax.ShapeDtypeStruct((S, D), F32)],
        compiler_params=_cp("arbitrary"),
    )(y, target)


def _mod_fwd(c_all, ada_w, name):
    L, D, n = ada_w.shape

    def body(c_ref, w_ref, o_ref):
        o_ref[0] = _dot(_silu(c_ref[...]), w_ref[0])

    return pl.pallas_call(
        body, name=name, grid=(L,),
        in_specs=[pl.BlockSpec(c_all.shape, lambda l: (0, 0)), pl.BlockSpec((1, D, n), lambda l: (l, 0, 0))],
        out_specs=pl.BlockSpec((1, N_DEV, n), lambda l: (l, 0, 0)),
        out_shape=jax.ShapeDtypeStruct((L, N_DEV, n), F32),
        compiler_params=_cp("parallel"),
    )(c_all, ada_w)


def _mod_bwd_w(c_all_t, dmod, name):
    L, _, n = dmod.shape
    D = c_all_t.shape[0]

    def body(c_ref, d_ref, o_ref):
        o_ref[0] = _dot(_silu(c_ref[...]), d_ref[0])

    return pl.pallas_call(
        body, name=name, grid=(L,),
        in_specs=[pl.BlockSpec(c_all_t.shape, lambda l: (0, 0)), pl.BlockSpec((1, N_DEV, n), lambda l: (l, 0, 0))],
        out_specs=pl.BlockSpec((1, D, n), lambda l: (l, 0, 0)),
        out_shape=jax.ShapeDtypeStruct((L, D, n), F32),
        compiler_params=_cp("parallel"),
    )(c_all_t, dmod)


def _me():
    return lax.axis_index("x"), lax.axis_index("y"), lax.axis_index("c")


def _peer(k):
    x, y, c = _me()
    return (1 - x if k & 4 else x, 1 - y if k & 2 else y, 1 - c if k & 1 else c)


def _lin(dev):
    return 4 * dev[0] + 2 * dev[1] + dev[2]


class _Exchange:
    def __init__(self, arrays, scatter):
        self.n = len(arrays)
        self.scatter = scatter
        self.out_shape = [jax.ShapeDtypeStruct((N_DEV,) + tuple(a.shape[1:] if scatter else a.shape), a.dtype)
                          for a in arrays]
        self.specs = [pl.BlockSpec(memory_space=pl.ANY)] * self.n
        self.scratch = [pltpu.SemaphoreType.DMA((self.n, N_DEV - 1)), pltpu.SemaphoreType.DMA((self.n, N_DEV - 1)),
                        pltpu.SemaphoreType.DMA((self.n,))]

    def _copies(self, ins, outs, sems):
        send_sems, recv_sems, local_sems = sems
        me = _lin(_me())
        local, direct, passed, landed = [], [], [], []
        for t in range(self.n):
            src_of = (lambda d, t=t: ins[t].at[d]) if self.scatter else (lambda d, t=t: ins[t])
            local.append(pltpu.make_async_copy(src_of(me), outs[t].at[me], local_sems.at[t]))
            for k in range(1, N_DEV):
                peer = _peer(k)
                pair = dict(send_sem=send_sems.at[t, k - 1], recv_sem=recv_sems.at[t, k - 1],
                            device_id_type=pl.DeviceIdType.MESH)
                slot = outs[t].at[_lin(peer)]
                landed.append(pltpu.make_async_remote_copy(src_ref=slot, dst_ref=slot, device_id=peer, **pair))
                if self.scatter or k in (1, 2, 4, 6):
                    direct.append(pltpu.make_async_remote_copy(src_ref=src_of(_lin(peer)), dst_ref=outs[t].at[me],
                                                               device_id=peer, **pair))
                else:
                    came = outs[t].at[_lin(_peer(k - 1))]
                    passed.append((landed[-2], pltpu.make_async_remote_copy(src_ref=came, dst_ref=came,
                                                                            device_id=_peer(1), **pair)))
        return local, direct, passed, landed

    def start(self, ins, outs, sems):
        local, direct, _, _ = self._copies(ins, outs, sems)
        for cp in local + direct:
            cp.start()

    def finish(self, ins, outs, sems):
        local, direct, passed, landed = self._copies(ins, outs, sems)
        arrived = set()
        for came, onward in passed:
            came.wait_recv()
            arrived.add(id(came))
            onward.start()
        for cp in landed:
            if id(cp) not in arrived:
                cp.wait_recv()
        for cp in direct + [onward for _, onward in passed]:
            cp.wait_send()
        for cp in local:
            cp.wait()


def _exchange(arrays, scatter, name):
    ex = _Exchange(arrays, scatter)

    def body(*refs):
        ins, outs, sems = refs[:ex.n], refs[ex.n:2 * ex.n], refs[2 * ex.n:]
        ex.start(ins, outs, sems)
        ex.finish(ins, outs, sems)

    return pl.pallas_call(body, name=name, in_specs=ex.specs, out_specs=ex.specs, out_shape=ex.out_shape,
                          scratch_shapes=ex.scratch)(*arrays)


def _sum_slots(a, name):
    _, R, C = a.shape
    tr = SUM_ROWS if R % SUM_ROWS == 0 else R

    def body(a_ref, o_ref):
        acc = a_ref[0].astype(F32)
        for s in range(1, N_DEV):
            acc = acc + a_ref[s].astype(F32)
        o_ref[...] = acc

    return pl.pallas_call(
        body, name=name, grid=(R // tr,),
        in_specs=[pl.BlockSpec((N_DEV, tr, C), lambda i: (0, i, 0))], out_specs=pl.BlockSpec((tr, C), lambda i: (i, 0)),
        out_shape=jax.ShapeDtypeStruct((R, C), F32), compiler_params=_cp("parallel"),
    )(a)


def _adamw(w, g, m, v, name):
    shape = w.shape
    C = shape[-1]
    R = w.size // C
    tr = R
    for cand in (256, 128, 64):
        if R > cand and R % cand == 0:
            tr = cand
            break
    c1 = 1.0 / (1.0 - ADAM_B1 ** ADAM_STEP)
    c2 = 1.0 / (1.0 - ADAM_B2 ** ADAM_STEP)

    def body(w_ref, g_ref, m_ref, v_ref, d_ref, nm_ref, nv_ref):
        gv = g_ref[...]
        nm = ADAM_B1 * m_ref[...] + (1.0 - ADAM_B1) * gv
        nv = ADAM_B2 * v_ref[...] + (1.0 - ADAM_B2) * (gv * gv)
        d_ref[...] = -ADAM_LR * ((nm * c1) / (jnp.sqrt(nv * c2) + ADAM_EPS) + ADAM_WD * w_ref[...])
        nm_ref[...] = nm
        nv_ref[...] = nv

    spec = pl.BlockSpec((tr, C), lambda i: (i, 0))
    sd = jax.ShapeDtypeStruct((R, C), F32)
    outs = pl.pallas_call(
        body, name=name, grid=(R // tr,), in_specs=[spec] * 4, out_specs=[spec] * 3, out_shape=[sd] * 3,
        compiler_params=_cp("parallel"),
    )(*(t.reshape(R, C) for t in (w, g, m, v)))
    return tuple(t.reshape(shape) for t in outs)


def _col_segments(D):
    return [(0, IN_MAIN, 0), (IN_COLS, IN_COLS + 2 * D, IN_MAIN), (IN_MAIN, IN_COLS, IN_MAIN + 2 * D)]


def _pad_cols_of_blocks(wi):
    _, D, n = wi.shape
    pieces = []
    for lo, hi, _ in _col_segments(D):
        for d in range(N_DEV):
            a, b = max(lo, d * n), min(hi, (d + 1) * n)
            if a < b:
                pieces.append(wi[d][:, a - d * n:b - d * n])
    return jnp.concatenate(pieces + [jnp.zeros((D, LANES - 2 * DN_HEADS), wi.dtype)], axis=1)


def _blocks_of_padded(dw, n):
    D = dw.shape[0]
    out = []
    for d in range(N_DEV):
        pieces = []
        for lo, hi, at in sorted(_col_segments(D)):
            a, b = max(lo, d * n), min(hi, (d + 1) * n)
            if a < b:
                pieces.append(dw[:, at + a - lo:at + b - lo])
        out.append(pieces[0] if len(pieces) == 1 else jnp.concatenate(pieces, axis=1))
    return jnp.stack(out)


def _gate_params(a_log, dt_bias):
    z = jnp.zeros((LANES,), F32)
    return jnp.stack([z.at[DN_HEADS:2 * DN_HEADS].set(a_log), z.at[DN_HEADS:2 * DN_HEADS].set(dt_bias)])


def _layer_fwd(l, x, mod, wts, carry=None):
    S, D = x.shape
    tag = f"l{l}_"
    p, h = _inproj_fwd(x, mod, wts["norm_g"], wts["w_in"], tag + "inproj_fwd")
    (o_sb,), carried = _sb_fwd(p, wts["sb_q_g"], wts["sb_k_g"].T, tag + "sb_fwd", carry)
    qkv = _dn_prep_fwd(p, wts["conv_w"], 4 * SB_W, tag + "dn_prep_fwd")
    pv = _gate_params(wts["dn_a_log"], wts["dn_dt_bias"])
    ba_blk = (IN_MAIN + 2 * D) // LANES
    bg = _dn_gate_fwd(p, pv, ba_blk, tag + "dn_gate_fwd")
    a_row = bg[:, DN_HEADS:2 * DN_HEADS].T.reshape(DN_HEADS, S // BLK, BLK)
    o_dn, states, invs = _delta_fwd(qkv, bg, a_row, tag + "delta_fwd")
    gate = mod[:, 2 * D:]
    out = _merge_fwd(x, gate, o_sb, o_dn, p, wts["dn_norm_g"], wts["w_branch_sb"], wts["w_branch_dn"], wts["w_out"],
                     tag + "merge_fwd")
    saved = dict(x=x, mod=mod, p=p, h=h, o_sb=o_sb, qkv=qkv, pv=pv, bg=bg, a_row=a_row, o_dn=o_dn, states=states,
                 invs=invs, gate=gate)
    return out, saved, carried


def _layer_bwd(l, dxn, sv, wts, carry_of=None, late_carry_of=None):
    S, D = dxn.shape
    tag = f"l{l}_"
    (dosb, dzsb, dodn, dzdn, dmsb, dmdn, dwo, dwbs, dwbd, dgate, dng) = _merge_bwd(
        dxn, sv["gate"], sv["o_sb"], sv["o_dn"], sv["p"], wts["dn_norm_g"], wts["w_branch_sb"], wts["w_branch_dn"],
        wts["w_out"], tag + "merge_bwd")
    carry = None if carry_of is None else carry_of(dict(w_out=dwo, w_branch_sb=dwbs, w_branch_dn=dwbd))
    (dq, dk, dv, dgq, dgkt), carried = _sb_bwd(sv["p"], sv["o_sb"], dosb, wts["sb_q_g"], wts["sb_q_g"].T,
                                                wts["sb_k_g"], wts["sb_k_g"].T, tag + "sb_bwd", carry)
    dqkv_n, dbg, dar = _delta_bwd(sv["qkv"], sv["bg"], sv["a_row"], sv["states"], sv["invs"], dodn, tag + "delta_bwd")
    dqkv, dconv = _dn_prep_bwd(sv["p"], wts["conv_w"], 4 * SB_W, dqkv_n, tag + "dn_prep_bwd")
    dbg = dbg.at[:, DN_HEADS:2 * DN_HEADS].add(dar.reshape(DN_HEADS, S).T)
    ba_blk = (IN_MAIN + 2 * D) // LANES
    dba, dpv = _dn_gate_bwd(sv["p"], sv["pv"], ba_blk, dbg, tag + "dn_gate_bwd")
    dp = jnp.concatenate([dq, dk, dv, dzsb, dqkv, dzdn, dmsb, dmdn, dba], axis=1)
    dw_in = _matmul_tn(sv["h"].T, dp, tag + "inproj_bwd_dw")
    late = None if late_carry_of is None else late_carry_of(dict(w_in=dw_in, conv_w=dconv))
    (dx, dmod, dg), carried_late = _inproj_bwd_dx(dp, wts["w_in"], sv["x"], sv["mod"], wts["norm_g"], dxn,
                                                  tag + "inproj_bwd_dx", late)
    dmod = dmod.at[:, 2 * D:].set(dgate)
    grads = dict(w_in=dw_in, w_branch_sb=dwbs, w_branch_dn=dwbd, w_out=dwo, conv_w=dconv,
                 mod=dmod[0], norm_g=dg[0], sb_q_g=jnp.sum(dgq, axis=0)[0], sb_k_g=jnp.sum(dgkt, axis=0)[:, 0],
                 dn_a_log=dpv[0, DN_HEADS:2 * DN_HEADS], dn_dt_bias=dpv[1, DN_HEADS:2 * DN_HEADS], dn_norm_g=dng[0])
    return dx, grads, carried, carried_late


def _pad_rows(a, mult):
    extra = (-a.shape[0]) % mult
    return a if extra == 0 else jnp.concatenate([a, jnp.zeros((extra,) + a.shape[1:], a.dtype)], axis=0)


def _pack_rows(parts, width, mult):
    flat = jnp.concatenate([t.reshape(-1) for t in parts])
    extra = (-flat.shape[0]) % width
    if extra:
        flat = jnp.concatenate([flat, jnp.zeros((extra,), flat.dtype)])
    return _pad_rows(flat.reshape(-1, width), mult)


def _take(flat, off, shape):
    n = math.prod(shape)
    return flat[..., off:off + n].reshape(flat.shape[:-1] + tuple(shape)), off + n


SMALL = ("mod", "norm_g", "sb_q_g", "sb_k_g", "dn_a_log", "dn_dt_bias", "dn_norm_g")


def kernel(x, c, ada_w, ada_b, norm_g, w_in, sb_q_g, sb_k_g, conv_w, dn_a_log, dn_dt_bias, dn_norm_g, w_branch_sb, w_branch_dn, w_out, loss_target, m_ada_w, m_ada_b, m_norm_g, m_w_in, m_sb_q_g, m_sb_k_g, m_conv_w, m_dn_a_log, m_dn_dt_bias, m_dn_norm_g, m_w_branch_sb, m_w_branch_dn, m_w_out, v_ada_w, v_ada_b, v_norm_g, v_w_in, v_sb_q_g, v_sb_k_g, v_conv_w, v_dn_a_log, v_dn_dt_bias, v_dn_norm_g, v_w_branch_sb, v_w_branch_dn, v_w_out):
    L, D = norm_g.shape
    S = x.shape[1]
    n_in = w_in.shape[2]
    n_ada = ada_w.shape[2]
    n_br = w_branch_sb.shape[2]
    n_out = w_out.shape[1]
    n_conv = conv_w.shape[2]
    me = _lin(_me())

    def cat(a):
        return jnp.concatenate([a[d] for d in range(N_DEV)], axis=1)

    c_all, conv_all = _exchange([c, conv_w.reshape(L * CONV_K, n_conv)], False, "gather_small")
    c_all = c_all.reshape(N_DEV, D)
    conv_full = cat(conv_all).reshape(L, CONV_K, N_DEV * n_conv)

    mod_part = _mod_fwd(c_all, ada_w, "mod_fwd")

    def shards16(l):
        return [w_in[l].astype(BF16), w_branch_sb[l].astype(BF16), w_branch_dn[l].astype(BF16), w_out[l].astype(BF16)]

    def whole(l, got):
        wi, wbs, wbd, wo = got
        return dict(norm_g=norm_g[l:l + 1], w_in=_pad_cols_of_blocks(wi), sb_q_g=sb_q_g[l:l + 1], sb_k_g=sb_k_g[l:l + 1],
                    conv_w=conv_full[l], dn_a_log=dn_a_log[l], dn_dt_bias=dn_dt_bias[l], dn_norm_g=dn_norm_g[l:l + 1],
                    w_branch_sb=cat(wbs), w_branch_dn=cat(wbd), w_out=wo.reshape(N_DEV * n_out, D))

    *got, mod_all = _exchange(shards16(0) + [mod_part.reshape(L * N_DEV, n_ada)], False, "gather_weights")
    mod_full = cat(mod_all).reshape(L, N_DEV, N_DEV * n_ada) + ada_b[:, None, :]
    mod_mine = lax.dynamic_slice_in_dim(mod_full, me, 1, axis=1)

    act = x[0]
    saved, wts = [], []
    for l in range(L):
        wts.append(whole(l, got))
        act, sv, got = _layer_fwd(l, act, mod_mine[l], wts[l], (shards16(l + 1), False) if l + 1 < L else None)
        saved.append(sv)
    loss_cols, dact = _loss_fwd_bwd(act, loss_target[0], "loss")
    loss = lax.psum(jnp.sum(loss_cols), ("x", "y", "c"))

    def blocks(name, g):
        if name == "w_out":
            return g.astype(BF16).reshape(N_DEV, n_out, D)
        if name == "w_in":
            return _blocks_of_padded(g, n_in).astype(BF16)
        n = g.shape[1] // N_DEV
        dtype = F32 if name == "conv_w" else BF16
        return jnp.stack([g[:, d * n:(d + 1) * n].astype(dtype) for d in range(N_DEV)])

    early, late = ("w_out", "w_branch_sb", "w_branch_dn"), ("w_in", "conv_w")
    grads, recv, pending = [None] * L, {}, []
    for l in reversed(range(L)):
        keys = [k for k, _ in pending] + [(n, l) for n in early]

        def carry_of(g_early, pending=pending):
            return [a for _, a in pending] + [blocks(n, g_early[n]) for n in early], True

        last = l == 0
        dact, grads[l], got, got_late = _layer_bwd(
            l, dact, saved[l], wts[l], carry_of, (lambda g: ([blocks(n, g[n]) for n in late], True)) if last else None)
        recv.update(zip(keys, got))
        recv.update(zip([(n, l) for n in late], got_late))
        pending = [] if last else [((n, l), blocks(n, grads[l][n])) for n in late]
    grad_x = dact[None]
    small_g = _pack_rows([grads[l][n] for l in range(L) for n in SMALL], LANES, 8)
    (small_all_g,) = _exchange([small_g], False, "gather_small_grads")
    small_sum = _sum_slots(small_all_g, "sum_small_grads").reshape(-1)
    shard_shapes = dict(w_in=w_in.shape, w_branch_sb=w_branch_sb.shape, w_branch_dn=w_branch_dn.shape,
                        conv_w=conv_w.shape, w_out=w_out.shape)
    g_out = {n: jnp.stack([_sum_slots(recv[(n, l)], f"sum_{n}_l{l}").reshape(shape[1:]) for l in range(L)])
             for n, shape in shard_shapes.items()}
    small_shapes = dict(mod=(3 * D,), norm_g=(D,), sb_q_g=(SB_DH,), sb_k_g=(SB_DH,), dn_a_log=(DN_HEADS,),
                        dn_dt_bias=(DN_HEADS,), dn_norm_g=(DN_DH,))
    off = 0
    off_all = 0
    small_each = small_all_g.reshape(N_DEV, -1)
    per_small = {n: [] for n in SMALL}
    dmod_all = []
    for l in range(L):
        for n in SMALL:
            t, off = _take(small_sum, off, small_shapes[n])
            per_small[n].append(t)
            if n == "mod":
                t_all, _ = _take(small_each, off_all, small_shapes[n])
                dmod_all.append(t_all)
            off_all += math.prod(small_shapes[n])
    for n in SMALL:
        g_out[n if n != "mod" else "ada_b"] = jnp.stack(per_small[n])
    dmod_all = jnp.stack(dmod_all)
    dmod_cols = lax.dynamic_slice_in_dim(dmod_all, me * n_ada, n_ada, axis=2)
    g_out["ada_w"] = _mod_bwd_w(c_all.T, dmod_cols, "mod_bwd_w")

    given = dict(ada_w=(ada_w, m_ada_w, v_ada_w), ada_b=(ada_b, m_ada_b, v_ada_b), norm_g=(norm_g, m_norm_g, v_norm_g),
                 w_in=(w_in, m_w_in, v_w_in), sb_q_g=(sb_q_g, m_sb_q_g, v_sb_q_g), sb_k_g=(sb_k_g, m_sb_k_g, v_sb_k_g),
                 conv_w=(conv_w, m_conv_w, v_conv_w), dn_a_log=(dn_a_log, m_dn_a_log, v_dn_a_log),
                 dn_dt_bias=(dn_dt_bias, m_dn_dt_bias, v_dn_dt_bias), dn_norm_g=(dn_norm_g, m_dn_norm_g, v_dn_norm_g),
                 w_branch_sb=(w_branch_sb, m_w_branch_sb, v_w_branch_sb),
                 w_branch_dn=(w_branch_dn, m_w_branch_dn, v_w_branch_dn), w_out=(w_out, m_w_out, v_w_out))
    order = list(given)
    upd = {n: _adamw(given[n][0], g_out[n], given[n][1], given[n][2], "adamw_" + n) for n in order}
    return (loss, grad_x, *[g_out[n] for n in order], *[upd[n][0] for n in order], *[upd[n][1] for n in order],
            *[upd[n][2] for n in order])
```

```python
import functools
import math

import jax
import jax.numpy as jnp
from jax import lax
from jax.experimental import pallas as pl
from jax.experimental.pallas import tpu as pltpu

F32 = jnp.float32
BF16 = jnp.bfloat16
HI = lax.Precision.HIGHEST

N_DEV = 8
EPS = 1e-6
SB_HEADS, SB_DH = 8, 64
DN_HEADS, DN_DH = 4, 128
SB_W = SB_HEADS * SB_DH
DN_W = DN_HEADS * DN_DH
CONV_K = 4
BLK = 128
SB_KEYS = 512
SB_QB = 512
LANES = 128
IN_MAIN = 4 * SB_W + 4 * DN_W
IN_COLS = IN_MAIN + 2 * DN_HEADS
ADAM_LR, ADAM_B1, ADAM_B2, ADAM_EPS, ADAM_WD, ADAM_STEP = 0.001, 0.9, 0.999, 1e-08, 0.01, 10
VMEM_LIMIT = 56 * 1024 * 1024
LOG2E = 1.4426950408889634
SUM_ROWS = 128


def _cp(*sem, vmem=VMEM_LIMIT):
    return pltpu.CompilerParams(dimension_semantics=sem if sem else None, vmem_limit_bytes=vmem)


def _dot(a, b, prec=HI):
    return lax.dot_general(a, b, (((1,), (0,)), ((), ())), precision=prec, preferred_element_type=F32)


def _dot_nt(a, b, prec=HI):
    return lax.dot_general(a, b, (((1,), (1,)), ((), ())), precision=prec, preferred_element_type=F32)


def _bdot(a, b):
    return lax.dot_general(a.astype(BF16), b.astype(BF16), (((1,), (0,)), ((), ())), preferred_element_type=F32)


def _bdot_nt(a, b):
    return lax.dot_general(a.astype(BF16), b.astype(BF16), (((1,), (1,)), ((), ())), preferred_element_type=F32)


def _bdot_tn(a, b):
    return lax.dot_general(a.astype(BF16), b.astype(BF16), (((0,), (0,)), ((), ())), preferred_element_type=F32)


def _split_dot(a, b01_twice):
    hi = a.astype(BF16)
    lo = (a - hi.astype(F32)).astype(BF16)
    return jnp.dot(jnp.concatenate([hi, lo], axis=1), b01_twice, preferred_element_type=F32)


def _sigmoid(x):
    return 1.0 / (1.0 + jnp.exp(-x))


def _silu(x):
    return x * _sigmoid(x)


def _softplus(x):
    return jnp.maximum(x, 0.0) + jnp.log(1.0 + jnp.exp(-jnp.abs(x)))


def _rms(x):
    return x * lax.rsqrt(jnp.mean(x * x, axis=-1, keepdims=True) + EPS)


def _prenorm(x, g, shift, scale):
    return _rms(x) * g * (1.0 + scale) + shift


def _inproj_fwd(x, mod, g, w, name):
    S, D = x.shape
    N = w.shape[1]
    tm = min(512, S)
    tn = 896 if N % 896 == 0 else 128

    def body(x_ref, mod_ref, g_ref, w_ref, p_ref, h_ref):
        @pl.when(pl.program_id(1) == 0)
        def _():
            h = _prenorm(x_ref[...], g_ref[...], mod_ref[:, 0:D], mod_ref[:, D:2 * D])
            h_ref[...] = h.astype(BF16)

        p_ref[...] = jnp.dot(h_ref[...], w_ref[...], preferred_element_type=F32)

    return pl.pallas_call(
        body, name=name, grid=(S // tm, N // tn),
        in_specs=[pl.BlockSpec((tm, D), lambda i, j: (i, 0)), pl.BlockSpec((1, 3 * D), lambda i, j: (0, 0)),
                  pl.BlockSpec((1, D), lambda i, j: (0, 0)), pl.BlockSpec((D, tn), lambda i, j: (0, j))],
        out_specs=[pl.BlockSpec((tm, tn), lambda i, j: (i, j)), pl.BlockSpec((tm, D), lambda i, j: (i, 0))],
        out_shape=[jax.ShapeDtypeStruct((S, N), F32), jax.ShapeDtypeStruct((S, D), BF16)],
        compiler_params=_cp("parallel", "arbitrary"),
    )(x, mod, g, w)


def _inproj_bwd_dx(dp, w, x, mod, g, dxn, name, carry=None):
    S, N = dp.shape
    D = x.shape[1]
    tm = min(512, S)
    tk = 896 if N % 896 == 0 else 128
    nk = N // tk

    def body(dp_ref, w_ref, x_ref, mod_ref, g_ref, dxn_ref, dx_ref, dmod_ref, dg_ref, acc):
        i, k = pl.program_id(0), pl.program_id(1)

        @pl.when(k == 0)
        def _():
            acc[...] = jnp.zeros_like(acc)

        @pl.when((i == 0) & (k == 0))
        def _():
            dmod_ref[...] = jnp.zeros_like(dmod_ref)
            dg_ref[...] = jnp.zeros_like(dg_ref)

        acc[...] += lax.dot_general(dp_ref[...], w_ref[...], (((1,), (1,)), ((), ())), preferred_element_type=F32)

        @pl.when(k == nk - 1)
        def _():
            _, vjp = jax.vjp(_prenorm, x_ref[...], g_ref[...], mod_ref[:, 0:D], mod_ref[:, D:2 * D])
            dx, dg, dshift, dscale = vjp(acc[...])
            dx_ref[...] = dxn_ref[...] + dx
            dg_ref[...] += dg
            dmod_ref[:, 0:D] += dshift
            dmod_ref[:, D:2 * D] += dscale

    return _call_carrying(
        body, name, (S // tm, nk), carry, [dp, w, x, mod, g, dxn],
        in_specs=[pl.BlockSpec((tm, tk), lambda i, k: (i, k)), pl.BlockSpec((D, tk), lambda i, k: (0, k)),
                  pl.BlockSpec((tm, D), lambda i, k: (i, 0)), pl.BlockSpec((1, 3 * D), lambda i, k: (0, 0)),
                  pl.BlockSpec((1, D), lambda i, k: (0, 0)), pl.BlockSpec((tm, D), lambda i, k: (i, 0))],
        out_specs=[pl.BlockSpec((tm, D), lambda i, k: (i, 0)), pl.BlockSpec((1, 3 * D), lambda i, k: (0, 0)),
                   pl.BlockSpec((1, D), lambda i, k: (0, 0))],
        out_shape=[jax.ShapeDtypeStruct((S, D), F32), jax.ShapeDtypeStruct((1, 3 * D), F32),
                   jax.ShapeDtypeStruct((1, D), F32)],
        scratch_shapes=[pltpu.VMEM((tm, D), F32)], vmem=VMEM_LIMIT)


def _matmul_tn(a_t, b, name):
    M, K = a_t.shape
    N = b.shape[1]
    tn = 896 if N % 896 == 0 else (512 if N % 512 == 0 else 128)
    tk = min(512, K)
    nk = K // tk

    def body(a_ref, b_ref, o_ref):
        @pl.when(pl.program_id(1) == 0)
        def _():
            o_ref[...] = jnp.zeros_like(o_ref)

        o_ref[...] += jnp.dot(a_ref[...], b_ref[...], preferred_element_type=F32)

    return pl.pallas_call(
        body, name=name, grid=(N // tn, nk),
        in_specs=[pl.BlockSpec((M, tk), lambda j, k: (0, k)), pl.BlockSpec((tk, tn), lambda j, k: (k, j))],
        out_specs=pl.BlockSpec((M, tn), lambda j, k: (0, j)),
        out_shape=jax.ShapeDtypeStruct((M, N), F32),
        compiler_params=_cp("parallel", "arbitrary"),
    )(a_t, b)


def _qk_norm(t, g, scale):
    return _rms(t) * g * scale


def _qk_norm_t(t, g_col, scale):
    return t * lax.rsqrt(jnp.mean(t * t, axis=0, keepdims=True) + EPS) * g_col * scale


def _suffix_sums(x, tri):
    half = tri.shape[1]
    lo, hi = x[:, :half], x[:, half:]
    hi_sum = jnp.sum(hi, axis=1, keepdims=True)
    y = jnp.concatenate([_split_dot(lo, tri) + hi_sum, _split_dot(hi, tri)], axis=1)
    return y, hi_sum + jnp.sum(lo, axis=1, keepdims=True)


def _sb_step(qi, kat_blk, cl, from_here, valid):
    z = jnp.dot(qi, kat_blk, preferred_element_type=F32)
    nz = -z
    lk = jnp.minimum(nz, 0.0) - jnp.log(1.0 + jnp.exp2(jnp.minimum(z, nz))) * LOG2E
    if valid is not None:
        lk = jnp.where(valid, lk, 0.0)
    later, tot = _suffix_sums(lk, from_here)
    w = jnp.exp2(z + later + cl)
    if valid is not None:
        w = jnp.where(valid, w, 0.0)
    return z, lk, w, tot


def _sb_masks(kb):
    half = kb // 2
    r = lax.broadcasted_iota(jnp.int32, (half, half), 0)
    c = lax.broadcasted_iota(jnp.int32, (half, half), 1)
    rq = lax.broadcasted_iota(jnp.int32, (SB_QB, kb), 0)
    ck = lax.broadcasted_iota(jnp.int32, (SB_QB, kb), 1)
    twice = lambda m: jnp.concatenate([m, m], axis=0).astype(BF16)
    return twice((r >= c).astype(F32)), ck - rq


def _sb_fwd(p, gq, gkt, name, carry=None):
    S, dh = p.shape[0], SB_DH
    kb = min(SB_KEYS, S)
    per = kb // SB_QB
    nb = S // SB_QB
    scale = 1.0 / math.sqrt(dh)
    pairs = SB_W // LANES

    def body(q_ref, k_ref, v_ref, gq_ref, gkt_ref, o_ref, kt2, qa, kat, vb):
        kt2[...] = k_ref[...].T
        from_here, diff = _sb_masks(kb)
        for hh in range(LANES // dh):
            lanes = slice(hh * dh, (hh + 1) * dh)
            qa[...] = _qk_norm(q_ref[:, lanes], gq_ref[...], scale * LOG2E).astype(BF16)
            kat[...] = _qk_norm_t(kt2[lanes, :], gkt_ref[...], 1.0).astype(BF16)
            vb[...] = v_ref[:, lanes].astype(BF16)

            def qblock(i, _):
                rows = pl.ds(pl.multiple_of(i * SB_QB, SB_QB), SB_QB)
                qi = qa[rows, :]
                sbd = i // per

                def step(sb, cl, acc, valid):
                    cols = pl.ds(pl.multiple_of(sb * kb, kb), kb)
                    _, _, w, tot = _sb_step(qi, kat[:, cols], cl, from_here, valid)
                    return cl + tot, acc + jnp.dot(w.astype(BF16), vb[cols, :], preferred_element_type=F32)

                cl, acc = step(sbd, jnp.zeros((SB_QB, 1), F32), jnp.zeros((SB_QB, dh), F32),
                               diff < (i - sbd * per) * SB_QB)
                _, acc = lax.fori_loop(0, sbd, lambda jj, c: step(sbd - 1 - jj, c[0], c[1], None), (cl, acc))
                o_ref[rows, lanes] = acc
                return 0

            lax.fori_loop(0, nb, qblock, 0)

    blk = lambda off: pl.BlockSpec((S, LANES), lambda g: (0, off + g))
    return _call_carrying(
        body, name, pairs, carry, [p, p, p, gq, gkt],
        in_specs=[blk(0), blk(pairs), blk(2 * pairs), pl.BlockSpec((1, dh), lambda g: (0, 0)),
                  pl.BlockSpec((dh, 1), lambda g: (0, 0))],
        out_specs=[blk(0)], out_shape=[jax.ShapeDtypeStruct((S, SB_W), F32)],
        scratch_shapes=[pltpu.VMEM((LANES, S), F32), pltpu.VMEM((S, dh), BF16), pltpu.VMEM((dh, S), BF16),
                        pltpu.VMEM((S, dh), BF16)],
        vmem=VMEM_LIMIT)


def _call_carrying(body, name, grid, carry, operands, in_specs, out_specs, out_shape, scratch_shapes, vmem):
    grid = (grid,) if isinstance(grid, int) else tuple(grid)
    if carry is None:
        res = pl.pallas_call(body, name=name, grid=grid, in_specs=in_specs, out_specs=out_specs,
                             out_shape=out_shape, scratch_shapes=scratch_shapes,
                             compiler_params=_cp(*["arbitrary"] * len(grid), vmem=vmem))(*operands)
        return res, []

    def at(corner):
        hit = pl.program_id(0) == corner(grid[0])
        for axis in range(1, len(grid)):
            hit = jnp.logical_and(hit, pl.program_id(axis) == corner(grid[axis]))
        return hit

    ex = _Exchange(*carry)
    n_in, n_out, n_scr = len(in_specs), len(out_specs), len(scratch_shapes)

    def wrapped(*refs):
        ins, refs = refs[:n_in], refs[n_in:]
        xin, refs = refs[:ex.n], refs[ex.n:]
        outs, refs = refs[:n_out], refs[n_out:]
        xout, refs = refs[:ex.n], refs[ex.n:]
        scr, sems = refs[:n_scr], refs[n_scr:]

        @pl.when(at(lambda n: 0))
        def _():
            ex.start(xin, xout, sems)

        body(*ins, *outs, *scr)

        @pl.when(at(lambda n: n - 1))
        def _():
            ex.finish(xin, xout, sems)

    res = pl.pallas_call(wrapped, name=name, grid=grid, in_specs=in_specs + ex.specs,
                         out_specs=out_specs + ex.specs, out_shape=out_shape + ex.out_shape,
                         scratch_shapes=scratch_shapes + ex.scratch,
                         compiler_params=_cp(*["arbitrary"] * len(grid), vmem=vmem))(*operands, *carry[0])
    return res[:n_out], res[n_out:]


def _sb_bwd(p, o, do, gq, gqt, gk, gkt, name, carry=None):
    S, dh = p.shape[0], SB_DH
    kb = min(SB_KEYS, S)
    per = kb // SB_QB
    nb = S // SB_QB
    scale = 1.0 / math.sqrt(dh)
    pairs = SB_W // LANES
    per_pair = LANES // dh

    def body(q_ref, k_ref, v_ref, o_ref, do_ref, gq_ref, gqt_ref, gk_ref, gkt_ref,
             dq_ref, dk_ref, dv_ref, dgq_ref, dgkt_ref,
             qt2, kt2, vt2, dot2, dkt2, dvt2, qa, qat, ka, kat, vb, vtb, dob, dotb, dqa):
        qt2[...] = q_ref[...].T
        kt2[...] = k_ref[...].T
        vt2[...] = v_ref[...].T
        dot2[...] = do_ref[...].T
        dkt2[...] = jnp.zeros_like(dkt2)
        dvt2[...] = jnp.zeros_like(dvt2)
        from_here, diff = _sb_masks(kb)
        for hh in range(per_pair):
            lanes = slice(hh * dh, (hh + 1) * dh)
            qa[...] = _qk_norm(q_ref[:, lanes], gq_ref[...], scale * LOG2E).astype(BF16)
            qat[...] = _qk_norm_t(qt2[lanes, :], gqt_ref[...], scale * LOG2E).astype(BF16)
            ka[...] = _qk_norm(k_ref[:, lanes], gk_ref[...], 1.0).astype(BF16)
            kat[...] = _qk_norm_t(kt2[lanes, :], gkt_ref[...], 1.0).astype(BF16)
            vb[...] = v_ref[:, lanes].astype(BF16)
            vtb[...] = vt2[lanes, :].astype(BF16)
            dob[...] = do_ref[:, lanes].astype(BF16)
            dotb[...] = dot2[lanes, :].astype(BF16)

            def qblock(i, _):
                rows = pl.ds(pl.multiple_of(i * SB_QB, SB_QB), SB_QB)
                qi, qit = qa[rows, :], qat[:, rows]
                doi, doit = dob[rows, :], dotb[:, rows]
                total = jnp.sum(doi.astype(F32) * o_ref[rows, lanes], axis=1, keepdims=True)
                sbd = i // per

                def step(sb, cl, cd, dqi, valid):
                    cols = pl.ds(pl.multiple_of(sb * kb, kb), kb)
                    z, lk, w, tot = _sb_step(qi, kat[:, cols], cl, from_here, valid)
                    w16 = w.astype(BF16)
                    dl = jnp.dot(doi, vtb[:, cols], preferred_element_type=F32) * w16.astype(F32)
                    incl, dtot = _suffix_sums(dl, from_here)
                    sig = jnp.exp2(z + lk)
                    dz = dl - sig * (dl + (total - cd - incl))
                    if valid is not None:
                        dz = jnp.where(valid, dz, 0.0)
                    dz16 = dz.astype(BF16)
                    dqi = dqi + jnp.dot(dz16, ka[cols, :], preferred_element_type=F32)
                    dkt2[lanes, cols] += jnp.dot(qit, dz16, preferred_element_type=F32)
                    dvt2[lanes, cols] += jnp.dot(doit, w16, preferred_element_type=F32)
                    return cl + tot, cd + dtot, dqi

                zero = jnp.zeros((SB_QB, 1), F32)
                first = step(sbd, zero, zero, jnp.zeros((SB_QB, dh), F32), diff < (i - sbd * per) * SB_QB)
                _, _, dqi = lax.fori_loop(0, sbd, lambda jj, c: step(sbd - 1 - jj, c[0], c[1], c[2], None), first)
                dqa[rows, :] = dqi
                return 0

            lax.fori_loop(0, nb, qblock, 0)
            _, vq = jax.vjp(lambda t, g: _qk_norm(t, g, scale), q_ref[:, lanes], gq_ref[...])
            dq, dgq = vq(dqa[...])
            dq_ref[:, lanes] = dq.astype(BF16)
            dgq_ref[hh] = dgq
            _, vk = jax.vjp(lambda t, g: _qk_norm_t(t, g, 1.0), kt2[lanes, :], gkt_ref[...])
            dkt, dgkt = vk(dkt2[lanes, :] * (1.0 / LOG2E))
            dkt2[lanes, :] = dkt
            dgkt_ref[hh] = dgkt
        dk_ref[...] = dkt2[...].T.astype(BF16)
        dv_ref[...] = dvt2[...].T.astype(BF16)

    blk = lambda off: pl.BlockSpec((S, LANES), lambda g: (0, off + g))
    once = lambda off: pl.BlockSpec((S, LANES), lambda g: (0, off + g), pipeline_mode=pl.Buffered(1))
    gr = pl.BlockSpec((1, dh), lambda g: (0, 0))
    gc = pl.BlockSpec((dh, 1), lambda g: (0, 0))
    sd = jax.ShapeDtypeStruct((S, SB_W), BF16)
    return _call_carrying(
        body, name, pairs, carry, [p, p, p, o, do, gq, gqt, gk, gkt],
        in_specs=[once(0), once(pairs), once(2 * pairs), once(0), once(0), gr, gc, gr, gc],
        out_specs=[blk(0), blk(0), blk(0), pl.BlockSpec((per_pair, 1, dh), lambda g: (g, 0, 0)),
                   pl.BlockSpec((per_pair, dh, 1), lambda g: (g, 0, 0))],
        out_shape=[sd, sd, sd, jax.ShapeDtypeStruct((SB_HEADS, 1, dh), F32),
                   jax.ShapeDtypeStruct((SB_HEADS, dh, 1), F32)],
        scratch_shapes=[pltpu.VMEM((LANES, S), F32)] * 6 + [pltpu.VMEM((S, dh), BF16), pltpu.VMEM((dh, S), BF16)] * 4
        + [pltpu.VMEM((S, dh), F32)],
        vmem=60 * 1024 * 1024)


def _shift_down(x, s, rows):
    if s == 0:
        return x
    return jnp.where(rows >= s, pltpu.roll(x, s, 0), 0.0)


def _shift_up(x, s, rows, n):
    if s == 0:
        return x
    return jnp.where(rows < n - s, pltpu.roll(x, n - s, 0), 0.0)


def _conv(x, w_ref, rows):
    y = x * w_ref[CONV_K - 1:CONV_K, :]
    for kk in range(CONV_K - 1):
        y = y + _shift_down(x, CONV_K - 1 - kk, rows) * w_ref[kk:kk + 1, :]
    return y


def _act_norm(y, normed):
    s = _silu(y)
    n = s * lax.rsqrt(jnp.sum(s * s, axis=-1, keepdims=True) + EPS)
    return jnp.where(normed, n, s)


def _dn_prep_fwd(p, conv_w, col0, name):
    S = p.shape[0]
    nblk = 3 * DN_HEADS
    b0 = col0 // DN_DH

    def body(x_ref, w_ref, o_ref):
        rows = lax.broadcasted_iota(jnp.int32, (S, DN_DH), 0)
        y = _conv(x_ref[...], w_ref, rows)
        o_ref[...] = _act_norm(y, pl.program_id(0) < 2 * DN_HEADS)

    return pl.pallas_call(
        body, name=name, grid=(nblk,),
        in_specs=[pl.BlockSpec((S, DN_DH), lambda j: (0, b0 + j)), pl.BlockSpec((CONV_K, DN_DH), lambda j: (0, j))],
        out_specs=pl.BlockSpec((S, DN_DH), lambda j: (0, j)),
        out_shape=jax.ShapeDtypeStruct((S, 3 * DN_W), F32),
        compiler_params=_cp("parallel"),
    )(p, conv_w)


def _dn_prep_bwd(p, conv_w, col0, dout, name):
    S = p.shape[0]
    nblk = 3 * DN_HEADS
    b0 = col0 // DN_DH

    def body(x_ref, w_ref, do_ref, dx_ref, dw_ref):
        rows = lax.broadcasted_iota(jnp.int32, (S, DN_DH), 0)
        x = x_ref[...]
        y = _conv(x, w_ref, rows)
        normed = pl.program_id(0) < 2 * DN_HEADS
        _, vjp = jax.vjp(lambda t: _act_norm(t, normed), y)
        (dy,) = vjp(do_ref[...])
        dx = dy * w_ref[CONV_K - 1:CONV_K, :]
        dw_ref[CONV_K - 1:CONV_K, :] = jnp.sum(dy * x, axis=0, keepdims=True)
        for kk in range(CONV_K - 1):
            s = CONV_K - 1 - kk
            dx = dx + _shift_up(dy, s, rows, S) * w_ref[kk:kk + 1, :]
            dw_ref[kk:kk + 1, :] = jnp.sum(dy * _shift_down(x, s, rows), axis=0, keepdims=True)
        dx_ref[...] = dx.astype(BF16)

    return pl.pallas_call(
        body, name=name, grid=(nblk,),
        in_specs=[pl.BlockSpec((S, DN_DH), lambda j: (0, b0 + j)), pl.BlockSpec((CONV_K, DN_DH), lambda j: (0, j)),
                  pl.BlockSpec((S, DN_DH), lambda j: (0, j))],
        out_specs=[pl.BlockSpec((S, DN_DH), lambda j: (0, j)), pl.BlockSpec((CONV_K, DN_DH), lambda j: (0, j))],
        out_shape=[jax.ShapeDtypeStruct((S, 3 * DN_W), BF16), jax.ShapeDtypeStruct((CONV_K, 3 * DN_W), F32)],
        compiler_params=_cp("parallel"),
    )(p, conv_w, dout)


def _gate_fn(x, pv):
    lane = lax.broadcasted_iota(jnp.int32, x.shape, 1)
    decay = -jnp.exp(pv[0:1, :]) * _softplus(x + pv[1:2, :])
    return jnp.where(lane < DN_HEADS, _sigmoid(x), decay)


def _dn_gate_fwd(p, pv, blk, name):
    S = p.shape[0]

    def body(x_ref, pv_ref, o_ref):
        o_ref[...] = _gate_fn(x_ref[...], pv_ref[...])

    return pl.pallas_call(
        body, name=name, grid=(1,),
        in_specs=[pl.BlockSpec((S, LANES), lambda i: (0, blk)), pl.BlockSpec((2, LANES), lambda i: (0, 0))],
        out_specs=pl.BlockSpec((S, LANES), lambda i: (0, 0)),
        out_shape=jax.ShapeDtypeStruct((S, LANES), F32),
        compiler_params=_cp("arbitrary"),
    )(p, pv)


def _dn_gate_bwd(p, pv, blk, dout, name):
    S = p.shape[0]

    def body(x_ref, pv_ref, do_ref, dx_ref, dpv_ref):
        _, vjp = jax.vjp(_gate_fn, x_ref[...], pv_ref[...])
        dx, dpv = vjp(do_ref[...])
        dx_ref[...] = dx.astype(BF16)
        dpv_ref[...] = dpv

    return pl.pallas_call(
        body, name=name, grid=(1,),
        in_specs=[pl.BlockSpec((S, LANES), lambda i: (0, blk)), pl.BlockSpec((2, LANES), lambda i: (0, 0)),
                  pl.BlockSpec((S, LANES), lambda i: (0, 0))],
        out_specs=[pl.BlockSpec((S, LANES), lambda i: (0, 0)), pl.BlockSpec((2, LANES), lambda i: (0, 0))],
        out_shape=[jax.ShapeDtypeStruct((S, LANES), BF16), jax.ShapeDtypeStruct((2, LANES), F32)],
        compiler_params=_cp("arbitrary"),
    )(p, pv, dout)


def _t(x):
    return jnp.swapaxes(x, -1, -2)


def _matmuls(prec, differentiable):
    cast = (lambda t: t.astype(BF16)) if prec is None else (lambda t: t)

    def mm(a, b):
        return lax.dot_general(cast(a), cast(b), (((2,), (1,)), ((0,), (0,))), precision=prec,
                               preferred_element_type=F32)

    def mm_nt(a, b):
        return lax.dot_general(cast(a), cast(b), (((2,), (2,)), ((0,), (0,))), precision=prec,
                               preferred_element_type=F32)

    if not differentiable:
        return mm, mm_nt
    dmm, dmm_nt = jax.custom_vjp(mm), jax.custom_vjp(mm_nt)
    dmm.defvjp(lambda a, b: (mm(a, b), (a, b)), lambda res, g: (mm_nt(g, res[1]), mm(_t(res[0]), g)))
    dmm_nt.defvjp(lambda a, b: (mm_nt(a, b), (a, b)), lambda res, g: (mm(g, res[1]), mm(_t(g), res[0])))
    return dmm, dmm_nt


def _exact_sums(differentiable):
    def three(x, axis):
        a = x.astype(BF16)
        r = x - a.astype(F32)
        b = r.astype(BF16)
        return jnp.concatenate([a, b, (r - b.astype(F32)).astype(BF16)], axis=axis)

    def left(tri, x):
        return lax.dot_general(jnp.concatenate([tri] * 3, axis=2).astype(BF16), three(x, 1),
                               (((2,), (1,)), ((0,), (0,))), preferred_element_type=F32)

    def right(x, tri):
        return lax.dot_general(three(x, 2), jnp.concatenate([tri] * 3, axis=1).astype(BF16),
                               (((2,), (1,)), ((0,), (0,))), preferred_element_type=F32)

    if not differentiable:
        return left, right
    dleft, dright = jax.custom_vjp(left), jax.custom_vjp(right)
    dleft.defvjp(lambda tri, x: (left(tri, x), tri), lambda tri, g: (jnp.zeros_like(tri), left(_t(tri), g)))
    dright.defvjp(lambda x, tri: (right(x, tri), tri), lambda tri, g: (right(g, _t(tri)), jnp.zeros_like(tri)))
    return dleft, dright


def _known_inverse(mm):
    f = jax.custom_vjp(lambda n, inv: inv)
    f.defvjp(lambda n, inv: (inv, inv),
             lambda inv, g: (mm(mm(_t(inv), g), _t(inv)), jnp.zeros_like(inv)))
    return f


def _delta_chunk(state, q, k, v, beta, a_col, a_row, differentiable=False, inv_known=None):
    mm, _ = _matmuls(lax.Precision.HIGH, differentiable)
    sum_left, sum_right = _exact_sums(differentiable)
    ein, ein_nt = _matmuls(None, differentiable)
    H, C, _ = q.shape
    r = lax.broadcasted_iota(jnp.int32, (H, C, C), 1)
    c = lax.broadcasted_iota(jnp.int32, (H, C, C), 2)
    tril, strict = r >= c, r > c
    eye = (r == c).astype(F32)
    g_c = sum_left(tril.astype(F32), jnp.broadcast_to(a_col, (H, C, C)))
    g_r = sum_right(jnp.broadcast_to(a_row, (H, C, C)), (r <= c).astype(F32))
    decay = jnp.where(tril, jnp.exp(jnp.where(tril, g_c - g_r, 0.0)), 0.0)
    eg = jnp.exp(g_c)
    g_last = jnp.sum(jnp.where(r == C - 1, g_c, 0.0), axis=1, keepdims=True)
    qs = q * (float(q.shape[2]) ** -0.5)
    kb = k * beta
    neg_m = jnp.where(strict, -(ein_nt(kb, k) * decay), 0.0)
    if inv_known is None:
        inv = eye + neg_m
        pw = neg_m
        for _ in range(int(math.log2(C)) - 1):
            pw = mm(pw, pw)
            inv = inv + mm(inv, pw)
    else:
        inv = _known_inverse(mm)(neg_m, inv_known)
    u = mm(inv, v * beta)
    w = mm(inv, kb * eg)
    intra = jnp.where(tril, ein_nt(qs, k) * decay, 0.0)
    v_new = u - ein(w, state)
    o = ein(qs * eg, state) + ein(intra, v_new)
    nxt = state * jnp.exp(g_last) + ein(_t(k * jnp.exp(g_last - g_c)), v_new)
    return o, nxt, inv


def _heads(t):
    return jnp.stack([t[:, h * DN_DH:(h + 1) * DN_DH] for h in range(DN_HEADS)])


def _delta_step(state, q, k, v, bg, a_row, differentiable=False, inv_known=None):
    lane = lax.broadcasted_iota(jnp.int32, bg.shape, 1)
    pick = lambda j: jnp.stack([jnp.sum(jnp.where(lane == j + h, bg, 0.0), axis=1, keepdims=True)
                                for h in range(DN_HEADS)])
    o, nxt, inv = _delta_chunk(state, _heads(q), _heads(k), _heads(v), pick(0), pick(DN_HEADS), a_row, differentiable,
                               inv_known)
    return jnp.concatenate([o[h] for h in range(DN_HEADS)], axis=1), nxt, inv


def _delta_fwd(qkv, bg, a_row, name):
    S = qkv.shape[0]
    nc = S // BLK

    def body(q_ref, k_ref, v_ref, bg_ref, ar_ref, o_ref, st_ref, inv_ref, state):
        ci = pl.program_id(0)

        @pl.when(ci == 0)
        def _():
            state[...] = jnp.zeros_like(state)

        st = state[...]
        st_ref[:, 0] = st
        o, nxt, inv = _delta_step(st, q_ref[...], k_ref[...], v_ref[...], bg_ref[...], ar_ref[:, pl.ds(ci, 1), :])
        o_ref[...] = o
        inv_ref[:, 0] = inv
        state[...] = nxt

    part = lambda j: pl.BlockSpec((BLK, DN_W), lambda c: (c, j))
    per_chunk = pl.BlockSpec((DN_HEADS, 1, DN_DH, DN_DH), lambda c: (0, c, 0, 0))
    mats = jax.ShapeDtypeStruct((DN_HEADS, nc, DN_DH, DN_DH), F32)
    return pl.pallas_call(
        body, name=name, grid=(nc,),
        in_specs=[part(0), part(1), part(2), pl.BlockSpec((BLK, LANES), lambda c: (c, 0)),
                  pl.BlockSpec((DN_HEADS, nc, BLK), lambda c: (0, 0, 0))],
        out_specs=[part(0), per_chunk, per_chunk], out_shape=[jax.ShapeDtypeStruct((S, DN_W), F32), mats, mats],
        scratch_shapes=[pltpu.VMEM((DN_HEADS, DN_DH, DN_DH), F32)],
        compiler_params=_cp("arbitrary"),
    )(qkv, qkv, qkv, bg, a_row)


def _delta_bwd(qkv, bg, a_row, states, invs, do, name):
    S = qkv.shape[0]
    nc = S // BLK

    def body(q_ref, k_ref, v_ref, bg_ref, ar_ref, st_ref, inv_ref, do_ref, dqkv_ref, dbg_ref, dar_ref, dstate):
        t = pl.program_id(0)
        ci = nc - 1 - t

        @pl.when(t == 0)
        def _():
            dstate[...] = jnp.zeros_like(dstate)

        step = lambda *a: _delta_step(*a, differentiable=True, inv_known=inv_ref[:, 0])[:2]
        _, vjp = jax.vjp(step, st_ref[:, 0], q_ref[...], k_ref[...], v_ref[...], bg_ref[...],
                         ar_ref[:, pl.ds(ci, 1), :])
        dprev, dq, dk, dv, dbg, dar = vjp((do_ref[...], dstate[...]))
        dqkv_ref[:, 0:DN_W] = dq
        dqkv_ref[:, DN_W:2 * DN_W] = dk
        dqkv_ref[:, 2 * DN_W:3 * DN_W] = dv
        dbg_ref[...] = dbg
        dar_ref[:, pl.ds(ci, 1), :] = dar
        dstate[...] = dprev

    part = lambda j: pl.BlockSpec((BLK, DN_W), lambda t: (nc - 1 - t, j))
    lanes = pl.BlockSpec((BLK, LANES), lambda t: (nc - 1 - t, 0))
    rows = pl.BlockSpec((DN_HEADS, nc, BLK), lambda t: (0, 0, 0))
    per_chunk = pl.BlockSpec((DN_HEADS, 1, DN_DH, DN_DH), lambda t: (0, nc - 1 - t, 0, 0))
    return pl.pallas_call(
        body, name=name, grid=(nc,),
        in_specs=[part(0), part(1), part(2), lanes, rows, per_chunk, per_chunk, part(0)],
        out_specs=[pl.BlockSpec((BLK, 3 * DN_W), lambda t: (nc - 1 - t, 0)), lanes, rows],
        out_shape=[jax.ShapeDtypeStruct((S, 3 * DN_W), F32), jax.ShapeDtypeStruct((S, LANES), F32),
                   jax.ShapeDtypeStruct((DN_HEADS, nc, BLK), F32)],
        scratch_shapes=[pltpu.VMEM((DN_HEADS, DN_DH, DN_DH), F32)],
        compiler_params=_cp("arbitrary"),
    )(qkv, qkv, qkv, bg, a_row, states, invs, do)


def _gate_sb(o, z):
    return o * _silu(z)


def _gate_dn(o, z, g):
    return jnp.concatenate(
        [_rms(o[:, h * DN_DH:(h + 1) * DN_DH]) * g * _silu(z[:, h * DN_DH:(h + 1) * DN_DH]) for h in range(DN_HEADS)],
        axis=1)


def _merge_specs(S, D, tm):
    row = lambda w, blk: pl.BlockSpec((tm, w), lambda i: (i, blk))
    full = lambda a, b: pl.BlockSpec((a, b), lambda i: (0, 0))
    return [row(D, 0), full(1, D), row(SB_W, 0), row(SB_W, 3), row(DN_W, 0), row(DN_W, 7),
            row(D, IN_MAIN // D), row(D, IN_MAIN // D + 1), full(1, DN_DH), full(SB_W, D), full(DN_W, D), full(D, D)]


def _merge_fwd(x, gate, o_sb, o_dn, p, ng, wbs, wbd, wo, name):
    S, D = x.shape
    tm = min(512, S)

    def body(x_ref, gate_ref, osb_ref, zsb_ref, odn_ref, zdn_ref, msb_ref, mdn_ref, ng_ref, wbs_ref, wbd_ref, wo_ref,
             out_ref):
        a = _gate_sb(osb_ref[...], zsb_ref[...])
        b = _gate_dn(odn_ref[...], zdn_ref[...], ng_ref[...])
        y = _sigmoid(msb_ref[...]) * _bdot(a, wbs_ref[...]) + _sigmoid(mdn_ref[...]) * _bdot(b, wbd_ref[...])
        out_ref[...] = x_ref[...] + gate_ref[...] * _bdot(y, wo_ref[...])

    return pl.pallas_call(
        body, name=name, grid=(S // tm,), in_specs=_merge_specs(S, D, tm),
        out_specs=pl.BlockSpec((tm, D), lambda i: (i, 0)), out_shape=jax.ShapeDtypeStruct((S, D), F32),
        compiler_params=_cp("parallel"),
    )(x, gate, o_sb, p, o_dn, p, p, p, ng, wbs, wbd, wo)


def _merge_bwd(dxn, gate, o_sb, o_dn, p, ng, wbs, wbd, wo, name):
    S, D = dxn.shape
    tm = min(256, S)

    def body(dxn_ref, gate_ref, osb_ref, zsb_ref, odn_ref, zdn_ref, msb_ref, mdn_ref, ng_ref, wbs_ref, wbd_ref, wo_ref,
             dosb_ref, dzsb_ref, dodn_ref, dzdn_ref, dmsb_ref, dmdn_ref, dwo_ref, dwbs_ref, dwbd_ref, dgate_ref, dng_ref):
        @pl.when(pl.program_id(0) == 0)
        def _():
            for ref in (dwo_ref, dwbs_ref, dwbd_ref, dgate_ref, dng_ref):
                ref[...] = jnp.zeros_like(ref)

        a, vjp_a = jax.vjp(_gate_sb, osb_ref[...], zsb_ref[...])
        b, vjp_b = jax.vjp(_gate_dn, odn_ref[...], zdn_ref[...], ng_ref[...])
        a16, b16 = a.astype(BF16), b.astype(BF16)
        ps = jnp.dot(a16, wbs_ref[...], preferred_element_type=F32)
        pd = jnp.dot(b16, wbd_ref[...], preferred_element_type=F32)
        ss, sd = _sigmoid(msb_ref[...]), _sigmoid(mdn_ref[...])
        y16 = (ss * ps + sd * pd).astype(BF16)
        out = jnp.dot(y16, wo_ref[...], preferred_element_type=F32)
        dxn_v = dxn_ref[...]
        dgate_ref[...] += jnp.sum(dxn_v * out, axis=0, keepdims=True)
        dout16 = (dxn_v * gate_ref[...]).astype(BF16)
        dwo_ref[...] += _bdot_tn(y16, dout16)
        dy = _bdot_nt(dout16, wo_ref[...])
        dmsb_ref[...] = (dy * ps * ss * (1.0 - ss)).astype(BF16)
        dmdn_ref[...] = (dy * pd * sd * (1.0 - sd)).astype(BF16)
        dps16, dpd16 = (dy * ss).astype(BF16), (dy * sd).astype(BF16)
        dwbs_ref[...] += _bdot_tn(a16, dps16)
        dwbd_ref[...] += _bdot_tn(b16, dpd16)
        dosb, dzsb = vjp_a(_bdot_nt(dps16, wbs_ref[...]))
        dodn, dzdn, dng = vjp_b(_bdot_nt(dpd16, wbd_ref[...]))
        dosb_ref[...] = dosb
        dzsb_ref[...] = dzsb.astype(BF16)
        dodn_ref[...] = dodn
        dzdn_ref[...] = dzdn.astype(BF16)
        dng_ref[...] += dng

    row = lambda w: pl.BlockSpec((tm, w), lambda i: (i, 0))
    full = lambda a, b: pl.BlockSpec((a, b), lambda i: (0, 0))
    sds = jax.ShapeDtypeStruct
    return pl.pallas_call(
        body, name=name, grid=(S // tm,), in_specs=_merge_specs(S, D, tm),
        out_specs=[row(SB_W), row(SB_W), row(DN_W), row(DN_W), row(D), row(D),
                   full(D, D), full(SB_W, D), full(DN_W, D), full(1, D), full(1, DN_DH)],
        out_shape=[sds((S, SB_W), F32), sds((S, SB_W), BF16), sds((S, DN_W), F32), sds((S, DN_W), BF16),
                   sds((S, D), BF16), sds((S, D), BF16), sds((D, D), F32), sds((SB_W, D), F32), sds((DN_W, D), F32),
                   sds((1, D), F32), sds((1, DN_DH), F32)],
        compiler_params=_cp("arbitrary"),
    )(dxn, gate, o_sb, p, o_dn, p, p, p, ng, wbs, wbd, wo)


def _loss_fwd_bwd(y, target, name):
    S, D = y.shape
    tm = min(512, S)

    def body(y_ref, t_ref, l_ref, dy_ref):
        @pl.when(pl.program_id(0) == 0)
        def _():
            l_ref[...] = jnp.zeros_like(l_ref)

        e = y_ref[...] - t_ref[...]
        l_ref[...] += jnp.sum(e * e, axis=0, keepdims=True) * (0.5 / D)
        dy_ref[...] = e * (1.0 / D)

    row = pl.BlockSpec((tm, D), lambda i: (i, 0))
    return pl.pallas_call(
        body, name=name, grid=(S // tm,), in_specs=[row, row],
        out_specs=[pl.BlockSpec((1, D), lambda i: (0, 0)), row],
        out_shape=[jax.ShapeDtypeStruct((1, D), F32), jax.ShapeDtypeStruct((S, D), F32)],
        compiler_params=_cp("arbitrary"),
    )(y, target)


def _mod_fwd(c_all, ada_w, name):
    L, D, n = ada_w.shape

    def body(c_ref, w_ref, o_ref):
        o_ref[0] = _dot(_silu(c_ref[...]), w_ref[0])

    return pl.pallas_call(
        body, name=name, grid=(L,),
        in_specs=[pl.BlockSpec(c_all.shape, lambda l: (0, 0)), pl.BlockSpec((1, D, n), lambda l: (l, 0, 0))],
        out_specs=pl.BlockSpec((1, N_DEV, n), lambda l: (l, 0, 0)),
        out_shape=jax.ShapeDtypeStruct((L, N_DEV, n), F32),
        compiler_params=_cp("parallel"),
    )(c_all, ada_w)


def _mod_bwd_w(c_all_t, dmod, name):
    L, _, n = dmod.shape
    D = c_all_t.shape[0]

    def body(c_ref, d_ref, o_ref):
        o_ref[0] = _dot(_silu(c_ref[...]), d_ref[0])

    return pl.pallas_call(
        body, name=name, grid=(L,),
        in_specs=[pl.BlockSpec(c_all_t.shape, lambda l: (0, 0)), pl.BlockSpec((1, N_DEV, n), lambda l: (l, 0, 0))],
        out_specs=pl.BlockSpec((1, D, n), lambda l: (l, 0, 0)),
        out_shape=jax.ShapeDtypeStruct((L, D, n), F32),
        compiler_params=_cp("parallel"),
    )(c_all_t, dmod)


def _me():
    return lax.axis_index("x"), lax.axis_index("y"), lax.axis_index("c")


def _peer(k):
    x, y, c = _me()
    return (1 - x if k & 4 else x, 1 - y if k & 2 else y, 1 - c if k & 1 else c)


def _lin(dev):
    return 4 * dev[0] + 2 * dev[1] + dev[2]


class _Exchange:
    def __init__(self, arrays, scatter):
        self.n = len(arrays)
        self.scatter = scatter
        self.out_shape = [jax.ShapeDtypeStruct((N_DEV,) + tuple(a.shape[1:] if scatter else a.shape), a.dtype)
                          for a in arrays]
        self.specs = [pl.BlockSpec(memory_space=pl.ANY)] * self.n
        self.scratch = [pltpu.SemaphoreType.DMA((self.n, N_DEV - 1)), pltpu.SemaphoreType.DMA((self.n, N_DEV - 1)),
                        pltpu.SemaphoreType.DMA((self.n,))]

    def _copies(self, ins, outs, sems):
        send_sems, recv_sems, local_sems = sems
        me = _lin(_me())
        local, direct, passed, landed = [], [], [], []
        for t in range(self.n):
            src_of = (lambda d, t=t: ins[t].at[d]) if self.scatter else (lambda d, t=t: ins[t])
            local.append(pltpu.make_async_copy(src_of(me), outs[t].at[me], local_sems.at[t]))
            for k in range(1, N_DEV):
                peer = _peer(k)
                pair = dict(send_sem=send_sems.at[t, k - 1], recv_sem=recv_sems.at[t, k - 1],
                            device_id_type=pl.DeviceIdType.MESH)
                slot = outs[t].at[_lin(peer)]
                landed.append(pltpu.make_async_remote_copy(src_ref=slot, dst_ref=slot, device_id=peer, **pair))
                if self.scatter or k in (1, 2, 4, 6):
                    direct.append(pltpu.make_async_remote_copy(src_ref=src_of(_lin(peer)), dst_ref=outs[t].at[me],
                                                               device_id=peer, **pair))
                else:
                    came = outs[t].at[_lin(_peer(k - 1))]
                    passed.append((landed[-2], pltpu.make_async_remote_copy(src_ref=came, dst_ref=came,
                                                                            device_id=_peer(1), **pair)))
        return local, direct, passed, landed

    def start(self, ins, outs, sems):
        local, direct, _, _ = self._copies(ins, outs, sems)
        for cp in local + direct:
            cp.start()

    def finish(self, ins, outs, sems):
        local, direct, passed, landed = self._copies(ins, outs, sems)
        arrived = set()
        for came, onward in passed:
            came.wait_recv()
            arrived.add(id(came))
            onward.start()
        for cp in landed:
            if id(cp) not in arrived:
                cp.wait_recv()
        for cp in direct + [onward for _, onward in passed]:
            cp.wait_send()
        for cp in local:
            cp.wait()


def _exchange(arrays, scatter, name):
    ex = _Exchange(arrays, scatter)

    def body(*refs):
        ins, outs, sems = refs[:ex.n], refs[ex.n:2 * ex.n], refs[2 * ex.n:]
        ex.start(ins, outs, sems)
        ex.finish(ins, outs, sems)

    return pl.pallas_call(body, name=name, in_specs=ex.specs, out_specs=ex.specs, out_shape=ex.out_shape,
                          scratch_shapes=ex.scratch)(*arrays)


def _sum_slots(a, name):
    _, R, C = a.shape
    tr = SUM_ROWS if R % SUM_ROWS == 0 else R

    def body(a_ref, o_ref):
        acc = a_ref[0].astype(F32)
        for s in range(1, N_DEV):
            acc = acc + a_ref[s].astype(F32)
        o_ref[...] = acc

    return pl.pallas_call(
        body, name=name, grid=(R // tr,),
        in_specs=[pl.BlockSpec((N_DEV, tr, C), lambda i: (0, i, 0))], out_specs=pl.BlockSpec((tr, C), lambda i: (i, 0)),
        out_shape=jax.ShapeDtypeStruct((R, C), F32), compiler_params=_cp("parallel"),
    )(a)


def _adamw(w, g, m, v, name):
    shape = w.shape
    C = shape[-1]
    R = w.size // C
    tr = R
    for cand in (256, 128, 64):
        if R > cand and R % cand == 0:
            tr = cand
            break
    c1 = 1.0 / (1.0 - ADAM_B1 ** ADAM_STEP)
    c2 = 1.0 / (1.0 - ADAM_B2 ** ADAM_STEP)

    def body(w_ref, g_ref, m_ref, v_ref, d_ref, nm_ref, nv_ref):
        gv = g_ref[...]
        nm = ADAM_B1 * m_ref[...] + (1.0 - ADAM_B1) * gv
        nv = ADAM_B2 * v_ref[...] + (1.0 - ADAM_B2) * (gv * gv)
        d_ref[...] = -ADAM_LR * ((nm * c1) / (jnp.sqrt(nv * c2) + ADAM_EPS) + ADAM_WD * w_ref[...])
        nm_ref[...] = nm
        nv_ref[...] = nv

    spec = pl.BlockSpec((tr, C), lambda i: (i, 0))
    sd = jax.ShapeDtypeStruct((R, C), F32)
    outs = pl.pallas_call(
        body, name=name, grid=(R // tr,), in_specs=[spec] * 4, out_specs=[spec] * 3, out_shape=[sd] * 3,
        compiler_params=_cp("parallel"),
    )(*(t.reshape(R, C) for t in (w, g, m, v)))
    return tuple(t.reshape(shape) for t in outs)


def _col_segments(D):
    return [(0, IN_MAIN, 0), (IN_COLS, IN_COLS + 2 * D, IN_MAIN), (IN_MAIN, IN_COLS, IN_MAIN + 2 * D)]


def _pad_cols_of_blocks(wi):
    _, D, n = wi.shape
    pieces = []
    for lo, hi, _ in _col_segments(D):
        for d in range(N_DEV):
            a, b = max(lo, d * n), min(hi, (d + 1) * n)
            if a < b:
                pieces.append(wi[d][:, a - d * n:b - d * n])
    return jnp.concatenate(pieces + [jnp.zeros((D, LANES - 2 * DN_HEADS), wi.dtype)], axis=1)


def _blocks_of_padded(dw, n):
    D = dw.shape[0]
    out = []
    for d in range(N_DEV):
        pieces = []
        for lo, hi, at in sorted(_col_segments(D)):
            a, b = max(lo, d * n), min(hi, (d + 1) * n)
            if a < b:
                pieces.append(dw[:, at + a - lo:at + b - lo])
        out.append(pieces[0] if len(pieces) == 1 else jnp.concatenate(pieces, axis=1))
    return jnp.stack(out)


def _gate_params(a_log, dt_bias):
    z = jnp.zeros((LANES,), F32)
    return jnp.stack([z.at[DN_HEADS:2 * DN_HEADS].set(a_log), z.at[DN_HEADS:2 * DN_HEADS].set(dt_bias)])


def _layer_fwd(l, x, mod, wts, carry=None):
    S, D = x.shape
    tag = f"l{l}_"
    p, h = _inproj_fwd(x, mod, wts["norm_g"], wts["w_in"], tag + "inproj_fwd")
    (o_sb,), carried = _sb_fwd(p, wts["sb_q_g"], wts["sb_k_g"].T, tag + "sb_fwd", carry)
    qkv = _dn_prep_fwd(p, wts["conv_w"], 4 * SB_W, tag + "dn_prep_fwd")
    pv = _gate_params(wts["dn_a_log"], wts["dn_dt_bias"])
    ba_blk = (IN_MAIN + 2 * D) // LANES
    bg = _dn_gate_fwd(p, pv, ba_blk, tag + "dn_gate_fwd")
    a_row = bg[:, DN_HEADS:2 * DN_HEADS].T.reshape(DN_HEADS, S // BLK, BLK)
    o_dn, states, invs = _delta_fwd(qkv, bg, a_row, tag + "delta_fwd")
    gate = mod[:, 2 * D:]
    out = _merge_fwd(x, gate, o_sb, o_dn, p, wts["dn_norm_g"], wts["w_branch_sb"], wts["w_branch_dn"], wts["w_out"],
                     tag + "merge_fwd")
    saved = dict(x=x, mod=mod, p=p, h=h, o_sb=o_sb, qkv=qkv, pv=pv, bg=bg, a_row=a_row, o_dn=o_dn, states=states,
                 invs=invs, gate=gate)
    return out, saved, carried


def _layer_bwd(l, dxn, sv, wts, carry_of=None, late_carry_of=None):
    S, D = dxn.shape
    tag = f"l{l}_"
    (dosb, dzsb, dodn, dzdn, dmsb, dmdn, dwo, dwbs, dwbd, dgate, dng) = _merge_bwd(
        dxn, sv["gate"], sv["o_sb"], sv["o_dn"], sv["p"], wts["dn_norm_g"], wts["w_branch_sb"], wts["w_branch_dn"],
        wts["w_out"], tag + "merge_bwd")
    carry = None if carry_of is None else carry_of(dict(w_out=dwo, w_branch_sb=dwbs, w_branch_dn=dwbd))
    (dq, dk, dv, dgq, dgkt), carried = _sb_bwd(sv["p"], sv["o_sb"], dosb, wts["sb_q_g"], wts["sb_q_g"].T,
                                                wts["sb_k_g"], wts["sb_k_g"].T, tag + "sb_bwd", carry)
    dqkv_n, dbg, dar = _delta_bwd(sv["qkv"], sv["bg"], sv["a_row"], sv["states"], sv["invs"], dodn, tag + "delta_bwd")
    dqkv, dconv = _dn_prep_bwd(sv["p"], wts["conv_w"], 4 * SB_W, dqkv_n, tag + "dn_prep_bwd")
    dbg = dbg.at[:, DN_HEADS:2 * DN_HEADS].add(dar.reshape(DN_HEADS, S).T)
    ba_blk = (IN_MAIN + 2 * D) // LANES
    dba, dpv = _dn_gate_bwd(sv["p"], sv["pv"], ba_blk, dbg, tag + "dn_gate_bwd")
    dp = jnp.concatenate([dq, dk, dv, dzsb, dqkv, dzdn, dmsb, dmdn, dba], axis=1)
    dw_in = _matmul_tn(sv["h"].T, dp, tag + "inproj_bwd_dw")
    late = None if late_carry_of is None else late_carry_of(dict(w_in=dw_in, conv_w=dconv))
    (dx, dmod, dg), carried_late = _inproj_bwd_dx(dp, wts["w_in"], sv["x"], sv["mod"], wts["norm_g"], dxn,
                                                  tag + "inproj_bwd_dx", late)
    dmod = dmod.at[:, 2 * D:].set(dgate)
    grads = dict(w_in=dw_in, w_branch_sb=dwbs, w_branch_dn=dwbd, w_out=dwo, conv_w=dconv,
                 mod=dmod[0], norm_g=dg[0], sb_q_g=jnp.sum(dgq, axis=0)[0], sb_k_g=jnp.sum(dgkt, axis=0)[:, 0],
                 dn_a_log=dpv[0, DN_HEADS:2 * DN_HEADS], dn_dt_bias=dpv[1, DN_HEADS:2 * DN_HEADS], dn_norm_g=dng[0])
    return dx, grads, carried, carried_late


def _pad_rows(a, mult):
    extra = (-a.shape[0]) % mult
    return a if extra == 0 else jnp.concatenate([a, jnp.zeros((extra,) + a.shape[1:], a.dtype)], axis=0)


def _pack_rows(parts, width, mult):
    flat = jnp.concatenate([t.reshape(-1) for t in parts])
    extra = (-flat.shape[0]) % width
    if extra:
        flat = jnp.concatenate([flat, jnp.zeros((extra,), flat.dtype)])
    return _pad_rows(flat.reshape(-1, width), mult)


def _take(flat, off, shape):
    n = math.prod(shape)
    return flat[..., off:off + n].reshape(flat.shape[:-1] + tuple(shape)), off + n


SMALL = ("mod", "norm_g", "sb_q_g", "sb_k_g", "dn_a_log", "dn_dt_bias", "dn_norm_g")


def kernel(x, c, ada_w, ada_b, norm_g, w_in, sb_q_g, sb_k_g, conv_w, dn_a_log, dn_dt_bias, dn_norm_g, w_branch_sb, w_branch_dn, w_out, loss_target, m_ada_w, m_ada_b, m_norm_g, m_w_in, m_sb_q_g, m_sb_k_g, m_conv_w, m_dn_a_log, m_dn_dt_bias, m_dn_norm_g, m_w_branch_sb, m_w_branch_dn, m_w_out, v_ada_w, v_ada_b, v_norm_g, v_w_in, v_sb_q_g, v_sb_k_g, v_conv_w, v_dn_a_log, v_dn_dt_bias, v_dn_norm_g, v_w_branch_sb, v_w_branch_dn, v_w_out):
    L, D = norm_g.shape
    S = x.shape[1]
    n_in = w_in.shape[2]
    n_ada = ada_w.shape[2]
    n_br = w_branch_sb.shape[2]
    n_out = w_out.shape[1]
    n_conv = conv_w.shape[2]
    me = _lin(_me())

    def cat(a):
        return jnp.concatenate([a[d] for d in range(N_DEV)], axis=1)

    c_all, conv_all = _exchange([c, conv_w.reshape(L * CONV_K, n_conv)], False, "gather_small")
    c_all = c_all.reshape(N_DEV, D)
    conv_full = cat(conv_all).reshape(L, CONV_K, N_DEV * n_conv)

    mod_part = _mod_fwd(c_all, ada_w, "mod_fwd")

    def shards16(l):
        return [w_in[l].astype(BF16), w_branch_sb[l].astype(BF16), w_branch_dn[l].astype(BF16), w_out[l].astype(BF16)]

    def whole(l, got):
        wi, wbs, wbd, wo = got
        return dict(norm_g=norm_g[l:l + 1], w_in=_pad_cols_of_blocks(wi), sb_q_g=sb_q_g[l:l + 1], sb_k_g=sb_k_g[l:l + 1],
                    conv_w=conv_full[l], dn_a_log=dn_a_log[l], dn_dt_bias=dn_dt_bias[l], dn_norm_g=dn_norm_g[l:l + 1],
                    w_branch_sb=cat(wbs), w_branch_dn=cat(wbd), w_out=wo.reshape(N_DEV * n_out, D))

    *got, mod_all = _exchange(shards16(0) + [mod_part.reshape(L * N_DEV, n_ada)], False, "gather_weights")
    mod_full = cat(mod_all).reshape(L, N_DEV, N_DEV * n_ada) + ada_b[:, None, :]
    mod_mine = lax.dynamic_slice_in_dim(mod_full, me, 1, axis=1)

    act = x[0]
    saved, wts = [], []
    for l in range(L):
        wts.append(whole(l, got))
        act, sv, got = _layer_fwd(l, act, mod_mine[l], wts[l], (shards16(l + 1), False) if l + 1 < L else None)
        saved.append(sv)
    loss_cols, dact = _loss_fwd_bwd(act, loss_target[0], "loss")
    loss = lax.psum(jnp.sum(loss_cols), ("x", "y", "c"))

    def blocks(name, g):
        if name == "w_out":
            return g.astype(BF16).reshape(N_DEV, n_out, D)
        if name == "w_in":
            return _blocks_of_padded(g, n_in).astype(BF16)
        n = g.shape[1] // N_DEV
        dtype = F32 if name == "conv_w" else BF16
        return jnp.stack([g[:, d * n:(d + 1) * n].astype(dtype) for d in range(N_DEV)])

    early, late = ("w_out", "w_branch_sb", "w_branch_dn"), ("w_in", "conv_w")
    grads, recv, pending = [None] * L, {}, []
    for l in reversed(range(L)):
        keys = [k for k, _ in pending] + [(n, l) for n in early]

        def carry_of(g_early, pending=pending):
            return [a for _, a in pending] + [blocks(n, g_early[n]) for n in early], True

        last = l == 0
        dact, grads[l], got, got_late = _layer_bwd(
            l, dact, saved[l], wts[l], carry_of, (lambda g: ([blocks(n, g[n]) for n in late], True)) if last else None)
        recv.update(zip(keys, got))
        recv.update(zip([(n, l) for n in late], got_late))
        pending = [] if last else [((n, l), blocks(n, grads[l][n])) for n in late]
    grad_x = dact[None]
    small_g = _pack_rows([grads[l][n] for l in range(L) for n in SMALL], LANES, 8)
    (small_all_g,) = _exchange([small_g], False, "gather_small_grads")
    small_sum = _sum_slots(small_all_g, "sum_small_grads").reshape(-1)
    shard_shapes = dict(w_in=w_in.shape, w_branch_sb=w_branch_sb.shape, w_branch_dn=w_branch_dn.shape,
                        conv_w=conv_w.shape, w_out=w_out.shape)
    g_out = {n: jnp.stack([_sum_slots(recv[(n, l)], f"sum_{n}_l{l}").reshape(shape[1:]) for l in range(L)])
             for n, shape in shard_shapes.items()}
    small_shapes = dict(mod=(3 * D,), norm_g=(D,), sb_q_g=(SB_DH,), sb_k_g=(SB_DH,), dn_a_log=(DN_HEADS,),
                        dn_dt_bias=(DN_HEADS,), dn_norm_g=(DN_DH,))
    off = 0
    off_all = 0
    small_each = small_all_g.reshape(N_DEV, -1)
    per_small = {n: [] for n in SMALL}
    dmod_all = []
    for l in range(L):
        for n in SMALL:
            t, off = _take(small_sum, off, small_shapes[n])
            per_small[n].append(t)
            if n == "mod":
                t_all, _ = _take(small_each, off_all, small_shapes[n])
                dmod_all.append(t_all)
            off_all += math.prod(small_shapes[n])
    for n in SMALL:
        g_out[n if n != "mod" else "ada_b"] = jnp.stack(per_small[n])
    dmod_all = jnp.stack(dmod_all)
    dmod_cols = lax.dynamic_slice_in_dim(dmod_all, me * n_ada, n_ada, axis=2)
    g_out["ada_w"] = _mod_bwd_w(c_all.T, dmod_cols, "mod_bwd_w")

    given = dict(ada_w=(ada_w, m_ada_w, v_ada_w), ada_b=(ada_b, m_ada_b, v_ada_b), norm_g=(norm_g, m_norm_g, v_norm_g),
                 w_in=(w_in, m_w_in, v_w_in), sb_q_g=(sb_q_g, m_sb_q_g, v_sb_q_g), sb_k_g=(sb_k_g, m_sb_k_g, v_sb_k_g),
                 conv_w=(conv_w, m_conv_w, v_conv_w), dn_a_log=(dn_a_log, m_dn_a_log, v_dn_a_log),
                 dn_dt_bias=(dn_dt_bias, m_dn_dt_bias, v_dn_dt_bias), dn_norm_g=(dn_norm_g, m_dn_norm_g, v_dn_norm_g),
                 w_branch_sb=(w_branch_sb, m_w_branch_sb, v_w_branch_sb),
                 w_branch_dn=(w_branch_dn, m_w_branch_dn, v_w_branch_dn), w_out=(w_out, m_w_out, v_w_out))
    order = list(given)
    upd = {n: _adamw(given[n][0], g_out[n], given[n][1], given[n][2], "adamw_" + n) for n in order}
    return (loss, grad_x, *[g_out[n] for n in order], *[upd[n][0] for n in order], *[upd[n][1] for n in order],
            *[upd[n][2] for n in order])
```

```python
import functools
import math

import jax
import jax.numpy as jnp
from jax import lax
from jax.experimental import pallas as pl
from jax.experimental.pallas import tpu as pltpu

F32 = jnp.float32
BF16 = jnp.bfloat16
HI = lax.Precision.HIGHEST

N_DEV = 8
EPS = 1e-6
SB_HEADS, SB_DH = 8, 64
DN_HEADS, DN_DH = 4, 128
SB_W = SB_HEADS * SB_DH
DN_W = DN_HEADS * DN_DH
CONV_K = 4
BLK = 128
SB_KEYS = 512
SB_QB = 512
LANES = 128
IN_MAIN = 4 * SB_W + 4 * DN_W
IN_COLS = IN_MAIN + 2 * DN_HEADS
ADAM_LR, ADAM_B1, ADAM_B2, ADAM_EPS, ADAM_WD, ADAM_STEP = 0.001, 0.9, 0.999, 1e-08, 0.01, 10
VMEM_LIMIT = 56 * 1024 * 1024
LOG2E = 1.4426950408889634
SUM_ROWS = 128


def _cp(*sem, vmem=VMEM_LIMIT):
    return pltpu.CompilerParams(dimension_semantics=sem if sem else None, vmem_limit_bytes=vmem)


def _dot(a, b, prec=HI):
    return lax.dot_general(a, b, (((1,), (0,)), ((), ())), precision=prec, preferred_element_type=F32)


def _dot_nt(a, b, prec=HI):
    return lax.dot_general(a, b, (((1,), (1,)), ((), ())), precision=prec, preferred_element_type=F32)


def _bdot(a, b):
    return lax.dot_general(a.astype(BF16), b.astype(BF16), (((1,), (0,)), ((), ())), preferred_element_type=F32)


def _bdot_nt(a, b):
    return lax.dot_general(a.astype(BF16), b.astype(BF16), (((1,), (1,)), ((), ())), preferred_element_type=F32)


def _bdot_tn(a, b):
    return lax.dot_general(a.astype(BF16), b.astype(BF16), (((0,), (0,)), ((), ())), preferred_element_type=F32)


def _split_dot(a, b01_twice):
    hi = a.astype(BF16)
    lo = (a - hi.astype(F32)).astype(BF16)
    return jnp.dot(jnp.concatenate([hi, lo], axis=1), b01_twice, preferred_element_type=F32)


def _sigmoid(x):
    return 1.0 / (1.0 + jnp.exp(-x))


def _silu(x):
    return x * _sigmoid(x)


def _softplus(x):
    return jnp.maximum(x, 0.0) + jnp.log(1.0 + jnp.exp(-jnp.abs(x)))


def _rms(x):
    return x * lax.rsqrt(jnp.mean(x * x, axis=-1, keepdims=True) + EPS)


def _prenorm(x, g, shift, scale):
    return _rms(x) * g * (1.0 + scale) + shift


def _inproj_fwd(x, mod, g, w, name, carry=None):
    S, D = x.shape
    N = w.shape[1]
    tm = min(512, S)
    tn = 896 if N % 896 == 0 else 128

    def body(x_ref, mod_ref, g_ref, w_ref, p_ref, h_ref):
        @pl.when(pl.program_id(1) == 0)
        def _():
            h = _prenorm(x_ref[...], g_ref[...], mod_ref[:, 0:D], mod_ref[:, D:2 * D])
            h_ref[...] = h.astype(BF16)

        p_ref[...] = jnp.dot(h_ref[...], w_ref[...], preferred_element_type=F32)

    return _call_carrying(
        body, name, (S // tm, N // tn), carry, [x, mod, g, w],
        in_specs=[pl.BlockSpec((tm, D), lambda i, j: (i, 0)), pl.BlockSpec((1, 3 * D), lambda i, j: (0, 0)),
                  pl.BlockSpec((1, D), lambda i, j: (0, 0)), pl.BlockSpec((D, tn), lambda i, j: (0, j))],
        out_specs=[pl.BlockSpec((tm, tn), lambda i, j: (i, j)), pl.BlockSpec((tm, D), lambda i, j: (i, 0))],
        out_shape=[jax.ShapeDtypeStruct((S, N), F32), jax.ShapeDtypeStruct((S, D), BF16)],
        scratch_shapes=[], vmem=VMEM_LIMIT)


def _inproj_bwd_dx(dp, w, x, mod, g, dxn, name, carry=None):
    S, N = dp.shape
    D = x.shape[1]
    tm = min(512, S)
    tk = 896 if N % 896 == 0 else 128
    nk = N // tk

    def body(dp_ref, w_ref, x_ref, mod_ref, g_ref, dxn_ref, dx_ref, dmod_ref, dg_ref, acc):
        i, k = pl.program_id(0), pl.program_id(1)

        @pl.when(k == 0)
        def _():
            acc[...] = jnp.zeros_like(acc)

        @pl.when((i == 0) & (k == 0))
        def _():
            dmod_ref[...] = jnp.zeros_like(dmod_ref)
            dg_ref[...] = jnp.zeros_like(dg_ref)

        acc[...] += lax.dot_general(dp_ref[...], w_ref[...], (((1,), (1,)), ((), ())), preferred_element_type=F32)

        @pl.when(k == nk - 1)
        def _():
            _, vjp = jax.vjp(_prenorm, x_ref[...], g_ref[...], mod_ref[:, 0:D], mod_ref[:, D:2 * D])
            dx, dg, dshift, dscale = vjp(acc[...])
            dx_ref[...] = dxn_ref[...] + dx
            dg_ref[...] += dg
            dmod_ref[:, 0:D] += dshift
            dmod_ref[:, D:2 * D] += dscale

    return _call_carrying(
        body, name, (S // tm, nk), carry, [dp, w, x, mod, g, dxn],
        in_specs=[pl.BlockSpec((tm, tk), lambda i, k: (i, k)), pl.BlockSpec((D, tk), lambda i, k: (0, k)),
                  pl.BlockSpec((tm, D), lambda i, k: (i, 0)), pl.BlockSpec((1, 3 * D), lambda i, k: (0, 0)),
                  pl.BlockSpec((1, D), lambda i, k: (0, 0)), pl.BlockSpec((tm, D), lambda i, k: (i, 0))],
        out_specs=[pl.BlockSpec((tm, D), lambda i, k: (i, 0)), pl.BlockSpec((1, 3 * D), lambda i, k: (0, 0)),
                   pl.BlockSpec((1, D), lambda i, k: (0, 0))],
        out_shape=[jax.ShapeDtypeStruct((S, D), F32), jax.ShapeDtypeStruct((1, 3 * D), F32),
                   jax.ShapeDtypeStruct((1, D), F32)],
        scratch_shapes=[pltpu.VMEM((tm, D), F32)], vmem=VMEM_LIMIT)


def _matmul_tn(a_t, b, name):
    M, K = a_t.shape
    N = b.shape[1]
    tn = 896 if N % 896 == 0 else (512 if N % 512 == 0 else 128)
    tk = min(512, K)
    nk = K // tk

    def body(a_ref, b_ref, o_ref):
        @pl.when(pl.program_id(1) == 0)
        def _():
            o_ref[...] = jnp.zeros_like(o_ref)

        o_ref[...] += jnp.dot(a_ref[...], b_ref[...], preferred_element_type=F32)

    return pl.pallas_call(
        body, name=name, grid=(N // tn, nk),
        in_specs=[pl.BlockSpec((M, tk), lambda j, k: (0, k)), pl.BlockSpec((tk, tn), lambda j, k: (k, j))],
        out_specs=pl.BlockSpec((M, tn), lambda j, k: (0, j)),
        out_shape=jax.ShapeDtypeStruct((M, N), F32),
        compiler_params=_cp("parallel", "arbitrary"),
    )(a_t, b)


def _qk_norm(t, g, scale):
    return _rms(t) * g * scale


def _qk_norm_t(t, g_col, scale):
    return t * lax.rsqrt(jnp.mean(t * t, axis=0, keepdims=True) + EPS) * g_col * scale


def _suffix_sums(x, tri):
    half = tri.shape[1]
    lo, hi = x[:, :half], x[:, half:]
    hi_sum = jnp.sum(hi, axis=1, keepdims=True)
    y = jnp.concatenate([_split_dot(lo, tri) + hi_sum, _split_dot(hi, tri)], axis=1)
    return y, hi_sum + jnp.sum(lo, axis=1, keepdims=True)


def _sb_step(qi, kat_blk, cl, from_here, valid):
    z = jnp.dot(qi, kat_blk, preferred_element_type=F32)
    nz = -z
    lk = jnp.minimum(nz, 0.0) - jnp.log(1.0 + jnp.exp2(jnp.minimum(z, nz))) * LOG2E
    if valid is not None:
        lk = jnp.where(valid, lk, 0.0)
    later, tot = _suffix_sums(lk, from_here)
    w = jnp.exp2(z + later + cl)
    if valid is not None:
        w = jnp.where(valid, w, 0.0)
    return z, lk, w, tot


def _sb_masks(kb):
    half = kb // 2
    r = lax.broadcasted_iota(jnp.int32, (half, half), 0)
    c = lax.broadcasted_iota(jnp.int32, (half, half), 1)
    rq = lax.broadcasted_iota(jnp.int32, (SB_QB, kb), 0)
    ck = lax.broadcasted_iota(jnp.int32, (SB_QB, kb), 1)
    twice = lambda m: jnp.concatenate([m, m], axis=0).astype(BF16)
    return twice((r >= c).astype(F32)), ck - rq


def _sb_fwd(p, gq, gkt, name, carry=None):
    S, dh = p.shape[0], SB_DH
    kb = min(SB_KEYS, S)
    per = kb // SB_QB
    nb = S // SB_QB
    scale = 1.0 / math.sqrt(dh)
    pairs = SB_W // LANES

    def body(q_ref, k_ref, v_ref, gq_ref, gkt_ref, o_ref, kt2, qa, kat, vb):
        kt2[...] = k_ref[...].T
        from_here, diff = _sb_masks(kb)
        for hh in range(LANES // dh):
            lanes = slice(hh * dh, (hh + 1) * dh)
            qa[...] = _qk_norm(q_ref[:, lanes], gq_ref[...], scale * LOG2E).astype(BF16)
            kat[...] = _qk_norm_t(kt2[lanes, :], gkt_ref[...], 1.0).astype(BF16)
            vb[...] = v_ref[:, lanes].astype(BF16)

            def qblock(i, _):
                rows = pl.ds(pl.multiple_of(i * SB_QB, SB_QB), SB_QB)
                qi = qa[rows, :]
                sbd = i // per

                def step(sb, cl, acc, valid):
                    cols = pl.ds(pl.multiple_of(sb * kb, kb), kb)
                    _, _, w, tot = _sb_step(qi, kat[:, cols], cl, from_here, valid)
                    return cl + tot, acc + jnp.dot(w.astype(BF16), vb[cols, :], preferred_element_type=F32)

                cl, acc = step(sbd, jnp.zeros((SB_QB, 1), F32), jnp.zeros((SB_QB, dh), F32),
                               diff < (i - sbd * per) * SB_QB)
                _, acc = lax.fori_loop(0, sbd, lambda jj, c: step(sbd - 1 - jj, c[0], c[1], None), (cl, acc))
                o_ref[rows, lanes] = acc
                return 0

            lax.fori_loop(0, nb, qblock, 0)

    blk = lambda off: pl.BlockSpec((S, LANES), lambda g: (0, off + g))
    return _call_carrying(
        body, name, pairs, carry, [p, p, p, gq, gkt],
        in_specs=[blk(0), blk(pairs), blk(2 * pairs), pl.BlockSpec((1, dh), lambda g: (0, 0)),
                  pl.BlockSpec((dh, 1), lambda g: (0, 0))],
        out_specs=[blk(0)], out_shape=[jax.ShapeDtypeStruct((S, SB_W), F32)],
        scratch_shapes=[pltpu.VMEM((LANES, S), F32), pltpu.VMEM((S, dh), BF16), pltpu.VMEM((dh, S), BF16),
                        pltpu.VMEM((S, dh), BF16)],
        vmem=VMEM_LIMIT)


def _call_carrying(body, name, grid, carry, operands, in_specs, out_specs, out_shape, scratch_shapes, vmem):
    grid = (grid,) if isinstance(grid, int) else tuple(grid)
    if carry is None:
        res = pl.pallas_call(body, name=name, grid=grid, in_specs=in_specs, out_specs=out_specs,
                             out_shape=out_shape, scratch_shapes=scratch_shapes,
                             compiler_params=_cp(*["arbitrary"] * len(grid), vmem=vmem))(*operands)
        return res, []

    def at(corner):
        hit = pl.program_id(0) == corner(grid[0])
        for axis in range(1, len(grid)):
            hit = jnp.logical_and(hit, pl.program_id(axis) == corner(grid[axis]))
        return hit

    ex = _Exchange(*carry)
    n_in, n_out, n_scr = len(in_specs), len(out_specs), len(scratch_shapes)

    def wrapped(*refs):
        ins, refs = refs[:n_in], refs[n_in:]
        xin, refs = refs[:ex.n], refs[ex.n:]
        outs, refs = refs[:n_out], refs[n_out:]
        xout, refs = refs[:ex.n], refs[ex.n:]
        scr, sems = refs[:n_scr], refs[n_scr:]

        @pl.when(at(lambda n: 0))
        def _():
            ex.start(xin, xout, sems)

        body(*ins, *outs, *scr)

        @pl.when(at(lambda n: n - 1))
        def _():
            ex.finish(xin, xout, sems)

    res = pl.pallas_call(wrapped, name=name, grid=grid, in_specs=in_specs + ex.specs,
                         out_specs=out_specs + ex.specs, out_shape=out_shape + ex.out_shape,
                         scratch_shapes=scratch_shapes + ex.scratch,
                         compiler_params=_cp(*["arbitrary"] * len(grid), vmem=vmem))(*operands, *carry[0])
    return res[:n_out], res[n_out:]


def _sb_bwd(p, o, do, gq, gqt, gk, gkt, name, carry=None):
    S, dh = p.shape[0], SB_DH
    kb = min(SB_KEYS, S)
    per = kb // SB_QB
    nb = S // SB_QB
    scale = 1.0 / math.sqrt(dh)
    pairs = SB_W // LANES
    per_pair = LANES // dh

    def body(q_ref, k_ref, v_ref, o_ref, do_ref, gq_ref, gqt_ref, gk_ref, gkt_ref,
             dq_ref, dk_ref, dv_ref, dgq_ref, dgkt_ref,
             qt2, kt2, vt2, dot2, dkt2, dvt2, qa, qat, ka, kat, vb, vtb, dob, dotb, dqa):
        qt2[...] = q_ref[...].T
        kt2[...] = k_ref[...].T
        vt2[...] = v_ref[...].T
        dot2[...] = do_ref[...].T
        dkt2[...] = jnp.zeros_like(dkt2)
        dvt2[...] = jnp.zeros_like(dvt2)
        from_here, diff = _sb_masks(kb)
        for hh in range(per_pair):
            lanes = slice(hh * dh, (hh + 1) * dh)
            qa[...] = _qk_norm(q_ref[:, lanes], gq_ref[...], scale * LOG2E).astype(BF16)
            qat[...] = _qk_norm_t(qt2[lanes, :], gqt_ref[...], scale * LOG2E).astype(BF16)
            ka[...] = _qk_norm(k_ref[:, lanes], gk_ref[...], 1.0).astype(BF16)
            kat[...] = _qk_norm_t(kt2[lanes, :], gkt_ref[...], 1.0).astype(BF16)
            vb[...] = v_ref[:, lanes].astype(BF16)
            vtb[...] = vt2[lanes, :].astype(BF16)
            dob[...] = do_ref[:, lanes].astype(BF16)
            dotb[...] = dot2[lanes, :].astype(BF16)

            def qblock(i, _):
                rows = pl.ds(pl.multiple_of(i * SB_QB, SB_QB), SB_QB)
                qi, qit = qa[rows, :], qat[:, rows]
                doi, doit = dob[rows, :], dotb[:, rows]
                total = jnp.sum(doi.astype(F32) * o_ref[rows, lanes], axis=1, keepdims=True)
                sbd = i // per

                def step(sb, cl, cd, dqi, valid):
                    cols = pl.ds(pl.multiple_of(sb * kb, kb), kb)
                    z, lk, w, tot = _sb_step(qi, kat[:, cols], cl, from_here, valid)
                    w16 = w.astype(BF16)
                    dl = jnp.dot(doi, vtb[:, cols], preferred_element_type=F32) * w16.astype(F32)
                    incl, dtot = _suffix_sums(dl, from_here)
                    sig = jnp.exp2(z + lk)
                    dz = dl - sig * (dl + (total - cd - incl))
                    if valid is not None:
                        dz = jnp.where(valid, dz, 0.0)
                    dz16 = dz.astype(BF16)
                    dqi = dqi + jnp.dot(dz16, ka[cols, :], preferred_element_type=F32)
                    dkt2[lanes, cols] += jnp.dot(qit, dz16, preferred_element_type=F32)
                    dvt2[lanes, cols] += jnp.dot(doit, w16, preferred_element_type=F32)
                    return cl + tot, cd + dtot, dqi

                zero = jnp.zeros((SB_QB, 1), F32)
                first = step(sbd, zero, zero, jnp.zeros((SB_QB, dh), F32), diff < (i - sbd * per) * SB_QB)
                _, _, dqi = lax.fori_loop(0, sbd, lambda jj, c: step(sbd - 1 - jj, c[0], c[1], c[2], None), first)
                dqa[rows, :] = dqi
                return 0

            lax.fori_loop(0, nb, qblock, 0)
            _, vq = jax.vjp(lambda t, g: _qk_norm(t, g, scale), q_ref[:, lanes], gq_ref[...])
            dq, dgq = vq(dqa[...])
            dq_ref[:, lanes] = dq.astype(BF16)
            dgq_ref[hh] = dgq
            _, vk = jax.vjp(lambda t, g: _qk_norm_t(t, g, 1.0), kt2[lanes, :], gkt_ref[...])
            dkt, dgkt = vk(dkt2[lanes, :] * (1.0 / LOG2E))
            dkt2[lanes, :] = dkt
            dgkt_ref[hh] = dgkt
        dk_ref[...] = dkt2[...].T.astype(BF16)
        dv_ref[...] = dvt2[...].T.astype(BF16)

    blk = lambda off: pl.BlockSpec((S, LANES), lambda g: (0, off + g))
    once = lambda off: pl.BlockSpec((S, LANES), lambda g: (0, off + g), pipeline_mode=pl.Buffered(1))
    gr = pl.BlockSpec((1, dh), lambda g: (0, 0))
    gc = pl.BlockSpec((dh, 1), lambda g: (0, 0))
    sd = jax.ShapeDtypeStruct((S, SB_W), BF16)
    return _call_carrying(
        body, name, pairs, carry, [p, p, p, o, do, gq, gqt, gk, gkt],
        in_specs=[once(0), once(pairs), once(2 * pairs), once(0), once(0), gr, gc, gr, gc],
        out_specs=[blk(0), blk(0), blk(0), pl.BlockSpec((per_pair, 1, dh), lambda g: (g, 0, 0)),
                   pl.BlockSpec((per_pair, dh, 1), lambda g: (g, 0, 0))],
        out_shape=[sd, sd, sd, jax.ShapeDtypeStruct((SB_HEADS, 1, dh), F32),
                   jax.ShapeDtypeStruct((SB_HEADS, dh, 1), F32)],
        scratch_shapes=[pltpu.VMEM((LANES, S), F32)] * 6 + [pltpu.VMEM((S, dh), BF16), pltpu.VMEM((dh, S), BF16)] * 4
        + [pltpu.VMEM((S, dh), F32)],
        vmem=60 * 1024 * 1024)


def _shift_down(x, s, rows):
    if s == 0:
        return x
    return jnp.where(rows >= s, pltpu.roll(x, s, 0), 0.0)


def _shift_up(x, s, rows, n):
    if s == 0:
        return x
    return jnp.where(rows < n - s, pltpu.roll(x, n - s, 0), 0.0)


def _conv(x, w_ref, rows):
    y = x * w_ref[CONV_K - 1:CONV_K, :]
    for kk in range(CONV_K - 1):
        y = y + _shift_down(x, CONV_K - 1 - kk, rows) * w_ref[kk:kk + 1, :]
    return y


def _act_norm(y, normed):
    s = _silu(y)
    n = s * lax.rsqrt(jnp.sum(s * s, axis=-1, keepdims=True) + EPS)
    return jnp.where(normed, n, s)


def _dn_prep_fwd(p, conv_w, col0, name):
    S = p.shape[0]
    nblk = 3 * DN_HEADS
    b0 = col0 // DN_DH

    def body(x_ref, w_ref, o_ref):
        rows = lax.broadcasted_iota(jnp.int32, (S, DN_DH), 0)
        y = _conv(x_ref[...], w_ref, rows)
        o_ref[...] = _act_norm(y, pl.program_id(0) < 2 * DN_HEADS)

    return pl.pallas_call(
        body, name=name, grid=(nblk,),
        in_specs=[pl.BlockSpec((S, DN_DH), lambda j: (0, b0 + j)), pl.BlockSpec((CONV_K, DN_DH), lambda j: (0, j))],
        out_specs=pl.BlockSpec((S, DN_DH), lambda j: (0, j)),
        out_shape=jax.ShapeDtypeStruct((S, 3 * DN_W), F32),
        compiler_params=_cp("parallel"),
    )(p, conv_w)


def _dn_prep_bwd(p, conv_w, col0, dout, name):
    S = p.shape[0]
    nblk = 3 * DN_HEADS
    b0 = col0 // DN_DH

    def body(x_ref, w_ref, do_ref, dx_ref, dw_ref):
        rows = lax.broadcasted_iota(jnp.int32, (S, DN_DH), 0)
        x = x_ref[...]
        y = _conv(x, w_ref, rows)
        normed = pl.program_id(0) < 2 * DN_HEADS
        _, vjp = jax.vjp(lambda t: _act_norm(t, normed), y)
        (dy,) = vjp(do_ref[...])
        dx = dy * w_ref[CONV_K - 1:CONV_K, :]
        dw_ref[CONV_K - 1:CONV_K, :] = jnp.sum(dy * x, axis=0, keepdims=True)
        for kk in range(CONV_K - 1):
            s = CONV_K - 1 - kk
            dx = dx + _shift_up(dy, s, rows, S) * w_ref[kk:kk + 1, :]
            dw_ref[kk:kk + 1, :] = jnp.sum(dy * _shift_down(x, s, rows), axis=0, keepdims=True)
        dx_ref[...] = dx.astype(BF16)

    return pl.pallas_call(
        body, name=name, grid=(nblk,),
        in_specs=[pl.BlockSpec((S, DN_DH), lambda j: (0, b0 + j)), pl.BlockSpec((CONV_K, DN_DH), lambda j: (0, j)),
                  pl.BlockSpec((S, DN_DH), lambda j: (0, j))],
        out_specs=[pl.BlockSpec((S, DN_DH), lambda j: (0, j)), pl.BlockSpec((CONV_K, DN_DH), lambda j: (0, j))],
        out_shape=[jax.ShapeDtypeStruct((S, 3 * DN_W), BF16), jax.ShapeDtypeStruct((CONV_K, 3 * DN_W), F32)],
        compiler_params=_cp("parallel"),
    )(p, conv_w, dout)


def _gate_fn(x, pv):
    lane = lax.broadcasted_iota(jnp.int32, x.shape, 1)
    decay = -jnp.exp(pv[0:1, :]) * _softplus(x + pv[1:2, :])
    return jnp.where(lane < DN_HEADS, _sigmoid(x), decay)


def _dn_gate_fwd(p, pv, blk, name):
    S = p.shape[0]

    def body(x_ref, pv_ref, o_ref):
        o_ref[...] = _gate_fn(x_ref[...], pv_ref[...])

    return pl.pallas_call(
        body, name=name, grid=(1,),
        in_specs=[pl.BlockSpec((S, LANES), lambda i: (0, blk)), pl.BlockSpec((2, LANES), lambda i: (0, 0))],
        out_specs=pl.BlockSpec((S, LANES), lambda i: (0, 0)),
        out_shape=jax.ShapeDtypeStruct((S, LANES), F32),
        compiler_params=_cp("arbitrary"),
    )(p, pv)


def _dn_gate_bwd(p, pv, blk, dout, name):
    S = p.shape[0]

    def body(x_ref, pv_ref, do_ref, dx_ref, dpv_ref):
        _, vjp = jax.vjp(_gate_fn, x_ref[...], pv_ref[...])
        dx, dpv = vjp(do_ref[...])
        dx_ref[...] = dx.astype(BF16)
        dpv_ref[...] = dpv

    return pl.pallas_call(
        body, name=name, grid=(1,),
        in_specs=[pl.BlockSpec((S, LANES), lambda i: (0, blk)), pl.BlockSpec((2, LANES), lambda i: (0, 0)),
                  pl.BlockSpec((S, LANES), lambda i: (0, 0))],
        out_specs=[pl.BlockSpec((S, LANES), lambda i: (0, 0)), pl.BlockSpec((2, LANES), lambda i: (0, 0))],
        out_shape=[jax.ShapeDtypeStruct((S, LANES), BF16), jax.ShapeDtypeStruct((2, LANES), F32)],
        compiler_params=_cp("arbitrary"),
    )(p, pv, dout)


def _t(x):
    return jnp.swapaxes(x, -1, -2)


def _matmuls(prec, differentiable):
    cast = (lambda t: t.astype(BF16)) if prec is None else (lambda t: t)

    def mm(a, b):
        return lax.dot_general(cast(a), cast(b), (((2,), (1,)), ((0,), (0,))), precision=prec,
                               preferred_element_type=F32)

    def mm_nt(a, b):
        return lax.dot_general(cast(a), cast(b), (((2,), (2,)), ((0,), (0,))), precision=prec,
                               preferred_element_type=F32)

    if not differentiable:
        return mm, mm_nt
    dmm, dmm_nt = jax.custom_vjp(mm), jax.custom_vjp(mm_nt)
    dmm.defvjp(lambda a, b: (mm(a, b), (a, b)), lambda res, g: (mm_nt(g, res[1]), mm(_t(res[0]), g)))
    dmm_nt.defvjp(lambda a, b: (mm_nt(a, b), (a, b)), lambda res, g: (mm(g, res[1]), mm(_t(g), res[0])))
    return dmm, dmm_nt


def _exact_sums(differentiable):
    def three(x, axis):
        a = x.astype(BF16)
        r = x - a.astype(F32)
        b = r.astype(BF16)
        return jnp.concatenate([a, b, (r - b.astype(F32)).astype(BF16)], axis=axis)

    def left(tri, x):
        return lax.dot_general(jnp.concatenate([tri] * 3, axis=2).astype(BF16), three(x, 1),
                               (((2,), (1,)), ((0,), (0,))), preferred_element_type=F32)

    def right(x, tri):
        return lax.dot_general(three(x, 2), jnp.concatenate([tri] * 3, axis=1).astype(BF16),
                               (((2,), (1,)), ((0,), (0,))), preferred_element_type=F32)

    if not differentiable:
        return left, right
    dleft, dright = jax.custom_vjp(left), jax.custom_vjp(right)
    dleft.defvjp(lambda tri, x: (left(tri, x), tri), lambda tri, g: (jnp.zeros_like(tri), left(_t(tri), g)))
    dright.defvjp(lambda x, tri: (right(x, tri), tri), lambda tri, g: (right(g, _t(tri)), jnp.zeros_like(tri)))
    return dleft, dright


def _known_inverse(mm):
    f = jax.custom_vjp(lambda n, inv: inv)
    f.defvjp(lambda n, inv: (inv, inv),
             lambda inv, g: (mm(mm(_t(inv), g), _t(inv)), jnp.zeros_like(inv)))
    return f


def _delta_chunk(state, q, k, v, beta, a_col, a_row, differentiable=False, inv_known=None):
    mm, _ = _matmuls(lax.Precision.HIGH, differentiable)
    sum_left, sum_right = _exact_sums(differentiable)
    ein, ein_nt = _matmuls(None, differentiable)
    H, C, _ = q.shape
    r = lax.broadcasted_iota(jnp.int32, (H, C, C), 1)
    c = lax.broadcasted_iota(jnp.int32, (H, C, C), 2)
    tril, strict = r >= c, r > c
    eye = (r == c).astype(F32)
    g_c = sum_left(tril.astype(F32), jnp.broadcast_to(a_col, (H, C, C)))
    g_r = sum_right(jnp.broadcast_to(a_row, (H, C, C)), (r <= c).astype(F32))
    decay = jnp.where(tril, jnp.exp(jnp.where(tril, g_c - g_r, 0.0)), 0.0)
    eg = jnp.exp(g_c)
    g_last = jnp.sum(jnp.where(r == C - 1, g_c, 0.0), axis=1, keepdims=True)
    qs = q * (float(q.shape[2]) ** -0.5)
    kb = k * beta
    neg_m = jnp.where(strict, -(ein_nt(kb, k) * decay), 0.0)
    if inv_known is None:
        inv = eye + neg_m
        pw = neg_m
        for _ in range(int(math.log2(C)) - 1):
            pw = mm(pw, pw)
            inv = inv + mm(inv, pw)
    else:
        inv = _known_inverse(mm)(neg_m, inv_known)
    u = mm(inv, v * beta)
    w = mm(inv, kb * eg)
    intra = jnp.where(tril, ein_nt(qs, k) * decay, 0.0)
    v_new = u - ein(w, state)
    o = ein(qs * eg, state) + ein(intra, v_new)
    nxt = state * jnp.exp(g_last) + ein(_t(k * jnp.exp(g_last - g_c)), v_new)
    return o, nxt, inv


def _heads(t):
    return jnp.stack([t[:, h * DN_DH:(h + 1) * DN_DH] for h in range(DN_HEADS)])


def _delta_step(state, q, k, v, bg, a_row, differentiable=False, inv_known=None):
    lane = lax.broadcasted_iota(jnp.int32, bg.shape, 1)
    pick = lambda j: jnp.stack([jnp.sum(jnp.where(lane == j + h, bg, 0.0), axis=1, keepdims=True)
                                for h in range(DN_HEADS)])
    o, nxt, inv = _delta_chunk(state, _heads(q), _heads(k), _heads(v), pick(0), pick(DN_HEADS), a_row, differentiable,
                               inv_known)
    return jnp.concatenate([o[h] for h in range(DN_HEADS)], axis=1), nxt, inv


def _delta_fwd(qkv, bg, a_row, name):
    S = qkv.shape[0]
    nc = S // BLK

    def body(q_ref, k_ref, v_ref, bg_ref, ar_ref, o_ref, st_ref, inv_ref, state):
        ci = pl.program_id(0)

        @pl.when(ci == 0)
        def _():
            state[...] = jnp.zeros_like(state)

        st = state[...]
        st_ref[:, 0] = st
        o, nxt, inv = _delta_step(st, q_ref[...], k_ref[...], v_ref[...], bg_ref[...], ar_ref[:, pl.ds(ci, 1), :])
        o_ref[...] = o
        inv_ref[:, 0] = inv
        state[...] = nxt

    part = lambda j: pl.BlockSpec((BLK, DN_W), lambda c: (c, j))
    per_chunk = pl.BlockSpec((DN_HEADS, 1, DN_DH, DN_DH), lambda c: (0, c, 0, 0))
    mats = jax.ShapeDtypeStruct((DN_HEADS, nc, DN_DH, DN_DH), F32)
    return pl.pallas_call(
        body, name=name, grid=(nc,),
        in_specs=[part(0), part(1), part(2), pl.BlockSpec((BLK, LANES), lambda c: (c, 0)),
                  pl.BlockSpec((DN_HEADS, nc, BLK), lambda c: (0, 0, 0))],
        out_specs=[part(0), per_chunk, per_chunk], out_shape=[jax.ShapeDtypeStruct((S, DN_W), F32), mats, mats],
        scratch_shapes=[pltpu.VMEM((DN_HEADS, DN_DH, DN_DH), F32)],
        compiler_params=_cp("arbitrary"),
    )(qkv, qkv, qkv, bg, a_row)


def _delta_bwd(qkv, bg, a_row, states, invs, do, name):
    S = qkv.shape[0]
    nc = S // BLK

    def body(q_ref, k_ref, v_ref, bg_ref, ar_ref, st_ref, inv_ref, do_ref, dqkv_ref, dbg_ref, dar_ref, dstate):
        t = pl.program_id(0)
        ci = nc - 1 - t

        @pl.when(t == 0)
        def _():
            dstate[...] = jnp.zeros_like(dstate)

        step = lambda *a: _delta_step(*a, differentiable=True, inv_known=inv_ref[:, 0])[:2]
        _, vjp = jax.vjp(step, st_ref[:, 0], q_ref[...], k_ref[...], v_ref[...], bg_ref[...],
                         ar_ref[:, pl.ds(ci, 1), :])
        dprev, dq, dk, dv, dbg, dar = vjp((do_ref[...], dstate[...]))
        dqkv_ref[:, 0:DN_W] = dq
        dqkv_ref[:, DN_W:2 * DN_W] = dk
        dqkv_ref[:, 2 * DN_W:3 * DN_W] = dv
        dbg_ref[...] = dbg
        dar_ref[:, pl.ds(ci, 1), :] = dar
        dstate[...] = dprev

    part = lambda j: pl.BlockSpec((BLK, DN_W), lambda t: (nc - 1 - t, j))
    lanes = pl.BlockSpec((BLK, LANES), lambda t: (nc - 1 - t, 0))
    rows = pl.BlockSpec((DN_HEADS, nc, BLK), lambda t: (0, 0, 0))
    per_chunk = pl.BlockSpec((DN_HEADS, 1, DN_DH, DN_DH), lambda t: (0, nc - 1 - t, 0, 0))
    return pl.pallas_call(
        body, name=name, grid=(nc,),
        in_specs=[part(0), part(1), part(2), lanes, rows, per_chunk, per_chunk, part(0)],
        out_specs=[pl.BlockSpec((BLK, 3 * DN_W), lambda t: (nc - 1 - t, 0)), lanes, rows],
        out_shape=[jax.ShapeDtypeStruct((S, 3 * DN_W), F32), jax.ShapeDtypeStruct((S, LANES), F32),
                   jax.ShapeDtypeStruct((DN_HEADS, nc, BLK), F32)],
        scratch_shapes=[pltpu.VMEM((DN_HEADS, DN_DH, DN_DH), F32)],
        compiler_params=_cp("arbitrary"),
    )(qkv, qkv, qkv, bg, a_row, states, invs, do)


def _gate_sb(o, z):
    return o * _silu(z)


def _gate_dn(o, z, g):
    return jnp.concatenate(
        [_rms(o[:, h * DN_DH:(h + 1) * DN_DH]) * g * _silu(z[:, h * DN_DH:(h + 1) * DN_DH]) for h in range(DN_HEADS)],
        axis=1)


def _merge_specs(S, D, tm):
    row = lambda w, blk: pl.BlockSpec((tm, w), lambda i: (i, blk))
    full = lambda a, b: pl.BlockSpec((a, b), lambda i: (0, 0))
    return [row(D, 0), full(1, D), row(SB_W, 0), row(SB_W, 3), row(DN_W, 0), row(DN_W, 7),
            row(D, IN_MAIN // D), row(D, IN_MAIN // D + 1), full(1, DN_DH), full(SB_W, D), full(DN_W, D), full(D, D)]


def _merge_fwd(x, gate, o_sb, o_dn, p, ng, wbs, wbd, wo, name):
    S, D = x.shape
    tm = min(512, S)

    def body(x_ref, gate_ref, osb_ref, zsb_ref, odn_ref, zdn_ref, msb_ref, mdn_ref, ng_ref, wbs_ref, wbd_ref, wo_ref,
             out_ref):
        a = _gate_sb(osb_ref[...], zsb_ref[...])
        b = _gate_dn(odn_ref[...], zdn_ref[...], ng_ref[...])
        y = _sigmoid(msb_ref[...]) * _bdot(a, wbs_ref[...]) + _sigmoid(mdn_ref[...]) * _bdot(b, wbd_ref[...])
        out_ref[...] = x_ref[...] + gate_ref[...] * _bdot(y, wo_ref[...])

    return pl.pallas_call(
        body, name=name, grid=(S // tm,), in_specs=_merge_specs(S, D, tm),
        out_specs=pl.BlockSpec((tm, D), lambda i: (i, 0)), out_shape=jax.ShapeDtypeStruct((S, D), F32),
        compiler_params=_cp("parallel"),
    )(x, gate, o_sb, p, o_dn, p, p, p, ng, wbs, wbd, wo)


def _merge_bwd(dxn, gate, o_sb, o_dn, p, ng, wbs, wbd, wo, name):
    S, D = dxn.shape
    tm = min(256, S)

    def body(dxn_ref, gate_ref, osb_ref, zsb_ref, odn_ref, zdn_ref, msb_ref, mdn_ref, ng_ref, wbs_ref, wbd_ref, wo_ref,
             dosb_ref, dzsb_ref, dodn_ref, dzdn_ref, dmsb_ref, dmdn_ref, dwo_ref, dwbs_ref, dwbd_ref, dgate_ref, dng_ref):
        @pl.when(pl.program_id(0) == 0)
        def _():
            for ref in (dwo_ref, dwbs_ref, dwbd_ref, dgate_ref, dng_ref):
                ref[...] = jnp.zeros_like(ref)

        a, vjp_a = jax.vjp(_gate_sb, osb_ref[...], zsb_ref[...])
        b, vjp_b = jax.vjp(_gate_dn, odn_ref[...], zdn_ref[...], ng_ref[...])
        a16, b16 = a.astype(BF16), b.astype(BF16)
        ps = jnp.dot(a16, wbs_ref[...], preferred_element_type=F32)
        pd = jnp.dot(b16, wbd_ref[...], preferred_element_type=F32)
        ss, sd = _sigmoid(msb_ref[...]), _sigmoid(mdn_ref[...])
        y16 = (ss * ps + sd * pd).astype(BF16)
        out = jnp.dot(y16, wo_ref[...], preferred_element_type=F32)
        dxn_v = dxn_ref[...]
        dgate_ref[...] += jnp.sum(dxn_v * out, axis=0, keepdims=True)
        dout16 = (dxn_v * gate_ref[...]).astype(BF16)
        dwo_ref[...] += _bdot_tn(y16, dout16)
        dy = _bdot_nt(dout16, wo_ref[...])
        dmsb_ref[...] = (dy * ps * ss * (1.0 - ss)).astype(BF16)
        dmdn_ref[...] = (dy * pd * sd * (1.0 - sd)).astype(BF16)
        dps16, dpd16 = (dy * ss).astype(BF16), (dy * sd).astype(BF16)
        dwbs_ref[...] += _bdot_tn(a16, dps16)
        dwbd_ref[...] += _bdot_tn(b16, dpd16)
        dosb, dzsb = vjp_a(_bdot_nt(dps16, wbs_ref[...]))
        dodn, dzdn, dng = vjp_b(_bdot_nt(dpd16, wbd_ref[...]))
        dosb_ref[...] = dosb
        dzsb_ref[...] = dzsb.astype(BF16)
        dodn_ref[...] = dodn
        dzdn_ref[...] = dzdn.astype(BF16)
        dng_ref[...] += dng

    row = lambda w: pl.BlockSpec((tm, w), lambda i: (i, 0))
    full = lambda a, b: pl.BlockSpec((a, b), lambda i: (0, 0))
    sds = jax.ShapeDtypeStruct
    return pl.pallas_call(
        body, name=name, grid=(S // tm,), in_specs=_merge_specs(S, D, tm),
        out_specs=[row(SB_W), row(SB_W), row(DN_W), row(DN_W), row(D), row(D),
                   full(D, D), full(SB_W, D), full(DN_W, D), full(1, D), full(1, DN_DH)],
        out_shape=[sds((S, SB_W), F32), sds((S, SB_W), BF16), sds((S, DN_W), F32), sds((S, DN_W), BF16),
                   sds((S, D), BF16), sds((S, D), BF16), sds((D, D), F32), sds((SB_W, D), F32), sds((DN_W, D), F32),
                   sds((1, D), F32), sds((1, DN_DH), F32)],
        compiler_params=_cp("arbitrary"),
    )(dxn, gate, o_sb, p, o_dn, p, p, p, ng, wbs, wbd, wo)


def _loss_fwd_bwd(y, target, name):
    S, D = y.shape
    tm = min(512, S)

    def body(y_ref, t_ref, l_ref, dy_ref):
        @pl.when(pl.program_id(0) == 0)
        def _():
            l_ref[...] = jnp.zeros_like(l_ref)

        e = y_ref[...] - t_ref[...]
        l_ref[...] += jnp.sum(e * e, axis=0, keepdims=True) * (0.5 / D)
        dy_ref[...] = e * (1.0 / D)

    row = pl.BlockSpec((tm, D), lambda i: (i, 0))
    return pl.pallas_call(
        body, name=name, grid=(S // tm,), in_specs=[row, row],
        out_specs=[pl.BlockSpec((1, D), lambda i: (0, 0)), row],
        out_shape=[jax.ShapeDtypeStruct((1, D), F32), jax.ShapeDtypeStruct((S, D), F32)],
        compiler_params=_cp("arbitrary"),
    )(y, target)


def _mod_fwd(c_all, ada_w, name):
    L, D, n = ada_w.shape

    def body(c_ref, w_ref, o_ref):
        o_ref[0] = _dot(_silu(c_ref[...]), w_ref[0])

    return pl.pallas_call(
        body, name=name, grid=(L,),
        in_specs=[pl.BlockSpec(c_all.shape, lambda l: (0, 0)), pl.BlockSpec((1, D, n), lambda l: (l, 0, 0))],
        out_specs=pl.BlockSpec((1, N_DEV, n), lambda l: (l, 0, 0)),
        out_shape=jax.ShapeDtypeStruct((L, N_DEV, n), F32),
        compiler_params=_cp("parallel"),
    )(c_all, ada_w)


def _mod_bwd_w(c_all_t, dmod, name):
    L, _, n = dmod.shape
    D = c_all_t.shape[0]

    def body(c_ref, d_ref, o_ref):
        o_ref[0] = _dot(_silu(c_ref[...]), d_ref[0])

    return pl.pallas_call(
        body, name=name, grid=(L,),
        in_specs=[pl.BlockSpec(c_all_t.shape, lambda l: (0, 0)), pl.BlockSpec((1, N_DEV, n), lambda l: (l, 0, 0))],
        out_specs=pl.BlockSpec((1, D, n), lambda l: (l, 0, 0)),
        out_shape=jax.ShapeDtypeStruct((L, D, n), F32),
        compiler_params=_cp("parallel"),
    )(c_all_t, dmod)


def _me():
    return lax.axis_index("x"), lax.axis_index("y"), lax.axis_index("c")


def _peer(k):
    x, y, c = _me()
    return (1 - x if k & 4 else x, 1 - y if k & 2 else y, 1 - c if k & 1 else c)


def _lin(dev):
    return 4 * dev[0] + 2 * dev[1] + dev[2]


class _Exchange:
    def __init__(self, arrays, scatter):
        self.n = len(arrays)
        self.scatter = scatter
        self.out_shape = [jax.ShapeDtypeStruct((N_DEV,) + tuple(a.shape[1:] if scatter else a.shape), a.dtype)
                          for a in arrays]
        self.specs = [pl.BlockSpec(memory_space=pl.ANY)] * self.n
        self.scratch = [pltpu.SemaphoreType.DMA((self.n, N_DEV - 1)), pltpu.SemaphoreType.DMA((self.n, N_DEV - 1)),
                        pltpu.SemaphoreType.DMA((self.n,))]

    def _copies(self, ins, outs, sems):
        send_sems, recv_sems, local_sems = sems
        me = _lin(_me())
        local, direct, passed, landed = [], [], [], []
        for t in range(self.n):
            src_of = (lambda d, t=t: ins[t].at[d]) if self.scatter else (lambda d, t=t: ins[t])
            local.append(pltpu.make_async_copy(src_of(me), outs[t].at[me], local_sems.at[t]))
            for k in range(1, N_DEV):
                peer = _peer(k)
                pair = dict(send_sem=send_sems.at[t, k - 1], recv_sem=recv_sems.at[t, k - 1],
                            device_id_type=pl.DeviceIdType.MESH)
                slot = outs[t].at[_lin(peer)]
                landed.append(pltpu.make_async_remote_copy(src_ref=slot, dst_ref=slot, device_id=peer, **pair))
                if self.scatter or k in (1, 2, 4, 6):
                    direct.append(pltpu.make_async_remote_copy(src_ref=src_of(_lin(peer)), dst_ref=outs[t].at[me],
                                                               device_id=peer, **pair))
                else:
                    came = outs[t].at[_lin(_peer(k - 1))]
                    passed.append((landed[-2], pltpu.make_async_remote_copy(src_ref=came, dst_ref=came,
                                                                            device_id=_peer(1), **pair)))
        return local, direct, passed, landed

    def start(self, ins, outs, sems):
        local, direct, _, _ = self._copies(ins, outs, sems)
        for cp in local + direct:
            cp.start()

    def finish(self, ins, outs, sems):
        local, direct, passed, landed = self._copies(ins, outs, sems)
        arrived = set()
        for came, onward in passed:
            came.wait_recv()
            arrived.add(id(came))
            onward.start()
        for cp in landed:
            if id(cp) not in arrived:
                cp.wait_recv()
        for cp in direct + [onward for _, onward in passed]:
            cp.wait_send()
        for cp in local:
            cp.wait()


def _exchange(arrays, scatter, name):
    ex = _Exchange(arrays, scatter)

    def body(*refs):
        ins, outs, sems = refs[:ex.n], refs[ex.n:2 * ex.n], refs[2 * ex.n:]
        ex.start(ins, outs, sems)
        ex.finish(ins, outs, sems)

    return pl.pallas_call(body, name=name, in_specs=ex.specs, out_specs=ex.specs, out_shape=ex.out_shape,
                          scratch_shapes=ex.scratch)(*arrays)


def _sum_slots(a, name):
    _, R, C = a.shape
    tr = SUM_ROWS if R % SUM_ROWS == 0 else R

    def body(a_ref, o_ref):
        acc = a_ref[0].astype(F32)
        for s in range(1, N_DEV):
            acc = acc + a_ref[s].astype(F32)
        o_ref[...] = acc

    return pl.pallas_call(
        body, name=name, grid=(R // tr,),
        in_specs=[pl.BlockSpec((N_DEV, tr, C), lambda i: (0, i, 0))], out_specs=pl.BlockSpec((tr, C), lambda i: (i, 0)),
        out_shape=jax.ShapeDtypeStruct((R, C), F32), compiler_params=_cp("parallel"),
    )(a)


def _adamw(w, g, m, v, name):
    shape = w.shape
    C = shape[-1]
    R = w.size // C
    tr = R
    for cand in (256, 128, 64):
        if R > cand and R % cand == 0:
            tr = cand
            break
    c1 = 1.0 / (1.0 - ADAM_B1 ** ADAM_STEP)
    c2 = 1.0 / (1.0 - ADAM_B2 ** ADAM_STEP)

    def body(w_ref, g_ref, m_ref, v_ref, d_ref, nm_ref, nv_ref):
        gv = g_ref[...]
        nm = ADAM_B1 * m_ref[...] + (1.0 - ADAM_B1) * gv
        nv = ADAM_B2 * v_ref[...] + (1.0 - ADAM_B2) * (gv * gv)
        d_ref[...] = -ADAM_LR * ((nm * c1) / (jnp.sqrt(nv * c2) + ADAM_EPS) + ADAM_WD * w_ref[...])
        nm_ref[...] = nm
        nv_ref[...] = nv

    spec = pl.BlockSpec((tr, C), lambda i: (i, 0))
    sd = jax.ShapeDtypeStruct((R, C), F32)
    outs = pl.pallas_call(
        body, name=name, grid=(R // tr,), in_specs=[spec] * 4, out_specs=[spec] * 3, out_shape=[sd] * 3,
        compiler_params=_cp("parallel"),
    )(*(t.reshape(R, C) for t in (w, g, m, v)))
    return tuple(t.reshape(shape) for t in outs)


def _col_segments(D):
    return [(0, IN_MAIN, 0), (IN_COLS, IN_COLS + 2 * D, IN_MAIN), (IN_MAIN, IN_COLS, IN_MAIN + 2 * D)]


def _pad_cols_of_blocks(wi):
    _, D, n = wi.shape
    pieces = []
    for lo, hi, _ in _col_segments(D):
        for d in range(N_DEV):
            a, b = max(lo, d * n), min(hi, (d + 1) * n)
            if a < b:
                pieces.append(wi[d][:, a - d * n:b - d * n])
    return jnp.concatenate(pieces + [jnp.zeros((D, LANES - 2 * DN_HEADS), wi.dtype)], axis=1)


def _blocks_of_padded(dw, n):
    D = dw.shape[0]
    out = []
    for d in range(N_DEV):
        pieces = []
        for lo, hi, at in sorted(_col_segments(D)):
            a, b = max(lo, d * n), min(hi, (d + 1) * n)
            if a < b:
                pieces.append(dw[:, at + a - lo:at + b - lo])
        out.append(pieces[0] if len(pieces) == 1 else jnp.concatenate(pieces, axis=1))
    return jnp.stack(out)


def _gate_params(a_log, dt_bias):
    z = jnp.zeros((LANES,), F32)
    return jnp.stack([z.at[DN_HEADS:2 * DN_HEADS].set(a_log), z.at[DN_HEADS:2 * DN_HEADS].set(dt_bias)])


def _layer_fwd(l, x, mod, wts, carry=None, rest=None):
    S, D = x.shape
    tag = f"l{l}_"
    (p, h), got = _inproj_fwd(x, mod, wts["norm_g"], wts["w_in"], tag + "inproj_fwd",
                              None if rest is None else (rest[0], False))
    if rest is not None:
        wts = {**wts, **rest[1](got)}
    (o_sb,), carried = _sb_fwd(p, wts["sb_q_g"], wts["sb_k_g"].T, tag + "sb_fwd", carry)
    qkv = _dn_prep_fwd(p, wts["conv_w"], 4 * SB_W, tag + "dn_prep_fwd")
    pv = _gate_params(wts["dn_a_log"], wts["dn_dt_bias"])
    ba_blk = (IN_MAIN + 2 * D) // LANES
    bg = _dn_gate_fwd(p, pv, ba_blk, tag + "dn_gate_fwd")
    a_row = bg[:, DN_HEADS:2 * DN_HEADS].T.reshape(DN_HEADS, S // BLK, BLK)
    o_dn, states, invs = _delta_fwd(qkv, bg, a_row, tag + "delta_fwd")
    gate = mod[:, 2 * D:]
    out = _merge_fwd(x, gate, o_sb, o_dn, p, wts["dn_norm_g"], wts["w_branch_sb"], wts["w_branch_dn"], wts["w_out"],
                     tag + "merge_fwd")
    saved = dict(x=x, mod=mod, p=p, h=h, o_sb=o_sb, qkv=qkv, pv=pv, bg=bg, a_row=a_row, o_dn=o_dn, states=states,
                 invs=invs, gate=gate)
    return out, saved, carried, wts


def _layer_bwd(l, dxn, sv, wts, carry_of=None, late_carry_of=None):
    S, D = dxn.shape
    tag = f"l{l}_"
    (dosb, dzsb, dodn, dzdn, dmsb, dmdn, dwo, dwbs, dwbd, dgate, dng) = _merge_bwd(
        dxn, sv["gate"], sv["o_sb"], sv["o_dn"], sv["p"], wts["dn_norm_g"], wts["w_branch_sb"], wts["w_branch_dn"],
        wts["w_out"], tag + "merge_bwd")
    carry = None if carry_of is None else carry_of(dict(w_out=dwo, w_branch_sb=dwbs, w_branch_dn=dwbd))
    (dq, dk, dv, dgq, dgkt), carried = _sb_bwd(sv["p"], sv["o_sb"], dosb, wts["sb_q_g"], wts["sb_q_g"].T,
                                                wts["sb_k_g"], wts["sb_k_g"].T, tag + "sb_bwd", carry)
    dqkv_n, dbg, dar = _delta_bwd(sv["qkv"], sv["bg"], sv["a_row"], sv["states"], sv["invs"], dodn, tag + "delta_bwd")
    dqkv, dconv = _dn_prep_bwd(sv["p"], wts["conv_w"], 4 * SB_W, dqkv_n, tag + "dn_prep_bwd")
    dbg = dbg.at[:, DN_HEADS:2 * DN_HEADS].add(dar.reshape(DN_HEADS, S).T)
    ba_blk = (IN_MAIN + 2 * D) // LANES
    dba, dpv = _dn_gate_bwd(sv["p"], sv["pv"], ba_blk, dbg, tag + "dn_gate_bwd")
    dp = jnp.concatenate([dq, dk, dv, dzsb, dqkv, dzdn, dmsb, dmdn, dba], axis=1)
    dw_in = _matmul_tn(sv["h"].T, dp, tag + "inproj_bwd_dw")
    late = None if late_carry_of is None else late_carry_of(dict(w_in=dw_in, conv_w=dconv))
    (dx, dmod, dg), carried_late = _inproj_bwd_dx(dp, wts["w_in"], sv["x"], sv["mod"], wts["norm_g"], dxn,
                                                  tag + "inproj_bwd_dx", late)
    dmod = dmod.at[:, 2 * D:].set(dgate)
    grads = dict(w_in=dw_in, w_branch_sb=dwbs, w_branch_dn=dwbd, w_out=dwo, conv_w=dconv,
                 mod=dmod[0], norm_g=dg[0], sb_q_g=jnp.sum(dgq, axis=0)[0], sb_k_g=jnp.sum(dgkt, axis=0)[:, 0],
                 dn_a_log=dpv[0, DN_HEADS:2 * DN_HEADS], dn_dt_bias=dpv[1, DN_HEADS:2 * DN_HEADS], dn_norm_g=dng[0])
    return dx, grads, carried, carried_late


def _pad_rows(a, mult):
    extra = (-a.shape[0]) % mult
    return a if extra == 0 else jnp.concatenate([a, jnp.zeros((extra,) + a.shape[1:], a.dtype)], axis=0)


def _pack_rows(parts, width, mult):
    flat = jnp.concatenate([t.reshape(-1) for t in parts])
    extra = (-flat.shape[0]) % width
    if extra:
        flat = jnp.concatenate([flat, jnp.zeros((extra,), flat.dtype)])
    return _pad_rows(flat.reshape(-1, width), mult)


def _take(flat, off, shape):
    n = math.prod(shape)
    return flat[..., off:off + n].reshape(flat.shape[:-1] + tuple(shape)), off + n


SMALL = ("mod", "norm_g", "sb_q_g", "sb_k_g", "dn_a_log", "dn_dt_bias", "dn_norm_g")


def kernel(x, c, ada_w, ada_b, norm_g, w_in, sb_q_g, sb_k_g, conv_w, dn_a_log, dn_dt_bias, dn_norm_g, w_branch_sb, w_branch_dn, w_out, loss_target, m_ada_w, m_ada_b, m_norm_g, m_w_in, m_sb_q_g, m_sb_k_g, m_conv_w, m_dn_a_log, m_dn_dt_bias, m_dn_norm_g, m_w_branch_sb, m_w_branch_dn, m_w_out, v_ada_w, v_ada_b, v_norm_g, v_w_in, v_sb_q_g, v_sb_k_g, v_conv_w, v_dn_a_log, v_dn_dt_bias, v_dn_norm_g, v_w_branch_sb, v_w_branch_dn, v_w_out):
    L, D = norm_g.shape
    S = x.shape[1]
    n_in = w_in.shape[2]
    n_ada = ada_w.shape[2]
    n_br = w_branch_sb.shape[2]
    n_out = w_out.shape[1]
    n_conv = conv_w.shape[2]
    me = _lin(_me())

    def cat(a):
        return jnp.concatenate([a[d] for d in range(N_DEV)], axis=1)

    c_all, conv_all = _exchange([c, conv_w.reshape(L * CONV_K, n_conv)], False, "gather_small")
    c_all = c_all.reshape(N_DEV, D)
    conv_full = cat(conv_all).reshape(L, CONV_K, N_DEV * n_conv)

    mod_part = _mod_fwd(c_all, ada_w, "mod_fwd")

    def shards16(l):
        return [w_in[l].astype(BF16), w_branch_sb[l].astype(BF16), w_branch_dn[l].astype(BF16), w_out[l].astype(BF16)]

    def late(got):
        wbs, wbd, wo = got
        return dict(w_branch_sb=cat(wbs), w_branch_dn=cat(wbd), w_out=wo.reshape(N_DEV * n_out, D))

    def whole(l, got):
        return dict(norm_g=norm_g[l:l + 1], w_in=_pad_cols_of_blocks(got[0]), sb_q_g=sb_q_g[l:l + 1],
                    sb_k_g=sb_k_g[l:l + 1], conv_w=conv_full[l], dn_a_log=dn_a_log[l], dn_dt_bias=dn_dt_bias[l],
                    dn_norm_g=dn_norm_g[l:l + 1], **(late(got[1:]) if len(got) > 1 else {}))

    first = shards16(0)
    wi0, mod_all = _exchange([first[0], mod_part.reshape(L * N_DEV, n_ada)], False, "gather_weights")
    got = [wi0]
    mod_full = cat(mod_all).reshape(L, N_DEV, N_DEV * n_ada) + ada_b[:, None, :]
    mod_mine = lax.dynamic_slice_in_dim(mod_full, me, 1, axis=1)

    act = x[0]
    saved, wts = [], []
    for l in range(L):
        act, sv, got, w_l = _layer_fwd(l, act, mod_mine[l], whole(l, got),
                                       (shards16(l + 1), False) if l + 1 < L else None,
                                       (first[1:], late) if l == 0 else None)
        wts.append(w_l)
        saved.append(sv)
    loss_cols, dact = _loss_fwd_bwd(act, loss_target[0], "loss")
    loss = lax.psum(jnp.sum(loss_cols), ("x", "y", "c"))

    def blocks(name, g):
        if name == "w_out":
            return g.astype(BF16).reshape(N_DEV, n_out, D)
        if name == "w_in":
            return _blocks_of_padded(g, n_in).astype(BF16)
        n = g.shape[1] // N_DEV
        dtype = F32 if name == "conv_w" else BF16
        return jnp.stack([g[:, d * n:(d + 1) * n].astype(dtype) for d in range(N_DEV)])

    early, late = ("w_out", "w_branch_sb", "w_branch_dn"), ("w_in", "conv_w")
    grads, recv, pending = [None] * L, {}, []
    for l in reversed(range(L)):
        keys = [k for k, _ in pending] + [(n, l) for n in early]

        def carry_of(g_early, pending=pending):
            return [a for _, a in pending] + [blocks(n, g_early[n]) for n in early], True

        last = l == 0
        dact, grads[l], got, got_late = _layer_bwd(
            l, dact, saved[l], wts[l], carry_of, (lambda g: ([blocks(n, g[n]) for n in late], True)) if last else None)
        recv.update(zip(keys, got))
        recv.update(zip([(n, l) for n in late], got_late))
        pending = [] if last else [((n, l), blocks(n, grads[l][n])) for n in late]
    grad_x = dact[None]
    small_g = _pack_rows([grads[l][n] for l in range(L) for n in SMALL], LANES, 8)
    (small_all_g,) = _exchange([small_g], False, "gather_small_grads")
    small_sum = _sum_slots(small_all_g, "sum_small_grads").reshape(-1)
    shard_shapes = dict(w_in=w_in.shape, w_branch_sb=w_branch_sb.shape, w_branch_dn=w_branch_dn.shape,
                        conv_w=conv_w.shape, w_out=w_out.shape)
    g_out = {n: jnp.stack([_sum_slots(recv[(n, l)], f"sum_{n}_l{l}").reshape(shape[1:]) for l in range(L)])
             for n, shape in shard_shapes.items()}
    small_shapes = dict(mod=(3 * D,), norm_g=(D,), sb_q_g=(SB_DH,), sb_k_g=(SB_DH,), dn_a_log=(DN_HEADS,),
                        dn_dt_bias=(DN_HEADS,), dn_norm_g=(DN_DH,))
    off = 0
    off_all = 0
    small_each = small_all_g.reshape(N_DEV, -1)
    per_small = {n: [] for n in SMALL}
    dmod_all = []
    for l in range(L):
        for n in SMALL:
            t, off = _take(small_sum, off, small_shapes[n])
            per_small[n].append(t)
            if n == "mod":
                t_all, _ = _take(small_each, off_all, small_shapes[n])
                dmod_all.append(t_all)
            off_all += math.prod(small_shapes[n])
    for n in SMALL:
        g_out[n if n != "mod" else "ada_b"] = jnp.stack(per_small[n])
    dmod_all = jnp.stack(dmod_all)
    dmod_cols = lax.dynamic_slice_in_dim(dmod_all, me * n_ada, n_ada, axis=2)
    g_out["ada_w"] = _mod_bwd_w(c_all.T, dmod_cols, "mod_bwd_w")

    given = dict(ada_w=(ada_w, m_ada_w, v_ada_w), ada_b=(ada_b, m_ada_b, v_ada_b), norm_g=(norm_g, m_norm_g, v_norm_g),
                 w_in=(w_in, m_w_in, v_w_in), sb_q_g=(sb_q_g, m_sb_q_g, v_sb_q_g), sb_k_g=(sb_k_g, m_sb_k_g, v_sb_k_g),
                 conv_w=(conv_w, m_conv_w, v_conv_w), dn_a_log=(dn_a_log, m_dn_a_log, v_dn_a_log),
                 dn_dt_bias=(dn_dt_bias, m_dn_dt_bias, v_dn_dt_bias), dn_norm_g=(dn_norm_g, m_dn_norm_g, v_dn_norm_g),
                 w_branch_sb=(w_branch_sb, m_w_branch_sb, v_w_branch_sb),
                 w_branch_dn=(w_branch_dn, m_w_branch_dn, v_w_branch_dn), w_out=(w_out, m_w_out, v_w_out))
    order = list(given)
    upd = {n: _adamw(given[n][0], g_out[n], given[n][1], given[n][2], "adamw_" + n) for n in order}
    return (loss, grad_x, *[g_out[n] for n in order], *[upd[n][0] for n in order], *[upd[n][1] for n in order],
            *[upd[n][2] for n in order])
```

```python
import functools
import math

import jax
import jax.numpy as jnp
from jax import lax
from jax.experimental import pallas as pl
from jax.experimental.pallas import tpu as pltpu

F32 = jnp.float32
BF16 = jnp.bfloat16
HI = lax.Precision.HIGHEST

N_DEV = 8
EPS = 1e-6
SB_HEADS, SB_DH = 8, 64
DN_HEADS, DN_DH = 4, 128
SB_W = SB_HEADS * SB_DH
DN_W = DN_HEADS * DN_DH
CONV_K = 4
BLK = 128
SB_KEYS = 512
SB_QB = 512
LANES = 128
IN_MAIN = 4 * SB_W + 4 * DN_W
IN_COLS = IN_MAIN + 2 * DN_HEADS
ADAM_LR, ADAM_B1, ADAM_B2, ADAM_EPS, ADAM_WD, ADAM_STEP = 0.001, 0.9, 0.999, 1e-08, 0.01, 10
VMEM_LIMIT = 56 * 1024 * 1024
LOG2E = 1.4426950408889634
SUM_ROWS = 128


def _cp(*sem, vmem=VMEM_LIMIT):
    return pltpu.CompilerParams(dimension_semantics=sem if sem else None, vmem_limit_bytes=vmem)


def _dot(a, b, prec=HI):
    return lax.dot_general(a, b, (((1,), (0,)), ((), ())), precision=prec, preferred_element_type=F32)


def _dot_nt(a, b, prec=HI):
    return lax.dot_general(a, b, (((1,), (1,)), ((), ())), precision=prec, preferred_element_type=F32)


def _bdot(a, b):
    return lax.dot_general(a.astype(BF16), b.astype(BF16), (((1,), (0,)), ((), ())), preferred_element_type=F32)


def _bdot_nt(a, b):
    return lax.dot_general(a.astype(BF16), b.astype(BF16), (((1,), (1,)), ((), ())), preferred_element_type=F32)


def _bdot_tn(a, b):
    return lax.dot_general(a.astype(BF16), b.astype(BF16), (((0,), (0,)), ((), ())), preferred_element_type=F32)


def _split_dot(a, b01_twice):
    hi = a.astype(BF16)
    lo = (a - hi.astype(F32)).astype(BF16)
    return jnp.dot(jnp.concatenate([hi, lo], axis=1), b01_twice, preferred_element_type=F32)


def _sigmoid(x):
    return 1.0 / (1.0 + jnp.exp(-x))


def _silu(x):
    return x * _sigmoid(x)


def _softplus(x):
    return jnp.maximum(x, 0.0) + jnp.log(1.0 + jnp.exp(-jnp.abs(x)))


def _rms(x):
    return x * lax.rsqrt(jnp.mean(x * x, axis=-1, keepdims=True) + EPS)


def _prenorm(x, g, shift, scale):
    return _rms(x) * g * (1.0 + scale) + shift


def _inproj_fwd(x, mod, g, w, name, carry=None):
    S, D = x.shape
    N = w.shape[1]
    tm = min(512, S)
    tn = 896 if N % 896 == 0 else 128

    def body(x_ref, mod_ref, g_ref, w_ref, p_ref, h_ref):
        @pl.when(pl.program_id(1) == 0)
        def _():
            h = _prenorm(x_ref[...], g_ref[...], mod_ref[:, 0:D], mod_ref[:, D:2 * D])
            h_ref[...] = h.astype(BF16)

        p_ref[...] = jnp.dot(h_ref[...], w_ref[...], preferred_element_type=F32)

    return _call_carrying(
        body, name, (S // tm, N // tn), carry, [x, mod, g, w],
        in_specs=[pl.BlockSpec((tm, D), lambda i, j: (i, 0)), pl.BlockSpec((1, 3 * D), lambda i, j: (0, 0)),
                  pl.BlockSpec((1, D), lambda i, j: (0, 0)), pl.BlockSpec((D, tn), lambda i, j: (0, j))],
        out_specs=[pl.BlockSpec((tm, tn), lambda i, j: (i, j)), pl.BlockSpec((tm, D), lambda i, j: (i, 0))],
        out_shape=[jax.ShapeDtypeStruct((S, N), F32), jax.ShapeDtypeStruct((S, D), BF16)],
        scratch_shapes=[], vmem=VMEM_LIMIT)


def _inproj_bwd_dx(dp, w, x, mod, g, dxn, name, carry=None):
    S, N = dp.shape
    D = x.shape[1]
    tm = min(512, S)
    tk = 896 if N % 896 == 0 else 128
    nk = N // tk

    def body(dp_ref, w_ref, x_ref, mod_ref, g_ref, dxn_ref, dx_ref, dmod_ref, dg_ref, acc):
        i, k = pl.program_id(0), pl.program_id(1)

        @pl.when(k == 0)
        def _():
            acc[...] = jnp.zeros_like(acc)

        @pl.when((i == 0) & (k == 0))
        def _():
            dmod_ref[...] = jnp.zeros_like(dmod_ref)
            dg_ref[...] = jnp.zeros_like(dg_ref)

        acc[...] += lax.dot_general(dp_ref[...], w_ref[...], (((1,), (1,)), ((), ())), preferred_element_type=F32)

        @pl.when(k == nk - 1)
        def _():
            _, vjp = jax.vjp(_prenorm, x_ref[...], g_ref[...], mod_ref[:, 0:D], mod_ref[:, D:2 * D])
            dx, dg, dshift, dscale = vjp(acc[...])
            dx_ref[...] = dxn_ref[...] + dx
            dg_ref[...] += dg
            dmod_ref[:, 0:D] += dshift
            dmod_ref[:, D:2 * D] += dscale

    return _call_carrying(
        body, name, (S // tm, nk), carry, [dp, w, x, mod, g, dxn],
        in_specs=[pl.BlockSpec((tm, tk), lambda i, k: (i, k)), pl.BlockSpec((D, tk), lambda i, k: (0, k)),
                  pl.BlockSpec((tm, D), lambda i, k: (i, 0)), pl.BlockSpec((1, 3 * D), lambda i, k: (0, 0)),
                  pl.BlockSpec((1, D), lambda i, k: (0, 0)), pl.BlockSpec((tm, D), lambda i, k: (i, 0))],
        out_specs=[pl.BlockSpec((tm, D), lambda i, k: (i, 0)), pl.BlockSpec((1, 3 * D), lambda i, k: (0, 0)),
                   pl.BlockSpec((1, D), lambda i, k: (0, 0))],
        out_shape=[jax.ShapeDtypeStruct((S, D), F32), jax.ShapeDtypeStruct((1, 3 * D), F32),
                   jax.ShapeDtypeStruct((1, D), F32)],
        scratch_shapes=[pltpu.VMEM((tm, D), F32)], vmem=VMEM_LIMIT)


def _matmul_tn(a_t, b, name):
    M, K = a_t.shape
    N = b.shape[1]
    tn = 896 if N % 896 == 0 else (512 if N % 512 == 0 else 128)
    tk = min(512, K)
    nk = K // tk

    def body(a_ref, b_ref, o_ref):
        @pl.when(pl.program_id(1) == 0)
        def _():
            o_ref[...] = jnp.zeros_like(o_ref)

        o_ref[...] += jnp.dot(a_ref[...], b_ref[...], preferred_element_type=F32)

    return pl.pallas_call(
        body, name=name, grid=(N // tn, nk),
        in_specs=[pl.BlockSpec((M, tk), lambda j, k: (0, k)), pl.BlockSpec((tk, tn), lambda j, k: (k, j))],
        out_specs=pl.BlockSpec((M, tn), lambda j, k: (0, j)),
        out_shape=jax.ShapeDtypeStruct((M, N), F32),
        compiler_params=_cp("parallel", "arbitrary"),
    )(a_t, b)


def _qk_norm(t, g, scale):
    return _rms(t) * g * scale


def _qk_norm_t(t, g_col, scale):
    return t * lax.rsqrt(jnp.mean(t * t, axis=0, keepdims=True) + EPS) * g_col * scale


def _suffix_sums(x, tri):
    half = tri.shape[1]
    lo, hi = x[:, :half], x[:, half:]
    hi_sum = jnp.sum(hi, axis=1, keepdims=True)
    y = jnp.concatenate([_split_dot(lo, tri) + hi_sum, _split_dot(hi, tri)], axis=1)
    return y, hi_sum + jnp.sum(lo, axis=1, keepdims=True)


def _sb_step(qi, kat_blk, cl, from_here, valid):
    z = jnp.dot(qi, kat_blk, preferred_element_type=F32)
    nz = -z
    lk = jnp.minimum(nz, 0.0) - jnp.log(1.0 + jnp.exp2(jnp.minimum(z, nz))) * LOG2E
    if valid is not None:
        lk = jnp.where(valid, lk, 0.0)
    later, tot = _suffix_sums(lk, from_here)
    w = jnp.exp2(z + later + cl)
    if valid is not None:
        w = jnp.where(valid, w, 0.0)
    return z, lk, w, tot


def _sb_masks(kb):
    half = kb // 2
    r = lax.broadcasted_iota(jnp.int32, (half, half), 0)
    c = lax.broadcasted_iota(jnp.int32, (half, half), 1)
    rq = lax.broadcasted_iota(jnp.int32, (SB_QB, kb), 0)
    ck = lax.broadcasted_iota(jnp.int32, (SB_QB, kb), 1)
    twice = lambda m: jnp.concatenate([m, m], axis=0).astype(BF16)
    return twice((r >= c).astype(F32)), ck - rq


def _sb_fwd(p, gq, gkt, name, carry=None):
    S, dh = p.shape[0], SB_DH
    kb = min(SB_KEYS, S)
    per = kb // SB_QB
    nb = S // SB_QB
    scale = 1.0 / math.sqrt(dh)
    pairs = SB_W // LANES

    def body(q_ref, k_ref, v_ref, gq_ref, gkt_ref, o_ref, kt2, qa, kat, vb):
        kt2[...] = k_ref[...].T
        from_here, diff = _sb_masks(kb)
        for hh in range(LANES // dh):
            lanes = slice(hh * dh, (hh + 1) * dh)
            qa[...] = _qk_norm(q_ref[:, lanes], gq_ref[...], scale * LOG2E).astype(BF16)
            kat[...] = _qk_norm_t(kt2[lanes, :], gkt_ref[...], 1.0).astype(BF16)
            vb[...] = v_ref[:, lanes].astype(BF16)

            def qblock(i, _):
                rows = pl.ds(pl.multiple_of(i * SB_QB, SB_QB), SB_QB)
                qi = qa[rows, :]
                sbd = i // per

                def step(sb, cl, acc, valid):
                    cols = pl.ds(pl.multiple_of(sb * kb, kb), kb)
                    _, _, w, tot = _sb_step(qi, kat[:, cols], cl, from_here, valid)
                    return cl + tot, acc + jnp.dot(w.astype(BF16), vb[cols, :], preferred_element_type=F32)

                cl, acc = step(sbd, jnp.zeros((SB_QB, 1), F32), jnp.zeros((SB_QB, dh), F32),
                               diff < (i - sbd * per) * SB_QB)
                _, acc = lax.fori_loop(0, sbd, lambda jj, c: step(sbd - 1 - jj, c[0], c[1], None), (cl, acc))
                o_ref[rows, lanes] = acc
                return 0

            lax.fori_loop(0, nb, qblock, 0)

    blk = lambda off: pl.BlockSpec((S, LANES), lambda g: (0, off + g))
    return _call_carrying(
        body, name, pairs, carry, [p, p, p, gq, gkt],
        in_specs=[blk(0), blk(pairs), blk(2 * pairs), pl.BlockSpec((1, dh), lambda g: (0, 0)),
                  pl.BlockSpec((dh, 1), lambda g: (0, 0))],
        out_specs=[blk(0)], out_shape=[jax.ShapeDtypeStruct((S, SB_W), F32)],
        scratch_shapes=[pltpu.VMEM((LANES, S), F32), pltpu.VMEM((S, dh), BF16), pltpu.VMEM((dh, S), BF16),
                        pltpu.VMEM((S, dh), BF16)],
        vmem=VMEM_LIMIT)


def _call_carrying(body, name, grid, carry, operands, in_specs, out_specs, out_shape, scratch_shapes, vmem):
    grid = (grid,) if isinstance(grid, int) else tuple(grid)
    if carry is None:
        res = pl.pallas_call(body, name=name, grid=grid, in_specs=in_specs, out_specs=out_specs,
                             out_shape=out_shape, scratch_shapes=scratch_shapes,
                             compiler_params=_cp(*["arbitrary"] * len(grid), vmem=vmem))(*operands)
        return res, []

    def at(corner):
        hit = pl.program_id(0) == corner(grid[0])
        for axis in range(1, len(grid)):
            hit = jnp.logical_and(hit, pl.program_id(axis) == corner(grid[axis]))
        return hit

    ex = _Exchange(*carry)
    n_in, n_out, n_scr = len(in_specs), len(out_specs), len(scratch_shapes)

    def wrapped(*refs):
        ins, refs = refs[:n_in], refs[n_in:]
        xin, refs = refs[:ex.n], refs[ex.n:]
        outs, refs = refs[:n_out], refs[n_out:]
        xout, refs = refs[:ex.n], refs[ex.n:]
        scr, sems = refs[:n_scr], refs[n_scr:]

        @pl.when(at(lambda n: 0))
        def _():
            ex.start(xin, xout, sems)

        body(*ins, *outs, *scr)

        @pl.when(at(lambda n: n - 1))
        def _():
            ex.finish(xin, xout, sems)

    res = pl.pallas_call(wrapped, name=name, grid=grid, in_specs=in_specs + ex.specs,
                         out_specs=out_specs + ex.specs, out_shape=out_shape + ex.out_shape,
                         scratch_shapes=scratch_shapes + ex.scratch,
                         compiler_params=_cp(*["arbitrary"] * len(grid), vmem=vmem))(*operands, *carry[0])
    return res[:n_out], res[n_out:]


def _sb_bwd(p, o, do, gq, gqt, gk, gkt, name, carry=None):
    S, dh = p.shape[0], SB_DH
    kb = min(SB_KEYS, S)
    per = kb // SB_QB
    nb = S // SB_QB
    scale = 1.0 / math.sqrt(dh)
    pairs = SB_W // LANES
    per_pair = LANES // dh

    def body(q_ref, k_ref, v_ref, o_ref, do_ref, gq_ref, gqt_ref, gk_ref, gkt_ref,
             dq_ref, dk_ref, dv_ref, dgq_ref, dgkt_ref,
             qt2, kt2, vt2, dot2, dkt2, dvt2, qa, qat, ka, kat, vb, vtb, dob, dotb, dqa):
        qt2[...] = q_ref[...].T
        kt2[...] = k_ref[...].T
        vt2[...] = v_ref[...].astype(BF16).T
        dot2[...] = do_ref[...].astype(BF16).T
        dkt2[...] = jnp.zeros_like(dkt2)
        dvt2[...] = jnp.zeros_like(dvt2)
        from_here, diff = _sb_masks(kb)
        for hh in range(per_pair):
            lanes = slice(hh * dh, (hh + 1) * dh)
            qa[...] = _qk_norm(q_ref[:, lanes], gq_ref[...], scale * LOG2E).astype(BF16)
            qat[...] = _qk_norm_t(qt2[lanes, :], gqt_ref[...], scale * LOG2E).astype(BF16)
            ka[...] = _qk_norm(k_ref[:, lanes], gk_ref[...], 1.0).astype(BF16)
            kat[...] = _qk_norm_t(kt2[lanes, :], gkt_ref[...], 1.0).astype(BF16)
            vb[...] = v_ref[:, lanes].astype(BF16)
            vtb[...] = vt2[lanes, :]
            dob[...] = do_ref[:, lanes].astype(BF16)
            dotb[...] = dot2[lanes, :]

            def qblock(i, _):
                rows = pl.ds(pl.multiple_of(i * SB_QB, SB_QB), SB_QB)
                qi, qit = qa[rows, :], qat[:, rows]
                doi, doit = dob[rows, :], dotb[:, rows]
                total = jnp.sum(doi.astype(F32) * o_ref[rows, lanes], axis=1, keepdims=True)
                sbd = i // per

                def step(sb, cl, cd, dqi, valid):
                    cols = pl.ds(pl.multiple_of(sb * kb, kb), kb)
                    z, lk, w, tot = _sb_step(qi, kat[:, cols], cl, from_here, valid)
                    w16 = w.astype(BF16)
                    dl = jnp.dot(doi, vtb[:, cols], preferred_element_type=F32) * w16.astype(F32)
                    incl, dtot = _suffix_sums(dl, from_here)
                    sig = jnp.exp2(z + lk)
                    dz = dl - sig * (dl + (total - cd - incl))
                    if valid is not None:
                        dz = jnp.where(valid, dz, 0.0)
                    dz16 = dz.astype(BF16)
                    dqi = dqi + jnp.dot(dz16, ka[cols, :], preferred_element_type=F32)
                    dkt2[lanes, cols] += jnp.dot(qit, dz16, preferred_element_type=F32)
                    dvt2[lanes, cols] += jnp.dot(doit, w16, preferred_element_type=F32)
                    return cl + tot, cd + dtot, dqi

                zero = jnp.zeros((SB_QB, 1), F32)
                first = step(sbd, zero, zero, jnp.zeros((SB_QB, dh), F32), diff < (i - sbd * per) * SB_QB)
                _, _, dqi = lax.fori_loop(0, sbd, lambda jj, c: step(sbd - 1 - jj, c[0], c[1], c[2], None), first)
                dqa[rows, :] = dqi
                return 0

            lax.fori_loop(0, nb, qblock, 0)
            _, vq = jax.vjp(lambda t, g: _qk_norm(t, g, scale), q_ref[:, lanes], gq_ref[...])
            dq, dgq = vq(dqa[...])
            dq_ref[:, lanes] = dq.astype(BF16)
            dgq_ref[hh] = dgq
            _, vk = jax.vjp(lambda t, g: _qk_norm_t(t, g, 1.0), kt2[lanes, :], gkt_ref[...])
            dkt, dgkt = vk(dkt2[lanes, :] * (1.0 / LOG2E))
            dkt2[lanes, :] = dkt
            dgkt_ref[hh] = dgkt
        dk_ref[...] = dkt2[...].T.astype(BF16)
        dv_ref[...] = dvt2[...].T.astype(BF16)

    blk = lambda off: pl.BlockSpec((S, LANES), lambda g: (0, off + g))
    once = lambda off: pl.BlockSpec((S, LANES), lambda g: (0, off + g), pipeline_mode=pl.Buffered(1))
    gr = pl.BlockSpec((1, dh), lambda g: (0, 0))
    gc = pl.BlockSpec((dh, 1), lambda g: (0, 0))
    sd = jax.ShapeDtypeStruct((S, SB_W), BF16)
    return _call_carrying(
        body, name, pairs, carry, [p, p, p, o, do, gq, gqt, gk, gkt],
        in_specs=[once(0), once(pairs), once(2 * pairs), once(0), once(0), gr, gc, gr, gc],
        out_specs=[blk(0), blk(0), blk(0), pl.BlockSpec((per_pair, 1, dh), lambda g: (g, 0, 0)),
                   pl.BlockSpec((per_pair, dh, 1), lambda g: (g, 0, 0))],
        out_shape=[sd, sd, sd, jax.ShapeDtypeStruct((SB_HEADS, 1, dh), F32),
                   jax.ShapeDtypeStruct((SB_HEADS, dh, 1), F32)],
        scratch_shapes=[pltpu.VMEM((LANES, S), F32)] * 2 + [pltpu.VMEM((LANES, S), BF16)] * 2
        + [pltpu.VMEM((LANES, S), F32)] * 2 + [pltpu.VMEM((S, dh), BF16), pltpu.VMEM((dh, S), BF16)] * 4
        + [pltpu.VMEM((S, dh), F32)],
        vmem=60 * 1024 * 1024)


def _shift_down(x, s, rows):
    if s == 0:
        return x
    return jnp.where(rows >= s, pltpu.roll(x, s, 0), 0.0)


def _shift_up(x, s, rows, n):
    if s == 0:
        return x
    return jnp.where(rows < n - s, pltpu.roll(x, n - s, 0), 0.0)


def _conv(x, w_ref, rows):
    y = x * w_ref[CONV_K - 1:CONV_K, :]
    for kk in range(CONV_K - 1):
        y = y + _shift_down(x, CONV_K - 1 - kk, rows) * w_ref[kk:kk + 1, :]
    return y


def _act_norm(y, normed):
    s = _silu(y)
    n = s * lax.rsqrt(jnp.sum(s * s, axis=-1, keepdims=True) + EPS)
    return jnp.where(normed, n, s)


def _dn_prep_fwd(p, conv_w, col0, name):
    S = p.shape[0]
    nblk = 3 * DN_HEADS
    b0 = col0 // DN_DH

    def body(x_ref, w_ref, o_ref):
        rows = lax.broadcasted_iota(jnp.int32, (S, DN_DH), 0)
        y = _conv(x_ref[...], w_ref, rows)
        o_ref[...] = _act_norm(y, pl.program_id(0) < 2 * DN_HEADS)

    return pl.pallas_call(
        body, name=name, grid=(nblk,),
        in_specs=[pl.BlockSpec((S, DN_DH), lambda j: (0, b0 + j)), pl.BlockSpec((CONV_K, DN_DH), lambda j: (0, j))],
        out_specs=pl.BlockSpec((S, DN_DH), lambda j: (0, j)),
        out_shape=jax.ShapeDtypeStruct((S, 3 * DN_W), F32),
        compiler_params=_cp("parallel"),
    )(p, conv_w)


def _dn_prep_bwd(p, conv_w, col0, dout, name):
    S = p.shape[0]
    nblk = 3 * DN_HEADS
    b0 = col0 // DN_DH

    def body(x_ref, w_ref, do_ref, dx_ref, dw_ref):
        rows = lax.broadcasted_iota(jnp.int32, (S, DN_DH), 0)
        x = x_ref[...]
        y = _conv(x, w_ref, rows)
        normed = pl.program_id(0) < 2 * DN_HEADS
        _, vjp = jax.vjp(lambda t: _act_norm(t, normed), y)
        (dy,) = vjp(do_ref[...])
        dx = dy * w_ref[CONV_K - 1:CONV_K, :]
        dw_ref[CONV_K - 1:CONV_K, :] = jnp.sum(dy * x, axis=0, keepdims=True)
        for kk in range(CONV_K - 1):
            s = CONV_K - 1 - kk
            dx = dx + _shift_up(dy, s, rows, S) * w_ref[kk:kk + 1, :]
            dw_ref[kk:kk + 1, :] = jnp.sum(dy * _shift_down(x, s, rows), axis=0, keepdims=True)
        dx_ref[...] = dx.astype(BF16)

    return pl.pallas_call(
        body, name=name, grid=(nblk,),
        in_specs=[pl.BlockSpec((S, DN_DH), lambda j: (0, b0 + j)), pl.BlockSpec((CONV_K, DN_DH), lambda j: (0, j)),
                  pl.BlockSpec((S, DN_DH), lambda j: (0, j))],
        out_specs=[pl.BlockSpec((S, DN_DH), lambda j: (0, j)), pl.BlockSpec((CONV_K, DN_DH), lambda j: (0, j))],
        out_shape=[jax.ShapeDtypeStruct((S, 3 * DN_W), BF16), jax.ShapeDtypeStruct((CONV_K, 3 * DN_W), F32)],
        compiler_params=_cp("parallel"),
    )(p, conv_w, dout)


def _gate_fn(x, pv):
    lane = lax.broadcasted_iota(jnp.int32, x.shape, 1)
    decay = -jnp.exp(pv[0:1, :]) * _softplus(x + pv[1:2, :])
    return jnp.where(lane < DN_HEADS, _sigmoid(x), decay)


def _dn_gate_fwd(p, pv, blk, name):
    S = p.shape[0]

    def body(x_ref, pv_ref, o_ref):
        o_ref[...] = _gate_fn(x_ref[...], pv_ref[...])

    return pl.pallas_call(
        body, name=name, grid=(1,),
        in_specs=[pl.BlockSpec((S, LANES), lambda i: (0, blk)), pl.BlockSpec((2, LANES), lambda i: (0, 0))],
        out_specs=pl.BlockSpec((S, LANES), lambda i: (0, 0)),
        out_shape=jax.ShapeDtypeStruct((S, LANES), F32),
        compiler_params=_cp("arbitrary"),
    )(p, pv)


def _dn_gate_bwd(p, pv, blk, dout, name):
    S = p.shape[0]

    def body(x_ref, pv_ref, do_ref, dx_ref, dpv_ref):
        _, vjp = jax.vjp(_gate_fn, x_ref[...], pv_ref[...])
        dx, dpv = vjp(do_ref[...])
        dx_ref[...] = dx.astype(BF16)
        dpv_ref[...] = dpv

    return pl.pallas_call(
        body, name=name, grid=(1,),
        in_specs=[pl.BlockSpec((S, LANES), lambda i: (0, blk)), pl.BlockSpec((2, LANES), lambda i: (0, 0)),
                  pl.BlockSpec((S, LANES), lambda i: (0, 0))],
        out_specs=[pl.BlockSpec((S, LANES), lambda i: (0, 0)), pl.BlockSpec((2, LANES), lambda i: (0, 0))],
        out_shape=[jax.ShapeDtypeStruct((S, LANES), BF16), jax.ShapeDtypeStruct((2, LANES), F32)],
        compiler_params=_cp("arbitrary"),
    )(p, pv, dout)


def _t(x):
    return jnp.swapaxes(x, -1, -2)


def _matmuls(prec, differentiable):
    cast = (lambda t: t.astype(BF16)) if prec is None else (lambda t: t)

    def mm(a, b):
        return lax.dot_general(cast(a), cast(b), (((2,), (1,)), ((0,), (0,))), precision=prec,
                               preferred_element_type=F32)

    def mm_nt(a, b):
        return lax.dot_general(cast(a), cast(b), (((2,), (2,)), ((0,), (0,))), precision=prec,
                               preferred_element_type=F32)

    if not differentiable:
        return mm, mm_nt
    dmm, dmm_nt = jax.custom_vjp(mm), jax.custom_vjp(mm_nt)
    dmm.defvjp(lambda a, b: (mm(a, b), (a, b)), lambda res, g: (mm_nt(g, res[1]), mm(_t(res[0]), g)))
    dmm_nt.defvjp(lambda a, b: (mm_nt(a, b), (a, b)), lambda res, g: (mm(g, res[1]), mm(_t(g), res[0])))
    return dmm, dmm_nt


def _exact_sums(differentiable):
    def three(x, axis):
        a = x.astype(BF16)
        r = x - a.astype(F32)
        b = r.astype(BF16)
        return jnp.concatenate([a, b, (r - b.astype(F32)).astype(BF16)], axis=axis)

    def left(tri, x):
        return lax.dot_general(jnp.concatenate([tri] * 3, axis=2).astype(BF16), three(x, 1),
                               (((2,), (1,)), ((0,), (0,))), preferred_element_type=F32)

    def right(x, tri):
        return lax.dot_general(three(x, 2), jnp.concatenate([tri] * 3, axis=1).astype(BF16),
                               (((2,), (1,)), ((0,), (0,))), preferred_element_type=F32)

    if not differentiable:
        return left, right
    dleft, dright = jax.custom_vjp(left), jax.custom_vjp(right)
    dleft.defvjp(lambda tri, x: (left(tri, x), tri), lambda tri, g: (jnp.zeros_like(tri), left(_t(tri), g)))
    dright.defvjp(lambda x, tri: (right(x, tri), tri), lambda tri, g: (right(g, _t(tri)), jnp.zeros_like(tri)))
    return dleft, dright


def _known_inverse(mm):
    f = jax.custom_vjp(lambda n, inv: inv)
    f.defvjp(lambda n, inv: (inv, inv),
             lambda inv, g: (mm(mm(_t(inv), g), _t(inv)), jnp.zeros_like(inv)))
    return f


def _delta_chunk(state, q, k, v, beta, a_col, a_row, differentiable=False, inv_known=None):
    mm, _ = _matmuls(lax.Precision.HIGH, differentiable)
    sum_left, sum_right = _exact_sums(differentiable)
    ein, ein_nt = _matmuls(None, differentiable)
    H, C, _ = q.shape
    r = lax.broadcasted_iota(jnp.int32, (H, C, C), 1)
    c = lax.broadcasted_iota(jnp.int32, (H, C, C), 2)
    tril, strict = r >= c, r > c
    eye = (r == c).astype(F32)
    g_c = sum_left(tril.astype(F32), jnp.broadcast_to(a_col, (H, C, C)))
    g_r = sum_right(jnp.broadcast_to(a_row, (H, C, C)), (r <= c).astype(F32))
    decay = jnp.where(tril, jnp.exp(jnp.where(tril, g_c - g_r, 0.0)), 0.0)
    eg = jnp.exp(g_c)
    g_last = jnp.sum(jnp.where(r == C - 1, g_c, 0.0), axis=1, keepdims=True)
    qs = q * (float(q.shape[2]) ** -0.5)
    kb = k * beta
    neg_m = jnp.where(strict, -(ein_nt(kb, k) * decay), 0.0)
    if inv_known is None:
        inv = eye + neg_m
        pw = neg_m
        for _ in range(int(math.log2(C)) - 1):
            pw = mm(pw, pw)
            inv = inv + mm(inv, pw)
    else:
        inv = _known_inverse(mm)(neg_m, inv_known)
    u = mm(inv, v * beta)
    w = mm(inv, kb * eg)
    intra = jnp.where(tril, ein_nt(qs, k) * decay, 0.0)
    v_new = u - ein(w, state)
    o = ein(qs * eg, state) + ein(intra, v_new)
    nxt = state * jnp.exp(g_last) + ein(_t(k * jnp.exp(g_last - g_c)), v_new)
    return o, nxt, inv


def _heads(t):
    return jnp.stack([t[:, h * DN_DH:(h + 1) * DN_DH] for h in range(DN_HEADS)])


def _delta_step(state, q, k, v, bg, a_row, differentiable=False, inv_known=None):
    lane = lax.broadcasted_iota(jnp.int32, bg.shape, 1)
    pick = lambda j: jnp.stack([jnp.sum(jnp.where(lane == j + h, bg, 0.0), axis=1, keepdims=True)
                                for h in range(DN_HEADS)])
    o, nxt, inv = _delta_chunk(state, _heads(q), _heads(k), _heads(v), pick(0), pick(DN_HEADS), a_row, differentiable,
                               inv_known)
    return jnp.concatenate([o[h] for h in range(DN_HEADS)], axis=1), nxt, inv


def _delta_fwd(qkv, bg, a_row, name):
    S = qkv.shape[0]
    nc = S // BLK

    def body(q_ref, k_ref, v_ref, bg_ref, ar_ref, o_ref, st_ref, inv_ref, state):
        ci = pl.program_id(0)

        @pl.when(ci == 0)
        def _():
            state[...] = jnp.zeros_like(state)

        st = state[...]
        st_ref[:, 0] = st
        o, nxt, inv = _delta_step(st, q_ref[...], k_ref[...], v_ref[...], bg_ref[...], ar_ref[:, pl.ds(ci, 1), :])
        o_ref[...] = o
        inv_ref[:, 0] = inv
        state[...] = nxt

    part = lambda j: pl.BlockSpec((BLK, DN_W), lambda c: (c, j))
    per_chunk = pl.BlockSpec((DN_HEADS, 1, DN_DH, DN_DH), lambda c: (0, c, 0, 0))
    mats = jax.ShapeDtypeStruct((DN_HEADS, nc, DN_DH, DN_DH), F32)
    return pl.pallas_call(
        body, name=name, grid=(nc,),
        in_specs=[part(0), part(1), part(2), pl.BlockSpec((BLK, LANES), lambda c: (c, 0)),
                  pl.BlockSpec((DN_HEADS, nc, BLK), lambda c: (0, 0, 0))],
        out_specs=[part(0), per_chunk, per_chunk], out_shape=[jax.ShapeDtypeStruct((S, DN_W), F32), mats, mats],
        scratch_shapes=[pltpu.VMEM((DN_HEADS, DN_DH, DN_DH), F32)],
        compiler_params=_cp("arbitrary"),
    )(qkv, qkv, qkv, bg, a_row)


def _delta_bwd(qkv, bg, a_row, states, invs, do, name):
    S = qkv.shape[0]
    nc = S // BLK

    def body(q_ref, k_ref, v_ref, bg_ref, ar_ref, st_ref, inv_ref, do_ref, dqkv_ref, dbg_ref, dar_ref, dstate):
        t = pl.program_id(0)
        ci = nc - 1 - t

        @pl.when(t == 0)
        def _():
            dstate[...] = jnp.zeros_like(dstate)

        step = lambda *a: _delta_step(*a, differentiable=True, inv_known=inv_ref[:, 0])[:2]
        _, vjp = jax.vjp(step, st_ref[:, 0], q_ref[...], k_ref[...], v_ref[...], bg_ref[...],
                         ar_ref[:, pl.ds(ci, 1), :])
        dprev, dq, dk, dv, dbg, dar = vjp((do_ref[...], dstate[...]))
        dqkv_ref[:, 0:DN_W] = dq
        dqkv_ref[:, DN_W:2 * DN_W] = dk
        dqkv_ref[:, 2 * DN_W:3 * DN_W] = dv
        dbg_ref[...] = dbg
        dar_ref[:, pl.ds(ci, 1), :] = dar
        dstate[...] = dprev

    part = lambda j: pl.BlockSpec((BLK, DN_W), lambda t: (nc - 1 - t, j))
    lanes = pl.BlockSpec((BLK, LANES), lambda t: (nc - 1 - t, 0))
    rows = pl.BlockSpec((DN_HEADS, nc, BLK), lambda t: (0, 0, 0))
    per_chunk = pl.BlockSpec((DN_HEADS, 1, DN_DH, DN_DH), lambda t: (0, nc - 1 - t, 0, 0))
    return pl.pallas_call(
        body, name=name, grid=(nc,),
        in_specs=[part(0), part(1), part(2), lanes, rows, per_chunk, per_chunk, part(0)],
        out_specs=[pl.BlockSpec((BLK, 3 * DN_W), lambda t: (nc - 1 - t, 0)), lanes, rows],
        out_shape=[jax.ShapeDtypeStruct((S, 3 * DN_W), F32), jax.ShapeDtypeStruct((S, LANES), F32),
                   jax.ShapeDtypeStruct((DN_HEADS, nc, BLK), F32)],
        scratch_shapes=[pltpu.VMEM((DN_HEADS, DN_DH, DN_DH), F32)],
        compiler_params=_cp("arbitrary"),
    )(qkv, qkv, qkv, bg, a_row, states, invs, do)


def _gate_sb(o, z):
    return o * _silu(z)


def _gate_dn(o, z, g):
    return jnp.concatenate(
        [_rms(o[:, h * DN_DH:(h + 1) * DN_DH]) * g * _silu(z[:, h * DN_DH:(h + 1) * DN_DH]) for h in range(DN_HEADS)],
        axis=1)


def _merge_specs(S, D, tm):
    row = lambda w, blk: pl.BlockSpec((tm, w), lambda i: (i, blk))
    full = lambda a, b: pl.BlockSpec((a, b), lambda i: (0, 0))
    return [row(D, 0), full(1, D), row(SB_W, 0), row(SB_W, 3), row(DN_W, 0), row(DN_W, 7),
            row(D, IN_MAIN // D), row(D, IN_MAIN // D + 1), full(1, DN_DH), full(SB_W, D), full(DN_W, D), full(D, D)]


def _merge_fwd(x, gate, o_sb, o_dn, p, ng, wbs, wbd, wo, name):
    S, D = x.shape
    tm = min(512, S)

    def body(x_ref, gate_ref, osb_ref, zsb_ref, odn_ref, zdn_ref, msb_ref, mdn_ref, ng_ref, wbs_ref, wbd_ref, wo_ref,
             out_ref):
        a = _gate_sb(osb_ref[...], zsb_ref[...])
        b = _gate_dn(odn_ref[...], zdn_ref[...], ng_ref[...])
        y = _sigmoid(msb_ref[...]) * _bdot(a, wbs_ref[...]) + _sigmoid(mdn_ref[...]) * _bdot(b, wbd_ref[...])
        out_ref[...] = x_ref[...] + gate_ref[...] * _bdot(y, wo_ref[...])

    return pl.pallas_call(
        body, name=name, grid=(S // tm,), in_specs=_merge_specs(S, D, tm),
        out_specs=pl.BlockSpec((tm, D), lambda i: (i, 0)), out_shape=jax.ShapeDtypeStruct((S, D), F32),
        compiler_params=_cp("parallel"),
    )(x, gate, o_sb, p, o_dn, p, p, p, ng, wbs, wbd, wo)


def _merge_bwd(dxn, gate, o_sb, o_dn, p, ng, wbs, wbd, wo, name):
    S, D = dxn.shape
    tm = min(256, S)

    def body(dxn_ref, gate_ref, osb_ref, zsb_ref, odn_ref, zdn_ref, msb_ref, mdn_ref, ng_ref, wbs_ref, wbd_ref, wo_ref,
             dosb_ref, dzsb_ref, dodn_ref, dzdn_ref, dmsb_ref, dmdn_ref, dwo_ref, dwbs_ref, dwbd_ref, dgate_ref, dng_ref):
        @pl.when(pl.program_id(0) == 0)
        def _():
            for ref in (dwo_ref, dwbs_ref, dwbd_ref, dgate_ref, dng_ref):
                ref[...] = jnp.zeros_like(ref)

        a, vjp_a = jax.vjp(_gate_sb, osb_ref[...], zsb_ref[...])
        b, vjp_b = jax.vjp(_gate_dn, odn_ref[...], zdn_ref[...], ng_ref[...])
        a16, b16 = a.astype(BF16), b.astype(BF16)
        ps = jnp.dot(a16, wbs_ref[...], preferred_element_type=F32)
        pd = jnp.dot(b16, wbd_ref[...], preferred_element_type=F32)
        ss, sd = _sigmoid(msb_ref[...]), _sigmoid(mdn_ref[...])
        y16 = (ss * ps + sd * pd).astype(BF16)
        out = jnp.dot(y16, wo_ref[...], preferred_element_type=F32)
        dxn_v = dxn_ref[...]
        dgate_ref[...] += jnp.sum(dxn_v * out, axis=0, keepdims=True)
        dout16 = (dxn_v * gate_ref[...]).astype(BF16)
        dwo_ref[...] += _bdot_tn(y16, dout16)
        dy = _bdot_nt(dout16, wo_ref[...])
        dmsb_ref[...] = (dy * ps * ss * (1.0 - ss)).astype(BF16)
        dmdn_ref[...] = (dy * pd * sd * (1.0 - sd)).astype(BF16)
        dps16, dpd16 = (dy * ss).astype(BF16), (dy * sd).astype(BF16)
        dwbs_ref[...] += _bdot_tn(a16, dps16)
        dwbd_ref[...] += _bdot_tn(b16, dpd16)
        dosb, dzsb = vjp_a(_bdot_nt(dps16, wbs_ref[...]))
        dodn, dzdn, dng = vjp_b(_bdot_nt(dpd16, wbd_ref[...]))
        dosb_ref[...] = dosb
        dzsb_ref[...] = dzsb.astype(BF16)
        dodn_ref[...] = dodn
        dzdn_ref[...] = dzdn.astype(BF16)
        dng_ref[...] += dng

    row = lambda w: pl.BlockSpec((tm, w), lambda i: (i, 0))
    full = lambda a, b: pl.BlockSpec((a, b), lambda i: (0, 0))
    sds = jax.ShapeDtypeStruct
    return pl.pallas_call(
        body, name=name, grid=(S // tm,), in_specs=_merge_specs(S, D, tm),
        out_specs=[row(SB_W), row(SB_W), row(DN_W), row(DN_W), row(D), row(D),
                   full(D, D), full(SB_W, D), full(DN_W, D), full(1, D), full(1, DN_DH)],
        out_shape=[sds((S, SB_W), F32), sds((S, SB_W), BF16), sds((S, DN_W), F32), sds((S, DN_W), BF16),
                   sds((S, D), BF16), sds((S, D), BF16), sds((D, D), F32), sds((SB_W, D), F32), sds((DN_W, D), F32),
                   sds((1, D), F32), sds((1, DN_DH), F32)],
        compiler_params=_cp("arbitrary"),
    )(dxn, gate, o_sb, p, o_dn, p, p, p, ng, wbs, wbd, wo)


def _loss_fwd_bwd(y, target, name):
    S, D = y.shape
    tm = min(512, S)

    def body(y_ref, t_ref, l_ref, dy_ref):
        @pl.when(pl.program_id(0) == 0)
        def _():
            l_ref[...] = jnp.zeros_like(l_ref)

        e = y_ref[...] - t_ref[...]
        l_ref[...] += jnp.sum(e * e, axis=0, keepdims=True) * (0.5 / D)
        dy_ref[...] = e * (1.0 / D)

    row = pl.BlockSpec((tm, D), lambda i: (i, 0))
    return pl.pallas_call(
        body, name=name, grid=(S // tm,), in_specs=[row, row],
        out_specs=[pl.BlockSpec((1, D), lambda i: (0, 0)), row],
        out_shape=[jax.ShapeDtypeStruct((1, D), F32), jax.ShapeDtypeStruct((S, D), F32)],
        compiler_params=_cp("arbitrary"),
    )(y, target)


def _mod_fwd(c_all, ada_w, name):
    L, D, n = ada_w.shape

    def body(c_ref, w_ref, o_ref):
        o_ref[0] = _dot(_silu(c_ref[...]), w_ref[0])

    return pl.pallas_call(
        body, name=name, grid=(L,),
        in_specs=[pl.BlockSpec(c_all.shape, lambda l: (0, 0)), pl.BlockSpec((1, D, n), lambda l: (l, 0, 0))],
        out_specs=pl.BlockSpec((1, N_DEV, n), lambda l: (l, 0, 0)),
        out_shape=jax.ShapeDtypeStruct((L, N_DEV, n), F32),
        compiler_params=_cp("parallel"),
    )(c_all, ada_w)


def _mod_bwd_w(c_all_t, dmod, name):
    L, _, n = dmod.shape
    D = c_all_t.shape[0]

    def body(c_ref, d_ref, o_ref):
        o_ref[0] = _dot(_silu(c_ref[...]), d_ref[0])

    return pl.pallas_call(
        body, name=name, grid=(L,),
        in_specs=[pl.BlockSpec(c_all_t.shape, lambda l: (0, 0)), pl.BlockSpec((1, N_DEV, n), lambda l: (l, 0, 0))],
        out_specs=pl.BlockSpec((1, D, n), lambda l: (l, 0, 0)),
        out_shape=jax.ShapeDtypeStruct((L, D, n), F32),
        compiler_params=_cp("parallel"),
    )(c_all_t, dmod)


def _me():
    return lax.axis_index("x"), lax.axis_index("y"), lax.axis_index("c")


def _peer(k):
    x, y, c = _me()
    return (1 - x if k & 4 else x, 1 - y if k & 2 else y, 1 - c if k & 1 else c)


def _lin(dev):
    return 4 * dev[0] + 2 * dev[1] + dev[2]


class _Exchange:
    def __init__(self, arrays, scatter):
        self.n = len(arrays)
        self.scatter = scatter
        self.out_shape = [jax.ShapeDtypeStruct((N_DEV,) + tuple(a.shape[1:] if scatter else a.shape), a.dtype)
                          for a in arrays]
        self.specs = [pl.BlockSpec(memory_space=pl.ANY)] * self.n
        self.scratch = [pltpu.SemaphoreType.DMA((self.n, N_DEV - 1)), pltpu.SemaphoreType.DMA((self.n, N_DEV - 1)),
                        pltpu.SemaphoreType.DMA((self.n,))]

    def _copies(self, ins, outs, sems):
        send_sems, recv_sems, local_sems = sems
        me = _lin(_me())
        local, direct, passed, landed = [], [], [], []
        for t in range(self.n):
            src_of = (lambda d, t=t: ins[t].at[d]) if self.scatter else (lambda d, t=t: ins[t])
            local.append(pltpu.make_async_copy(src_of(me), outs[t].at[me], local_sems.at[t]))
            for k in range(1, N_DEV):
                peer = _peer(k)
                pair = dict(send_sem=send_sems.at[t, k - 1], recv_sem=recv_sems.at[t, k - 1],
                            device_id_type=pl.DeviceIdType.MESH)
                slot = outs[t].at[_lin(peer)]
                landed.append(pltpu.make_async_remote_copy(src_ref=slot, dst_ref=slot, device_id=peer, **pair))
                if self.scatter or k in (1, 2, 4, 6):
                    direct.append(pltpu.make_async_remote_copy(src_ref=src_of(_lin(peer)), dst_ref=outs[t].at[me],
                                                               device_id=peer, **pair))
                else:
                    came = outs[t].at[_lin(_peer(k - 1))]
                    passed.append((landed[-2], pltpu.make_async_remote_copy(src_ref=came, dst_ref=came,
                                                                            device_id=_peer(1), **pair)))
        return local, direct, passed, landed

    def start(self, ins, outs, sems):
        local, direct, _, _ = self._copies(ins, outs, sems)
        for cp in local + direct:
            cp.start()

    def finish(self, ins, outs, sems):
        local, direct, passed, landed = self._copies(ins, outs, sems)
        arrived = set()
        for came, onward in passed:
            came.wait_recv()
            arrived.add(id(came))
            onward.start()
        for cp in landed:
            if id(cp) not in arrived:
                cp.wait_recv()
        for cp in direct + [onward for _, onward in passed]:
            cp.wait_send()
        for cp in local:
            cp.wait()


def _exchange(arrays, scatter, name):
    ex = _Exchange(arrays, scatter)

    def body(*refs):
        ins, outs, sems = refs[:ex.n], refs[ex.n:2 * ex.n], refs[2 * ex.n:]
        ex.start(ins, outs, sems)
        ex.finish(ins, outs, sems)

    return pl.pallas_call(body, name=name, in_specs=ex.specs, out_specs=ex.specs, out_shape=ex.out_shape,
                          scratch_shapes=ex.scratch)(*arrays)


def _sum_slots(a, name):
    _, R, C = a.shape
    tr = SUM_ROWS if R % SUM_ROWS == 0 else R

    def body(a_ref, o_ref):
        acc = a_ref[0].astype(F32)
        for s in range(1, N_DEV):
            acc = acc + a_ref[s].astype(F32)
        o_ref[...] = acc

    return pl.pallas_call(
        body, name=name, grid=(R // tr,),
        in_specs=[pl.BlockSpec((N_DEV, tr, C), lambda i: (0, i, 0))], out_specs=pl.BlockSpec((tr, C), lambda i: (i, 0)),
        out_shape=jax.ShapeDtypeStruct((R, C), F32), compiler_params=_cp("parallel"),
    )(a)


def _adamw(w, g, m, v, name):
    shape = w.shape
    C = shape[-1]
    R = w.size // C
    tr = R
    for cand in (256, 128, 64):
        if R > cand and R % cand == 0:
            tr = cand
            break
    c1 = 1.0 / (1.0 - ADAM_B1 ** ADAM_STEP)
    c2 = 1.0 / (1.0 - ADAM_B2 ** ADAM_STEP)

    def body(w_ref, g_ref, m_ref, v_ref, d_ref, nm_ref, nv_ref):
        gv = g_ref[...]
        nm = ADAM_B1 * m_ref[...] + (1.0 - ADAM_B1) * gv
        nv = ADAM_B2 * v_ref[...] + (1.0 - ADAM_B2) * (gv * gv)
        d_ref[...] = -ADAM_LR * ((nm * c1) / (jnp.sqrt(nv * c2) + ADAM_EPS) + ADAM_WD * w_ref[...])
        nm_ref[...] = nm
        nv_ref[...] = nv

    spec = pl.BlockSpec((tr, C), lambda i: (i, 0))
    sd = jax.ShapeDtypeStruct((R, C), F32)
    outs = pl.pallas_call(
        body, name=name, grid=(R // tr,), in_specs=[spec] * 4, out_specs=[spec] * 3, out_shape=[sd] * 3,
        compiler_params=_cp("parallel"),
    )(*(t.reshape(R, C) for t in (w, g, m, v)))
    return tuple(t.reshape(shape) for t in outs)


def _col_segments(D):
    return [(0, IN_MAIN, 0), (IN_COLS, IN_COLS + 2 * D, IN_MAIN), (IN_MAIN, IN_COLS, IN_MAIN + 2 * D)]


def _pad_cols_of_blocks(wi):
    _, D, n = wi.shape
    pieces = []
    for lo, hi, _ in _col_segments(D):
        for d in range(N_DEV):
            a, b = max(lo, d * n), min(hi, (d + 1) * n)
            if a < b:
                pieces.append(wi[d][:, a - d * n:b - d * n])
    return jnp.concatenate(pieces + [jnp.zeros((D, LANES - 2 * DN_HEADS), wi.dtype)], axis=1)


def _blocks_of_padded(dw, n):
    D = dw.shape[0]
    out = []
    for d in range(N_DEV):
        pieces = []
        for lo, hi, at in sorted(_col_segments(D)):
            a, b = max(lo, d * n), min(hi, (d + 1) * n)
            if a < b:
                pieces.append(dw[:, at + a - lo:at + b - lo])
        out.append(pieces[0] if len(pieces) == 1 else jnp.concatenate(pieces, axis=1))
    return jnp.stack(out)


def _gate_params(a_log, dt_bias):
    z = jnp.zeros((LANES,), F32)
    return jnp.stack([z.at[DN_HEADS:2 * DN_HEADS].set(a_log), z.at[DN_HEADS:2 * DN_HEADS].set(dt_bias)])


def _layer_fwd(l, x, mod, wts, carry=None, rest=None):
    S, D = x.shape
    tag = f"l{l}_"
    (p, h), got = _inproj_fwd(x, mod, wts["norm_g"], wts["w_in"], tag + "inproj_fwd",
                              None if rest is None else (rest[0], False))
    if rest is not None:
        wts = {**wts, **rest[1](got)}
    (o_sb,), carried = _sb_fwd(p, wts["sb_q_g"], wts["sb_k_g"].T, tag + "sb_fwd", carry)
    qkv = _dn_prep_fwd(p, wts["conv_w"], 4 * SB_W, tag + "dn_prep_fwd")
    pv = _gate_params(wts["dn_a_log"], wts["dn_dt_bias"])
    ba_blk = (IN_MAIN + 2 * D) // LANES
    bg = _dn_gate_fwd(p, pv, ba_blk, tag + "dn_gate_fwd")
    a_row = bg[:, DN_HEADS:2 * DN_HEADS].T.reshape(DN_HEADS, S // BLK, BLK)
    o_dn, states, invs = _delta_fwd(qkv, bg, a_row, tag + "delta_fwd")
    gate = mod[:, 2 * D:]
    out = _merge_fwd(x, gate, o_sb, o_dn, p, wts["dn_norm_g"], wts["w_branch_sb"], wts["w_branch_dn"], wts["w_out"],
                     tag + "merge_fwd")
    saved = dict(x=x, mod=mod, p=p, h=h, o_sb=o_sb, qkv=qkv, pv=pv, bg=bg, a_row=a_row, o_dn=o_dn, states=states,
                 invs=invs, gate=gate)
    return out, saved, carried, wts


def _layer_bwd(l, dxn, sv, wts, carry_of=None, late_carry_of=None):
    S, D = dxn.shape
    tag = f"l{l}_"
    (dosb, dzsb, dodn, dzdn, dmsb, dmdn, dwo, dwbs, dwbd, dgate, dng) = _merge_bwd(
        dxn, sv["gate"], sv["o_sb"], sv["o_dn"], sv["p"], wts["dn_norm_g"], wts["w_branch_sb"], wts["w_branch_dn"],
        wts["w_out"], tag + "merge_bwd")
    carry = None if carry_of is None else carry_of(dict(w_out=dwo, w_branch_sb=dwbs, w_branch_dn=dwbd))
    (dq, dk, dv, dgq, dgkt), carried = _sb_bwd(sv["p"], sv["o_sb"], dosb, wts["sb_q_g"], wts["sb_q_g"].T,
                                                wts["sb_k_g"], wts["sb_k_g"].T, tag + "sb_bwd", carry)
    dqkv_n, dbg, dar = _delta_bwd(sv["qkv"], sv["bg"], sv["a_row"], sv["states"], sv["invs"], dodn, tag + "delta_bwd")
    dqkv, dconv = _dn_prep_bwd(sv["p"], wts["conv_w"], 4 * SB_W, dqkv_n, tag + "dn_prep_bwd")
    dbg = dbg.at[:, DN_HEADS:2 * DN_HEADS].add(dar.reshape(DN_HEADS, S).T)
    ba_blk = (IN_MAIN + 2 * D) // LANES
    dba, dpv = _dn_gate_bwd(sv["p"], sv["pv"], ba_blk, dbg, tag + "dn_gate_bwd")
    dp = jnp.concatenate([dq, dk, dv, dzsb, dqkv, dzdn, dmsb, dmdn, dba], axis=1)
    dw_in = _matmul_tn(sv["h"].T, dp, tag + "inproj_bwd_dw")
    late = None if late_carry_of is None else late_carry_of(dict(w_in=dw_in, conv_w=dconv))
    (dx, dmod, dg), carried_late = _inproj_bwd_dx(dp, wts["w_in"], sv["x"], sv["mod"], wts["norm_g"], dxn,
                                                  tag + "inproj_bwd_dx", late)
    dmod = dmod.at[:, 2 * D:].set(dgate)
    grads = dict(w_in=dw_in, w_branch_sb=dwbs, w_branch_dn=dwbd, w_out=dwo, conv_w=dconv,
                 mod=dmod[0], norm_g=dg[0], sb_q_g=jnp.sum(dgq, axis=0)[0], sb_k_g=jnp.sum(dgkt, axis=0)[:, 0],
                 dn_a_log=dpv[0, DN_HEADS:2 * DN_HEADS], dn_dt_bias=dpv[1, DN_HEADS:2 * DN_HEADS], dn_norm_g=dng[0])
    return dx, grads, carried, carried_late


def _pad_rows(a, mult):
    extra = (-a.shape[0]) % mult
    return a if extra == 0 else jnp.concatenate([a, jnp.zeros((extra,) + a.shape[1:], a.dtype)], axis=0)


def _pack_rows(parts, width, mult):
    flat = jnp.concatenate([t.reshape(-1) for t in parts])
    extra = (-flat.shape[0]) % width
    if extra:
        flat = jnp.concatenate([flat, jnp.zeros((extra,), flat.dtype)])
    return _pad_rows(flat.reshape(-1, width), mult)


def _take(flat, off, shape):
    n = math.prod(shape)
    return flat[..., off:off + n].reshape(flat.shape[:-1] + tuple(shape)), off + n


SMALL = ("mod", "norm_g", "sb_q_g", "sb_k_g", "dn_a_log", "dn_dt_bias", "dn_norm_g")


def kernel(x, c, ada_w, ada_b, norm_g, w_in, sb_q_g, sb_k_g, conv_w, dn_a_log, dn_dt_bias, dn_norm_g, w_branch_sb, w_branch_dn, w_out, loss_target, m_ada_w, m_ada_b, m_norm_g, m_w_in, m_sb_q_g, m_sb_k_g, m_conv_w, m_dn_a_log, m_dn_dt_bias, m_dn_norm_g, m_w_branch_sb, m_w_branch_dn, m_w_out, v_ada_w, v_ada_b, v_norm_g, v_w_in, v_sb_q_g, v_sb_k_g, v_conv_w, v_dn_a_log, v_dn_dt_bias, v_dn_norm_g, v_w_branch_sb, v_w_branch_dn, v_w_out):
    L, D = norm_g.shape
    S = x.shape[1]
    n_in = w_in.shape[2]
    n_ada = ada_w.shape[2]
    n_br = w_branch_sb.shape[2]
    n_out = w_out.shape[1]
    n_conv = conv_w.shape[2]
    me = _lin(_me())

    def cat(a):
        return jnp.concatenate([a[d] for d in range(N_DEV)], axis=1)

    c_all, conv_all = _exchange([c, conv_w.reshape(L * CONV_K, n_conv)], False, "gather_small")
    c_all = c_all.reshape(N_DEV, D)
    conv_full = cat(conv_all).reshape(L, CONV_K, N_DEV * n_conv)

    mod_part = _mod_fwd(c_all, ada_w, "mod_fwd")

    def shards16(l):
        return [w_in[l].astype(BF16), w_branch_sb[l].astype(BF16), w_branch_dn[l].astype(BF16), w_out[l].astype(BF16)]

    def late(got):
        wbs, wbd, wo = got
        return dict(w_branch_sb=cat(wbs), w_branch_dn=cat(wbd), w_out=wo.reshape(N_DEV * n_out, D))

    def whole(l, got):
        return dict(norm_g=norm_g[l:l + 1], w_in=_pad_cols_of_blocks(got[0]), sb_q_g=sb_q_g[l:l + 1],
                    sb_k_g=sb_k_g[l:l + 1], conv_w=conv_full[l], dn_a_log=dn_a_log[l], dn_dt_bias=dn_dt_bias[l],
                    dn_norm_g=dn_norm_g[l:l + 1], **(late(got[1:]) if len(got) > 1 else {}))

    first = shards16(0)
    wi0, mod_all = _exchange([first[0], mod_part.reshape(L * N_DEV, n_ada)], False, "gather_weights")
    got = [wi0]
    mod_full = cat(mod_all).reshape(L, N_DEV, N_DEV * n_ada) + ada_b[:, None, :]
    mod_mine = lax.dynamic_slice_in_dim(mod_full, me, 1, axis=1)

    act = x[0]
    saved, wts = [], []
    for l in range(L):
        act, sv, got, w_l = _layer_fwd(l, act, mod_mine[l], whole(l, got),
                                       (shards16(l + 1), False) if l + 1 < L else None,
                                       (first[1:], late) if l == 0 else None)
        wts.append(w_l)
        saved.append(sv)
    loss_cols, dact = _loss_fwd_bwd(act, loss_target[0], "loss")
    loss = lax.psum(jnp.sum(loss_cols), ("x", "y", "c"))

    def blocks(name, g):
        if name == "w_out":
            return g.astype(BF16).reshape(N_DEV, n_out, D)
        if name == "w_in":
            return _blocks_of_padded(g, n_in).astype(BF16)
        n = g.shape[1] // N_DEV
        dtype = F32 if name == "conv_w" else BF16
        return jnp.stack([g[:, d * n:(d + 1) * n].astype(dtype) for d in range(N_DEV)])

    early, late = ("w_out", "w_branch_sb", "w_branch_dn"), ("w_in", "conv_w")
    grads, recv, pending = [None] * L, {}, []
    for l in reversed(range(L)):
        keys = [k for k, _ in pending] + [(n, l) for n in early]

        def carry_of(g_early, pending=pending):
            return [a for _, a in pending] + [blocks(n, g_early[n]) for n in early], True

        last = l == 0
        dact, grads[l], got, got_late = _layer_bwd(
            l, dact, saved[l], wts[l], carry_of, (lambda g: ([blocks(n, g[n]) for n in late], True)) if last else None)
        recv.update(zip(keys, got))
        recv.update(zip([(n, l) for n in late], got_late))
        pending = [] if last else [((n, l), blocks(n, grads[l][n])) for n in late]
    grad_x = dact[None]
    small_g = _pack_rows([grads[l][n] for l in range(L) for n in SMALL], LANES, 8)
    (small_all_g,) = _exchange([small_g], False, "gather_small_grads")
    small_sum = _sum_slots(small_all_g, "sum_small_grads").reshape(-1)
    shard_shapes = dict(w_in=w_in.shape, w_branch_sb=w_branch_sb.shape, w_branch_dn=w_branch_dn.shape,
                        conv_w=conv_w.shape, w_out=w_out.shape)
    g_out = {n: jnp.stack([_sum_slots(recv[(n, l)], f"sum_{n}_l{l}").reshape(shape[1:]) for l in range(L)])
             for n, shape in shard_shapes.items()}
    small_shapes = dict(mod=(3 * D,), norm_g=(D,), sb_q_g=(SB_DH,), sb_k_g=(SB_DH,), dn_a_log=(DN_HEADS,),
                        dn_dt_bias=(DN_HEADS,), dn_norm_g=(DN_DH,))
    off = 0
    off_all = 0
    small_each = small_all_g.reshape(N_DEV, -1)
    per_small = {n: [] for n in SMALL}
    dmod_all = []
    for l in range(L):
        for n in SMALL:
            t, off = _take(small_sum, off, small_shapes[n])
            per_small[n].append(t)
            if n == "mod":
                t_all, _ = _take(small_each, off_all, small_shapes[n])
                dmod_all.append(t_all)
            off_all += math.prod(small_shapes[n])
    for n in SMALL:
        g_out[n if n != "mod" else "ada_b"] = jnp.stack(per_small[n])
    dmod_all = jnp.stack(dmod_all)
    dmod_cols = lax.dynamic_slice_in_dim(dmod_all, me * n_ada, n_ada, axis=2)
    g_out["ada_w"] = _mod_bwd_w(c_all.T, dmod_cols, "mod_bwd_w")

    given = dict(ada_w=(ada_w, m_ada_w, v_ada_w), ada_b=(ada_b, m_ada_b, v_ada_b), norm_g=(norm_g, m_norm_g, v_norm_g),
                 w_in=(w_in, m_w_in, v_w_in), sb_q_g=(sb_q_g, m_sb_q_g, v_sb_q_g), sb_k_g=(sb_k_g, m_sb_k_g, v_sb_k_g),
                 conv_w=(conv_w, m_conv_w, v_conv_w), dn_a_log=(dn_a_log, m_dn_a_log, v_dn_a_log),
                 dn_dt_bias=(dn_dt_bias, m_dn_dt_bias, v_dn_dt_bias), dn_norm_g=(dn_norm_g, m_dn_norm_g, v_dn_norm_g),
                 w_branch_sb=(w_branch_sb, m_w_branch_sb, v_w_branch_sb),
                 w_branch_dn=(w_branch_dn, m_w_branch_dn, v_w_branch_dn), w_out=(w_out, m_w_out, v_w_out))
    order = list(given)
    upd = {n: _adamw(given[n][0], g_out[n], given[n][1], given[n][2], "adamw_" + n) for n in order}
    return (loss, grad_x, *[g_out[n] for n in order], *[upd[n][0] for n in order], *[upd[n][1] for n in order],
            *[upd[n][2] for n in order])
```
